```python
import math
import jax, jax.numpy as jnp
from jax import lax
import numpy as np

D_MODEL = 1024
BATCH = 16
SEQ = 256
DEPTH = 2
DEC_BATCH = 2
DEC_SEQ = 1024
PAST_LEN = 512

GRID_W = 64
N_EVEN = (DEPTH + 1) // 2
N_ODD = DEPTH // 2
LRU_WIDTH = D_MODEL // 2
LRU_BLOCKS = 8
LRU_BLOCK = LRU_WIDTH // LRU_BLOCKS
CONV_W = 4
LRU_C = 8.0
WIN_HEADS = 8
WIN_KV_HEADS = 2
WIN_REP = WIN_HEADS // WIN_KV_HEADS
WIN_HEAD_DIM = 64
WINDOW = 128
WIN_Q = WIN_HEADS * WIN_HEAD_DIM
WIN_KV = WIN_KV_HEADS * WIN_HEAD_DIM
EVEN_IN = 2 * LRU_WIDTH + WIN_Q + 2 * WIN_KV
EVEN_MIX = LRU_WIDTH + WIN_Q
DIFF_HEADS = 8
DIFF_HEAD_DIM = 64
DIFF_QK = DIFF_HEADS * 2 * DIFF_HEAD_DIM
DIFF_V = DIFF_HEADS * 2 * DIFF_HEAD_DIM
ODD_IN = 2 * DIFF_QK + DIFF_V
N_EXPERTS = 16
EXPERT_FF = 2 * D_MODEL
CAPACITY_FACTOR = 2
Q_BLOCK = 128
ROPE_BASE = 10000.0
EPS = 1e-6
NEG_INF = -1e30

kernel_name = "hybrid_diffusion_lru_window_diff_ec"


def _rmsnorm(x, g):
    xf = x.astype(jnp.float32)
    y = xf * lax.rsqrt(jnp.mean(xf * xf, axis=-1, keepdims=True) + EPS)
    return (y * g.astype(jnp.float32)).astype(x.dtype)


def _modulate(x, g, shift, scale):
    return _rmsnorm(x, g) * (1.0 + scale[:, None]) + shift[:, None]


def _adaln(cvec, w, b):
    m = jax.nn.silu(cvec) @ w + b
    return m.reshape(cvec.shape[0], 6, D_MODEL)


def _axial_rope(rows, head_dim):
    f32 = jnp.float32
    row = jnp.repeat(jnp.arange(rows, dtype=f32), GRID_W)
    col = jnp.tile(jnp.arange(GRID_W, dtype=f32), rows)
    nf = head_dim // 4
    inv = ROPE_BASE ** (-jnp.arange(nf, dtype=f32) / nf)
    ar = row[:, None] * inv[None]
    ac = col[:, None] * inv[None]
    ang = jnp.concatenate([ar, ar, ac, ac], axis=-1)
    return jnp.cos(ang), jnp.sin(ang)


def _rope(x, cos, sin):
    half = x.shape[-1] // 2
    nf = half // 2
    def rot(z):
        return jnp.concatenate([-z[..., nf:], z[..., :nf]], axis=-1)
    rotated = jnp.concatenate([rot(x[..., :half]), rot(x[..., half:])], axis=-1)
    return (x * cos + rotated * sin).astype(x.dtype)


def _centered_dwconv(x, w, b):
    S = x.shape[1]
    left = CONV_W // 2
    right = CONV_W - 1 - left
    xp = jnp.pad(x, ((0, 0), (left, right), (0, 0)))
    y = xp[:, 0:S] * w[0]
    for j in range(1, CONV_W):
        y = y + xp[:, j:j + S] * w[j]
    return y + b


def _affine_combine(left, right):
    al, bl = left
    ar, br = right
    return al * ar, ar * bl + br


def _rglru_dir(xc, wa, ba, wx, bx, lam, h0):
    B_, S_, _ = xc.shape
    xb = xc.reshape(B_, S_, LRU_BLOCKS, LRU_BLOCK)
    r = jax.nn.sigmoid(jnp.einsum('bsnk,nkj->bsnj', xb, wa.astype(jnp.float32)).reshape(B_, S_, LRU_WIDTH) + ba)
    i = jax.nn.sigmoid(jnp.einsum('bsnk,nkj->bsnj', xb, wx.astype(jnp.float32)).reshape(B_, S_, LRU_WIDTH) + bx)
    log_a = -LRU_C * r * jax.nn.softplus(-lam.astype(jnp.float32))
    a = jnp.exp(log_a)
    b = jnp.sqrt(-jnp.expm1(2.0 * log_a)) * (i * xc)
    b = b.at[:, 0].add(a[:, 0] * h0)
    _, h = lax.associative_scan(_affine_combine, (a, b), axis=1)
    return h


def _rglru_bidir(xl, gl, conv_w, conv_b, wa, ba, wx, bx, lam, h0):
    f32 = jnp.float32
    xc = _centered_dwconv(xl, conv_w, conv_b).astype(f32)
    h0 = h0.astype(f32)
    hf = _rglru_dir(xc, wa[0], ba[0], wx[0], bx[0], lam[0], h0[:, 0])
    hb = _rglru_dir(xc[:, ::-1], wa[1], ba[1], wx[1], bx[1], lam[1], h0[:, 1])
    y = (hf + hb[:, ::-1]) * jax.nn.gelu(gl.astype(f32))
    h_fin = jnp.stack([hf[:, -1], hb[:, -1]], axis=1)
    return y, h_fin


def _attn_full_with_sink(q, k, v, sink):
    B_, G, R, Sq, dh = q.shape
    nb = Sq // Q_BLOCK
    scale = dh ** -0.5
    qb = jnp.moveaxis(q.reshape(B_, G, R, nb, Q_BLOCK, dh), 3, 0)
    sk = jnp.broadcast_to(sink.reshape(1, G, R, 1, 1).astype(jnp.float32), (B_, G, R, Q_BLOCK, 1))
    def one(qi):
        s = jnp.einsum('bgrqd,bgkd->bgrqk', qi, k).astype(jnp.float32) * scale
        p = jax.nn.softmax(jnp.concatenate([sk, s], axis=-1), axis=-1)[..., 1:]
        return jnp.einsum('bgrqk,bgkd->bgrqd', p.astype(v.dtype), v)
    o = lax.map(one, qb)
    return jnp.moveaxis(o, 0, 3).reshape(B_, G, R, Sq, dh)


def _window_attn_latent(q, k, v, kc, vc, sink):
    B_, G, R, S, dh = q.shape
    W = WINDOW
    nb = S // W
    scale = dh ** -0.5
    qb = jnp.moveaxis(q.reshape(B_, G, R, nb, W, dh), 3, 0)
    def band(t):
        tp = jnp.pad(t, ((0, 0), (0, 0), (W, W), (0, 0))).reshape(B_, G, nb + 2, W, dh)
        tb = jnp.concatenate([tp[:, :, :-2], tp[:, :, 1:-1], tp[:, :, 2:]], axis=3)
        return jnp.moveaxis(tb, 2, 0)
    kb = band(k)
    vb = band(v)
    qpos = jnp.arange(nb)[:, None] * W + jnp.arange(W)[None]
    kpos = (jnp.arange(nb)[:, None] - 1) * W + jnp.arange(3 * W)[None]
    valid = ((jnp.abs(qpos[:, :, None] - kpos[:, None, :]) <= WINDOW)
             & (kpos >= 0)[:, None, :] & (kpos < S)[:, None, :])
    sk = jnp.broadcast_to(sink.reshape(1, G, R, 1, 1).astype(jnp.float32), (B_, G, R, W, 1))
    def one(args):
        qi, ki, vi, mi = args
        s_loc = jnp.einsum('bgrqd,bgkd->bgrqk', qi, ki).astype(jnp.float32) * scale
        s_loc = jnp.where(mi, s_loc, NEG_INF)
        s_ctx = jnp.einsum('bgrqd,bgcd->bgrqc', qi, kc).astype(jnp.float32) * scale
        p = jax.nn.softmax(jnp.concatenate([sk, s_loc, s_ctx], axis=-1), axis=-1).astype(v.dtype)
        return (jnp.einsum('bgrqk,bgkd->bgrqd', p[..., 1:1 + 3 * W], vi)
                + jnp.einsum('bgrqc,bgcd->bgrqd', p[..., 1 + 3 * W:], vc))
    o = lax.map(one, (qb, kb, vb, valid))
    return jnp.moveaxis(o, 0, 3).reshape(B_, G, R, S, dh)


def _diff_attn(q, k, v, lam):
    B_, H, _, Sq, dh = q.shape
    nb = Sq // Q_BLOCK
    scale = dh ** -0.5
    qb = jnp.moveaxis(q.reshape(B_, H, 2, nb, Q_BLOCK, dh), 3, 0)
    def one(qi):
        s = jnp.einsum('bhiqd,bhikd->bhiqk', qi, k).astype(jnp.float32) * scale
        p = jax.nn.softmax(s, axis=-1)
        w = p[:, :, 0] - lam * p[:, :, 1]
        return jnp.einsum('bhqk,bhkd->bhqd', w.astype(v.dtype), v)
    o = lax.map(one, qb)
    return jnp.moveaxis(o, 0, 2).reshape(B_, H, Sq, 2 * dh)


def _even_mixer(h, w_in, w_out, conv_w, conv_b, wa, ba, wx, bx, lam, sink, h0, rope, kv_ctx):
    B_, S_, _ = h.shape
    proj = h @ w_in
    c1 = LRU_WIDTH
    c2 = 2 * LRU_WIDTH
    c3 = c2 + WIN_Q
    c4 = c3 + WIN_KV
    xl, gl, q, k, v = proj[..., :c1], proj[..., c1:c2], proj[..., c2:c3], proj[..., c3:c4], proj[..., c4:]
    y_lru, h_fin = _rglru_bidir(xl, gl, conv_w, conv_b, wa, ba, wx, bx, lam, h0)
    q = q.reshape(B_, S_, WIN_KV_HEADS, WIN_REP, WIN_HEAD_DIM).transpose(0, 2, 3, 1, 4)
    k = k.reshape(B_, S_, WIN_KV_HEADS, WIN_HEAD_DIM).transpose(0, 2, 1, 3)
    v = v.reshape(B_, S_, WIN_KV_HEADS, WIN_HEAD_DIM).transpose(0, 2, 1, 3)
    sink = sink.reshape(WIN_KV_HEADS, WIN_REP)
    if kv_ctx is None:
        o = _attn_full_with_sink(q, k, v, sink)
    else:
        cos, sin = rope
        o = _window_attn_latent(_rope(q, cos, sin), _rope(k, cos, sin), v, kv_ctx[0], kv_ctx[1], sink)
    o = o.transpose(0, 3, 1, 2, 4).reshape(B_, S_, WIN_Q)
    out = jnp.concatenate([y_lru.astype(h.dtype), o], axis=-1) @ w_out
    return out, k, v, h_fin


def _odd_mixer(h, w_in, w_out, lq1, lk1, lq2, lk2, subln_g, lambda_init, rope, kv_ctx):
    B_, S_, _ = h.shape
    proj = h @ w_in
    q = proj[..., :DIFF_QK].reshape(B_, S_, DIFF_HEADS, 2, DIFF_HEAD_DIM).transpose(0, 2, 3, 1, 4)
    k = proj[..., DIFF_QK:2 * DIFF_QK].reshape(B_, S_, DIFF_HEADS, 2, DIFF_HEAD_DIM).transpose(0, 2, 3, 1, 4)
    v = proj[..., 2 * DIFF_QK:].reshape(B_, S_, DIFF_HEADS, 2 * DIFF_HEAD_DIM).transpose(0, 2, 1, 3)
    k_cat = k.transpose(0, 1, 3, 2, 4).reshape(B_, DIFF_HEADS, S_, 2 * DIFF_HEAD_DIM)
    if kv_ctx is None:
        kk, vv = k, v
    else:
        cos, sin = rope
        q = _rope(q, cos, sin)
        P = kv_ctx[0].shape[2]
        kc = kv_ctx[0].reshape(B_, DIFF_HEADS, P, 2, DIFF_HEAD_DIM).transpose(0, 1, 3, 2, 4)
        kk = jnp.concatenate([_rope(k, cos, sin), kc], axis=3)
        vv = jnp.concatenate([v, kv_ctx[1]], axis=2)
    f32 = jnp.float32
    lam = (jnp.exp(jnp.sum(lq1.astype(f32) * lk1.astype(f32)))
           - jnp.exp(jnp.sum(lq2.astype(f32) * lk2.astype(f32))) + lambda_init)
    o = _diff_attn(q, kk, vv, lam).astype(f32)
    o = o * lax.rsqrt(jnp.mean(o * o, axis=-1, keepdims=True) + EPS) * subln_g.astype(f32)[None, :, None, :]
    o = (o * (1.0 - lambda_init)).astype(h.dtype)
    out = o.transpose(0, 2, 1, 3).reshape(B_, S_, DIFF_V) @ w_out
    return out, k_cat, v


def _ec_moe(h, w_router, w_gate, w_up, w_down):
    B_, N, D = h.shape
    cap = CAPACITY_FACTOR * N // N_EXPERTS
    aff = jax.nn.softmax((h @ w_router).astype(jnp.float32), axis=-1)
    g, idx = lax.top_k(jnp.swapaxes(aff, 1, 2), cap)
    xs = jax.vmap(lambda hb, ib: hb[ib])(h, idx)
    a = jnp.einsum('becd,edf->becf', xs, w_gate)
    u = jnp.einsum('becd,edf->becf', xs, w_up)
    y = jnp.einsum('becf,efd->becd', jax.nn.silu(a) * u, w_down) * g[..., None].astype(h.dtype)
    return jax.vmap(lambda yb, ib: jnp.zeros((N, D), yb.dtype).at[ib.reshape(-1)].add(yb.reshape(-1, D)))(y, idx)


def setup_inputs(seed: int = 0) -> dict:
    key = jax.random.key(seed)
    ks = iter(jax.random.split(key, 48))
    f32 = jnp.float32
    def nrm(shape, s):
        return jax.random.normal(next(ks), shape, f32) * s
    u = jax.random.uniform(next(ks), (N_EVEN, 2, LRU_WIDTH), f32, minval=0.9, maxval=0.999)
    s = u ** (1.0 / LRU_C)
    lru_lambda = jnp.log(s) - jnp.log1p(-s)
    return {
        "x_prompt": nrm((BATCH, SEQ, D_MODEL), 1.0),
        "x_sample": nrm((DEC_BATCH, DEC_SEQ, D_MODEL), 1.0),
        "cache_win_k": nrm((DEC_BATCH, N_EVEN, WIN_KV_HEADS, PAST_LEN, WIN_HEAD_DIM), 1.0),
        "cache_win_v": nrm((DEC_BATCH, N_EVEN, WIN_KV_HEADS, PAST_LEN, WIN_HEAD_DIM), 1.0),
        "state_lru": nrm((DEC_BATCH, N_EVEN, 2, LRU_WIDTH), 0.5),
        "cache_diff_k": nrm((DEC_BATCH, N_ODD, DIFF_HEADS, PAST_LEN, 2 * DIFF_HEAD_DIM), 1.0),
        "cache_diff_v": nrm((DEC_BATCH, N_ODD, DIFF_HEADS, PAST_LEN, 2 * DIFF_HEAD_DIM), 1.0),
        "c": nrm((DEC_BATCH, D_MODEL), 1.0),
        "c_ctx": nrm((D_MODEL,), 1.0),
        "ada_w": nrm((DEPTH, D_MODEL, 6 * D_MODEL), 0.5 * D_MODEL ** -0.5),
        "ada_b": nrm((DEPTH, 6 * D_MODEL), 0.02),
        "norm_g": 1.0 + nrm((DEPTH, 2, D_MODEL), 0.05),
        "final_g": 1.0 + nrm((D_MODEL,), 0.05),
        "even_w_in": nrm((N_EVEN, D_MODEL, EVEN_IN), D_MODEL ** -0.5),
        "even_w_out": nrm((N_EVEN, EVEN_MIX, D_MODEL), EVEN_MIX ** -0.5),
        "conv_w": nrm((N_EVEN, CONV_W, LRU_WIDTH), CONV_W ** -0.5),
        "conv_b": nrm((N_EVEN, LRU_WIDTH), 0.02),
        "lru_wa": nrm((N_EVEN, 2, LRU_BLOCKS, LRU_BLOCK, LRU_BLOCK), LRU_BLOCK ** -0.5),
        "lru_ba": nrm((N_EVEN, 2, LRU_WIDTH), 0.02),
        "lru_wx": nrm((N_EVEN, 2, LRU_BLOCKS, LRU_BLOCK, LRU_BLOCK), LRU_BLOCK ** -0.5),
        "lru_bx": nrm((N_EVEN, 2, LRU_WIDTH), 0.02),
        "lru_lambda": lru_lambda,
        "win_sink": nrm((N_EVEN, WIN_HEADS), 0.5),
        "odd_w_in": nrm((N_ODD, D_MODEL, ODD_IN), D_MODEL ** -0.5),
        "odd_w_out": nrm((N_ODD, DIFF_V, D_MODEL), DIFF_V ** -0.5),
        "diff_lq1": nrm((N_ODD, DIFF_HEAD_DIM), 0.1),
        "diff_lk1": nrm((N_ODD, DIFF_HEAD_DIM), 0.1),
        "diff_lq2": nrm((N_ODD, DIFF_HEAD_DIM), 0.1),
        "diff_lk2": nrm((N_ODD, DIFF_HEAD_DIM), 0.1),
        "diff_subln_g": 1.0 + nrm((N_ODD, DIFF_HEADS, 2 * DIFF_HEAD_DIM), 0.05),
        "moe_router": nrm((DEPTH, D_MODEL, N_EXPERTS), D_MODEL ** -0.5),
        "moe_w_gate": nrm((DEPTH, N_EXPERTS, D_MODEL, EXPERT_FF), D_MODEL ** -0.5),
        "moe_w_up": nrm((DEPTH, N_EXPERTS, D_MODEL, EXPERT_FF), D_MODEL ** -0.5),
        "moe_w_down": nrm((DEPTH, N_EXPERTS, EXPERT_FF, D_MODEL), EXPERT_FF ** -0.5),
    }


def reference(x_prompt, x_sample, cache_win_k, cache_win_v, state_lru, cache_diff_k, cache_diff_v, c, c_ctx,
              ada_w, ada_b, norm_g, final_g, even_w_in, even_w_out, conv_w, conv_b, lru_wa, lru_ba, lru_wx,
              lru_bx, lru_lambda, win_sink, odd_w_in, odd_w_out, diff_lq1, diff_lk1, diff_lq2, diff_lk2,
              diff_subln_g, moe_router, moe_w_gate, moe_w_up, moe_w_down):
    xp = x_prompt
    Bp = xp.shape[0]
    wk_list, wv_list, lru_list, dk_list, dv_list = [], [], [], [], []
    for l in range(DEPTH):
        mod = _adaln(c_ctx[None], ada_w[l], ada_b[l])
        h = _modulate(xp, norm_g[l, 0], mod[:, 0], mod[:, 1])
        if l % 2 == 0:
            e = l // 2
            h0 = jnp.zeros((Bp, 2, LRU_WIDTH), jnp.float32)
            out, k, v, h_fin = _even_mixer(h, even_w_in[e], even_w_out[e], conv_w[e], conv_b[e], lru_wa[e],
                                           lru_ba[e], lru_wx[e], lru_bx[e], lru_lambda[e], win_sink[e], h0,
                                           None, None)
            wk_list.append(k)
            wv_list.append(v)
            lru_list.append(h_fin)
        else:
            o = l // 2
            lambda_init = 0.8 - 0.6 * math.exp(-0.3 * l)
            out, k, v = _odd_mixer(h, odd_w_in[o], odd_w_out[o], diff_lq1[o], diff_lk1[o], diff_lq2[o],
                                   diff_lk2[o], diff_subln_g[o], lambda_init, None, None)
            dk_list.append(k)
            dv_list.append(v)
        xp = xp + mod[:, 2][:, None] * out
        h = _modulate(xp, norm_g[l, 1], mod[:, 3], mod[:, 4])
        xp = xp + mod[:, 5][:, None] * _ec_moe(h, moe_router[l], moe_w_gate[l], moe_w_up[l], moe_w_down[l])
    y_prompt = _rmsnorm(xp, final_g)
    new_win_k = jnp.stack(wk_list, axis=1)
    new_win_v = jnp.stack(wv_list, axis=1)
    new_lru = jnp.stack(lru_list, axis=1)
    new_diff_k = jnp.stack(dk_list, axis=1)
    new_diff_v = jnp.stack(dv_list, axis=1)

    rows = x_sample.shape[1] // GRID_W
    rope_win = _axial_rope(rows, WIN_HEAD_DIM)
    rope_diff = _axial_rope(rows, DIFF_HEAD_DIM)
    xs = x_sample
    for l in range(DEPTH):
        mod = _adaln(c, ada_w[l], ada_b[l])
        h = _modulate(xs, norm_g[l, 0], mod[:, 0], mod[:, 1])
        if l % 2 == 0:
            e = l // 2
            out, _, _, _ = _even_mixer(h, even_w_in[e], even_w_out[e], conv_w[e], conv_b[e], lru_wa[e],
                                       lru_ba[e], lru_wx[e], lru_bx[e], lru_lambda[e], win_sink[e],
                                       state_lru[:, e], rope_win, (cache_win_k[:, e], cache_win_v[:, e]))
        else:
            o = l // 2
            lambda_init = 0.8 - 0.6 * math.exp(-0.3 * l)
            out, _, _ = _odd_mixer(h, odd_w_in[o], odd_w_out[o], diff_lq1[o], diff_lk1[o], diff_lq2[o],
                                   diff_lk2[o], diff_subln_g[o], lambda_init, rope_diff,
                                   (cache_diff_k[:, o], cache_diff_v[:, o]))
        xs = xs + mod[:, 2][:, None] * out
        h = _modulate(xs, norm_g[l, 1], mod[:, 3], mod[:, 4])
        xs = xs + mod[:, 5][:, None] * _ec_moe(h, moe_router[l], moe_w_gate[l], moe_w_up[l], moe_w_down[l])
    y_sample = _rmsnorm(xs, final_g)
    return (y_prompt, y_sample, new_win_k, new_win_v, new_lru, new_diff_k, new_diff_v)
```

```python
import functools
import math

import jax
import jax.numpy as jnp
from jax import lax
from jax.experimental import pallas as pl
from jax.experimental.pallas import tpu as pltpu

F32 = jnp.float32
BF16 = jnp.bfloat16
I32 = jnp.int32

D_MODEL = 1024
BATCH = 16
SEQ = 256
DEPTH = 2
DEC_BATCH = 2
DEC_SEQ = 1024
PAST_LEN = 512
GRID_W = 64
LRU_WIDTH = D_MODEL // 2
LRU_BLOCKS = 8
LRU_BLOCK = LRU_WIDTH // LRU_BLOCKS
CONV_W = 4
LRU_C = 8.0
WIN_HEADS = 8
WIN_KV_HEADS = 2
WIN_REP = WIN_HEADS // WIN_KV_HEADS
WIN_HEAD_DIM = 64
WINDOW = 128
WIN_Q = WIN_HEADS * WIN_HEAD_DIM
WIN_KV = WIN_KV_HEADS * WIN_HEAD_DIM
EVEN_IN = 2 * LRU_WIDTH + WIN_Q + 2 * WIN_KV
DIFF_HEADS = 8
DIFF_HEAD_DIM = 64
DIFF_QK = DIFF_HEADS * 2 * DIFF_HEAD_DIM
DIFF_V = DIFF_HEADS * 2 * DIFF_HEAD_DIM
ODD_IN = 2 * DIFF_QK + DIFF_V
N_EXPERTS = 16
EXPERT_FF = 2 * D_MODEL
CAPACITY_FACTOR = 2
ROPE_BASE = 10000.0
EPS = 1e-6
NEG_INF = -1e30

LANES = 128
SUBLANES = 8
VMEM_LIMIT_BYTES = 48 * 1024 * 1024

N_COND = 1 + DEC_BATCH
COND_ROWS = SUBLANES
TOKEN_TILE = 512
LOGIT_TILE = 256
FF_TILE = 512
LRU_CHUNK = 128
LAT_Q_TILE = 256


class Stream:
    def __init__(self, n_req, seq, cond0, cond_step):
        self.n_req, self.seq, self.cond0, self.cond_step = n_req, seq, cond0, cond_step
        self.tokens = n_req * seq
        self.cap = CAPACITY_FACTOR * seq // N_EXPERTS

    def cond_of_row(self, row):
        return self.cond0 + self.cond_step * (row // self.seq)


CTX = Stream(BATCH, SEQ, 0, 0)
LAT = Stream(DEC_BATCH, DEC_SEQ, 1, 1)


def _params(n_axes):
    return pltpu.CompilerParams(dimension_semantics=("arbitrary",) * n_axes,
                                vmem_limit_bytes=VMEM_LIMIT_BYTES)


def _mod_spec(layer, stream, k, rows_per_step):
    return pl.BlockSpec(
        (1, 1, D_MODEL),
        lambda i, *_: (layer * COND_ROWS + stream.cond_of_row(i * rows_per_step), 0, k))


def _dot(a, b):
    return jnp.dot(a, b, preferred_element_type=F32)


def _dot_nt(a, b):
    return lax.dot_general(a, b, (((1,), (1,)), ((), ())), preferred_element_type=F32)


def _modnorm(x, g, shift, scale):
    y = x * lax.rsqrt(jnp.mean(x * x, axis=-1, keepdims=True) + EPS)
    return (y * g) * (1.0 + scale) + shift


def _lane_half_masks(shape):
    lane = lax.broadcasted_iota(I32, shape, len(shape) - 1)
    left = (lane & (LANES - 1)) < LANES // 2
    return left, jnp.logical_not(left)


def _rope(x, cos, sin):
    parts = []
    for c in range(x.shape[1] // LANES):
        xs = x[:, c * LANES:(c + 1) * LANES]
        lane = lax.broadcasted_iota(I32, xs.shape, 1)
        first = (lane & 31) < 16
        rot = jnp.where(first, -pltpu.roll(xs, LANES - 16, axis=1), pltpu.roll(xs, 16, axis=1))
        parts.append(xs * cos[:, c * LANES:(c + 1) * LANES] + rot * sin[:, c * LANES:(c + 1) * LANES])
    return parts[0] if len(parts) == 1 else jnp.concatenate(parts, axis=1)


def _adaln_kernel(cv_ref, w_ref, b_ref, o_ref):
    cv = cv_ref[...]
    s = cv * jax.nn.sigmoid(cv)
    w = w_ref[0]
    ridx = lax.broadcasted_iota(I32, (COND_ROWS, w.shape[1]), 0)
    out = jnp.zeros((COND_ROWS, w.shape[1]), F32)
    for r in range(N_COND):
        out = jnp.where(ridx == r, jnp.sum(w * s[:, r:r + 1], axis=0, keepdims=True), out)
    o_ref[0] = out + b_ref[0]


def _adaln(cv_t, ada_w, ada_b):
    tn = 1024
    return pl.pallas_call(
        _adaln_kernel,
        grid=(DEPTH, 6 * D_MODEL // tn),
        in_specs=[pl.BlockSpec((D_MODEL, COND_ROWS), lambda l, j: (0, 0)),
                  pl.BlockSpec((1, D_MODEL, tn), lambda l, j: (l, 0, j)),
                  pl.BlockSpec((1, 1, tn), lambda l, j: (l, 0, j))],
        out_specs=pl.BlockSpec((1, COND_ROWS, tn), lambda l, j: (l, 0, j)),
        out_shape=jax.ShapeDtypeStruct((DEPTH, COND_ROWS, 6 * D_MODEL), F32),
        compiler_params=_params(2),
        name="adaln",
    )(cv_t, ada_w, ada_b.reshape(DEPTH, 1, 6 * D_MODEL))


def _proj_kernel(x_ref, g_ref, sh_ref, sc_ref, w_ref, o_ref):
    h = _modnorm(x_ref[...], g_ref[...], sh_ref[0], sc_ref[0])
    o_ref[...] = _dot(h.astype(BF16), w_ref[...])


def _proj(x, g_row, mods, layer, stream, w):
    n_out = w.shape[1]
    tm = TOKEN_TILE
    return pl.pallas_call(
        _proj_kernel,
        grid=(stream.tokens // tm,),
        in_specs=[pl.BlockSpec((tm, D_MODEL), lambda i: (i, 0)),
                  pl.BlockSpec((1, D_MODEL), lambda i: (0, 0)),
                  _mod_spec(layer, stream, 0, tm),
                  _mod_spec(layer, stream, 1, tm),
                  pl.BlockSpec((D_MODEL, n_out), lambda i: (0, 0))],
        out_specs=pl.BlockSpec((tm, n_out), lambda i: (i, 0)),
        out_shape=jax.ShapeDtypeStruct((stream.tokens, n_out), F32),
        compiler_params=_params(1),
        name=f"proj_l{layer}_s{stream.seq}",
    )(x, g_row, mods, mods, w)


def _lru_kernel(seq, xl_ref, gl_ref, cw_ref, cb_ref, wa_ref, wx_ref, ba_ref, bx_ref, lam_ref, h0_ref,
                y_ref, hfin_ref, xpad, af, bf, ab, bb):
    width = LRU_WIDTH
    ch = LRU_CHUNK
    halo = SUBLANES
    xpad[0:halo, :] = jnp.zeros((halo, width), F32)
    xpad[halo + seq:2 * halo + seq, :] = jnp.zeros((halo, width), F32)
    xpad[halo:halo + seq, :] = xl_ref[...]

    lam = lam_ref[...]
    z = -lam
    softplus = jnp.maximum(z, 0.0) + jnp.log1p(jnp.exp(-jnp.abs(z)))
    sub = lax.broadcasted_iota(I32, (ch, width), 0) & (SUBLANES - 1)
    cw = cw_ref[...]
    cb = cb_ref[...]

    def gates_chunk(c, carry):
        r0 = pl.multiple_of(c * ch, ch)
        win = xpad[pl.ds(r0, ch + 2 * halo), :]
        n_win = ch + 2 * halo

        def tap(j):
            return pltpu.roll(win, n_win - (halo - 2 + j), axis=0)[0:ch]

        xc = tap(0) * cw[0:1]
        for j in range(1, CONV_W):
            xc = xc + tap(j) * cw[j:j + 1]
        xc = xc + cb
        xcb = xc.astype(BF16)
        for d, (a_s, b_s) in enumerate(((af, bf), (ab, bb))):
            r = jax.nn.sigmoid(_dot(xcb, wa_ref[d]) + ba_ref[d:d + 1])
            ig = jax.nn.sigmoid(_dot(xcb, wx_ref[d]) + bx_ref[d:d + 1])
            log_a = (-LRU_C * r) * softplus[d:d + 1]
            a = jnp.exp(log_a)
            b = jnp.sqrt(jnp.tanh(-log_a) * (a * a + 1.0)) * (ig * xc)
            for s in (1, 2, 4):
                if d == 0:
                    keep = sub >= s
                    a_sh = jnp.where(keep, pltpu.roll(a, s, axis=0), 1.0)
                    b_sh = jnp.where(keep, pltpu.roll(b, s, axis=0), 0.0)
                else:
                    keep = sub < SUBLANES - s
                    a_sh = jnp.where(keep, pltpu.roll(a, ch - s, axis=0), 1.0)
                    b_sh = jnp.where(keep, pltpu.roll(b, ch - s, axis=0), 0.0)
                b = a * b_sh + b
                a = a * a_sh
            a_s[pl.ds(r0, ch), :] = a
            b_s[pl.ds(r0, ch), :] = b
        return carry

    lax.fori_loop(0, seq // ch, gates_chunk, 0)

    n_tiles = seq // SUBLANES
    h0 = h0_ref[0]

    def tile_step(k, carry):
        cf, cbw = carry
        rf = pl.multiple_of(k * SUBLANES, SUBLANES)
        rb = pl.multiple_of((n_tiles - 1 - k) * SUBLANES, SUBLANES)
        hf = af[pl.ds(rf, SUBLANES), :] * cf + bf[pl.ds(rf, SUBLANES), :]
        bf[pl.ds(rf, SUBLANES), :] = hf
        hb = ab[pl.ds(rb, SUBLANES), :] * cbw + bb[pl.ds(rb, SUBLANES), :]
        bb[pl.ds(rb, SUBLANES), :] = hb
        return hf[SUBLANES - 1:SUBLANES], hb[0:1]

    cf, cbw = lax.fori_loop(0, n_tiles, tile_step, (h0[0:1], h0[1:2]))
    hfin_ref[0, 0:1, :] = cf
    hfin_ref[0, 1:2, :] = cbw

    def out_chunk(c, carry):
        r0 = pl.multiple_of(c * ch, ch)
        hsum = bf[pl.ds(r0, ch), :] + bb[pl.ds(r0, ch), :]
        y_ref[pl.ds(r0, ch), :] = (hsum * jax.nn.gelu(gl_ref[pl.ds(r0, ch), :])).astype(BF16)
        return carry

    lax.fori_loop(0, seq // ch, out_chunk, 0)


def _lru(proj, stream, conv_w, conv_b, wa, wx, ba, bx, lam, h0):
    seq, width = stream.seq, LRU_WIDTH
    full2 = lambda b: (0, 0)
    full3 = lambda b: (0, 0, 0)
    return pl.pallas_call(
        functools.partial(_lru_kernel, seq),
        grid=(stream.n_req,),
        in_specs=[pl.BlockSpec((seq, width), lambda b: (b, 0)),
                  pl.BlockSpec((seq, width), lambda b: (b, 1)),
                  pl.BlockSpec((CONV_W, width), full2),
                  pl.BlockSpec((1, width), full2),
                  pl.BlockSpec((2, width, width), full3),
                  pl.BlockSpec((2, width, width), full3),
                  pl.BlockSpec((2, width), full2),
                  pl.BlockSpec((2, width), full2),
                  pl.BlockSpec((2, width), full2),
                  pl.BlockSpec((1, 2, width), lambda b: (b, 0, 0))],
        out_specs=[pl.BlockSpec((seq, width), lambda b: (b, 0)),
                   pl.BlockSpec((1, 2, width), lambda b: (b, 0, 0))],
        out_shape=[jax.ShapeDtypeStruct((stream.tokens, width), BF16),
                   jax.ShapeDtypeStruct((stream.n_req, 2, width), F32)],
        scratch_shapes=[pltpu.VMEM((seq + 2 * SUBLANES, width), F32)] + [pltpu.VMEM((seq, width), F32)] * 4,
        compiler_params=_params(1),
        name=f"lru_s{seq}",
    )(proj, proj, conv_w, conv_b, wa, wx, ba, bx, lam, h0)


def _split_groups(kk):
    left, right = _lane_half_masks(kk.shape)
    g0_l = jnp.where(left, kk, 0.0)
    g1_r = jnp.where(right, kk, 0.0)
    return ((g0_l, pltpu.roll(g0_l, LANES // 2, axis=1)), (pltpu.roll(g1_r, LANES // 2, axis=1), g1_r))


def _win_ctx_kernel(sink_ref, q_ref, kv_ref, o_ref):
    scale = WIN_HEAD_DIM ** -0.5
    ks = _split_groups(kv_ref[:, 0:LANES])
    vs = _split_groups(kv_ref[:, LANES:2 * LANES])
    for pair in range(WIN_HEADS // 2):
        g = pair // (WIN_REP // 2)
        qp = q_ref[:, pair * LANES:(pair + 1) * LANES].astype(BF16)
        acc = None
        for side in range(2):
            sk = sink_ref[2 * pair + side]
            s = _dot_nt(qp, ks[g][side].astype(BF16)) * scale
            m = jnp.maximum(jnp.max(s, axis=-1, keepdims=True), sk)
            e = jnp.exp(s - m)
            den = jnp.sum(e, axis=-1, keepdims=True) + jnp.exp(sk - m)
            o = _dot(e.astype(BF16), vs[g][side].astype(BF16)) / den
            acc = o if acc is None else acc + o
        o_ref[:, pair * LANES:(pair + 1) * LANES] = acc.astype(BF16)


def _win_lat_kernel(sink_ref, q_ref, kv_ref, ck_ref, cv_ref, cos_ref, sin_ref, o_ref,
                    kl_s, vl_s, kc_s, vc_s):
    seq, wn = DEC_SEQ, WINDOW
    scale = WIN_HEAD_DIM ** -0.5
    kr = _rope(kv_ref[:, 0:LANES], cos_ref[:, 0:LANES], sin_ref[:, 0:LANES])
    for src, dst in ((_split_groups(kr), kl_s), (_split_groups(kv_ref[:, LANES:2 * LANES]), vl_s),
                     (_split_groups(ck_ref[0]), kc_s), (_split_groups(cv_ref[0]), vc_s)):
        for g in range(WIN_KV_HEADS):
            for side in range(2):
                dst[2 * g + side] = src[g][side].astype(BF16)

    def q_block(i, carry):
        r0 = pl.multiple_of(i * wn, wn)
        start = pl.multiple_of(jnp.clip((i - 1) * wn, 0, seq - 3 * wn), wn)
        qr = _rope(q_ref[pl.ds(r0, wn), :], cos_ref[pl.ds(r0, wn), :], sin_ref[pl.ds(r0, wn), :])
        qpos = r0 + lax.broadcasted_iota(I32, (wn, 3 * wn), 0)
        kpos = start + lax.broadcasted_iota(I32, (wn, 3 * wn), 1)
        valid = jnp.abs(qpos - kpos) <= wn
        for pair in range(WIN_HEADS // 2):
            g = pair // (WIN_REP // 2)
            qp = qr[:, pair * LANES:(pair + 1) * LANES].astype(BF16)
            acc = None
            for side in range(2):
                idx = 2 * g + side
                sk = sink_ref[2 * pair + side]
                sl = _dot_nt(qp, kl_s[idx, pl.ds(start, 3 * wn), :]) * scale
                sl = jnp.where(valid, sl, NEG_INF)
                sc = _dot_nt(qp, kc_s[idx]) * scale
                m = jnp.maximum(jnp.maximum(jnp.max(sl, axis=-1, keepdims=True),
                                            jnp.max(sc, axis=-1, keepdims=True)), sk)
                el = jnp.exp(sl - m)
                ec = jnp.exp(sc - m)
                den = (jnp.sum(el, axis=-1, keepdims=True) + jnp.sum(ec, axis=-1, keepdims=True)
                       + jnp.exp(sk - m))
                o = (_dot(el.astype(BF16), vl_s[idx, pl.ds(start, 3 * wn), :])
                     + _dot(ec.astype(BF16), vc_s[idx])) / den
                acc = o if acc is None else acc + o
            o_ref[pl.ds(r0, wn), pair * LANES:(pair + 1) * LANES] = acc.astype(BF16)
        return carry

    lax.fori_loop(0, seq // wn, q_block, 0)


def _win_attn(proj, stream, sink, cache=None, rope=None):
    seq = stream.seq
    q_spec = pl.BlockSpec((seq, WIN_Q), lambda b: (b, 2 * LRU_WIDTH // WIN_Q))
    kv_spec = pl.BlockSpec((seq, 2 * WIN_KV), lambda b: (b, (2 * LRU_WIDTH + WIN_Q) // (2 * WIN_KV)))
    sink_spec = pl.BlockSpec(memory_space=pltpu.SMEM)
    out_spec = pl.BlockSpec((seq, WIN_Q), lambda b: (b, 0))
    out_shape = jax.ShapeDtypeStruct((stream.tokens, WIN_Q), BF16)
    if cache is None:
        return pl.pallas_call(
            _win_ctx_kernel, grid=(stream.n_req,),
            in_specs=[sink_spec, q_spec, kv_spec], out_specs=out_spec, out_shape=out_shape,
            compiler_params=_params(1), name="win_attn_ctx",
        )(sink, proj, proj)
    ck, cv = cache
    cos, sin = rope
    cache_spec = pl.BlockSpec((1, PAST_LEN, LANES), lambda b: (b, 0, 0))
    table_spec = pl.BlockSpec((seq, WIN_Q), lambda b: (0, 0))
    return pl.pallas_call(
        _win_lat_kernel, grid=(stream.n_req,),
        in_specs=[sink_spec, q_spec, kv_spec, cache_spec, cache_spec, table_spec, table_spec],
        out_specs=out_spec, out_shape=out_shape,
        scratch_shapes=[pltpu.VMEM((4, seq, LANES), BF16), pltpu.VMEM((4, seq, LANES), BF16),
                        pltpu.VMEM((4, PAST_LEN, LANES), BF16), pltpu.VMEM((4, PAST_LEN, LANES), BF16)],
        compiler_params=_params(1), name="win_attn_lat",
    )(sink, proj, proj, ck, cv, cos, sin)


def _diff_lambda(lq1_ref, lk1_ref, lq2_ref, lk2_ref, lambda_init):
    t1 = jnp.sum(lq1_ref[...] * lk1_ref[...], axis=-1, keepdims=True)
    t2 = jnp.sum(lq2_ref[...] * lk2_ref[...], axis=-1, keepdims=True)
    return jnp.exp(t1) - jnp.exp(t2) + lambda_init


def _subln(o, g_row, lambda_init):
    o = o * lax.rsqrt(jnp.mean(o * o, axis=-1, keepdims=True) + EPS) * g_row
    return o * (1.0 - lambda_init)


def _diff_ctx_kernel(lambda_init, lq1_ref, lk1_ref, lq2_ref, lk2_ref, sg_ref, q_ref, k_ref, v_ref, o_ref):
    scale = DIFF_HEAD_DIM ** -0.5
    lam = _diff_lambda(lq1_ref, lk1_ref, lq2_ref, lk2_ref, lambda_init)
    for h in range(DIFF_HEADS):
        cols = slice(h * LANES, (h + 1) * LANES)
        q = q_ref[:, cols].astype(BF16)
        k = k_ref[:, cols]
        left, right = _lane_half_masks(k.shape)
        probs = []
        for mask in (left, right):
            s = _dot_nt(q, jnp.where(mask, k, 0.0).astype(BF16)) * scale
            e = jnp.exp(s - jnp.max(s, axis=-1, keepdims=True))
            probs.append(e / jnp.sum(e, axis=-1, keepdims=True))
        w = probs[0] - lam * probs[1]
        o = _dot(w.astype(BF16), v_ref[:, cols].astype(BF16))
        o_ref[:, cols] = _subln(o, sg_ref[h:h + 1, :], lambda_init).astype(BF16)


def _diff_lat_kernel(lambda_init, lq1_ref, lk1_ref, lq2_ref, lk2_ref, sg_ref, q_ref, k_ref, v_ref,
                     ck_ref, cv_ref, cos_ref, sin_ref, o_ref, kl_s, kc_s, vl_s, vc_s):
    seq, tq = DEC_SEQ, LAT_Q_TILE
    scale = DIFF_HEAD_DIM ** -0.5
    lam = _diff_lambda(lq1_ref, lk1_ref, lq2_ref, lk2_ref, lambda_init)
    kr = _rope(k_ref[...], cos_ref[...], sin_ref[...])
    kc = ck_ref[0, 0, 0]
    for src, dst in ((kr, kl_s), (kc, kc_s)):
        left, right = _lane_half_masks(src.shape)
        dst[0] = jnp.where(left, src, 0.0).astype(BF16)
        dst[1] = jnp.where(right, src, 0.0).astype(BF16)
    vl_s[...] = v_ref[...].astype(BF16)
    vc_s[...] = cv_ref[0, 0, 0].astype(BF16)
    sg = sg_ref[0]

    def q_block(i, carry):
        r0 = pl.multiple_of(i * tq, tq)
        qr = _rope(q_ref[pl.ds(r0, tq), :], cos_ref[pl.ds(r0, tq), :], sin_ref[pl.ds(r0, tq), :]).astype(BF16)
        pls, pcs = [], []
        for comp in range(2):
            sl = _dot_nt(qr, kl_s[comp]) * scale
            sc = _dot_nt(qr, kc_s[comp]) * scale
            m = jnp.maximum(jnp.max(sl, axis=-1, keepdims=True), jnp.max(sc, axis=-1, keepdims=True))
            el = jnp.exp(sl - m)
            ec = jnp.exp(sc - m)
            den = jnp.sum(el, axis=-1, keepdims=True) + jnp.sum(ec, axis=-1, keepdims=True)
            pls.append(el / den)
            pcs.append(ec / den)
        wl = pls[0] - lam * pls[1]
        wc = pcs[0] - lam * pcs[1]
        o = _dot(wl.astype(BF16), vl_s[...]) + _dot(wc.astype(BF16), vc_s[...])
        o_ref[pl.ds(r0, tq), :] = _subln(o, sg, lambda_init).astype(BF16)
        return carry

    lax.fori_loop(0, seq // tq, q_block, 0)


def _diff_attn(proj, stream, lambda_init, lq1, lk1, lq2, lk2, subln_g, cache=None, rope=None):
    seq = stream.seq
    vec_spec = lambda nd: pl.BlockSpec((1, DIFF_HEAD_DIM), lambda *_: (0, 0))
    out_shape = jax.ShapeDtypeStruct((stream.tokens, DIFF_V), BF16)
    if cache is None:
        blk = lambda c: pl.BlockSpec((seq, DIFF_QK), lambda b: (b, c))
        return pl.pallas_call(
            functools.partial(_diff_ctx_kernel, lambda_init), grid=(stream.n_req,),
            in_specs=[vec_spec(1)] * 4 + [pl.BlockSpec((DIFF_HEADS, LANES), lambda b: (0, 0)),
                                          blk(0), blk(1), blk(2)],
            out_specs=pl.BlockSpec((seq, DIFF_V), lambda b: (b, 0)), out_shape=out_shape,
            compiler_params=_params(1), name="diff_attn_ctx",
        )(lq1, lk1, lq2, lk2, subln_g, proj, proj, proj)
    ck, cv = cache
    cos, sin = rope
    blk = lambda c: pl.BlockSpec((seq, LANES), lambda b, h: (b, c * DIFF_HEADS + h))
    cache_spec = pl.BlockSpec((1, 1, 1, PAST_LEN, LANES), lambda b, h: (b, 0, h, 0, 0))
    table_spec = pl.BlockSpec((seq, LANES), lambda b, h: (0, 0))
    return pl.pallas_call(
        functools.partial(_diff_lat_kernel, lambda_init), grid=(stream.n_req, DIFF_HEADS),
        in_specs=[vec_spec(2)] * 4 + [pl.BlockSpec((1, 1, LANES), lambda b, h: (h, 0, 0)),
                                      blk(0), blk(1), blk(2), cache_spec, cache_spec, table_spec, table_spec],
        out_specs=pl.BlockSpec((seq, LANES), lambda b, h: (b, h)), out_shape=out_shape,
        scratch_shapes=[pltpu.VMEM((2, seq, LANES), BF16), pltpu.VMEM((2, PAST_LEN, LANES), BF16),
                        pltpu.VMEM((seq, LANES), BF16), pltpu.VMEM((PAST_LEN, LANES), BF16)],
        compiler_params=_params(2), name="diff_attn_lat",
    )(lq1, lk1, lq2, lk2, subln_g.reshape(DIFF_HEADS, 1, LANES), proj, proj, proj, ck, cv, cos, sin)


def _mix_out_kernel(n_in, *refs):
    a_refs = refs[:n_in]
    w_ref, x_ref, gate_ref, g2_ref, sh2_ref, sc2_ref, wr_ref, xn_ref, h2_ref, lg_ref = refs[n_in:]
    kp = D_MODEL // n_in
    acc = None
    for k, a_ref in enumerate(a_refs):
        part = _dot(a_ref[...], w_ref[k * kp:(k + 1) * kp, :])
        acc = part if acc is None else acc + part
    xn = x_ref[...] + gate_ref[0] * acc
    xn_ref[...] = xn
    h2 = _modnorm(xn, g2_ref[...], sh2_ref[0], sc2_ref[0])
    h2_ref[...] = h2.astype(BF16)
    lg = lax.dot_general(wr_ref[...], h2, (((1,), (1,)), ((), ())),
                         precision=lax.Precision.HIGHEST, preferred_element_type=F32)
    for c in range(lg.shape[1] // LOGIT_TILE):
        lg_ref[c] = lg[:, c * LOGIT_TILE:(c + 1) * LOGIT_TILE]


def _mix_out(mixed, w_out, x, g2_row, mods, layer, stream, wr_t):
    tm = TOKEN_TILE
    n_in = len(mixed)
    kp = D_MODEL // n_in
    row_blk = lambda width: pl.BlockSpec((tm, width), lambda i: (i, 0))
    return pl.pallas_call(
        functools.partial(_mix_out_kernel, n_in),
        grid=(stream.tokens // tm,),
        in_specs=[row_blk(kp)] * n_in + [
            pl.BlockSpec((D_MODEL, D_MODEL), lambda i: (0, 0)),
            row_blk(D_MODEL),
            _mod_spec(layer, stream, 2, tm),
            pl.BlockSpec((1, D_MODEL), lambda i: (0, 0)),
            _mod_spec(layer, stream, 3, tm),
            _mod_spec(layer, stream, 4, tm),
            pl.BlockSpec((N_EXPERTS, D_MODEL), lambda i: (0, 0))],
        out_specs=[row_blk(D_MODEL), row_blk(D_MODEL),
                   pl.BlockSpec((tm // LOGIT_TILE, N_EXPERTS, LOGIT_TILE), lambda i: (i, 0, 0))],
        out_shape=[jax.ShapeDtypeStruct((stream.tokens, D_MODEL), F32),
                   jax.ShapeDtypeStruct((stream.tokens, D_MODEL), BF16),
                   jax.ShapeDtypeStruct((stream.tokens // LOGIT_TILE, N_EXPERTS, LOGIT_TILE), F32)],
        compiler_params=_params(1),
        name=f"mix_out_l{layer}_s{stream.seq}",
    )(*mixed, w_out, x, mods, g2_row, mods, mods, wr_t)


def _sort_desc_lanes(x):
    rows, n = x.shape
    tiles = [x[:, c * LANES:(c + 1) * LANES] for c in range(n // LANES)]
    lane = lax.broadcasted_iota(I32, (rows, LANES), 1)
    k = 2
    while k <= n:
        j = k // 2
        while j >= 1:
            if j < LANES:
                lower = (lane & j) == 0
                for c in range(len(tiles)):
                    t = tiles[c]
                    partner = jnp.where(lower, pltpu.roll(t, LANES - j, axis=1), pltpu.roll(t, j, axis=1))
                    desc = ((lane & k) == 0) if k < LANES else (((c * LANES) & k) == 0)
                    take_max = (lower == desc) if k < LANES else (lower if desc else jnp.logical_not(lower))
                    tiles[c] = jnp.where(take_max, jnp.maximum(t, partner), jnp.minimum(t, partner))
            else:
                jc = j // LANES
                new = list(tiles)
                for c in range(len(tiles)):
                    take_max = ((c & jc) == 0) == (((c * LANES) & k) == 0)
                    new[c] = (jnp.maximum if take_max else jnp.minimum)(tiles[c], tiles[c ^ jc])
                tiles = new
            j //= 2
        k *= 2
    return tiles


def _router_kernel(cap, lg_ref, pos_ref, g_ref):
    x = lg_ref[...]
    n_b, n_e, n_tok = x.shape
    e = jnp.exp(x - jnp.max(x, axis=1, keepdims=True))
    aff = (e / jnp.sum(e, axis=1, keepdims=True)).reshape(n_b * n_e, n_tok)
    srt = _sort_desc_lanes(aff)
    lane_k = (cap - 1) % LANES
    thr = srt[(cap - 1) // LANES][:, lane_k:lane_k + 1]
    gt = aff > thr
    eq = aff == thr
    n_gt = jnp.sum(gt.astype(F32), axis=1, keepdims=True)
    before = (lax.broadcasted_iota(I32, (n_tok, n_tok), 0)
              < lax.broadcasted_iota(I32, (n_tok, n_tok), 1)).astype(BF16)
    eq_rank = _dot(eq.astype(BF16), before)
    sel = gt | (eq & (eq_rank < cap - n_gt))
    slot = _dot(sel.astype(BF16), before).astype(I32)
    pos_ref[...] = jnp.where(sel, slot, -1).reshape(n_b, n_e, n_tok)
    g_ref[...] = jnp.where(sel, aff, 0.0).reshape(n_b, n_e, n_tok)


def _router(logits, stream):
    shape = (stream.n_req, N_EXPERTS, stream.seq)
    spec = pl.BlockSpec(shape, lambda: (0, 0, 0))
    return pl.pallas_call(
        functools.partial(_router_kernel, stream.cap),
        in_specs=[spec], out_specs=[spec, spec],
        out_shape=[jax.ShapeDtypeStruct(shape, I32), jax.ShapeDtypeStruct(shape, F32)],
        compiler_params=pltpu.CompilerParams(vmem_limit_bytes=VMEM_LIMIT_BYTES),
        name=f"router_s{stream.seq}",
    )(logits)


def _gather_kernel(cap, h_ref, pos_ref, xs_ref):
    n_tok = h_ref.shape[0]
    slot = lax.broadcasted_iota(I32, (cap, n_tok), 0)
    for e in range(N_EXPERTS):
        onehot = (slot == pos_ref[0, e:e + 1, :]).astype(BF16)
        xs_ref[e] = _dot(onehot, h_ref[...]).astype(BF16)


def _gather(h2, pos, stream):
    cap = stream.cap
    return pl.pallas_call(
        functools.partial(_gather_kernel, cap),
        grid=(stream.n_req,),
        in_specs=[pl.BlockSpec((stream.seq, D_MODEL), lambda b: (b, 0)),
                  pl.BlockSpec((1, N_EXPERTS, stream.seq), lambda b: (b, 0, 0))],
        out_specs=pl.BlockSpec((N_EXPERTS, cap, D_MODEL), lambda b: (0, b, 0)),
        out_shape=jax.ShapeDtypeStruct((N_EXPERTS, stream.n_req * cap, D_MODEL), BF16),
        compiler_params=_params(1),
        name=f"gather_s{stream.seq}",
    )(h2, pos)


def _ffn_kernel(xa_ref, xb_ref, wg_ref, wu_ref, wd_ref, ya_ref, yb_ref, acc_a, acc_b):
    j = pl.program_id(1)
    wg = wg_ref[0, 0].astype(BF16)
    wu = wu_ref[0, 0].astype(BF16)
    wd = wd_ref[0, 0].astype(BF16)
    for x_ref, y_ref, acc in ((xa_ref, ya_ref, acc_a), (xb_ref, yb_ref, acc_b)):
        x = x_ref[0]
        a = _dot(x, wg)
        u = _dot(x, wu)
        part = _dot(((a * jax.nn.sigmoid(a)) * u).astype(BF16), wd)

        @pl.when(j == 0)
        def _():
            acc[...] = part

        @pl.when(j > 0)
        def _():
            acc[...] += part

        @pl.when(j == pl.num_programs(1) - 1)
        def _():
            y_ref[0] = acc[...].astype(BF16)


def _ffn(xs_a, xs_b, layer, w_gate, w_up, w_down):
    tf = FF_TILE
    rows_a, rows_b = xs_a.shape[1], xs_b.shape[1]
    x_spec = lambda rows: pl.BlockSpec((1, rows, D_MODEL), lambda e, j: (e, 0, 0))
    return pl.pallas_call(
        _ffn_kernel,
        grid=(N_EXPERTS, EXPERT_FF // tf),
        in_specs=[x_spec(rows_a), x_spec(rows_b),
                  pl.BlockSpec((1, 1, D_MODEL, tf), lambda e, j: (layer, e, 0, j)),
                  pl.BlockSpec((1, 1, D_MODEL, tf), lambda e, j: (layer, e, 0, j)),
                  pl.BlockSpec((1, 1, tf, D_MODEL), lambda e, j: (layer, e, j, 0))],
        out_specs=[x_spec(rows_a), x_spec(rows_b)],
        out_shape=[jax.ShapeDtypeStruct(xs_a.shape, BF16), jax.ShapeDtypeStruct(xs_b.shape, BF16)],
        scratch_shapes=[pltpu.VMEM((rows_a, D_MODEL), F32), pltpu.VMEM((rows_b, D_MODEL), F32)],
        compiler_params=_params(2),
        name=f"ffn_l{layer}",
    )(xs_a, xs_b, w_gate, w_up, w_down)


def _scatter_kernel(cap, final, y_ref, pos_ref, g_ref, x_ref, gate_ref, fg_ref, o_ref, acc_ref):
    n_tok = x_ref.shape[0]
    slot = lax.broadcasted_iota(I32, (cap, n_tok), 0)
    acc_ref[...] = jnp.zeros_like(acc_ref)
    for e in range(N_EXPERTS):
        hit = slot == pos_ref[0, e:e + 1, :]
        gate = jnp.sum(jnp.where(hit, g_ref[0, e:e + 1, :], 0.0), axis=1, keepdims=True)
        yy = (y_ref[e].astype(F32) * gate).astype(BF16)
        acc_ref[...] += lax.dot_general(hit.astype(BF16), yy, (((0,), (0,)), ((), ())),
                                        preferred_element_type=F32)
    xn = x_ref[...] + gate_ref[0] * acc_ref[...]
    if final:
        xn = xn * lax.rsqrt(jnp.mean(xn * xn, axis=-1, keepdims=True) + EPS) * fg_ref[...]
    o_ref[...] = xn


def _scatter(y, pos, g, xn, mods, layer, stream, final, final_g_row):
    cap, seq = stream.cap, stream.seq
    tok_blk = pl.BlockSpec((seq, D_MODEL), lambda b: (b, 0))
    sel_blk = pl.BlockSpec((1, N_EXPERTS, seq), lambda b: (b, 0, 0))
    return pl.pallas_call(
        functools.partial(_scatter_kernel, cap, final),
        grid=(stream.n_req,),
        in_specs=[pl.BlockSpec((N_EXPERTS, cap, D_MODEL), lambda b: (0, b, 0)),
                  sel_blk, sel_blk, tok_blk,
                  _mod_spec(layer, stream, 5, seq),
                  pl.BlockSpec((1, D_MODEL), lambda b: (0, 0))],
        out_specs=tok_blk,
        out_shape=jax.ShapeDtypeStruct((stream.tokens, D_MODEL), F32),
        scratch_shapes=[pltpu.VMEM((seq, D_MODEL), F32)],
        compiler_params=_params(1),
        name=f"scatter_l{layer}_s{seq}",
    )(y, pos, g, xn, mods, final_g_row)


def _axial_rope_tables(rows, head_dim):
    row = jnp.repeat(jnp.arange(rows, dtype=F32), GRID_W)
    col = jnp.tile(jnp.arange(GRID_W, dtype=F32), rows)
    nf = head_dim // 4
    inv = ROPE_BASE ** (-jnp.arange(nf, dtype=F32) / nf)
    ar = row[:, None] * inv[None]
    ac = col[:, None] * inv[None]
    ang = jnp.concatenate([ar, ar, ac, ac], axis=-1)
    return jnp.cos(ang), jnp.sin(ang)


def _block_diag(w):
    eye = jnp.eye(LRU_BLOCKS, dtype=w.dtype)
    return jnp.einsum('dnkj,nm->dnkmj', w, eye).reshape(2, LRU_WIDTH, LRU_WIDTH)


def kernel(x_prompt, x_sample, cache_win_k, cache_win_v, state_lru, cache_diff_k, cache_diff_v, c, c_ctx, ada_w, ada_b, norm_g, final_g, even_w_in, even_w_out, conv_w, conv_b, lru_wa, lru_ba, lru_wx, lru_bx, lru_lambda, win_sink, odd_w_in, odd_w_out, diff_lq1, diff_lk1, diff_lq2, diff_lk2, diff_subln_g, moe_router, moe_w_gate, moe_w_up, moe_w_down):
    cv_t = jnp.concatenate([c_ctx[None], c, jnp.zeros((COND_ROWS - N_COND, D_MODEL), F32)], axis=0).T
    mods = _adaln(cv_t, ada_w, ada_b).reshape(DEPTH * COND_ROWS, 1, 6 * D_MODEL)

    cos, sin = _axial_rope_tables(DEC_SEQ // GRID_W, WIN_HEAD_DIM)
    rope_win = (jnp.tile(cos, (1, WIN_HEADS)), jnp.tile(sin, (1, WIN_HEADS)))
    rope_diff = (jnp.tile(cos, (1, 2)), jnp.tile(sin, (1, 2)))

    streams = (CTX, LAT)
    xs = [x_prompt.reshape(CTX.tokens, D_MODEL), x_sample.reshape(LAT.tokens, D_MODEL)]
    final_g_row = final_g.reshape(1, D_MODEL)
    outs = {}

    for layer in range(DEPTH):
        idx = layer // 2
        even = layer % 2 == 0
        w_in = (even_w_in if even else odd_w_in)[idx].astype(BF16)
        w_out = (even_w_out if even else odd_w_out)[idx].astype(BF16)
        wr_t = moe_router[layer].T
        g1_row = norm_g[layer, 0].reshape(1, D_MODEL)
        g2_row = norm_g[layer, 1].reshape(1, D_MODEL)
        if even:
            wa = _block_diag(lru_wa[idx]).astype(BF16)
            wx = _block_diag(lru_wx[idx]).astype(BF16)

        routed = []
        for si, stream in enumerate(streams):
            is_ctx = stream is CTX
            proj = _proj(xs[si], g1_row, mods, layer, stream, w_in)
            if even:
                h0 = jnp.zeros((stream.n_req, 2, LRU_WIDTH), F32) if is_ctx else state_lru[:, idx]
                y_lru, h_fin = _lru(proj, stream, conv_w[idx], conv_b[idx].reshape(1, LRU_WIDTH), wa, wx,
                                    lru_ba[idx], lru_bx[idx], lru_lambda[idx], h0)
                if is_ctx:
                    o = _win_attn(proj, stream, win_sink[idx])
                    k0 = 2 * LRU_WIDTH + WIN_Q
                    to_heads = lambda t: t.reshape(BATCH, SEQ, WIN_KV_HEADS, WIN_HEAD_DIM).transpose(0, 2, 1, 3)[:, None]
                    outs["win_k"] = to_heads(proj[:, k0:k0 + WIN_KV])
                    outs["win_v"] = to_heads(proj[:, k0 + WIN_KV:k0 + 2 * WIN_KV])
                    outs["lru"] = h_fin[:, None]
                else:
                    pack = lambda t: t[:, idx].transpose(0, 2, 1, 3).reshape(DEC_BATCH, PAST_LEN, WIN_KV)
                    o = _win_attn(proj, stream, win_sink[idx],
                                  cache=(pack(cache_win_k), pack(cache_win_v)), rope=rope_win)
                mixed = [y_lru, o]
            else:
                lambda_init = 0.8 - 0.6 * math.exp(-0.3 * layer)
                vec = lambda t: t[idx].reshape(1, DIFF_HEAD_DIM)
                args = (vec(diff_lq1), vec(diff_lk1), vec(diff_lq2), vec(diff_lk2), diff_subln_g[idx])
                if is_ctx:
                    o = _diff_attn(proj, stream, lambda_init, *args)
                    to_heads = lambda t: t.reshape(BATCH, SEQ, DIFF_HEADS, 2 * DIFF_HEAD_DIM).transpose(0, 2, 1, 3)[:, None]
                    outs["diff_k"] = to_heads(proj[:, DIFF_QK:2 * DIFF_QK])
                    outs["diff_v"] = to_heads(proj[:, 2 * DIFF_QK:])
                else:
                    o = _diff_attn(proj, stream, lambda_init, *args,
                                   cache=(cache_diff_k[:, idx:idx + 1], cache_diff_v[:, idx:idx + 1]), rope=rope_diff)
                mixed = [o]
            xn, h2, lg = _mix_out(mixed, w_out, xs[si], g2_row, mods, layer, stream, wr_t)
            per_req = stream.seq // LOGIT_TILE
            lg = lg.reshape(stream.n_req, per_req, N_EXPERTS, LOGIT_TILE).transpose(0, 2, 1, 3)
            pos, gate = _router(lg.reshape(stream.n_req, N_EXPERTS, stream.seq), stream)
            routed.append((xn, pos, gate, _gather(h2, pos, stream)))

        ys = _ffn(routed[0][3], routed[1][3], layer, moe_w_gate, moe_w_up, moe_w_down)
        final = layer == DEPTH - 1
        for si, stream in enumerate(streams):
            xn, pos, gate, _ = routed[si]
            xs[si] = _scatter(ys[si], pos, gate, xn, mods, layer, stream, final, final_g_row)

    y_prompt = xs[0].reshape(BATCH, SEQ, D_MODEL)
    y_sample = xs[1].reshape(DEC_BATCH, DEC_SEQ, D_MODEL)
    return (y_prompt, y_sample, outs["win_k"], outs["win_v"], outs["lru"], outs["diff_k"], outs["diff_v"])
```

```python
import functools
import math

import jax
import jax.numpy as jnp
from jax import lax
from jax.experimental import pallas as pl
from jax.experimental.pallas import tpu as pltpu

F32 = jnp.float32
BF16 = jnp.bfloat16
I32 = jnp.int32

D_MODEL = 1024
BATCH = 16
SEQ = 256
DEPTH = 2
DEC_BATCH = 2
DEC_SEQ = 1024
PAST_LEN = 512
GRID_W = 64
LRU_WIDTH = D_MODEL // 2
LRU_BLOCKS = 8
LRU_BLOCK = LRU_WIDTH // LRU_BLOCKS
CONV_W = 4
LRU_C = 8.0
WIN_HEADS = 8
WIN_KV_HEADS = 2
WIN_REP = WIN_HEADS // WIN_KV_HEADS
WIN_HEAD_DIM = 64
WINDOW = 128
WIN_Q = WIN_HEADS * WIN_HEAD_DIM
WIN_KV = WIN_KV_HEADS * WIN_HEAD_DIM
EVEN_IN = 2 * LRU_WIDTH + WIN_Q + 2 * WIN_KV
DIFF_HEADS = 8
DIFF_HEAD_DIM = 64
DIFF_QK = DIFF_HEADS * 2 * DIFF_HEAD_DIM
DIFF_V = DIFF_HEADS * 2 * DIFF_HEAD_DIM
ODD_IN = 2 * DIFF_QK + DIFF_V
N_EXPERTS = 16
EXPERT_FF = 2 * D_MODEL
CAPACITY_FACTOR = 2
ROPE_BASE = 10000.0
EPS = 1e-6
NEG_INF = -1e30

LANES = 128
SUBLANES = 8
VMEM_LIMIT_BYTES = 48 * 1024 * 1024

N_COND = 1 + DEC_BATCH
COND_ROWS = SUBLANES
TOKEN_TILE = 512
LOGIT_TILE = 256
FF_TILE = 512
LRU_CHUNK = 128
LAT_Q_TILE = 256


class Stream:
    def __init__(self, n_req, seq, cond0, cond_step):
        self.n_req, self.seq, self.cond0, self.cond_step = n_req, seq, cond0, cond_step
        self.tokens = n_req * seq
        self.cap = CAPACITY_FACTOR * seq // N_EXPERTS

    def cond_of_row(self, row):
        return self.cond0 + self.cond_step * (row // self.seq)


CTX = Stream(BATCH, SEQ, 0, 0)
LAT = Stream(DEC_BATCH, DEC_SEQ, 1, 1)


def _params(n_axes):
    return pltpu.CompilerParams(dimension_semantics=("arbitrary",) * n_axes,
                                vmem_limit_bytes=VMEM_LIMIT_BYTES)


def _mod_spec(layer, stream, k, rows_per_step):
    return pl.BlockSpec(
        (1, 1, D_MODEL),
        lambda i, *_: (layer * COND_ROWS + stream.cond_of_row(i * rows_per_step), 0, k))


def _dot(a, b):
    return jnp.dot(a, b, preferred_element_type=F32)


def _dot_nt(a, b):
    return lax.dot_general(a, b, (((1,), (1,)), ((), ())), preferred_element_type=F32)


def _modnorm(x, g, shift, scale):
    y = x * lax.rsqrt(jnp.mean(x * x, axis=-1, keepdims=True) + EPS)
    return (y * g) * (1.0 + scale) + shift


def _lane_half_masks(shape):
    lane = lax.broadcasted_iota(I32, shape, len(shape) - 1)
    left = (lane & (LANES - 1)) < LANES // 2
    return left, jnp.logical_not(left)


def _rope(x, cos, sin):
    parts = []
    for c in range(x.shape[1] // LANES):
        xs = x[:, c * LANES:(c + 1) * LANES]
        lane = lax.broadcasted_iota(I32, xs.shape, 1)
        first = (lane & 31) < 16
        rot = jnp.where(first, -pltpu.roll(xs, LANES - 16, axis=1), pltpu.roll(xs, 16, axis=1))
        parts.append(xs * cos[:, c * LANES:(c + 1) * LANES] + rot * sin[:, c * LANES:(c + 1) * LANES])
    return parts[0] if len(parts) == 1 else jnp.concatenate(parts, axis=1)


def _adaln_kernel(cv_ref, w_ref, b_ref, o_ref):
    cv = cv_ref[...]
    s = cv * jax.nn.sigmoid(cv)
    w = w_ref[0]
    ridx = lax.broadcasted_iota(I32, (COND_ROWS, w.shape[1]), 0)
    out = jnp.zeros((COND_ROWS, w.shape[1]), F32)
    for r in range(N_COND):
        out = jnp.where(ridx == r, jnp.sum(w * s[:, r:r + 1], axis=0, keepdims=True), out)
    o_ref[0] = out + b_ref[0]


def _adaln(cv_t, ada_w, ada_b):
    tn = 1024
    return pl.pallas_call(
        _adaln_kernel,
        grid=(DEPTH, 6 * D_MODEL // tn),
        in_specs=[pl.BlockSpec((D_MODEL, COND_ROWS), lambda l, j: (0, 0)),
                  pl.BlockSpec((1, D_MODEL, tn), lambda l, j: (l, 0, j)),
                  pl.BlockSpec((1, 1, tn), lambda l, j: (l, 0, j))],
        out_specs=pl.BlockSpec((1, COND_ROWS, tn), lambda l, j: (l, 0, j)),
        out_shape=jax.ShapeDtypeStruct((DEPTH, COND_ROWS, 6 * D_MODEL), F32),
        compiler_params=_params(2),
        name="adaln",
    )(cv_t, ada_w, ada_b.reshape(DEPTH, 1, 6 * D_MODEL))


def _proj_kernel(x_ref, g_ref, sh_ref, sc_ref, w_ref, o_ref):
    h = _modnorm(x_ref[...], g_ref[...], sh_ref[0], sc_ref[0])
    o_ref[...] = _dot(h.astype(BF16), w_ref[...])


def _proj(x, g_row, mods, layer, stream, w):
    n_out = w.shape[1]
    tm = TOKEN_TILE
    return pl.pallas_call(
        _proj_kernel,
        grid=(stream.tokens // tm,),
        in_specs=[pl.BlockSpec((tm, D_MODEL), lambda i: (i, 0)),
                  pl.BlockSpec((1, D_MODEL), lambda i: (0, 0)),
                  _mod_spec(layer, stream, 0, tm),
                  _mod_spec(layer, stream, 1, tm),
                  pl.BlockSpec((D_MODEL, n_out), lambda i: (0, 0))],
        out_specs=pl.BlockSpec((tm, n_out), lambda i: (i, 0)),
        out_shape=jax.ShapeDtypeStruct((stream.tokens, n_out), F32),
        compiler_params=_params(1),
        name=f"proj_l{layer}_s{stream.seq}",
    )(x, g_row, mods, mods, w)


def _proj_heads_kernel(seq, x_ref, g_ref, sh_ref, sc_ref, w_ref, q_ref, k_ref, v_ref):
    h = _modnorm(x_ref[...], g_ref[...], sh_ref[0], sc_ref[0])
    res = _dot(h.astype(BF16), w_ref[...])
    q_ref[...] = res[:, 0:DIFF_QK]
    for r in range(x_ref.shape[0] // seq):
        rows = slice(r * seq, (r + 1) * seq)
        for hh in range(DIFF_HEADS):
            k_ref[r, hh] = res[rows, DIFF_QK + hh * LANES:DIFF_QK + (hh + 1) * LANES]
            v_ref[r, hh] = res[rows, 2 * DIFF_QK + hh * LANES:2 * DIFF_QK + (hh + 1) * LANES]


def _proj_heads(x, g_row, mods, layer, stream, w):
    tm, seq = TOKEN_TILE, stream.seq
    head_shape = (stream.n_req, DIFF_HEADS, seq, 2 * DIFF_HEAD_DIM)
    head_spec = pl.BlockSpec((tm // seq, DIFF_HEADS, seq, 2 * DIFF_HEAD_DIM), lambda i: (i, 0, 0, 0))
    return pl.pallas_call(
        functools.partial(_proj_heads_kernel, seq),
        grid=(stream.tokens // tm,),
        in_specs=[pl.BlockSpec((tm, D_MODEL), lambda i: (i, 0)),
                  pl.BlockSpec((1, D_MODEL), lambda i: (0, 0)),
                  _mod_spec(layer, stream, 0, tm),
                  _mod_spec(layer, stream, 1, tm),
                  pl.BlockSpec((D_MODEL, ODD_IN), lambda i: (0, 0))],
        out_specs=[pl.BlockSpec((tm, DIFF_QK), lambda i: (i, 0)), head_spec, head_spec],
        out_shape=[jax.ShapeDtypeStruct((stream.tokens, DIFF_QK), F32),
                   jax.ShapeDtypeStruct(head_shape, F32), jax.ShapeDtypeStruct(head_shape, F32)],
        compiler_params=_params(1),
        name=f"proj_heads_l{layer}_s{seq}",
    )(x, g_row, mods, mods, w)


def _lru_kernel(seq, xl_ref, gl_ref, cw_ref, cb_ref, wa_ref, wx_ref, ba_ref, bx_ref, lam_ref, h0_ref,
                y_ref, hfin_ref, xpad, af, bf, ab, bb):
    width = LRU_WIDTH
    ch = LRU_CHUNK
    halo = SUBLANES
    xpad[0:halo, :] = jnp.zeros((halo, width), F32)
    xpad[halo + seq:2 * halo + seq, :] = jnp.zeros((halo, width), F32)
    xpad[halo:halo + seq, :] = xl_ref[...]

    lam = lam_ref[...]
    z = -lam
    softplus = jnp.maximum(z, 0.0) + jnp.log1p(jnp.exp(-jnp.abs(z)))
    sub = lax.broadcasted_iota(I32, (ch // SUBLANES, SUBLANES, width), 1)
    cw = cw_ref[...]
    cb = cb_ref[...]

    def gates_chunk(c, carry):
        r0 = pl.multiple_of(c * ch, ch)
        win = xpad[pl.ds(r0, ch + 2 * halo), :]
        n_win = ch + 2 * halo

        def tap(j):
            return pltpu.roll(win, n_win - (halo - 2 + j), axis=0)[0:ch]

        xc = tap(0) * cw[0:1]
        for j in range(1, CONV_W):
            xc = xc + tap(j) * cw[j:j + 1]
        xc = xc + cb
        xcb = xc.astype(BF16)
        for d, (a_s, b_s) in enumerate(((af, bf), (ab, bb))):
            r = jax.nn.sigmoid(_dot(xcb, wa_ref[d]) + ba_ref[d:d + 1])
            ig = jax.nn.sigmoid(_dot(xcb, wx_ref[d]) + bx_ref[d:d + 1])
            log_a = (-LRU_C * r) * softplus[d:d + 1]
            a = jnp.exp(log_a)
            b = jnp.sqrt(jnp.tanh(-log_a) * (a * a + 1.0)) * (ig * xc)
            a = a.reshape(ch // SUBLANES, SUBLANES, width)
            b = b.reshape(ch // SUBLANES, SUBLANES, width)
            for s in (1, 2, 4):
                keep = (sub >= s) if d == 0 else (sub < SUBLANES - s)
                shift = s if d == 0 else SUBLANES - s
                a_sh = jnp.where(keep, pltpu.roll(a, shift, axis=1), 1.0)
                b_sh = jnp.where(keep, pltpu.roll(b, shift, axis=1), 0.0)
                b = a * b_sh + b
                a = a * a_sh
            a_s[pl.ds(r0, ch), :] = a.reshape(ch, width)
            b_s[pl.ds(r0, ch), :] = b.reshape(ch, width)
        return carry

    lax.fori_loop(0, seq // ch, gates_chunk, 0)

    n_tiles = seq // SUBLANES
    h0 = h0_ref[0]

    def tile_step(k, carry):
        cf, cbw = carry
        rf = pl.multiple_of(k * SUBLANES, SUBLANES)
        rb = pl.multiple_of((n_tiles - 1 - k) * SUBLANES, SUBLANES)
        hf = af[pl.ds(rf, SUBLANES), :] * cf + bf[pl.ds(rf, SUBLANES), :]
        bf[pl.ds(rf, SUBLANES), :] = hf
        hb = ab[pl.ds(rb, SUBLANES), :] * cbw + bb[pl.ds(rb, SUBLANES), :]
        bb[pl.ds(rb, SUBLANES), :] = hb
        return hf[SUBLANES - 1:SUBLANES], hb[0:1]

    cf, cbw = lax.fori_loop(0, n_tiles, tile_step, (h0[0:1], h0[1:2]))
    hfin_ref[0, 0:1, :] = cf
    hfin_ref[0, 1:2, :] = cbw

    def out_chunk(c, carry):
        r0 = pl.multiple_of(c * ch, ch)
        hsum = bf[pl.ds(r0, ch), :] + bb[pl.ds(r0, ch), :]
        y_ref[pl.ds(r0, ch), :] = (hsum * jax.nn.gelu(gl_ref[pl.ds(r0, ch), :])).astype(BF16)
        return carry

    lax.fori_loop(0, seq // ch, out_chunk, 0)


def _lru(proj, stream, conv_w, conv_b, wa, wx, ba, bx, lam, h0):
    seq, width = stream.seq, LRU_WIDTH
    full2 = lambda b: (0, 0)
    full3 = lambda b: (0, 0, 0)
    return pl.pallas_call(
        functools.partial(_lru_kernel, seq),
        grid=(stream.n_req,),
        in_specs=[pl.BlockSpec((seq, width), lambda b: (b, 0)),
                  pl.BlockSpec((seq, width), lambda b: (b, 1)),
                  pl.BlockSpec((CONV_W, width), full2),
                  pl.BlockSpec((1, width), full2),
                  pl.BlockSpec((2, width, width), full3),
                  pl.BlockSpec((2, width, width), full3),
                  pl.BlockSpec((2, width), full2),
                  pl.BlockSpec((2, width), full2),
                  pl.BlockSpec((2, width), full2),
                  pl.BlockSpec((1, 2, width), lambda b: (b, 0, 0))],
        out_specs=[pl.BlockSpec((seq, width), lambda b: (b, 0)),
                   pl.BlockSpec((1, 2, width), lambda b: (b, 0, 0))],
        out_shape=[jax.ShapeDtypeStruct((stream.tokens, width), BF16),
                   jax.ShapeDtypeStruct((stream.n_req, 2, width), F32)],
        scratch_shapes=[pltpu.VMEM((seq + 2 * SUBLANES, width), F32)] + [pltpu.VMEM((seq, width), F32)] * 4,
        compiler_params=_params(1),
        name=f"lru_s{seq}",
    )(proj, proj, conv_w, conv_b, wa, wx, ba, bx, lam, h0)


def _split_groups(kk):
    left, right = _lane_half_masks(kk.shape)
    g0_l = jnp.where(left, kk, 0.0)
    g1_r = jnp.where(right, kk, 0.0)
    return ((g0_l, pltpu.roll(g0_l, LANES // 2, axis=1)), (pltpu.roll(g1_r, LANES // 2, axis=1), g1_r))


def _win_ctx_kernel(sink_ref, q_ref, kv_ref, o_ref):
    scale = WIN_HEAD_DIM ** -0.5
    ks = _split_groups(kv_ref[:, 0:LANES])
    vs = _split_groups(kv_ref[:, LANES:2 * LANES])
    for pair in range(WIN_HEADS // 2):
        g = pair // (WIN_REP // 2)
        qp = (q_ref[:, pair * LANES:(pair + 1) * LANES] * scale).astype(BF16)
        acc = None
        for side in range(2):
            sk = sink_ref[2 * pair + side]
            s = _dot_nt(qp, ks[g][side].astype(BF16))
            m = jnp.maximum(jnp.max(s, axis=-1, keepdims=True), sk)
            e = jnp.exp(s - m)
            den = jnp.sum(e, axis=-1, keepdims=True) + jnp.exp(sk - m)
            o = _dot(e.astype(BF16), vs[g][side].astype(BF16)) * (1.0 / den)
            acc = o if acc is None else acc + o
        o_ref[:, pair * LANES:(pair + 1) * LANES] = acc.astype(BF16)


def _win_lat_kernel(sink_ref, q_ref, kv_ref, ck_ref, cv_ref, cos_ref, sin_ref, o_ref,
                    kl_s, vl_s, kc_s, vc_s):
    seq, wn = DEC_SEQ, WINDOW
    scale = WIN_HEAD_DIM ** -0.5
    kr = _rope(kv_ref[:, 0:LANES], cos_ref[:, 0:LANES], sin_ref[:, 0:LANES])
    for src, dst in ((_split_groups(kr), kl_s), (_split_groups(kv_ref[:, LANES:2 * LANES]), vl_s),
                     (_split_groups(ck_ref[0]), kc_s), (_split_groups(cv_ref[0]), vc_s)):
        for g in range(WIN_KV_HEADS):
            for side in range(2):
                dst[2 * g + side] = src[g][side].astype(BF16)

    def q_block(i, carry):
        r0 = pl.multiple_of(i * wn, wn)
        start = pl.multiple_of(jnp.clip((i - 1) * wn, 0, seq - 3 * wn), wn)
        qr = _rope(q_ref[pl.ds(r0, wn), :], cos_ref[pl.ds(r0, wn), :], sin_ref[pl.ds(r0, wn), :])
        qpos = r0 + lax.broadcasted_iota(I32, (wn, 3 * wn), 0)
        kpos = start + lax.broadcasted_iota(I32, (wn, 3 * wn), 1)
        valid = jnp.abs(qpos - kpos) <= wn
        for pair in range(WIN_HEADS // 2):
            g = pair // (WIN_REP // 2)
            qp = (qr[:, pair * LANES:(pair + 1) * LANES] * scale).astype(BF16)
            acc = None
            for side in range(2):
                idx = 2 * g + side
                sk = sink_ref[2 * pair + side]
                sl = _dot_nt(qp, kl_s[idx, pl.ds(start, 3 * wn), :])
                sl = jnp.where(valid, sl, NEG_INF)
                sc = _dot_nt(qp, kc_s[idx])
                m = jnp.maximum(jnp.maximum(jnp.max(sl, axis=-1, keepdims=True),
                                            jnp.max(sc, axis=-1, keepdims=True)), sk)
                el = jnp.exp(sl - m)
                ec = jnp.exp(sc - m)
                den = (jnp.sum(el, axis=-1, keepdims=True) + jnp.sum(ec, axis=-1, keepdims=True)
                       + jnp.exp(sk - m))
                o = (_dot(el.astype(BF16), vl_s[idx, pl.ds(start, 3 * wn), :])
                     + _dot(ec.astype(BF16), vc_s[idx])) * (1.0 / den)
                acc = o if acc is None else acc + o
            o_ref[pl.ds(r0, wn), pair * LANES:(pair + 1) * LANES] = acc.astype(BF16)
        return carry

    lax.fori_loop(0, seq // wn, q_block, 0)


def _win_attn(proj, stream, sink, cache=None, rope=None):
    seq = stream.seq
    q_spec = pl.BlockSpec((seq, WIN_Q), lambda b: (b, 2 * LRU_WIDTH // WIN_Q))
    kv_spec = pl.BlockSpec((seq, 2 * WIN_KV), lambda b: (b, (2 * LRU_WIDTH + WIN_Q) // (2 * WIN_KV)))
    sink_spec = pl.BlockSpec(memory_space=pltpu.SMEM)
    out_spec = pl.BlockSpec((seq, WIN_Q), lambda b: (b, 0))
    out_shape = jax.ShapeDtypeStruct((stream.tokens, WIN_Q), BF16)
    if cache is None:
        return pl.pallas_call(
            _win_ctx_kernel, grid=(stream.n_req,),
            in_specs=[sink_spec, q_spec, kv_spec], out_specs=out_spec, out_shape=out_shape,
            compiler_params=_params(1), name="win_attn_ctx",
        )(sink, proj, proj)
    ck, cv = cache
    cos, sin = rope
    cache_spec = pl.BlockSpec((1, PAST_LEN, LANES), lambda b: (b, 0, 0))
    table_spec = pl.BlockSpec((seq, WIN_Q), lambda b: (0, 0))
    return pl.pallas_call(
        _win_lat_kernel, grid=(stream.n_req,),
        in_specs=[sink_spec, q_spec, kv_spec, cache_spec, cache_spec, table_spec, table_spec],
        out_specs=out_spec, out_shape=out_shape,
        scratch_shapes=[pltpu.VMEM((4, seq, LANES), BF16), pltpu.VMEM((4, seq, LANES), BF16),
                        pltpu.VMEM((4, PAST_LEN, LANES), BF16), pltpu.VMEM((4, PAST_LEN, LANES), BF16)],
        compiler_params=_params(1), name="win_attn_lat",
    )(sink, proj, proj, ck, cv, cos, sin)


def _diff_lambda(lq1_ref, lk1_ref, lq2_ref, lk2_ref, lambda_init):
    t1 = jnp.sum(lq1_ref[...] * lk1_ref[...], axis=-1, keepdims=True)
    t2 = jnp.sum(lq2_ref[...] * lk2_ref[...], axis=-1, keepdims=True)
    return jnp.exp(t1) - jnp.exp(t2) + lambda_init


def _subln(o, g_row, lambda_init):
    o = o * lax.rsqrt(jnp.mean(o * o, axis=-1, keepdims=True) + EPS) * g_row
    return o * (1.0 - lambda_init)


def _diff_ctx_kernel(lambda_init, lq1_ref, lk1_ref, lq2_ref, lk2_ref, sg_ref, q_ref, k_ref, v_ref, o_ref):
    scale = DIFF_HEAD_DIM ** -0.5
    lam = _diff_lambda(lq1_ref, lk1_ref, lq2_ref, lk2_ref, lambda_init)
    for h in range(DIFF_HEADS):
        cols = slice(h * LANES, (h + 1) * LANES)
        q = (q_ref[:, cols] * scale).astype(BF16)
        k = k_ref[0, h]
        left, right = _lane_half_masks(k.shape)
        es, invs = [], []
        for mask in (left, right):
            s = _dot_nt(q, jnp.where(mask, k, 0.0).astype(BF16))
            e = jnp.exp(s - jnp.max(s, axis=-1, keepdims=True))
            es.append(e)
            invs.append(1.0 / jnp.sum(e, axis=-1, keepdims=True))
        w = es[0] * invs[0] - es[1] * (lam * invs[1])
        o = _dot(w.astype(BF16), v_ref[0, h].astype(BF16))
        o_ref[:, cols] = _subln(o, sg_ref[h:h + 1, :], lambda_init).astype(BF16)


def _diff_lat_kernel(lambda_init, lq1_ref, lk1_ref, lq2_ref, lk2_ref, sg_ref, q_ref, k_ref, v_ref,
                     ck_ref, cv_ref, cos_ref, sin_ref, o_ref, kl_s, kc_s, vl_s, vc_s):
    seq, tq = DEC_SEQ, LAT_Q_TILE
    scale = DIFF_HEAD_DIM ** -0.5
    lam = _diff_lambda(lq1_ref, lk1_ref, lq2_ref, lk2_ref, lambda_init)
    kr = _rope(k_ref[...], cos_ref[...], sin_ref[...])
    kc = ck_ref[0, 0, 0]
    for src, dst in ((kr, kl_s), (kc, kc_s)):
        left, right = _lane_half_masks(src.shape)
        dst[0] = jnp.where(left, src, 0.0).astype(BF16)
        dst[1] = jnp.where(right, src, 0.0).astype(BF16)
    vl_s[...] = v_ref[...].astype(BF16)
    vc_s[...] = cv_ref[0, 0, 0].astype(BF16)
    sg = sg_ref[0]

    def q_block(i, carry):
        r0 = pl.multiple_of(i * tq, tq)
        qr = _rope(q_ref[pl.ds(r0, tq), :], cos_ref[pl.ds(r0, tq), :], sin_ref[pl.ds(r0, tq), :])
        qr = (qr * scale).astype(BF16)
        els, ecs, invs = [], [], []
        for comp in range(2):
            sl = _dot_nt(qr, kl_s[comp])
            sc = _dot_nt(qr, kc_s[comp])
            m = jnp.maximum(jnp.max(sl, axis=-1, keepdims=True), jnp.max(sc, axis=-1, keepdims=True))
            els.append(jnp.exp(sl - m))
            ecs.append(jnp.exp(sc - m))
            invs.append(1.0 / (jnp.sum(els[comp], axis=-1, keepdims=True)
                               + jnp.sum(ecs[comp], axis=-1, keepdims=True)))
        neg = lam * invs[1]
        wl = els[0] * invs[0] - els[1] * neg
        wc = ecs[0] * invs[0] - ecs[1] * neg
        o = _dot(wl.astype(BF16), vl_s[...]) + _dot(wc.astype(BF16), vc_s[...])
        o_ref[pl.ds(r0, tq), :] = _subln(o, sg, lambda_init).astype(BF16)
        return carry

    lax.fori_loop(0, seq // tq, q_block, 0)


def _diff_attn(proj, stream, lambda_init, lq1, lk1, lq2, lk2, subln_g, cache=None, rope=None):
    seq = stream.seq
    vec_spec = lambda nd: pl.BlockSpec((1, DIFF_HEAD_DIM), lambda *_: (0, 0))
    out_shape = jax.ShapeDtypeStruct((stream.tokens, DIFF_V), BF16)
    if cache is None:
        q, kh, vh = proj
        head_spec = pl.BlockSpec((1, DIFF_HEADS, seq, LANES), lambda b: (b, 0, 0, 0))
        return pl.pallas_call(
            functools.partial(_diff_ctx_kernel, lambda_init), grid=(stream.n_req,),
            in_specs=[vec_spec(1)] * 4 + [pl.BlockSpec((DIFF_HEADS, LANES), lambda b: (0, 0)),
                                          pl.BlockSpec((seq, DIFF_QK), lambda b: (b, 0)), head_spec, head_spec],
            out_specs=pl.BlockSpec((seq, DIFF_V), lambda b: (b, 0)), out_shape=out_shape,
            compiler_params=_params(1), name="diff_attn_ctx",
        )(lq1, lk1, lq2, lk2, subln_g, q, kh, vh)
    ck, cv = cache
    cos, sin = rope
    blk = lambda c: pl.BlockSpec((seq, LANES), lambda b, h: (b, c * DIFF_HEADS + h))
    cache_spec = pl.BlockSpec((1, 1, 1, PAST_LEN, LANES), lambda b, h: (b, 0, h, 0, 0))
    table_spec = pl.BlockSpec((seq, LANES), lambda b, h: (0, 0))
    return pl.pallas_call(
        functools.partial(_diff_lat_kernel, lambda_init), grid=(stream.n_req, DIFF_HEADS),
        in_specs=[vec_spec(2)] * 4 + [pl.BlockSpec((1, 1, LANES), lambda b, h: (h, 0, 0)),
                                      blk(0), blk(1), blk(2), cache_spec, cache_spec, table_spec, table_spec],
        out_specs=pl.BlockSpec((seq, LANES), lambda b, h: (b, h)), out_shape=out_shape,
        scratch_shapes=[pltpu.VMEM((2, seq, LANES), BF16), pltpu.VMEM((2, PAST_LEN, LANES), BF16),
                        pltpu.VMEM((seq, LANES), BF16), pltpu.VMEM((PAST_LEN, LANES), BF16)],
        compiler_params=_params(2), name="diff_attn_lat",
    )(lq1, lk1, lq2, lk2, subln_g.reshape(DIFF_HEADS, 1, LANES), proj, proj, proj, ck, cv, cos, sin)


def _mix_out_kernel(n_in, *refs):
    a_refs = refs[:n_in]
    w_ref, x_ref, gate_ref, g2_ref, sh2_ref, sc2_ref, wr_ref, xn_ref, h2_ref, lg_ref = refs[n_in:]
    kp = D_MODEL // n_in
    acc = None
    for k, a_ref in enumerate(a_refs):
        part = _dot(a_ref[...], w_ref[k * kp:(k + 1) * kp, :])
        acc = part if acc is None else acc + part
    xn = x_ref[...] + gate_ref[0] * acc
    xn_ref[...] = xn
    h2 = _modnorm(xn, g2_ref[...], sh2_ref[0], sc2_ref[0])
    h2_ref[...] = h2.astype(BF16)
    lg = lax.dot_general(wr_ref[...], h2, (((1,), (1,)), ((), ())),
                         precision=lax.Precision.HIGHEST, preferred_element_type=F32)
    for c in range(lg.shape[1] // LOGIT_TILE):
        lg_ref[c] = lg[:, c * LOGIT_TILE:(c + 1) * LOGIT_TILE]


def _mix_out(mixed, w_out, x, g2_row, mods, layer, stream, wr_t):
    tm = TOKEN_TILE
    n_in = len(mixed)
    kp = D_MODEL // n_in
    row_blk = lambda width: pl.BlockSpec((tm, width), lambda i: (i, 0))
    return pl.pallas_call(
        functools.partial(_mix_out_kernel, n_in),
        grid=(stream.tokens // tm,),
        in_specs=[row_blk(kp)] * n_in + [
            pl.BlockSpec((D_MODEL, D_MODEL), lambda i: (0, 0)),
            row_blk(D_MODEL),
            _mod_spec(layer, stream, 2, tm),
            pl.BlockSpec((1, D_MODEL), lambda i: (0, 0)),
            _mod_spec(layer, stream, 3, tm),
            _mod_spec(layer, stream, 4, tm),
            pl.BlockSpec((N_EXPERTS, D_MODEL), lambda i: (0, 0))],
        out_specs=[row_blk(D_MODEL), row_blk(D_MODEL),
                   pl.BlockSpec((tm // LOGIT_TILE, N_EXPERTS, LOGIT_TILE), lambda i: (i, 0, 0))],
        out_shape=[jax.ShapeDtypeStruct((stream.tokens, D_MODEL), F32),
                   jax.ShapeDtypeStruct((stream.tokens, D_MODEL), BF16),
                   jax.ShapeDtypeStruct((stream.tokens // LOGIT_TILE, N_EXPERTS, LOGIT_TILE), F32)],
        compiler_params=_params(1),
        name=f"mix_out_l{layer}_s{stream.seq}",
    )(*mixed, w_out, x, mods, g2_row, mods, mods, wr_t)


def _sort_desc_lanes(x):
    rows, n = x.shape
    tiles = [x[:, c * LANES:(c + 1) * LANES] for c in range(n // LANES)]
    lane = lax.broadcasted_iota(I32, (rows, LANES), 1)
    k = 2
    while k <= n:
        j = k // 2
        while j >= 1:
            if j < LANES:
                lower = (lane & j) == 0
                for c in range(len(tiles)):
                    t = tiles[c]
                    partner = jnp.where(lower, pltpu.roll(t, LANES - j, axis=1), pltpu.roll(t, j, axis=1))
                    desc = ((lane & k) == 0) if k < LANES else (((c * LANES) & k) == 0)
                    take_max = (lower == desc) if k < LANES else (lower if desc else jnp.logical_not(lower))
                    tiles[c] = jnp.where(take_max, jnp.maximum(t, partner), jnp.minimum(t, partner))
            else:
                jc = j // LANES
                new = list(tiles)
                for c in range(len(tiles)):
                    take_max = ((c & jc) == 0) == (((c * LANES) & k) == 0)
                    new[c] = (jnp.maximum if take_max else jnp.minimum)(tiles[c], tiles[c ^ jc])
                tiles = new
            j //= 2
        k *= 2
    return tiles


def _router_kernel(cap, lg_ref, pos_ref, g_ref):
    x = lg_ref[...]
    n_b, n_e, n_tok = x.shape
    e = jnp.exp(x - jnp.max(x, axis=1, keepdims=True))
    aff = (e / jnp.sum(e, axis=1, keepdims=True)).reshape(n_b * n_e, n_tok)
    srt = _sort_desc_lanes(aff)
    lane_k = (cap - 1) % LANES
    thr = srt[(cap - 1) // LANES][:, lane_k:lane_k + 1]
    gt = aff > thr
    eq = aff == thr
    n_gt = jnp.sum(gt.astype(F32), axis=1, keepdims=True)
    before = (lax.broadcasted_iota(I32, (n_tok, n_tok), 0)
              < lax.broadcasted_iota(I32, (n_tok, n_tok), 1)).astype(BF16)
    eq_rank = _dot(eq.astype(BF16), before)
    sel = gt | (eq & (eq_rank < cap - n_gt))
    slot = _dot(sel.astype(BF16), before).astype(I32)
    pos_ref[...] = jnp.where(sel, slot, -1).reshape(n_b, n_e, n_tok)
    g_ref[...] = jnp.where(sel, aff, 0.0).reshape(n_b, n_e, n_tok)


def _router(logits, stream):
    shape = (stream.n_req, N_EXPERTS, stream.seq)
    spec = pl.BlockSpec(shape, lambda: (0, 0, 0))
    return pl.pallas_call(
        functools.partial(_router_kernel, stream.cap),
        in_specs=[spec], out_specs=[spec, spec],
        out_shape=[jax.ShapeDtypeStruct(shape, I32), jax.ShapeDtypeStruct(shape, F32)],
        compiler_params=pltpu.CompilerParams(vmem_limit_bytes=VMEM_LIMIT_BYTES),
        name=f"router_s{stream.seq}",
    )(logits)


def _gather_kernel(cap, h_ref, pos_ref, xs_ref):
    n_tok = h_ref.shape[0]
    slot = lax.broadcasted_iota(I32, (cap, n_tok), 0)
    onehot = jnp.concatenate([(slot == pos_ref[0, e:e + 1, :]).astype(BF16) for e in range(N_EXPERTS)], axis=0)
    xs = _dot(onehot, h_ref[...]).astype(BF16)
    for e in range(N_EXPERTS):
        xs_ref[e] = xs[e * cap:(e + 1) * cap]


def _gather(h2, pos, stream):
    cap = stream.cap
    return pl.pallas_call(
        functools.partial(_gather_kernel, cap),
        grid=(stream.n_req,),
        in_specs=[pl.BlockSpec((stream.seq, D_MODEL), lambda b: (b, 0)),
                  pl.BlockSpec((1, N_EXPERTS, stream.seq), lambda b: (b, 0, 0))],
        out_specs=pl.BlockSpec((N_EXPERTS, cap, D_MODEL), lambda b: (0, b, 0)),
        out_shape=jax.ShapeDtypeStruct((N_EXPERTS, stream.n_req * cap, D_MODEL), BF16),
        compiler_params=_params(1),
        name=f"gather_s{stream.seq}",
    )(h2, pos)


def _ffn_kernel(xa_ref, xb_ref, wg_ref, wu_ref, wd_ref, ya_ref, yb_ref, acc):
    j = pl.program_id(1)
    rows_a = xa_ref.shape[1]

    @pl.when(j == 0)
    def _():
        acc[...] = jnp.zeros_like(acc)

    x = jnp.concatenate([xa_ref[0], xb_ref[0]], axis=0)
    a = _dot(x, wg_ref[0, 0].astype(BF16))
    u = _dot(x, wu_ref[0, 0].astype(BF16))
    acc[...] += _dot(((a * jax.nn.sigmoid(a)) * u).astype(BF16), wd_ref[0, 0].astype(BF16))

    @pl.when(j == pl.num_programs(1) - 1)
    def _():
        ya_ref[0] = acc[0:rows_a, :].astype(BF16)
        yb_ref[0] = acc[rows_a:, :].astype(BF16)


def _ffn(xs_a, xs_b, layer, w_gate, w_up, w_down):
    tf = FF_TILE
    rows_a, rows_b = xs_a.shape[1], xs_b.shape[1]
    x_spec = lambda rows: pl.BlockSpec((1, rows, D_MODEL), lambda e, j: (e, 0, 0))
    return pl.pallas_call(
        _ffn_kernel,
        grid=(N_EXPERTS, EXPERT_FF // tf),
        in_specs=[x_spec(rows_a), x_spec(rows_b),
                  pl.BlockSpec((1, 1, D_MODEL, tf), lambda e, j: (layer, e, 0, j)),
                  pl.BlockSpec((1, 1, D_MODEL, tf), lambda e, j: (layer, e, 0, j)),
                  pl.BlockSpec((1, 1, tf, D_MODEL), lambda e, j: (layer, e, j, 0))],
        out_specs=[x_spec(rows_a), x_spec(rows_b)],
        out_shape=[jax.ShapeDtypeStruct(xs_a.shape, BF16), jax.ShapeDtypeStruct(xs_b.shape, BF16)],
        scratch_shapes=[pltpu.VMEM((rows_a + rows_b, D_MODEL), F32)],
        compiler_params=_params(2),
        name=f"ffn_l{layer}",
    )(xs_a, xs_b, w_gate, w_up, w_down)


def _scatter_kernel(cap, final, y_ref, pos_ref, g_ref, x_ref, gate_ref, fg_ref, o_ref):
    n_tok = x_ref.shape[0]
    slot = lax.broadcasted_iota(I32, (cap, n_tok), 0)
    onehots, gated = [], []
    for e in range(N_EXPERTS):
        hit = slot == pos_ref[0, e:e + 1, :]
        gate = jnp.sum(jnp.where(hit, g_ref[0, e:e + 1, :], 0.0), axis=1, keepdims=True)
        gated.append((y_ref[e].astype(F32) * gate).astype(BF16))
        onehots.append(hit.astype(BF16))
    moe = lax.dot_general(jnp.concatenate(onehots, axis=0), jnp.concatenate(gated, axis=0),
                          (((0,), (0,)), ((), ())), preferred_element_type=F32)
    xn = x_ref[...] + gate_ref[0] * moe
    if final:
        xn = xn * lax.rsqrt(jnp.mean(xn * xn, axis=-1, keepdims=True) + EPS) * fg_ref[...]
    o_ref[...] = xn


def _scatter(y, pos, g, xn, mods, layer, stream, final, final_g_row):
    cap, seq = stream.cap, stream.seq
    tok_blk = pl.BlockSpec((seq, D_MODEL), lambda b: (b, 0))
    sel_blk = pl.BlockSpec((1, N_EXPERTS, seq), lambda b: (b, 0, 0))
    return pl.pallas_call(
        functools.partial(_scatter_kernel, cap, final),
        grid=(stream.n_req,),
        in_specs=[pl.BlockSpec((N_EXPERTS, cap, D_MODEL), lambda b: (0, b, 0)),
                  sel_blk, sel_blk, tok_blk,
                  _mod_spec(layer, stream, 5, seq),
                  pl.BlockSpec((1, D_MODEL), lambda b: (0, 0))],
        out_specs=tok_blk,
        out_shape=jax.ShapeDtypeStruct((stream.tokens, D_MODEL), F32),
        compiler_params=_params(1),
        name=f"scatter_l{layer}_s{seq}",
    )(y, pos, g, xn, mods, final_g_row)


def _axial_rope_tables(rows, head_dim):
    row = jnp.repeat(jnp.arange(rows, dtype=F32), GRID_W)
    col = jnp.tile(jnp.arange(GRID_W, dtype=F32), rows)
    nf = head_dim // 4
    inv = ROPE_BASE ** (-jnp.arange(nf, dtype=F32) / nf)
    ar = row[:, None] * inv[None]
    ac = col[:, None] * inv[None]
    ang = jnp.concatenate([ar, ar, ac, ac], axis=-1)
    return jnp.cos(ang), jnp.sin(ang)


def _block_diag(w):
    eye = jnp.eye(LRU_BLOCKS, dtype=w.dtype)
    return jnp.einsum('dnkj,nm->dnkmj', w, eye).reshape(2, LRU_WIDTH, LRU_WIDTH)


def kernel(x_prompt, x_sample, cache_win_k, cache_win_v, state_lru, cache_diff_k, cache_diff_v, c, c_ctx, ada_w, ada_b, norm_g, final_g, even_w_in, even_w_out, conv_w, conv_b, lru_wa, lru_ba, lru_wx, lru_bx, lru_lambda, win_sink, odd_w_in, odd_w_out, diff_lq1, diff_lk1, diff_lq2, diff_lk2, diff_subln_g, moe_router, moe_w_gate, moe_w_up, moe_w_down):
    cv_t = jnp.concatenate([c_ctx[None], c, jnp.zeros((COND_ROWS - N_COND, D_MODEL), F32)], axis=0).T
    mods = _adaln(cv_t, ada_w, ada_b).reshape(DEPTH * COND_ROWS, 1, 6 * D_MODEL)

    cos, sin = _axial_rope_tables(DEC_SEQ // GRID_W, WIN_HEAD_DIM)
    rope_win = (jnp.tile(cos, (1, WIN_HEADS)), jnp.tile(sin, (1, WIN_HEADS)))
    rope_diff = (jnp.tile(cos, (1, 2)), jnp.tile(sin, (1, 2)))

    streams = (CTX, LAT)
    xs = [x_prompt.reshape(CTX.tokens, D_MODEL), x_sample.reshape(LAT.tokens, D_MODEL)]
    final_g_row = final_g.reshape(1, D_MODEL)
    outs = {}

    for layer in range(DEPTH):
        idx = layer // 2
        even = layer % 2 == 0
        w_in = (even_w_in if even else odd_w_in)[idx].astype(BF16)
        w_out = (even_w_out if even else odd_w_out)[idx].astype(BF16)
        wr_t = moe_router[layer].T
        g1_row = norm_g[layer, 0].reshape(1, D_MODEL)
        g2_row = norm_g[layer, 1].reshape(1, D_MODEL)
        if even:
            wa = _block_diag(lru_wa[idx]).astype(BF16)
            wx = _block_diag(lru_wx[idx]).astype(BF16)

        routed = []
        for si, stream in enumerate(streams):
            is_ctx = stream is CTX
            if even or not is_ctx:
                proj = _proj(xs[si], g1_row, mods, layer, stream, w_in)
            if even:
                h0 = jnp.zeros((stream.n_req, 2, LRU_WIDTH), F32) if is_ctx else state_lru[:, idx]
                y_lru, h_fin = _lru(proj, stream, conv_w[idx], conv_b[idx].reshape(1, LRU_WIDTH), wa, wx,
                                    lru_ba[idx], lru_bx[idx], lru_lambda[idx], h0)
                if is_ctx:
                    o = _win_attn(proj, stream, win_sink[idx])
                    k0 = 2 * LRU_WIDTH + WIN_Q
                    to_heads = lambda t: t.reshape(BATCH, SEQ, WIN_KV_HEADS, WIN_HEAD_DIM).transpose(0, 2, 1, 3)[:, None]
                    outs["win_k"] = to_heads(proj[:, k0:k0 + WIN_KV])
                    outs["win_v"] = to_heads(proj[:, k0 + WIN_KV:k0 + 2 * WIN_KV])
                    outs["lru"] = h_fin[:, None]
                else:
                    pack = lambda t: t[:, idx].transpose(0, 2, 1, 3).reshape(DEC_BATCH, PAST_LEN, WIN_KV)
                    o = _win_attn(proj, stream, win_sink[idx],
                                  cache=(pack(cache_win_k), pack(cache_win_v)), rope=rope_win)
                mixed = [y_lru, o]
            else:
                lambda_init = 0.8 - 0.6 * math.exp(-0.3 * layer)
                vec = lambda t: t[idx].reshape(1, DIFF_HEAD_DIM)
                args = (vec(diff_lq1), vec(diff_lk1), vec(diff_lq2), vec(diff_lk2), diff_subln_g[idx])
                if is_ctx:
                    q, kh, vh = _proj_heads(xs[si], g1_row, mods, layer, stream, w_in)
                    o = _diff_attn((q, kh, vh), stream, lambda_init, *args)
                    outs["diff_k"] = kh[:, None]
                    outs["diff_v"] = vh[:, None]
                else:
                    o = _diff_attn(proj, stream, lambda_init, *args,
                                   cache=(cache_diff_k[:, idx:idx + 1], cache_diff_v[:, idx:idx + 1]), rope=rope_diff)
                mixed = [o]
            xn, h2, lg = _mix_out(mixed, w_out, xs[si], g2_row, mods, layer, stream, wr_t)
            per_req = stream.seq // LOGIT_TILE
            lg = lg.reshape(stream.n_req, per_req, N_EXPERTS, LOGIT_TILE).transpose(0, 2, 1, 3)
            pos, gate = _router(lg.reshape(stream.n_req, N_EXPERTS, stream.seq), stream)
            routed.append((xn, pos, gate, _gather(h2, pos, stream)))

        ys = _ffn(routed[0][3], routed[1][3], layer, moe_w_gate, moe_w_up, moe_w_down)
        final = layer == DEPTH - 1
        for si, stream in enumerate(streams):
            xn, pos, gate, _ = routed[si]
            xs[si] = _scatter(ys[si], pos, gate, xn, mods, layer, stream, final, final_g_row)

    y_prompt = xs[0].reshape(BATCH, SEQ, D_MODEL)
    y_sample = xs[1].reshape(DEC_BATCH, DEC_SEQ, D_MODEL)
    return (y_prompt, y_sample, outs["win_k"], outs["win_v"], outs["lru"], outs["diff_k"], outs["diff_v"])
```

```python
import functools
import math

import jax
import jax.numpy as jnp
from jax import lax
from jax.experimental import pallas as pl
from jax.experimental.pallas import tpu as pltpu

F32 = jnp.float32
BF16 = jnp.bfloat16
I32 = jnp.int32

D_MODEL = 1024
BATCH = 16
SEQ = 256
DEPTH = 2
DEC_BATCH = 2
DEC_SEQ = 1024
PAST_LEN = 512
GRID_W = 64
LRU_WIDTH = D_MODEL // 2
LRU_BLOCKS = 8
LRU_BLOCK = LRU_WIDTH // LRU_BLOCKS
CONV_W = 4
LRU_C = 8.0
WIN_HEADS = 8
WIN_KV_HEADS = 2
WIN_REP = WIN_HEADS // WIN_KV_HEADS
WIN_HEAD_DIM = 64
WINDOW = 128
WIN_Q = WIN_HEADS * WIN_HEAD_DIM
WIN_KV = WIN_KV_HEADS * WIN_HEAD_DIM
EVEN_IN = 2 * LRU_WIDTH + WIN_Q + 2 * WIN_KV
DIFF_HEADS = 8
DIFF_HEAD_DIM = 64
DIFF_QK = DIFF_HEADS * 2 * DIFF_HEAD_DIM
DIFF_V = DIFF_HEADS * 2 * DIFF_HEAD_DIM
ODD_IN = 2 * DIFF_QK + DIFF_V
N_EXPERTS = 16
EXPERT_FF = 2 * D_MODEL
CAPACITY_FACTOR = 2
ROPE_BASE = 10000.0
EPS = 1e-6
NEG_INF = -1e30

LANES = 128
SUBLANES = 8
VMEM_LIMIT_BYTES = 48 * 1024 * 1024

N_COND = 1 + DEC_BATCH
COND_ROWS = SUBLANES
TOKEN_TILE = 512
MIX_TILE = 1024
LOGIT_TILE = 256
FF_TILE = 1024
FFN_VMEM_LIMIT_BYTES = 56 * 1024 * 1024
LRU_CHUNK = 128
ATT_Q_TILE = 256
ATT_K_TILE = 256


class Stream:
    def __init__(self, n_req, seq, cond0, cond_step):
        self.n_req, self.seq, self.cond0, self.cond_step = n_req, seq, cond0, cond_step
        self.tokens = n_req * seq
        self.cap = CAPACITY_FACTOR * seq // N_EXPERTS

    def cond_of_row(self, row):
        return self.cond0 + self.cond_step * (row // self.seq)


CTX = Stream(BATCH, SEQ, 0, 0)
LAT = Stream(DEC_BATCH, DEC_SEQ, 1, 1)


def _params(n_axes, vmem_limit_bytes=VMEM_LIMIT_BYTES):
    return pltpu.CompilerParams(dimension_semantics=("arbitrary",) * n_axes,
                                vmem_limit_bytes=vmem_limit_bytes)


def _mod_spec(layer, stream, k, rows_per_step):
    return pl.BlockSpec(
        (1, 1, D_MODEL),
        lambda i, *_: (layer * COND_ROWS + stream.cond_of_row(i * rows_per_step), 0, k))


def _dot(a, b):
    return jnp.dot(a, b, preferred_element_type=F32)


def _dot_nt(a, b):
    return lax.dot_general(a, b, (((1,), (1,)), ((), ())), preferred_element_type=F32)


def _modnorm(x, g, shift, scale):
    y = x * lax.rsqrt(jnp.mean(x * x, axis=-1, keepdims=True) + EPS)
    return (y * g) * (1.0 + scale) + shift


def _lane_half_masks(shape):
    lane = lax.broadcasted_iota(I32, shape, len(shape) - 1)
    left = (lane & (LANES - 1)) < LANES // 2
    return left, jnp.logical_not(left)


def _rope(x, cos, sin):
    parts = []
    for c in range(x.shape[1] // LANES):
        xs = x[:, c * LANES:(c + 1) * LANES]
        lane = lax.broadcasted_iota(I32, xs.shape, 1)
        first = (lane & 31) < 16
        rot = jnp.where(first, -pltpu.roll(xs, LANES - 16, axis=1), pltpu.roll(xs, 16, axis=1))
        parts.append(xs * cos[:, c * LANES:(c + 1) * LANES] + rot * sin[:, c * LANES:(c + 1) * LANES])
    return parts[0] if len(parts) == 1 else jnp.concatenate(parts, axis=1)


def _adaln_kernel(cv_ref, w_ref, b_ref, o_ref):
    cv = cv_ref[...]
    s = cv * jax.nn.sigmoid(cv)
    w = w_ref[0]
    ridx = lax.broadcasted_iota(I32, (COND_ROWS, w.shape[1]), 0)
    out = jnp.zeros((COND_ROWS, w.shape[1]), F32)
    for r in range(N_COND):
        out = jnp.where(ridx == r, jnp.sum(w * s[:, r:r + 1], axis=0, keepdims=True), out)
    o_ref[0] = out + b_ref[0]


def _adaln(cv_t, ada_w, ada_b):
    tn = 1024
    return pl.pallas_call(
        _adaln_kernel,
        grid=(DEPTH, 6 * D_MODEL // tn),
        in_specs=[pl.BlockSpec((D_MODEL, COND_ROWS), lambda l, j: (0, 0)),
                  pl.BlockSpec((1, D_MODEL, tn), lambda l, j: (l, 0, j)),
                  pl.BlockSpec((1, 1, tn), lambda l, j: (l, 0, j))],
        out_specs=pl.BlockSpec((1, COND_ROWS, tn), lambda l, j: (l, 0, j)),
        out_shape=jax.ShapeDtypeStruct((DEPTH, COND_ROWS, 6 * D_MODEL), F32),
        compiler_params=_params(2),
        name="adaln",
    )(cv_t, ada_w, ada_b.reshape(DEPTH, 1, 6 * D_MODEL))


def _proj_kernel(x_ref, g_ref, sh_ref, sc_ref, w_ref, o_ref):
    h = _modnorm(x_ref[...], g_ref[...], sh_ref[0], sc_ref[0])
    o_ref[...] = _dot(h.astype(BF16), w_ref[...])


def _proj(x, g_row, mods, layer, stream, w):
    n_out = w.shape[1]
    tm = TOKEN_TILE
    return pl.pallas_call(
        _proj_kernel,
        grid=(stream.tokens // tm,),
        in_specs=[pl.BlockSpec((tm, D_MODEL), lambda i: (i, 0)),
                  pl.BlockSpec((1, D_MODEL), lambda i: (0, 0)),
                  _mod_spec(layer, stream, 0, tm),
                  _mod_spec(layer, stream, 1, tm),
                  pl.BlockSpec((D_MODEL, n_out), lambda i: (0, 0))],
        out_specs=pl.BlockSpec((tm, n_out), lambda i: (i, 0)),
        out_shape=jax.ShapeDtypeStruct((stream.tokens, n_out), F32),
        compiler_params=_params(1),
        name=f"proj_l{layer}_s{stream.seq}",
    )(x, g_row, mods, mods, w)


def _proj_heads_kernel(seq, x_ref, g_ref, sh_ref, sc_ref, w_ref, q_ref, k_ref, v_ref):
    h = _modnorm(x_ref[...], g_ref[...], sh_ref[0], sc_ref[0])
    res = _dot(h.astype(BF16), w_ref[...])
    q_ref[...] = res[:, 0:DIFF_QK]
    for r in range(x_ref.shape[0] // seq):
        rows = slice(r * seq, (r + 1) * seq)
        for hh in range(DIFF_HEADS):
            k_ref[r, hh] = res[rows, DIFF_QK + hh * LANES:DIFF_QK + (hh + 1) * LANES]
            v_ref[r, hh] = res[rows, 2 * DIFF_QK + hh * LANES:2 * DIFF_QK + (hh + 1) * LANES]


def _proj_heads(x, g_row, mods, layer, stream, w):
    tm, seq = TOKEN_TILE, stream.seq
    head_shape = (stream.n_req, DIFF_HEADS, seq, 2 * DIFF_HEAD_DIM)
    head_spec = pl.BlockSpec((tm // seq, DIFF_HEADS, seq, 2 * DIFF_HEAD_DIM), lambda i: (i, 0, 0, 0))
    return pl.pallas_call(
        functools.partial(_proj_heads_kernel, seq),
        grid=(stream.tokens // tm,),
        in_specs=[pl.BlockSpec((tm, D_MODEL), lambda i: (i, 0)),
                  pl.BlockSpec((1, D_MODEL), lambda i: (0, 0)),
                  _mod_spec(layer, stream, 0, tm),
                  _mod_spec(layer, stream, 1, tm),
                  pl.BlockSpec((D_MODEL, ODD_IN), lambda i: (0, 0))],
        out_specs=[pl.BlockSpec((tm, DIFF_QK), lambda i: (i, 0)), head_spec, head_spec],
        out_shape=[jax.ShapeDtypeStruct((stream.tokens, DIFF_QK), F32),
                   jax.ShapeDtypeStruct(head_shape, F32), jax.ShapeDtypeStruct(head_shape, F32)],
        compiler_params=_params(1),
        name=f"proj_heads_l{layer}_s{seq}",
    )(x, g_row, mods, mods, w)


def _lru_kernel(seq, xl_ref, gl_ref, cw_ref, cb_ref, wa_ref, wx_ref, ba_ref, bx_ref, lam_ref, h0_ref,
                y_ref, hfin_ref, xpad, af, bf, ab, bb):
    width = LRU_WIDTH
    ch = LRU_CHUNK
    halo = SUBLANES
    xpad[0:halo, :] = jnp.zeros((halo, width), F32)
    xpad[halo + seq:2 * halo + seq, :] = jnp.zeros((halo, width), F32)
    xpad[halo:halo + seq, :] = xl_ref[...]

    lam = lam_ref[...]
    z = -lam
    softplus = jnp.maximum(z, 0.0) + jnp.log1p(jnp.exp(-jnp.abs(z)))
    sub = lax.broadcasted_iota(I32, (ch // SUBLANES, SUBLANES, width), 1)
    cw = cw_ref[...]
    cb = cb_ref[...]

    def gates_chunk(c, carry):
        r0 = pl.multiple_of(c * ch, ch)
        win = xpad[pl.ds(r0, ch + 2 * halo), :]
        n_win = ch + 2 * halo

        def tap(j):
            return pltpu.roll(win, n_win - (halo - 2 + j), axis=0)[0:ch]

        xc = tap(0) * cw[0:1]
        for j in range(1, CONV_W):
            xc = xc + tap(j) * cw[j:j + 1]
        xc = xc + cb
        xcb = xc.astype(BF16)
        for d, (a_s, b_s) in enumerate(((af, bf), (ab, bb))):
            r = jax.nn.sigmoid(_dot(xcb, wa_ref[d]) + ba_ref[d:d + 1])
            ig = jax.nn.sigmoid(_dot(xcb, wx_ref[d]) + bx_ref[d:d + 1])
            log_a = (-LRU_C * r) * softplus[d:d + 1]
            a = jnp.exp(log_a)
            b = jnp.sqrt(jnp.tanh(-log_a) * (a * a + 1.0)) * (ig * xc)
            a = a.reshape(ch // SUBLANES, SUBLANES, width)
            b = b.reshape(ch // SUBLANES, SUBLANES, width)
            for s in (1, 2, 4):
                keep = (sub >= s) if d == 0 else (sub < SUBLANES - s)
                shift = s if d == 0 else SUBLANES - s
                a_sh = jnp.where(keep, pltpu.roll(a, shift, axis=1), 1.0)
                b_sh = jnp.where(keep, pltpu.roll(b, shift, axis=1), 0.0)
                b = a * b_sh + b
                a = a * a_sh
            a_s[pl.ds(r0, ch), :] = a.reshape(ch, width)
            b_s[pl.ds(r0, ch), :] = b.reshape(ch, width)
        return carry

    lax.fori_loop(0, seq // ch, gates_chunk, 0)

    n_tiles = seq // SUBLANES
    h0 = h0_ref[0]

    def tile_step(k, carry):
        cf, cbw = carry
        rf = pl.multiple_of(k * SUBLANES, SUBLANES)
        rb = pl.multiple_of((n_tiles - 1 - k) * SUBLANES, SUBLANES)
        hf = af[pl.ds(rf, SUBLANES), :] * cf + bf[pl.ds(rf, SUBLANES), :]
        bf[pl.ds(rf, SUBLANES), :] = hf
        hb = ab[pl.ds(rb, SUBLANES), :] * cbw + bb[pl.ds(rb, SUBLANES), :]
        bb[pl.ds(rb, SUBLANES), :] = hb
        return hf[SUBLANES - 1:SUBLANES], hb[0:1]

    cf, cbw = lax.fori_loop(0, n_tiles, tile_step, (h0[0:1], h0[1:2]))
    hfin_ref[0, 0:1, :] = cf
    hfin_ref[0, 1:2, :] = cbw

    def out_chunk(c, carry):
        r0 = pl.multiple_of(c * ch, ch)
        hsum = bf[pl.ds(r0, ch), :] + bb[pl.ds(r0, ch), :]
        y_ref[pl.ds(r0, ch), :] = (hsum * jax.nn.gelu(gl_ref[pl.ds(r0, ch), :])).astype(BF16)
        return carry

    lax.fori_loop(0, seq // ch, out_chunk, 0)


def _lru(proj, stream, conv_w, conv_b, wa, wx, ba, bx, lam, h0):
    seq, width = stream.seq, LRU_WIDTH
    full2 = lambda b: (0, 0)
    full3 = lambda b: (0, 0, 0)
    return pl.pallas_call(
        functools.partial(_lru_kernel, seq),
        grid=(stream.n_req,),
        in_specs=[pl.BlockSpec((seq, width), lambda b: (b, 0)),
                  pl.BlockSpec((seq, width), lambda b: (b, 1)),
                  pl.BlockSpec((CONV_W, width), full2),
                  pl.BlockSpec((1, width), full2),
                  pl.BlockSpec((2, width, width), full3),
                  pl.BlockSpec((2, width, width), full3),
                  pl.BlockSpec((2, width), full2),
                  pl.BlockSpec((2, width), full2),
                  pl.BlockSpec((2, width), full2),
                  pl.BlockSpec((1, 2, width), lambda b: (b, 0, 0))],
        out_specs=[pl.BlockSpec((seq, width), lambda b: (b, 0)),
                   pl.BlockSpec((1, 2, width), lambda b: (b, 0, 0))],
        out_shape=[jax.ShapeDtypeStruct((stream.tokens, width), BF16),
                   jax.ShapeDtypeStruct((stream.n_req, 2, width), F32)],
        scratch_shapes=[pltpu.VMEM((seq + 2 * SUBLANES, width), F32)] + [pltpu.VMEM((seq, width), F32)] * 4,
        compiler_params=_params(1),
        name=f"lru_s{seq}",
    )(proj, proj, conv_w, conv_b, wa, wx, ba, bx, lam, h0)


def _attend(q, chunks, s_ref):
    tile_max = None
    spans = []
    off = 0
    for keys, _, valid in chunks:
        s = _dot_nt(q, keys())
        if valid is not None:
            s = jnp.where(valid, s, NEG_INF)
        n = s.shape[1]
        s_ref[:, off:off + n] = s
        for c in range(n // LANES):
            t = s[:, c * LANES:(c + 1) * LANES]
            tile_max = t if tile_max is None else jnp.maximum(tile_max, t)
        spans.append((off, n))
        off += n
    m = jnp.max(tile_max, axis=-1, keepdims=True)
    acc = None
    for (_, values, _), (o, n) in zip(chunks, spans):
        part = _dot(jnp.exp(s_ref[:, o:o + n] - m).astype(BF16), values())
        acc = part if acc is None else acc + part
    return acc, m


def _split_groups(kk):
    left, right = _lane_half_masks(kk.shape)
    g0_l = jnp.where(left, kk, 0.0)
    g1_r = jnp.where(right, kk, 0.0)
    return ((g0_l, pltpu.roll(g0_l, LANES // 2, axis=1)), (pltpu.roll(g1_r, LANES // 2, axis=1), g1_r))


def _win_ctx_kernel(sink_ref, q_ref, kv_ref, o_ref):
    scale = WIN_HEAD_DIM ** -0.5
    ks = _split_groups(kv_ref[:, 0:LANES])
    vs = _split_groups(kv_ref[:, LANES:2 * LANES])
    for pair in range(WIN_HEADS // 2):
        g = pair // (WIN_REP // 2)
        qp = (q_ref[:, pair * LANES:(pair + 1) * LANES] * scale).astype(BF16)
        acc = None
        for side in range(2):
            sk = sink_ref[2 * pair + side]
            s = _dot_nt(qp, ks[g][side].astype(BF16))
            m = jnp.maximum(jnp.max(s, axis=-1, keepdims=True), sk)
            e = jnp.exp(s - m)
            den = jnp.sum(e, axis=-1, keepdims=True) + jnp.exp(sk - m)
            o = _dot(e.astype(BF16), vs[g][side].astype(BF16)) * (1.0 / den)
            acc = o if acc is None else acc + o
        o_ref[:, pair * LANES:(pair + 1) * LANES] = acc.astype(BF16)


def _win_lat_kernel(sink_ref, q_ref, kv_ref, ck_ref, cv_ref, cos_ref, sin_ref, o_ref,
                    kl_s, vl_s, kc_s, vc_s):
    seq, wn = DEC_SEQ, WINDOW
    scale = WIN_HEAD_DIM ** -0.5
    kr = _rope(kv_ref[:, 0:LANES], cos_ref[:, 0:LANES], sin_ref[:, 0:LANES])
    for src, dst in ((_split_groups(kr), kl_s), (_split_groups(kv_ref[:, LANES:2 * LANES]), vl_s),
                     (_split_groups(ck_ref[0]), kc_s), (_split_groups(cv_ref[0]), vc_s)):
        for g in range(WIN_KV_HEADS):
            for side in range(2):
                dst[2 * g + side] = src[g][side].astype(BF16)

    def q_block(i, carry):
        r0 = pl.multiple_of(i * wn, wn)
        start = pl.multiple_of(jnp.clip((i - 1) * wn, 0, seq - 3 * wn), wn)
        qr = _rope(q_ref[pl.ds(r0, wn), :], cos_ref[pl.ds(r0, wn), :], sin_ref[pl.ds(r0, wn), :])
        qpos = r0 + lax.broadcasted_iota(I32, (wn, 3 * wn), 0)
        kpos = start + lax.broadcasted_iota(I32, (wn, 3 * wn), 1)
        valid = jnp.abs(qpos - kpos) <= wn
        for pair in range(WIN_HEADS // 2):
            g = pair // (WIN_REP // 2)
            qp = (qr[:, pair * LANES:(pair + 1) * LANES] * scale).astype(BF16)
            acc = None
            for side in range(2):
                idx = 2 * g + side
                sk = sink_ref[2 * pair + side]
                sl = _dot_nt(qp, kl_s[idx, pl.ds(start, 3 * wn), :])
                sl = jnp.where(valid, sl, NEG_INF)
                sc = _dot_nt(qp, kc_s[idx])
                m = jnp.maximum(jnp.maximum(jnp.max(sl, axis=-1, keepdims=True),
                                            jnp.max(sc, axis=-1, keepdims=True)), sk)
                el = jnp.exp(sl - m)
                ec = jnp.exp(sc - m)
                den = (jnp.sum(el, axis=-1, keepdims=True) + jnp.sum(ec, axis=-1, keepdims=True)
                       + jnp.exp(sk - m))
                o = (_dot(el.astype(BF16), vl_s[idx, pl.ds(start, 3 * wn), :])
                     + _dot(ec.astype(BF16), vc_s[idx])) * (1.0 / den)
                acc = o if acc is None else acc + o
            o_ref[pl.ds(r0, wn), pair * LANES:(pair + 1) * LANES] = acc.astype(BF16)
        return carry

    lax.fori_loop(0, seq // wn, q_block, 0)


def _win_attn(proj, stream, sink, cache=None, rope=None):
    seq = stream.seq
    q_spec = pl.BlockSpec((seq, WIN_Q), lambda b: (b, 2 * LRU_WIDTH // WIN_Q))
    kv_spec = pl.BlockSpec((seq, 2 * WIN_KV), lambda b: (b, (2 * LRU_WIDTH + WIN_Q) // (2 * WIN_KV)))
    sink_spec = pl.BlockSpec(memory_space=pltpu.SMEM)
    out_spec = pl.BlockSpec((seq, WIN_Q), lambda b: (b, 0))
    out_shape = jax.ShapeDtypeStruct((stream.tokens, WIN_Q), BF16)
    if cache is None:
        return pl.pallas_call(
            _win_ctx_kernel, grid=(stream.n_req,),
            in_specs=[sink_spec, q_spec, kv_spec], out_specs=out_spec, out_shape=out_shape,
            compiler_params=_params(1), name="win_attn_ctx",
        )(sink, proj, proj)
    ck, cv = cache
    cos, sin = rope
    cache_spec = pl.BlockSpec((1, PAST_LEN, LANES), lambda b: (b, 0, 0))
    table_spec = pl.BlockSpec((seq, WIN_Q), lambda b: (0, 0))
    return pl.pallas_call(
        _win_lat_kernel, grid=(stream.n_req,),
        in_specs=[sink_spec, q_spec, kv_spec, cache_spec, cache_spec, table_spec, table_spec],
        out_specs=out_spec, out_shape=out_shape,
        scratch_shapes=[pltpu.VMEM((4, seq, LANES), BF16), pltpu.VMEM((4, seq, LANES), BF16),
                        pltpu.VMEM((4, PAST_LEN, LANES), BF16), pltpu.VMEM((4, PAST_LEN, LANES), BF16)],
        compiler_params=_params(1), name="win_attn_lat",
    )(sink, proj, proj, ck, cv, cos, sin)


def _diff_lambda(lq1_ref, lk1_ref, lq2_ref, lk2_ref, lambda_init):
    t1 = jnp.sum(lq1_ref[...] * lk1_ref[...], axis=-1, keepdims=True)
    t2 = jnp.sum(lq2_ref[...] * lk2_ref[...], axis=-1, keepdims=True)
    return jnp.exp(t1) - jnp.exp(t2) + lambda_init


def _subln(o, g_row, lambda_init):
    o = o * lax.rsqrt(jnp.mean(o * o, axis=-1, keepdims=True) + EPS) * g_row
    return o * (1.0 - lambda_init)


def _component_keys(k):
    left, right = _lane_half_masks(k.shape)
    return jnp.where(left, k, 0.0).astype(BF16), jnp.where(right, k, 0.0).astype(BF16)


def _values_with_ones(v):
    return jnp.concatenate([v.astype(BF16), jnp.ones(v.shape, BF16)], axis=1)


def _diff_combine(accs, lam):
    o1, o2 = accs[0][:, 0:LANES], accs[1][:, 0:LANES]
    return o1 * (1.0 / accs[0][:, LANES:]) - o2 * (lam * (1.0 / accs[1][:, LANES:]))


def _diff_ctx_kernel(lambda_init, lq1_ref, lk1_ref, lq2_ref, lk2_ref, sg_ref, q_ref, k_ref, v_ref, o_ref, s_ref):
    scale = DIFF_HEAD_DIM ** -0.5
    tq = ATT_Q_TILE
    lam = _diff_lambda(lq1_ref, lk1_ref, lq2_ref, lk2_ref, lambda_init)
    for h in range(DIFF_HEADS):
        cols = slice(h * LANES, (h + 1) * LANES)
        keys = _component_keys(k_ref[0, h])
        vals = _values_with_ones(v_ref[0, h])
        for qt in range(q_ref.shape[0] // tq):
            rows = slice(qt * tq, (qt + 1) * tq)
            q = (q_ref[rows, cols] * scale).astype(BF16)
            slots = [s_ref.at[(4 * h + 2 * qt + c) % s_ref.shape[0]] for c in range(2)]
            accs = [_attend(q, [(lambda: keys[c], lambda: vals, None)], slots[c])[0] for c in range(2)]
            o = _diff_combine(accs, lam)
            o_ref[rows, cols] = _subln(o, sg_ref[h:h + 1, :], lambda_init).astype(BF16)


def _diff_lat_kernel(lambda_init, lq1_ref, lk1_ref, lq2_ref, lk2_ref, sg_ref, q_ref, k_ref, v_ref,
                     ck_ref, cv_ref, cos_ref, sin_ref, o_ref, kl_s, kc_s, vl_s, vc_s, s_ref):
    seq, tq, tk = DEC_SEQ, ATT_Q_TILE, ATT_K_TILE
    scale = DIFF_HEAD_DIM ** -0.5
    lam = _diff_lambda(lq1_ref, lk1_ref, lq2_ref, lk2_ref, lambda_init)
    for src, dst in ((_rope(k_ref[...], cos_ref[...], sin_ref[...]), kl_s), (ck_ref[0, 0, 0], kc_s)):
        dst[0], dst[1] = _component_keys(src)
    vl_s[...] = _values_with_ones(v_ref[...])
    vc_s[...] = _values_with_ones(cv_ref[0, 0, 0])
    sg = sg_ref[0]

    def q_block(i, carry):
        for sub in range(2):
            r0 = pl.multiple_of((2 * i + sub) * tq, tq)
            qr = _rope(q_ref[pl.ds(r0, tq), :], cos_ref[pl.ds(r0, tq), :], sin_ref[pl.ds(r0, tq), :])
            qr = (qr * scale).astype(BF16)
            accs = []
            for comp in range(2):
                chunks = [(lambda j=j: kl_s[comp, j * tk:(j + 1) * tk, :],
                           lambda j=j: vl_s[j * tk:(j + 1) * tk, :], None) for j in range(seq // tk)]
                chunks += [(lambda j=j: kc_s[comp, j * tk:(j + 1) * tk, :],
                            lambda j=j: vc_s[j * tk:(j + 1) * tk, :], None) for j in range(PAST_LEN // tk)]
                accs.append(_attend(qr, chunks, s_ref.at[2 * sub + comp])[0])
            o_ref[pl.ds(r0, tq), :] = _subln(_diff_combine(accs, lam), sg, lambda_init).astype(BF16)
        return carry

    lax.fori_loop(0, seq // (2 * tq), q_block, 0)


def _diff_attn(proj, stream, lambda_init, lq1, lk1, lq2, lk2, subln_g, cache=None, rope=None):
    seq = stream.seq
    vec_spec = lambda nd: pl.BlockSpec((1, DIFF_HEAD_DIM), lambda *_: (0, 0))
    out_shape = jax.ShapeDtypeStruct((stream.tokens, DIFF_V), BF16)
    if cache is None:
        q, kh, vh = proj
        head_spec = pl.BlockSpec((1, DIFF_HEADS, seq, LANES), lambda b: (b, 0, 0, 0))
        return pl.pallas_call(
            functools.partial(_diff_ctx_kernel, lambda_init), grid=(stream.n_req,),
            in_specs=[vec_spec(1)] * 4 + [pl.BlockSpec((DIFF_HEADS, LANES), lambda b: (0, 0)),
                                          pl.BlockSpec((seq, DIFF_QK), lambda b: (b, 0)), head_spec, head_spec],
            out_specs=pl.BlockSpec((seq, DIFF_V), lambda b: (b, 0)), out_shape=out_shape,
            scratch_shapes=[pltpu.VMEM((8, ATT_Q_TILE, seq), F32)],
            compiler_params=_params(1), name="diff_attn_ctx",
        )(lq1, lk1, lq2, lk2, subln_g, q, kh, vh)
    ck, cv = cache
    cos, sin = rope
    blk = lambda c: pl.BlockSpec((seq, LANES), lambda b, h: (b, c * DIFF_HEADS + h))
    cache_spec = pl.BlockSpec((1, 1, 1, PAST_LEN, LANES), lambda b, h: (b, 0, h, 0, 0))
    table_spec = pl.BlockSpec((seq, LANES), lambda b, h: (0, 0))
    return pl.pallas_call(
        functools.partial(_diff_lat_kernel, lambda_init), grid=(stream.n_req, DIFF_HEADS),
        in_specs=[vec_spec(2)] * 4 + [pl.BlockSpec((1, 1, LANES), lambda b, h: (h, 0, 0)),
                                      blk(0), blk(1), blk(2), cache_spec, cache_spec, table_spec, table_spec],
        out_specs=pl.BlockSpec((seq, LANES), lambda b, h: (b, h)), out_shape=out_shape,
        scratch_shapes=[pltpu.VMEM((2, seq, LANES), BF16), pltpu.VMEM((2, PAST_LEN, LANES), BF16),
                        pltpu.VMEM((seq, 2 * LANES), BF16), pltpu.VMEM((PAST_LEN, 2 * LANES), BF16),
                        pltpu.VMEM((4, ATT_Q_TILE, seq + PAST_LEN), F32)],
        compiler_params=_params(2), name="diff_attn_lat",
    )(lq1, lk1, lq2, lk2, subln_g.reshape(DIFF_HEADS, 1, LANES), proj, proj, proj, ck, cv, cos, sin)


def _mix_out_kernel(n_in, *refs):
    a_refs = refs[:n_in]
    w_ref, x_ref, gate_ref, g2_ref, sh2_ref, sc2_ref, wr_ref, xn_ref, h2_ref, lg_ref = refs[n_in:]
    kp = D_MODEL // n_in
    acc = None
    for k, a_ref in enumerate(a_refs):
        part = _dot(a_ref[...], w_ref[k * kp:(k + 1) * kp, :])
        acc = part if acc is None else acc + part
    xn = x_ref[...] + gate_ref[0] * acc
    xn_ref[...] = xn
    h2 = _modnorm(xn, g2_ref[...], sh2_ref[0], sc2_ref[0])
    h_hi = h2.astype(BF16)
    h2_ref[...] = h_hi
    h_lo = (h2 - h_hi.astype(F32)).astype(BF16)
    wr = wr_ref[...]
    w_hi = wr.astype(BF16)
    w_lo = (wr - w_hi.astype(F32)).astype(BF16)
    by_hi = _dot_nt(jnp.concatenate([w_hi, w_lo], axis=0), h_hi)
    lg = by_hi[0:N_EXPERTS] + (by_hi[N_EXPERTS:] + _dot_nt(w_hi, h_lo))
    for c in range(lg.shape[1] // LOGIT_TILE):
        lg_ref[c] = lg[:, c * LOGIT_TILE:(c + 1) * LOGIT_TILE]


def _mix_out(mixed, w_out, x, g2_row, mods, layer, stream, wr_t):
    tm = MIX_TILE
    n_in = len(mixed)
    kp = D_MODEL // n_in
    row_blk = lambda width: pl.BlockSpec((tm, width), lambda i: (i, 0))
    return pl.pallas_call(
        functools.partial(_mix_out_kernel, n_in),
        grid=(stream.tokens // tm,),
        in_specs=[row_blk(kp)] * n_in + [
            pl.BlockSpec((D_MODEL, D_MODEL), lambda i: (0, 0)),
            row_blk(D_MODEL),
            _mod_spec(layer, stream, 2, tm),
            pl.BlockSpec((1, D_MODEL), lambda i: (0, 0)),
            _mod_spec(layer, stream, 3, tm),
            _mod_spec(layer, stream, 4, tm),
            pl.BlockSpec((N_EXPERTS, D_MODEL), lambda i: (0, 0))],
        out_specs=[row_blk(D_MODEL), row_blk(D_MODEL),
                   pl.BlockSpec((tm // LOGIT_TILE, N_EXPERTS, LOGIT_TILE), lambda i: (i, 0, 0))],
        out_shape=[jax.ShapeDtypeStruct((stream.tokens, D_MODEL), F32),
                   jax.ShapeDtypeStruct((stream.tokens, D_MODEL), BF16),
                   jax.ShapeDtypeStruct((stream.tokens // LOGIT_TILE, N_EXPERTS, LOGIT_TILE), F32)],
        compiler_params=_params(1),
        name=f"mix_out_l{layer}_s{stream.seq}",
    )(*mixed, w_out, x, mods, g2_row, mods, mods, wr_t)


def _sort_desc_lanes(x):
    rows, n = x.shape
    tiles = [x[:, c * LANES:(c + 1) * LANES] for c in range(n // LANES)]
    lane = lax.broadcasted_iota(I32, (rows, LANES), 1)
    k = 2
    while k <= n:
        j = k // 2
        while j >= 1:
            if j < LANES:
                lower = (lane & j) == 0
                for c in range(len(tiles)):
                    t = tiles[c]
                    partner = jnp.where(lower, pltpu.roll(t, LANES - j, axis=1), pltpu.roll(t, j, axis=1))
                    desc = ((lane & k) == 0) if k < LANES else (((c * LANES) & k) == 0)
                    take_max = (lower == desc) if k < LANES else (lower if desc else jnp.logical_not(lower))
                    tiles[c] = jnp.where(take_max, jnp.maximum(t, partner), jnp.minimum(t, partner))
            else:
                jc = j // LANES
                new = list(tiles)
                for c in range(len(tiles)):
                    take_max = ((c & jc) == 0) == (((c * LANES) & k) == 0)
                    new[c] = (jnp.maximum if take_max else jnp.minimum)(tiles[c], tiles[c ^ jc])
                tiles = new
            j //= 2
        k *= 2
    return tiles


def _router_kernel(cap, lg_ref, pos_ref, g_ref):
    x = lg_ref[...]
    n_b, n_e, n_tok = x.shape
    e = jnp.exp(x - jnp.max(x, axis=1, keepdims=True))
    aff = (e / jnp.sum(e, axis=1, keepdims=True)).reshape(n_b * n_e, n_tok)
    srt = _sort_desc_lanes(aff)
    lane_k = (cap - 1) % LANES
    thr = srt[(cap - 1) // LANES][:, lane_k:lane_k + 1]
    gt = aff > thr
    eq = aff == thr
    n_gt = jnp.sum(gt.astype(F32), axis=1, keepdims=True)
    before = (lax.broadcasted_iota(I32, (n_tok, n_tok), 0)
              < lax.broadcasted_iota(I32, (n_tok, n_tok), 1)).astype(BF16)
    eq_rank = _dot(eq.astype(BF16), before)
    sel = gt | (eq & (eq_rank < cap - n_gt))
    slot = _dot(sel.astype(BF16), before).astype(I32)
    pos_ref[...] = jnp.where(sel, slot, -1).reshape(n_b, n_e, n_tok)
    g_ref[...] = jnp.where(sel, aff, 0.0).reshape(n_b, n_e, n_tok)


def _router(logits, stream):
    shape = (stream.n_req, N_EXPERTS, stream.seq)
    spec = pl.BlockSpec(shape, lambda: (0, 0, 0))
    return pl.pallas_call(
        functools.partial(_router_kernel, stream.cap),
        in_specs=[spec], out_specs=[spec, spec],
        out_shape=[jax.ShapeDtypeStruct(shape, I32), jax.ShapeDtypeStruct(shape, F32)],
        compiler_params=pltpu.CompilerParams(vmem_limit_bytes=VMEM_LIMIT_BYTES),
        name=f"router_s{stream.seq}",
    )(logits)


def _gather_kernel(cap, h_ref, pos_ref, xs_ref):
    n_tok = h_ref.shape[0]
    slot = lax.broadcasted_iota(I32, (cap, n_tok), 0)
    onehot = jnp.concatenate([(slot == pos_ref[0, e:e + 1, :]).astype(BF16) for e in range(N_EXPERTS)], axis=0)
    xs = _dot(onehot, h_ref[...]).astype(BF16)
    for e in range(N_EXPERTS):
        xs_ref[e] = xs[e * cap:(e + 1) * cap]


def _gather(h2, pos, stream):
    cap = stream.cap
    return pl.pallas_call(
        functools.partial(_gather_kernel, cap),
        grid=(stream.n_req,),
        in_specs=[pl.BlockSpec((stream.seq, D_MODEL), lambda b: (b, 0)),
                  pl.BlockSpec((1, N_EXPERTS, stream.seq), lambda b: (b, 0, 0))],
        out_specs=pl.BlockSpec((N_EXPERTS, cap, D_MODEL), lambda b: (0, b, 0)),
        out_shape=jax.ShapeDtypeStruct((N_EXPERTS, stream.n_req * cap, D_MODEL), BF16),
        compiler_params=_params(1),
        name=f"gather_s{stream.seq}",
    )(h2, pos)


def _ffn_kernel(xa_ref, xb_ref, wg_ref, wu_ref, wd_ref, ya_ref, yb_ref, acc):
    j = pl.program_id(1)
    rows_a = xa_ref.shape[1]

    @pl.when(j == 0)
    def _():
        acc[...] = jnp.zeros_like(acc)

    x = jnp.concatenate([xa_ref[0], xb_ref[0]], axis=0)
    a = _dot(x, wg_ref[0, 0].astype(BF16))
    u = _dot(x, wu_ref[0, 0].astype(BF16))
    acc[...] += _dot(((a * jax.nn.sigmoid(a)) * u).astype(BF16), wd_ref[0, 0].astype(BF16))

    @pl.when(j == pl.num_programs(1) - 1)
    def _():
        ya_ref[0] = acc[0:rows_a, :].astype(BF16)
        yb_ref[0] = acc[rows_a:, :].astype(BF16)


def _ffn(xs_a, xs_b, layer, w_gate, w_up, w_down):
    tf = FF_TILE
    rows_a, rows_b = xs_a.shape[1], xs_b.shape[1]
    x_spec = lambda rows: pl.BlockSpec((1, rows, D_MODEL), lambda e, j: (e, 0, 0))
    return pl.pallas_call(
        _ffn_kernel,
        grid=(N_EXPERTS, EXPERT_FF // tf),
        in_specs=[x_spec(rows_a), x_spec(rows_b),
                  pl.BlockSpec((1, 1, D_MODEL, tf), lambda e, j: (layer, e, 0, j)),
                  pl.BlockSpec((1, 1, D_MODEL, tf), lambda e, j: (layer, e, 0, j)),
                  pl.BlockSpec((1, 1, tf, D_MODEL), lambda e, j: (layer, e, j, 0))],
        out_specs=[x_spec(rows_a), x_spec(rows_b)],
        out_shape=[jax.ShapeDtypeStruct(xs_a.shape, BF16), jax.ShapeDtypeStruct(xs_b.shape, BF16)],
        scratch_shapes=[pltpu.VMEM((rows_a + rows_b, D_MODEL), F32)],
        compiler_params=_params(2, FFN_VMEM_LIMIT_BYTES),
        name=f"ffn_l{layer}",
    )(xs_a, xs_b, w_gate, w_up, w_down)


def _scatter_kernel(cap, final, y_ref, pos_ref, g_ref, x_ref, gate_ref, fg_ref, o_ref):
    n_tok = x_ref.shape[0]
    slot = lax.broadcasted_iota(I32, (cap, n_tok), 0)
    onehots, gated = [], []
    for e in range(N_EXPERTS):
        hit = slot == pos_ref[0, e:e + 1, :]
        gate = jnp.sum(jnp.where(hit, g_ref[0, e:e + 1, :], 0.0), axis=1, keepdims=True)
        gated.append((y_ref[e].astype(F32) * gate).astype(BF16))
        onehots.append(hit.astype(BF16))
    moe = lax.dot_general(jnp.concatenate(onehots, axis=0), jnp.concatenate(gated, axis=0),
                          (((0,), (0,)), ((), ())), preferred_element_type=F32)
    xn = x_ref[...] + gate_ref[0] * moe
    if final:
        xn = xn * lax.rsqrt(jnp.mean(xn * xn, axis=-1, keepdims=True) + EPS) * fg_ref[...]
    o_ref[...] = xn


def _scatter(y, pos, g, xn, mods, layer, stream, final, final_g_row):
    cap, seq = stream.cap, stream.seq
    tok_blk = pl.BlockSpec((seq, D_MODEL), lambda b: (b, 0))
    sel_blk = pl.BlockSpec((1, N_EXPERTS, seq), lambda b: (b, 0, 0))
    return pl.pallas_call(
        functools.partial(_scatter_kernel, cap, final),
        grid=(stream.n_req,),
        in_specs=[pl.BlockSpec((N_EXPERTS, cap, D_MODEL), lambda b: (0, b, 0)),
                  sel_blk, sel_blk, tok_blk,
                  _mod_spec(layer, stream, 5, seq),
                  pl.BlockSpec((1, D_MODEL), lambda b: (0, 0))],
        out_specs=tok_blk,
        out_shape=jax.ShapeDtypeStruct((stream.tokens, D_MODEL), F32),
        compiler_params=_params(1),
        name=f"scatter_l{layer}_s{seq}",
    )(y, pos, g, xn, mods, final_g_row)


def _axial_rope_tables(rows, head_dim):
    row = jnp.repeat(jnp.arange(rows, dtype=F32), GRID_W)
    col = jnp.tile(jnp.arange(GRID_W, dtype=F32), rows)
    nf = head_dim // 4
    inv = ROPE_BASE ** (-jnp.arange(nf, dtype=F32) / nf)
    ar = row[:, None] * inv[None]
    ac = col[:, None] * inv[None]
    ang = jnp.concatenate([ar, ar, ac, ac], axis=-1)
    return jnp.cos(ang), jnp.sin(ang)


def _block_diag(w):
    eye = jnp.eye(LRU_BLOCKS, dtype=w.dtype)
    return jnp.einsum('dnkj,nm->dnkmj', w, eye).reshape(2, LRU_WIDTH, LRU_WIDTH)


def kernel(x_prompt, x_sample, cache_win_k, cache_win_v, state_lru, cache_diff_k, cache_diff_v, c, c_ctx, ada_w, ada_b, norm_g, final_g, even_w_in, even_w_out, conv_w, conv_b, lru_wa, lru_ba, lru_wx, lru_bx, lru_lambda, win_sink, odd_w_in, odd_w_out, diff_lq1, diff_lk1, diff_lq2, diff_lk2, diff_subln_g, moe_router, moe_w_gate, moe_w_up, moe_w_down):
    cv_t = jnp.concatenate([c_ctx[None], c, jnp.zeros((COND_ROWS - N_COND, D_MODEL), F32)], axis=0).T
    mods = _adaln(cv_t, ada_w, ada_b).reshape(DEPTH * COND_ROWS, 1, 6 * D_MODEL)

    cos, sin = _axial_rope_tables(DEC_SEQ // GRID_W, WIN_HEAD_DIM)
    rope_win = (jnp.tile(cos, (1, WIN_HEADS)), jnp.tile(sin, (1, WIN_HEADS)))
    rope_diff = (jnp.tile(cos, (1, 2)), jnp.tile(sin, (1, 2)))

    streams = (CTX, LAT)
    xs = [x_prompt.reshape(CTX.tokens, D_MODEL), x_sample.reshape(LAT.tokens, D_MODEL)]
    final_g_row = final_g.reshape(1, D_MODEL)
    outs = {}

    for layer in range(DEPTH):
        idx = layer // 2
        even = layer % 2 == 0
        w_in = (even_w_in if even else odd_w_in)[idx].astype(BF16)
        w_out = (even_w_out if even else odd_w_out)[idx].astype(BF16)
        wr_t = moe_router[layer].T
        g1_row = norm_g[layer, 0].reshape(1, D_MODEL)
        g2_row = norm_g[layer, 1].reshape(1, D_MODEL)
        if even:
            wa = _block_diag(lru_wa[idx]).astype(BF16)
            wx = _block_diag(lru_wx[idx]).astype(BF16)

        routed = []
        for si, stream in enumerate(streams):
            is_ctx = stream is CTX
            if even or not is_ctx:
                proj = _proj(xs[si], g1_row, mods, layer, stream, w_in)
            if even:
                h0 = jnp.zeros((stream.n_req, 2, LRU_WIDTH), F32) if is_ctx else state_lru[:, idx]
                y_lru, h_fin = _lru(proj, stream, conv_w[idx], conv_b[idx].reshape(1, LRU_WIDTH), wa, wx,
                                    lru_ba[idx], lru_bx[idx], lru_lambda[idx], h0)
                if is_ctx:
                    o = _win_attn(proj, stream, win_sink[idx])
                    k0 = 2 * LRU_WIDTH + WIN_Q
                    to_heads = lambda t: t.reshape(BATCH, SEQ, WIN_KV_HEADS, WIN_HEAD_DIM).transpose(0, 2, 1, 3)[:, None]
                    outs["win_k"] = to_heads(proj[:, k0:k0 + WIN_KV])
                    outs["win_v"] = to_heads(proj[:, k0 + WIN_KV:k0 + 2 * WIN_KV])
                    outs["lru"] = h_fin[:, None]
                else:
                    pack = lambda t: t[:, idx].transpose(0, 2, 1, 3).reshape(DEC_BATCH, PAST_LEN, WIN_KV)
                    o = _win_attn(proj, stream, win_sink[idx],
                                  cache=(pack(cache_win_k), pack(cache_win_v)), rope=rope_win)
                mixed = [y_lru, o]
            else:
                lambda_init = 0.8 - 0.6 * math.exp(-0.3 * layer)
                vec = lambda t: t[idx].reshape(1, DIFF_HEAD_DIM)
                args = (vec(diff_lq1), vec(diff_lk1), vec(diff_lq2), vec(diff_lk2), diff_subln_g[idx])
                if is_ctx:
                    q, kh, vh = _proj_heads(xs[si], g1_row, mods, layer, stream, w_in)
                    o = _diff_attn((q, kh, vh), stream, lambda_init, *args)
                    outs["diff_k"] = kh[:, None]
                    outs["diff_v"] = vh[:, None]
                else:
                    o = _diff_attn(proj, stream, lambda_init, *args,
                                   cache=(cache_diff_k[:, idx:idx + 1], cache_diff_v[:, idx:idx + 1]), rope=rope_diff)
                mixed = [o]
            xn, h2, lg = _mix_out(mixed, w_out, xs[si], g2_row, mods, layer, stream, wr_t)
            per_req = stream.seq // LOGIT_TILE
            lg = lg.reshape(stream.n_req, per_req, N_EXPERTS, LOGIT_TILE).transpose(0, 2, 1, 3)
            pos, gate = _router(lg.reshape(stream.n_req, N_EXPERTS, stream.seq), stream)
            routed.append((xn, pos, gate, _gather(h2, pos, stream)))

        ys = _ffn(routed[0][3], routed[1][3], layer, moe_w_gate, moe_w_up, moe_w_down)
        final = layer == DEPTH - 1
        for si, stream in enumerate(streams):
            xn, pos, gate, _ = routed[si]
            xs[si] = _scatter(ys[si], pos, gate, xn, mods, layer, stream, final, final_g_row)

    y_prompt = xs[0].reshape(BATCH, SEQ, D_MODEL)
    y_sample = xs[1].reshape(DEC_BATCH, DEC_SEQ, D_MODEL)
    return (y_prompt, y_sample, outs["win_k"], outs["win_v"], outs["lru"], outs["diff_k"], outs["diff_v"])
```

```python
import functools
import math

import jax
import jax.numpy as jnp
from jax import lax
from jax.experimental import pallas as pl
from jax.experimental.pallas import tpu as pltpu

F32 = jnp.float32
BF16 = jnp.bfloat16
I32 = jnp.int32

D_MODEL = 1024
BATCH = 16
SEQ = 256
DEPTH = 2
DEC_BATCH = 2
DEC_SEQ = 1024
PAST_LEN = 512
GRID_W = 64
LRU_WIDTH = D_MODEL // 2
LRU_BLOCKS = 8
LRU_BLOCK = LRU_WIDTH // LRU_BLOCKS
CONV_W = 4
LRU_C = 8.0
WIN_HEADS = 8
WIN_KV_HEADS = 2
WIN_REP = WIN_HEADS // WIN_KV_HEADS
WIN_HEAD_DIM = 64
WINDOW = 128
WIN_Q = WIN_HEADS * WIN_HEAD_DIM
WIN_KV = WIN_KV_HEADS * WIN_HEAD_DIM
EVEN_IN = 2 * LRU_WIDTH + WIN_Q + 2 * WIN_KV
DIFF_HEADS = 8
DIFF_HEAD_DIM = 64
DIFF_QK = DIFF_HEADS * 2 * DIFF_HEAD_DIM
DIFF_V = DIFF_HEADS * 2 * DIFF_HEAD_DIM
ODD_IN = 2 * DIFF_QK + DIFF_V
N_EXPERTS = 16
EXPERT_FF = 2 * D_MODEL
CAPACITY_FACTOR = 2
ROPE_BASE = 10000.0
EPS = 1e-6
NEG_INF = -1e30

LANES = 128
SUBLANES = 8
VMEM_LIMIT_BYTES = 48 * 1024 * 1024

N_COND = 1 + DEC_BATCH
COND_ROWS = SUBLANES
TOKEN_TILE = 512
MIX_TILE = 1024
ROUTE_ROWS = 1024
LOGIT_TILE = 256
FF_TILE = 1024
FFN_VMEM_LIMIT_BYTES = 56 * 1024 * 1024
LRU_CHUNK = 128
ATT_Q_TILE = 256
ATT_K_TILE = 256


class Stream:
    def __init__(self, n_req, seq, cond0, cond_step):
        self.n_req, self.seq, self.cond0, self.cond_step = n_req, seq, cond0, cond_step
        self.tokens = n_req * seq
        self.cap = CAPACITY_FACTOR * seq // N_EXPERTS
        self.req_per_step = ROUTE_ROWS // seq

    def cond_of_row(self, row):
        return self.cond0 + self.cond_step * (row // self.seq)


CTX = Stream(BATCH, SEQ, 0, 0)
LAT = Stream(DEC_BATCH, DEC_SEQ, 1, 1)


def _params(n_axes, vmem_limit_bytes=VMEM_LIMIT_BYTES):
    return pltpu.CompilerParams(dimension_semantics=("arbitrary",) * n_axes,
                                vmem_limit_bytes=vmem_limit_bytes)


def _mod_spec(layer, stream, k, rows_per_step):
    return pl.BlockSpec(
        (1, 1, D_MODEL),
        lambda i, *_: (layer * COND_ROWS + stream.cond_of_row(i * rows_per_step), 0, k))


def _dot(a, b):
    return jnp.dot(a, b, preferred_element_type=F32)


def _dot_nt(a, b):
    return lax.dot_general(a, b, (((1,), (1,)), ((), ())), preferred_element_type=F32)


def _modnorm(x, g, shift, scale):
    y = x * lax.rsqrt(jnp.mean(x * x, axis=-1, keepdims=True) + EPS)
    return (y * g) * (1.0 + scale) + shift


def _lane_half_masks(shape):
    lane = lax.broadcasted_iota(I32, shape, len(shape) - 1)
    left = (lane & (LANES - 1)) < LANES // 2
    return left, jnp.logical_not(left)


def _rope(x, cos, sin):
    parts = []
    for c in range(x.shape[1] // LANES):
        xs = x[:, c * LANES:(c + 1) * LANES]
        lane = lax.broadcasted_iota(I32, xs.shape, 1)
        first = (lane & 31) < 16
        rot = jnp.where(first, -pltpu.roll(xs, LANES - 16, axis=1), pltpu.roll(xs, 16, axis=1))
        parts.append(xs * cos[:, c * LANES:(c + 1) * LANES] + rot * sin[:, c * LANES:(c + 1) * LANES])
    return parts[0] if len(parts) == 1 else jnp.concatenate(parts, axis=1)


def _adaln_kernel(cv_ref, w_ref, b_ref, o_ref):
    cv = cv_ref[...]
    s = cv * jax.nn.sigmoid(cv)
    w = w_ref[0]
    ridx = lax.broadcasted_iota(I32, (COND_ROWS, w.shape[1]), 0)
    out = jnp.zeros((COND_ROWS, w.shape[1]), F32)
    for r in range(N_COND):
        out = jnp.where(ridx == r, jnp.sum(w * s[:, r:r + 1], axis=0, keepdims=True), out)
    o_ref[0] = out + b_ref[0]


def _adaln(cv_t, ada_w, ada_b):
    tn = 1024
    return pl.pallas_call(
        _adaln_kernel,
        grid=(DEPTH, 6 * D_MODEL // tn),
        in_specs=[pl.BlockSpec((D_MODEL, COND_ROWS), lambda l, j: (0, 0)),
                  pl.BlockSpec((1, D_MODEL, tn), lambda l, j: (l, 0, j)),
                  pl.BlockSpec((1, 1, tn), lambda l, j: (l, 0, j))],
        out_specs=pl.BlockSpec((1, COND_ROWS, tn), lambda l, j: (l, 0, j)),
        out_shape=jax.ShapeDtypeStruct((DEPTH, COND_ROWS, 6 * D_MODEL), F32),
        compiler_params=_params(2),
        name="adaln",
    )(cv_t, ada_w, ada_b.reshape(DEPTH, 1, 6 * D_MODEL))


def _proj_kernel(x_ref, g_ref, sh_ref, sc_ref, w_ref, o_ref):
    h = _modnorm(x_ref[...], g_ref[...], sh_ref[0], sc_ref[0])
    o_ref[...] = _dot(h.astype(BF16), w_ref[...])


def _proj(x, g_row, mods, layer, stream, w):
    n_out = w.shape[1]
    tm = TOKEN_TILE
    return pl.pallas_call(
        _proj_kernel,
        grid=(stream.tokens // tm,),
        in_specs=[pl.BlockSpec((tm, D_MODEL), lambda i: (i, 0)),
                  pl.BlockSpec((1, D_MODEL), lambda i: (0, 0)),
                  _mod_spec(layer, stream, 0, tm),
                  _mod_spec(layer, stream, 1, tm),
                  pl.BlockSpec((D_MODEL, n_out), lambda i: (0, 0))],
        out_specs=pl.BlockSpec((tm, n_out), lambda i: (i, 0)),
        out_shape=jax.ShapeDtypeStruct((stream.tokens, n_out), F32),
        compiler_params=_params(1),
        name=f"proj_l{layer}_s{stream.seq}",
    )(x, g_row, mods, mods, w)


def _proj_heads_kernel(seq, x_ref, g_ref, sh_ref, sc_ref, w_ref, q_ref, k_ref, v_ref):
    h = _modnorm(x_ref[...], g_ref[...], sh_ref[0], sc_ref[0])
    res = _dot(h.astype(BF16), w_ref[...])
    q_ref[...] = res[:, 0:DIFF_QK]
    for r in range(x_ref.shape[0] // seq):
        rows = slice(r * seq, (r + 1) * seq)
        for hh in range(DIFF_HEADS):
            k_ref[r, hh] = res[rows, DIFF_QK + hh * LANES:DIFF_QK + (hh + 1) * LANES]
            v_ref[r, hh] = res[rows, 2 * DIFF_QK + hh * LANES:2 * DIFF_QK + (hh + 1) * LANES]


def _proj_heads(x, g_row, mods, layer, stream, w):
    tm, seq = TOKEN_TILE, stream.seq
    head_shape = (stream.n_req, DIFF_HEADS, seq, 2 * DIFF_HEAD_DIM)
    head_spec = pl.BlockSpec((tm // seq, DIFF_HEADS, seq, 2 * DIFF_HEAD_DIM), lambda i: (i, 0, 0, 0))
    return pl.pallas_call(
        functools.partial(_proj_heads_kernel, seq),
        grid=(stream.tokens // tm,),
        in_specs=[pl.BlockSpec((tm, D_MODEL), lambda i: (i, 0)),
                  pl.BlockSpec((1, D_MODEL), lambda i: (0, 0)),
                  _mod_spec(layer, stream, 0, tm),
                  _mod_spec(layer, stream, 1, tm),
                  pl.BlockSpec((D_MODEL, ODD_IN), lambda i: (0, 0))],
        out_specs=[pl.BlockSpec((tm, DIFF_QK), lambda i: (i, 0)), head_spec, head_spec],
        out_shape=[jax.ShapeDtypeStruct((stream.tokens, DIFF_QK), F32),
                   jax.ShapeDtypeStruct(head_shape, F32), jax.ShapeDtypeStruct(head_shape, F32)],
        compiler_params=_params(1),
        name=f"proj_heads_l{layer}_s{seq}",
    )(x, g_row, mods, mods, w)


def _lru_kernel(seq, xl_ref, gl_ref, cw_ref, cb_ref, wa_ref, wx_ref, ba_ref, bx_ref, lam_ref, h0_ref,
                y_ref, hfin_ref, xpad, af, bf, ab, bb):
    width = LRU_WIDTH
    ch = LRU_CHUNK
    halo = SUBLANES
    xpad[0:halo, :] = jnp.zeros((halo, width), F32)
    xpad[halo + seq:2 * halo + seq, :] = jnp.zeros((halo, width), F32)
    xpad[halo:halo + seq, :] = xl_ref[...]

    lam = lam_ref[...]
    z = -lam
    softplus = jnp.maximum(z, 0.0) + jnp.log1p(jnp.exp(-jnp.abs(z)))
    sub = lax.broadcasted_iota(I32, (ch // SUBLANES, SUBLANES, width), 1)
    cw = cw_ref[...]
    cb = cb_ref[...]

    def gates_chunk(c, carry):
        r0 = pl.multiple_of(c * ch, ch)
        win = xpad[pl.ds(r0, ch + 2 * halo), :]
        n_win = ch + 2 * halo

        def tap(j):
            return pltpu.roll(win, n_win - (halo - 2 + j), axis=0)[0:ch]

        xc = tap(0) * cw[0:1]
        for j in range(1, CONV_W):
            xc = xc + tap(j) * cw[j:j + 1]
        xc = xc + cb
        xcb = xc.astype(BF16)
        for d, (a_s, b_s) in enumerate(((af, bf), (ab, bb))):
            r = jax.nn.sigmoid(_dot(xcb, wa_ref[d]) + ba_ref[d:d + 1])
            ig = jax.nn.sigmoid(_dot(xcb, wx_ref[d]) + bx_ref[d:d + 1])
            log_a = (-LRU_C * r) * softplus[d:d + 1]
            a = jnp.exp(log_a)
            b = jnp.sqrt(jnp.tanh(-log_a) * (a * a + 1.0)) * (ig * xc)
            a = a.reshape(ch // SUBLANES, SUBLANES, width)
            b = b.reshape(ch // SUBLANES, SUBLANES, width)
            for s in (1, 2, 4):
                keep = (sub >= s) if d == 0 else (sub < SUBLANES - s)
                shift = s if d == 0 else SUBLANES - s
                a_sh = jnp.where(keep, pltpu.roll(a, shift, axis=1), 1.0)
                b_sh = jnp.where(keep, pltpu.roll(b, shift, axis=1), 0.0)
                b = a * b_sh + b
                a = a * a_sh
            a_s[pl.ds(r0, ch), :] = a.reshape(ch, width)
            b_s[pl.ds(r0, ch), :] = b.reshape(ch, width)
        return carry

    lax.fori_loop(0, seq // ch, gates_chunk, 0)

    n_tiles = seq // SUBLANES
    h0 = h0_ref[0]

    def tile_step(k, carry):
        cf, cbw = carry
        rf = pl.multiple_of(k * SUBLANES, SUBLANES)
        rb = pl.multiple_of((n_tiles - 1 - k) * SUBLANES, SUBLANES)
        hf = af[pl.ds(rf, SUBLANES), :] * cf + bf[pl.ds(rf, SUBLANES), :]
        bf[pl.ds(rf, SUBLANES), :] = hf
        hb = ab[pl.ds(rb, SUBLANES), :] * cbw + bb[pl.ds(rb, SUBLANES), :]
        bb[pl.ds(rb, SUBLANES), :] = hb
        return hf[SUBLANES - 1:SUBLANES], hb[0:1]

    cf, cbw = lax.fori_loop(0, n_tiles, tile_step, (h0[0:1], h0[1:2]))
    hfin_ref[0, 0:1, :] = cf
    hfin_ref[0, 1:2, :] = cbw

    def out_chunk(c, carry):
        r0 = pl.multiple_of(c * ch, ch)
        hsum = bf[pl.ds(r0, ch), :] + bb[pl.ds(r0, ch), :]
        y_ref[pl.ds(r0, ch), :] = (hsum * jax.nn.gelu(gl_ref[pl.ds(r0, ch), :])).astype(BF16)
        return carry

    lax.fori_loop(0, seq // ch, out_chunk, 0)


def _lru(proj, stream, conv_w, conv_b, wa, wx, ba, bx, lam, h0):
    seq, width = stream.seq, LRU_WIDTH
    full2 = lambda b: (0, 0)
    full3 = lambda b: (0, 0, 0)
    return pl.pallas_call(
        functools.partial(_lru_kernel, seq),
        grid=(stream.n_req,),
        in_specs=[pl.BlockSpec((seq, width), lambda b: (b, 0)),
                  pl.BlockSpec((seq, width), lambda b: (b, 1)),
                  pl.BlockSpec((CONV_W, width), full2),
                  pl.BlockSpec((1, width), full2),
                  pl.BlockSpec((2, width, width), full3),
                  pl.BlockSpec((2, width, width), full3),
                  pl.BlockSpec((2, width), full2),
                  pl.BlockSpec((2, width), full2),
                  pl.BlockSpec((2, width), full2),
                  pl.BlockSpec((1, 2, width), lambda b: (b, 0, 0))],
        out_specs=[pl.BlockSpec((seq, width), lambda b: (b, 0)),
                   pl.BlockSpec((1, 2, width), lambda b: (b, 0, 0))],
        out_shape=[jax.ShapeDtypeStruct((stream.tokens, width), BF16),
                   jax.ShapeDtypeStruct((stream.n_req, 2, width), F32)],
        scratch_shapes=[pltpu.VMEM((seq + 2 * SUBLANES, width), F32)] + [pltpu.VMEM((seq, width), F32)] * 4,
        compiler_params=_params(1),
        name=f"lru_s{seq}",
    )(proj, proj, conv_w, conv_b, wa, wx, ba, bx, lam, h0)


def _attend(q, chunks, s_ref):
    tile_max = None
    spans = []
    off = 0
    for keys, _, valid in chunks:
        s = _dot_nt(q, keys())
        if valid is not None:
            s = jnp.where(valid, s, NEG_INF)
        n = s.shape[1]
        s_ref[:, off:off + n] = s
        for c in range(n // LANES):
            t = s[:, c * LANES:(c + 1) * LANES]
            tile_max = t if tile_max is None else jnp.maximum(tile_max, t)
        spans.append((off, n))
        off += n
    m = jnp.max(tile_max, axis=-1, keepdims=True)
    acc = None
    for (_, values, _), (o, n) in zip(chunks, spans):
        part = _dot(jnp.exp(s_ref[:, o:o + n] - m).astype(BF16), values())
        acc = part if acc is None else acc + part
    return acc, m


def _split_groups(kk):
    left, right = _lane_half_masks(kk.shape)
    g0_l = jnp.where(left, kk, 0.0)
    g1_r = jnp.where(right, kk, 0.0)
    return ((g0_l, pltpu.roll(g0_l, LANES // 2, axis=1)), (pltpu.roll(g1_r, LANES // 2, axis=1), g1_r))


def _win_ctx_kernel(sink_ref, q_ref, kv_ref, o_ref):
    scale = WIN_HEAD_DIM ** -0.5
    ks = _split_groups(kv_ref[:, 0:LANES])
    vs = _split_groups(kv_ref[:, LANES:2 * LANES])
    for pair in range(WIN_HEADS // 2):
        g = pair // (WIN_REP // 2)
        qp = (q_ref[:, pair * LANES:(pair + 1) * LANES] * scale).astype(BF16)
        acc = None
        for side in range(2):
            sk = sink_ref[2 * pair + side]
            s = _dot_nt(qp, ks[g][side].astype(BF16))
            m = jnp.maximum(jnp.max(s, axis=-1, keepdims=True), sk)
            e = jnp.exp(s - m)
            den = jnp.sum(e, axis=-1, keepdims=True) + jnp.exp(sk - m)
            o = _dot(e.astype(BF16), vs[g][side].astype(BF16)) * (1.0 / den)
            acc = o if acc is None else acc + o
        o_ref[:, pair * LANES:(pair + 1) * LANES] = acc.astype(BF16)


def _win_lat_kernel(sink_ref, q_ref, kv_ref, ck_ref, cv_ref, cos_ref, sin_ref, o_ref,
                    kl_s, vl_s, kc_s, vc_s):
    seq, wn = DEC_SEQ, WINDOW
    scale = WIN_HEAD_DIM ** -0.5
    kr = _rope(kv_ref[:, 0:LANES], cos_ref[:, 0:LANES], sin_ref[:, 0:LANES])
    for src, dst in ((_split_groups(kr), kl_s), (_split_groups(kv_ref[:, LANES:2 * LANES]), vl_s),
                     (_split_groups(ck_ref[0]), kc_s), (_split_groups(cv_ref[0]), vc_s)):
        for g in range(WIN_KV_HEADS):
            for side in range(2):
                dst[2 * g + side] = src[g][side].astype(BF16)

    def q_block(i, carry):
        r0 = pl.multiple_of(i * wn, wn)
        start = pl.multiple_of(jnp.clip((i - 1) * wn, 0, seq - 3 * wn), wn)
        qr = _rope(q_ref[pl.ds(r0, wn), :], cos_ref[pl.ds(r0, wn), :], sin_ref[pl.ds(r0, wn), :])
        qpos = r0 + lax.broadcasted_iota(I32, (wn, 3 * wn), 0)
        kpos = start + lax.broadcasted_iota(I32, (wn, 3 * wn), 1)
        valid = jnp.abs(qpos - kpos) <= wn
        for pair in range(WIN_HEADS // 2):
            g = pair // (WIN_REP // 2)
            qp = (qr[:, pair * LANES:(pair + 1) * LANES] * scale).astype(BF16)
            acc = None
            for side in range(2):
                idx = 2 * g + side
                sk = sink_ref[2 * pair + side]
                sl = _dot_nt(qp, kl_s[idx, pl.ds(start, 3 * wn), :])
                sl = jnp.where(valid, sl, NEG_INF)
                sc = _dot_nt(qp, kc_s[idx])
                m = jnp.maximum(jnp.maximum(jnp.max(sl, axis=-1, keepdims=True),
                                            jnp.max(sc, axis=-1, keepdims=True)), sk)
                el = jnp.exp(sl - m)
                ec = jnp.exp(sc - m)
                den = (jnp.sum(el, axis=-1, keepdims=True) + jnp.sum(ec, axis=-1, keepdims=True)
                       + jnp.exp(sk - m))
                o = (_dot(el.astype(BF16), vl_s[idx, pl.ds(start, 3 * wn), :])
                     + _dot(ec.astype(BF16), vc_s[idx])) * (1.0 / den)
                acc = o if acc is None else acc + o
            o_ref[pl.ds(r0, wn), pair * LANES:(pair + 1) * LANES] = acc.astype(BF16)
        return carry

    lax.fori_loop(0, seq // wn, q_block, 0)


def _win_attn(proj, stream, sink, cache=None, rope=None):
    seq = stream.seq
    q_spec = pl.BlockSpec((seq, WIN_Q), lambda b: (b, 2 * LRU_WIDTH // WIN_Q))
    kv_spec = pl.BlockSpec((seq, 2 * WIN_KV), lambda b: (b, (2 * LRU_WIDTH + WIN_Q) // (2 * WIN_KV)))
    sink_spec = pl.BlockSpec(memory_space=pltpu.SMEM)
    out_spec = pl.BlockSpec((seq, WIN_Q), lambda b: (b, 0))
    out_shape = jax.ShapeDtypeStruct((stream.tokens, WIN_Q), BF16)
    if cache is None:
        return pl.pallas_call(
            _win_ctx_kernel, grid=(stream.n_req,),
            in_specs=[sink_spec, q_spec, kv_spec], out_specs=out_spec, out_shape=out_shape,
            compiler_params=_params(1), name="win_attn_ctx",
        )(sink, proj, proj)
    ck, cv = cache
    cos, sin = rope
    cache_spec = pl.BlockSpec((1, PAST_LEN, LANES), lambda b: (b, 0, 0))
    table_spec = pl.BlockSpec((seq, WIN_Q), lambda b: (0, 0))
    return pl.pallas_call(
        _win_lat_kernel, grid=(stream.n_req,),
        in_specs=[sink_spec, q_spec, kv_spec, cache_spec, cache_spec, table_spec, table_spec],
        out_specs=out_spec, out_shape=out_shape,
        scratch_shapes=[pltpu.VMEM((4, seq, LANES), BF16), pltpu.VMEM((4, seq, LANES), BF16),
                        pltpu.VMEM((4, PAST_LEN, LANES), BF16), pltpu.VMEM((4, PAST_LEN, LANES), BF16)],
        compiler_params=_params(1), name="win_attn_lat",
    )(sink, proj, proj, ck, cv, cos, sin)


def _diff_lambda(lq1_ref, lk1_ref, lq2_ref, lk2_ref, lambda_init):
    t1 = jnp.sum(lq1_ref[...] * lk1_ref[...], axis=-1, keepdims=True)
    t2 = jnp.sum(lq2_ref[...] * lk2_ref[...], axis=-1, keepdims=True)
    return jnp.exp(t1) - jnp.exp(t2) + lambda_init


def _subln(o, g_row, lambda_init):
    o = o * lax.rsqrt(jnp.mean(o * o, axis=-1, keepdims=True) + EPS) * g_row
    return o * (1.0 - lambda_init)


def _component_keys(k):
    left, right = _lane_half_masks(k.shape)
    return jnp.where(left, k, 0.0).astype(BF16), jnp.where(right, k, 0.0).astype(BF16)


def _values_with_ones(v):
    return jnp.concatenate([v.astype(BF16), jnp.ones(v.shape, BF16)], axis=1)


def _diff_combine(accs, lam):
    o1, o2 = accs[0][:, 0:LANES], accs[1][:, 0:LANES]
    return o1 * (1.0 / accs[0][:, LANES:]) - o2 * (lam * (1.0 / accs[1][:, LANES:]))


def _diff_ctx_kernel(lambda_init, lq1_ref, lk1_ref, lq2_ref, lk2_ref, sg_ref, q_ref, k_ref, v_ref, o_ref, s_ref):
    scale = DIFF_HEAD_DIM ** -0.5
    tq = ATT_Q_TILE
    lam = _diff_lambda(lq1_ref, lk1_ref, lq2_ref, lk2_ref, lambda_init)
    for h in range(DIFF_HEADS):
        cols = slice(h * LANES, (h + 1) * LANES)
        keys = _component_keys(k_ref[0, h])
        vals = _values_with_ones(v_ref[0, h])
        for qt in range(q_ref.shape[0] // tq):
            rows = slice(qt * tq, (qt + 1) * tq)
            q = (q_ref[rows, cols] * scale).astype(BF16)
            slots = [s_ref.at[(4 * h + 2 * qt + c) % s_ref.shape[0]] for c in range(2)]
            accs = [_attend(q, [(lambda: keys[c], lambda: vals, None)], slots[c])[0] for c in range(2)]
            o = _diff_combine(accs, lam)
            o_ref[rows, cols] = _subln(o, sg_ref[h:h + 1, :], lambda_init).astype(BF16)


def _diff_lat_kernel(lambda_init, lq1_ref, lk1_ref, lq2_ref, lk2_ref, sg_ref, q_ref, k_ref, v_ref,
                     ck_ref, cv_ref, cos_ref, sin_ref, o_ref, kl_s, kc_s, vl_s, vc_s, s_ref):
    seq, tq, tk = DEC_SEQ, ATT_Q_TILE, ATT_K_TILE
    scale = DIFF_HEAD_DIM ** -0.5
    lam = _diff_lambda(lq1_ref, lk1_ref, lq2_ref, lk2_ref, lambda_init)
    for src, dst in ((_rope(k_ref[...], cos_ref[...], sin_ref[...]), kl_s), (ck_ref[0, 0, 0], kc_s)):
        dst[0], dst[1] = _component_keys(src)
    vl_s[...] = _values_with_ones(v_ref[...])
    vc_s[...] = _values_with_ones(cv_ref[0, 0, 0])
    sg = sg_ref[0]

    def q_block(i, carry):
        for sub in range(2):
            r0 = pl.multiple_of((2 * i + sub) * tq, tq)
            qr = _rope(q_ref[pl.ds(r0, tq), :], cos_ref[pl.ds(r0, tq), :], sin_ref[pl.ds(r0, tq), :])
            qr = (qr * scale).astype(BF16)
            accs = []
            for comp in range(2):
                chunks = [(lambda j=j: kl_s[comp, j * tk:(j + 1) * tk, :],
                           lambda j=j: vl_s[j * tk:(j + 1) * tk, :], None) for j in range(seq // tk)]
                chunks += [(lambda j=j: kc_s[comp, j * tk:(j + 1) * tk, :],
                            lambda j=j: vc_s[j * tk:(j + 1) * tk, :], None) for j in range(PAST_LEN // tk)]
                accs.append(_attend(qr, chunks, s_ref.at[2 * sub + comp])[0])
            o_ref[pl.ds(r0, tq), :] = _subln(_diff_combine(accs, lam), sg, lambda_init).astype(BF16)
        return carry

    lax.fori_loop(0, seq // (2 * tq), q_block, 0)


def _diff_attn(proj, stream, lambda_init, lq1, lk1, lq2, lk2, subln_g, cache=None, rope=None):
    seq = stream.seq
    vec_spec = lambda nd: pl.BlockSpec((1, DIFF_HEAD_DIM), lambda *_: (0, 0))
    out_shape = jax.ShapeDtypeStruct((stream.tokens, DIFF_V), BF16)
    if cache is None:
        q, kh, vh = proj
        head_spec = pl.BlockSpec((1, DIFF_HEADS, seq, LANES), lambda b: (b, 0, 0, 0))
        return pl.pallas_call(
            functools.partial(_diff_ctx_kernel, lambda_init), grid=(stream.n_req,),
            in_specs=[vec_spec(1)] * 4 + [pl.BlockSpec((DIFF_HEADS, LANES), lambda b: (0, 0)),
                                          pl.BlockSpec((seq, DIFF_QK), lambda b: (b, 0)), head_spec, head_spec],
            out_specs=pl.BlockSpec((seq, DIFF_V), lambda b: (b, 0)), out_shape=out_shape,
            scratch_shapes=[pltpu.VMEM((8, ATT_Q_TILE, seq), F32)],
            compiler_params=_params(1), name="diff_attn_ctx",
        )(lq1, lk1, lq2, lk2, subln_g, q, kh, vh)
    ck, cv = cache
    cos, sin = rope
    blk = lambda c: pl.BlockSpec((seq, LANES), lambda b, h: (b, c * DIFF_HEADS + h))
    cache_spec = pl.BlockSpec((1, 1, 1, PAST_LEN, LANES), lambda b, h: (b, 0, h, 0, 0))
    table_spec = pl.BlockSpec((seq, LANES), lambda b, h: (0, 0))
    return pl.pallas_call(
        functools.partial(_diff_lat_kernel, lambda_init), grid=(stream.n_req, DIFF_HEADS),
        in_specs=[vec_spec(2)] * 4 + [pl.BlockSpec((1, 1, LANES), lambda b, h: (h, 0, 0)),
                                      blk(0), blk(1), blk(2), cache_spec, cache_spec, table_spec, table_spec],
        out_specs=pl.BlockSpec((seq, LANES), lambda b, h: (b, h)), out_shape=out_shape,
        scratch_shapes=[pltpu.VMEM((2, seq, LANES), BF16), pltpu.VMEM((2, PAST_LEN, LANES), BF16),
                        pltpu.VMEM((seq, 2 * LANES), BF16), pltpu.VMEM((PAST_LEN, 2 * LANES), BF16),
                        pltpu.VMEM((4, ATT_Q_TILE, seq + PAST_LEN), F32)],
        compiler_params=_params(2), name="diff_attn_lat",
    )(lq1, lk1, lq2, lk2, subln_g.reshape(DIFF_HEADS, 1, LANES), proj, proj, proj, ck, cv, cos, sin)


def _mix_out_kernel(n_in, *refs):
    a_refs = refs[:n_in]
    w_ref, x_ref, gate_ref, g2_ref, sh2_ref, sc2_ref, wr_ref, xn_ref, h2_ref, lg_ref = refs[n_in:]
    kp = D_MODEL // n_in
    acc = None
    for k, a_ref in enumerate(a_refs):
        part = _dot(a_ref[...], w_ref[k * kp:(k + 1) * kp, :])
        acc = part if acc is None else acc + part
    xn = x_ref[...] + gate_ref[0] * acc
    xn_ref[...] = xn
    h2 = _modnorm(xn, g2_ref[...], sh2_ref[0], sc2_ref[0])
    h_hi = h2.astype(BF16)
    h2_ref[...] = h_hi
    h_lo = (h2 - h_hi.astype(F32)).astype(BF16)
    wr = wr_ref[...]
    w_hi = wr.astype(BF16)
    w_lo = (wr - w_hi.astype(F32)).astype(BF16)
    by_hi = _dot_nt(jnp.concatenate([w_hi, w_lo], axis=0), h_hi)
    lg = by_hi[0:N_EXPERTS] + (by_hi[N_EXPERTS:] + _dot_nt(w_hi, h_lo))
    for c in range(lg.shape[1] // LOGIT_TILE):
        lg_ref[c] = lg[:, c * LOGIT_TILE:(c + 1) * LOGIT_TILE]


def _mix_out(mixed, w_out, x, g2_row, mods, layer, stream, wr_t):
    tm = MIX_TILE
    n_in = len(mixed)
    kp = D_MODEL // n_in
    row_blk = lambda width: pl.BlockSpec((tm, width), lambda i: (i, 0))
    return pl.pallas_call(
        functools.partial(_mix_out_kernel, n_in),
        grid=(stream.tokens // tm,),
        in_specs=[row_blk(kp)] * n_in + [
            pl.BlockSpec((D_MODEL, D_MODEL), lambda i: (0, 0)),
            row_blk(D_MODEL),
            _mod_spec(layer, stream, 2, tm),
            pl.BlockSpec((1, D_MODEL), lambda i: (0, 0)),
            _mod_spec(layer, stream, 3, tm),
            _mod_spec(layer, stream, 4, tm),
            pl.BlockSpec((N_EXPERTS, D_MODEL), lambda i: (0, 0))],
        out_specs=[row_blk(D_MODEL), row_blk(D_MODEL),
                   pl.BlockSpec((tm // LOGIT_TILE, N_EXPERTS, LOGIT_TILE), lambda i: (i, 0, 0))],
        out_shape=[jax.ShapeDtypeStruct((stream.tokens, D_MODEL), F32),
                   jax.ShapeDtypeStruct((stream.tokens, D_MODEL), BF16),
                   jax.ShapeDtypeStruct((stream.tokens // LOGIT_TILE, N_EXPERTS, LOGIT_TILE), F32)],
        compiler_params=_params(1),
        name=f"mix_out_l{layer}_s{stream.seq}",
    )(*mixed, w_out, x, mods, g2_row, mods, mods, wr_t)


def _sort_desc_lanes(x):
    rows, n = x.shape
    tiles = [x[:, c * LANES:(c + 1) * LANES] for c in range(n // LANES)]
    lane = lax.broadcasted_iota(I32, (rows, LANES), 1)
    k = 2
    while k <= n:
        j = k // 2
        while j >= 1:
            if j < LANES:
                lower = (lane & j) == 0
                for c in range(len(tiles)):
                    t = tiles[c]
                    partner = jnp.where(lower, pltpu.roll(t, LANES - j, axis=1), pltpu.roll(t, j, axis=1))
                    desc = ((lane & k) == 0) if k < LANES else (((c * LANES) & k) == 0)
                    take_max = (lower == desc) if k < LANES else (lower if desc else jnp.logical_not(lower))
                    tiles[c] = jnp.where(take_max, jnp.maximum(t, partner), jnp.minimum(t, partner))
            else:
                jc = j // LANES
                new = list(tiles)
                for c in range(len(tiles)):
                    take_max = ((c & jc) == 0) == (((c * LANES) & k) == 0)
                    new[c] = (jnp.maximum if take_max else jnp.minimum)(tiles[c], tiles[c ^ jc])
                tiles = new
            j //= 2
        k *= 2
    return tiles


def _router_kernel(cap, lg_ref, pos_ref, g_ref):
    x = lg_ref[...]
    n_b, n_e, n_tok = x.shape
    e = jnp.exp(x - jnp.max(x, axis=1, keepdims=True))
    aff = (e / jnp.sum(e, axis=1, keepdims=True)).reshape(n_b * n_e, n_tok)
    srt = _sort_desc_lanes(aff)
    lane_k = (cap - 1) % LANES
    thr = srt[(cap - 1) // LANES][:, lane_k:lane_k + 1]
    gt = aff > thr
    eq = aff == thr
    n_gt = jnp.sum(gt.astype(F32), axis=1, keepdims=True)
    before = (lax.broadcasted_iota(I32, (n_tok, n_tok), 0)
              < lax.broadcasted_iota(I32, (n_tok, n_tok), 1)).astype(BF16)
    eq_rank = _dot(eq.astype(BF16), before)
    sel = gt | (eq & (eq_rank < cap - n_gt))
    slot = _dot(sel.astype(BF16), before).astype(I32)
    pos_ref[...] = jnp.where(sel, slot, -1).reshape(n_b, n_e, n_tok)
    g_ref[...] = jnp.where(sel, aff, 0.0).reshape(n_b, n_e, n_tok)


def _router(logits, stream):
    shape = (stream.n_req, N_EXPERTS, stream.seq)
    spec = pl.BlockSpec(shape, lambda: (0, 0, 0))
    return pl.pallas_call(
        functools.partial(_router_kernel, stream.cap),
        in_specs=[spec], out_specs=[spec, spec],
        out_shape=[jax.ShapeDtypeStruct(shape, I32), jax.ShapeDtypeStruct(shape, F32)],
        compiler_params=pltpu.CompilerParams(vmem_limit_bytes=VMEM_LIMIT_BYTES),
        name=f"router_s{stream.seq}",
    )(logits)


def _gather_kernel(cap, seq, h_ref, pos_ref, xs_ref):
    slot = lax.broadcasted_iota(I32, (cap, seq), 0)
    for r in range(pos_ref.shape[0]):
        onehot = jnp.concatenate([(slot == pos_ref[r, e:e + 1, :]).astype(BF16) for e in range(N_EXPERTS)], axis=0)
        xs = _dot(onehot, h_ref[r * seq:(r + 1) * seq, :]).astype(BF16)
        for e in range(N_EXPERTS):
            xs_ref[e, r * cap:(r + 1) * cap, :] = xs[e * cap:(e + 1) * cap]


def _gather(h2, pos, stream):
    cap, seq, per = stream.cap, stream.seq, stream.req_per_step
    return pl.pallas_call(
        functools.partial(_gather_kernel, cap, seq),
        grid=(stream.n_req // per,),
        in_specs=[pl.BlockSpec((per * seq, D_MODEL), lambda i: (i, 0)),
                  pl.BlockSpec((per, N_EXPERTS, seq), lambda i: (i, 0, 0))],
        out_specs=pl.BlockSpec((N_EXPERTS, per * cap, D_MODEL), lambda i: (0, i, 0)),
        out_shape=jax.ShapeDtypeStruct((N_EXPERTS, stream.n_req * cap, D_MODEL), BF16),
        compiler_params=_params(1),
        name=f"gather_s{seq}",
    )(h2, pos)


def _ffn_kernel(xa_ref, xb_ref, wg_ref, wu_ref, wd_ref, ya_ref, yb_ref, acc):
    j = pl.program_id(1)
    rows_a = xa_ref.shape[1]

    @pl.when(j == 0)
    def _():
        acc[...] = jnp.zeros_like(acc)

    x = jnp.concatenate([xa_ref[0], xb_ref[0]], axis=0)
    a = _dot(x, wg_ref[0, 0].astype(BF16))
    u = _dot(x, wu_ref[0, 0].astype(BF16))
    acc[...] += _dot(((a * jax.nn.sigmoid(a)) * u).astype(BF16), wd_ref[0, 0].astype(BF16))

    @pl.when(j == pl.num_programs(1) - 1)
    def _():
        ya_ref[0] = acc[0:rows_a, :].astype(BF16)
        yb_ref[0] = acc[rows_a:, :].astype(BF16)


def _ffn(xs_a, xs_b, layer, w_gate, w_up, w_down):
    tf = FF_TILE
    rows_a, rows_b = xs_a.shape[1], xs_b.shape[1]
    x_spec = lambda rows: pl.BlockSpec((1, rows, D_MODEL), lambda e, j: (e, 0, 0))
    return pl.pallas_call(
        _ffn_kernel,
        grid=(N_EXPERTS, EXPERT_FF // tf),
        in_specs=[x_spec(rows_a), x_spec(rows_b),
                  pl.BlockSpec((1, 1, D_MODEL, tf), lambda e, j: (layer, e, 0, j)),
                  pl.BlockSpec((1, 1, D_MODEL, tf), lambda e, j: (layer, e, 0, j)),
                  pl.BlockSpec((1, 1, tf, D_MODEL), lambda e, j: (layer, e, j, 0))],
        out_specs=[x_spec(rows_a), x_spec(rows_b)],
        out_shape=[jax.ShapeDtypeStruct(xs_a.shape, BF16), jax.ShapeDtypeStruct(xs_b.shape, BF16)],
        scratch_shapes=[pltpu.VMEM((rows_a + rows_b, D_MODEL), F32)],
        compiler_params=_params(2, FFN_VMEM_LIMIT_BYTES),
        name=f"ffn_l{layer}",
    )(xs_a, xs_b, w_gate, w_up, w_down)


def _scatter_kernel(cap, seq, final, y_ref, pos_ref, g_ref, x_ref, gate_ref, fg_ref, o_ref):
    slot = lax.broadcasted_iota(I32, (cap, seq), 0)
    for r in range(pos_ref.shape[0]):
        onehots, gated = [], []
        for e in range(N_EXPERTS):
            hit = slot == pos_ref[r, e:e + 1, :]
            gate = jnp.sum(jnp.where(hit, g_ref[r, e:e + 1, :], 0.0), axis=1, keepdims=True)
            gated.append((y_ref[e, r * cap:(r + 1) * cap, :].astype(F32) * gate).astype(BF16))
            onehots.append(hit.astype(BF16))
        moe = lax.dot_general(jnp.concatenate(onehots, axis=0), jnp.concatenate(gated, axis=0),
                              (((0,), (0,)), ((), ())), preferred_element_type=F32)
        rows = slice(r * seq, (r + 1) * seq)
        xn = x_ref[rows, :] + gate_ref[0] * moe
        if final:
            xn = xn * lax.rsqrt(jnp.mean(xn * xn, axis=-1, keepdims=True) + EPS) * fg_ref[...]
        o_ref[rows, :] = xn


def _scatter(y, pos, g, xn, mods, layer, stream, final, final_g_row):
    cap, seq, per = stream.cap, stream.seq, stream.req_per_step
    tok_blk = pl.BlockSpec((per * seq, D_MODEL), lambda i: (i, 0))
    sel_blk = pl.BlockSpec((per, N_EXPERTS, seq), lambda i: (i, 0, 0))
    return pl.pallas_call(
        functools.partial(_scatter_kernel, cap, seq, final),
        grid=(stream.n_req // per,),
        in_specs=[pl.BlockSpec((N_EXPERTS, per * cap, D_MODEL), lambda i: (0, i, 0)),
                  sel_blk, sel_blk, tok_blk,
                  _mod_spec(layer, stream, 5, per * seq),
                  pl.BlockSpec((1, D_MODEL), lambda i: (0, 0))],
        out_specs=tok_blk,
        out_shape=jax.ShapeDtypeStruct((stream.tokens, D_MODEL), F32),
        compiler_params=_params(1),
        name=f"scatter_l{layer}_s{seq}",
    )(y, pos, g, xn, mods, final_g_row)


def _axial_rope_tables(rows, head_dim):
    row = jnp.repeat(jnp.arange(rows, dtype=F32), GRID_W)
    col = jnp.tile(jnp.arange(GRID_W, dtype=F32), rows)
    nf = head_dim // 4
    inv = ROPE_BASE ** (-jnp.arange(nf, dtype=F32) / nf)
    ar = row[:, None] * inv[None]
    ac = col[:, None] * inv[None]
    ang = jnp.concatenate([ar, ar, ac, ac], axis=-1)
    return jnp.cos(ang), jnp.sin(ang)


def _block_diag(w):
    eye = jnp.eye(LRU_BLOCKS, dtype=w.dtype)
    return jnp.einsum('dnkj,nm->dnkmj', w, eye).reshape(2, LRU_WIDTH, LRU_WIDTH)


def kernel(x_prompt, x_sample, cache_win_k, cache_win_v, state_lru, cache_diff_k, cache_diff_v, c, c_ctx, ada_w, ada_b, norm_g, final_g, even_w_in, even_w_out, conv_w, conv_b, lru_wa, lru_ba, lru_wx, lru_bx, lru_lambda, win_sink, odd_w_in, odd_w_out, diff_lq1, diff_lk1, diff_lq2, diff_lk2, diff_subln_g, moe_router, moe_w_gate, moe_w_up, moe_w_down):
    cv_t = jnp.concatenate([c_ctx[None], c, jnp.zeros((COND_ROWS - N_COND, D_MODEL), F32)], axis=0).T
    mods = _adaln(cv_t, ada_w, ada_b).reshape(DEPTH * COND_ROWS, 1, 6 * D_MODEL)

    cos, sin = _axial_rope_tables(DEC_SEQ // GRID_W, WIN_HEAD_DIM)
    rope_win = (jnp.tile(cos, (1, WIN_HEADS)), jnp.tile(sin, (1, WIN_HEADS)))
    rope_diff = (jnp.tile(cos, (1, 2)), jnp.tile(sin, (1, 2)))

    streams = (CTX, LAT)
    xs = [x_prompt.reshape(CTX.tokens, D_MODEL), x_sample.reshape(LAT.tokens, D_MODEL)]
    final_g_row = final_g.reshape(1, D_MODEL)
    outs = {}

    for layer in range(DEPTH):
        idx = layer // 2
        even = layer % 2 == 0
        w_in = (even_w_in if even else odd_w_in)[idx].astype(BF16)
        w_out = (even_w_out if even else odd_w_out)[idx].astype(BF16)
        wr_t = moe_router[layer].T
        g1_row = norm_g[layer, 0].reshape(1, D_MODEL)
        g2_row = norm_g[layer, 1].reshape(1, D_MODEL)
        if even:
            wa = _block_diag(lru_wa[idx]).astype(BF16)
            wx = _block_diag(lru_wx[idx]).astype(BF16)

        routed = []
        for si, stream in enumerate(streams):
            is_ctx = stream is CTX
            if even or not is_ctx:
                proj = _proj(xs[si], g1_row, mods, layer, stream, w_in)
            if even:
                h0 = jnp.zeros((stream.n_req, 2, LRU_WIDTH), F32) if is_ctx else state_lru[:, idx]
                y_lru, h_fin = _lru(proj, stream, conv_w[idx], conv_b[idx].reshape(1, LRU_WIDTH), wa, wx,
                                    lru_ba[idx], lru_bx[idx], lru_lambda[idx], h0)
                if is_ctx:
                    o = _win_attn(proj, stream, win_sink[idx])
                    k0 = 2 * LRU_WIDTH + WIN_Q
                    to_heads = lambda t: t.reshape(BATCH, SEQ, WIN_KV_HEADS, WIN_HEAD_DIM).transpose(0, 2, 1, 3)[:, None]
                    outs["win_k"] = to_heads(proj[:, k0:k0 + WIN_KV])
                    outs["win_v"] = to_heads(proj[:, k0 + WIN_KV:k0 + 2 * WIN_KV])
                    outs["lru"] = h_fin[:, None]
                else:
                    pack = lambda t: t[:, idx].transpose(0, 2, 1, 3).reshape(DEC_BATCH, PAST_LEN, WIN_KV)
                    o = _win_attn(proj, stream, win_sink[idx],
                                  cache=(pack(cache_win_k), pack(cache_win_v)), rope=rope_win)
                mixed = [y_lru, o]
            else:
                lambda_init = 0.8 - 0.6 * math.exp(-0.3 * layer)
                vec = lambda t: t[idx].reshape(1, DIFF_HEAD_DIM)
                args = (vec(diff_lq1), vec(diff_lk1), vec(diff_lq2), vec(diff_lk2), diff_subln_g[idx])
                if is_ctx:
                    q, kh, vh = _proj_heads(xs[si], g1_row, mods, layer, stream, w_in)
                    o = _diff_attn((q, kh, vh), stream, lambda_init, *args)
                    outs["diff_k"] = kh[:, None]
                    outs["diff_v"] = vh[:, None]
                else:
                    o = _diff_attn(proj, stream, lambda_init, *args,
                                   cache=(cache_diff_k[:, idx:idx + 1], cache_diff_v[:, idx:idx + 1]), rope=rope_diff)
                mixed = [o]
            xn, h2, lg = _mix_out(mixed, w_out, xs[si], g2_row, mods, layer, stream, wr_t)
            per_req = stream.seq // LOGIT_TILE
            lg = lg.reshape(stream.n_req, per_req, N_EXPERTS, LOGIT_TILE).transpose(0, 2, 1, 3)
            pos, gate = _router(lg.reshape(stream.n_req, N_EXPERTS, stream.seq), stream)
            routed.append((xn, pos, gate, _gather(h2, pos, stream)))

        ys = _ffn(routed[0][3], routed[1][3], layer, moe_w_gate, moe_w_up, moe_w_down)
        final = layer == DEPTH - 1
        for si, stream in enumerate(streams):
            xn, pos, gate, _ = routed[si]
            xs[si] = _scatter(ys[si], pos, gate, xn, mods, layer, stream, final, final_g_row)

    y_prompt = xs[0].reshape(BATCH, SEQ, D_MODEL)
    y_sample = xs[1].reshape(DEC_BATCH, DEC_SEQ, D_MODEL)
    return (y_prompt, y_sample, outs["win_k"], outs["win_v"], outs["lru"], outs["diff_k"], outs["diff_v"])
```

```python
import functools
import math

import jax
import jax.numpy as jnp
from jax import lax
from jax.experimental import pallas as pl
from jax.experimental.pallas import tpu as pltpu

F32 = jnp.float32
BF16 = jnp.bfloat16
I32 = jnp.int32

D_MODEL = 1024
BATCH = 16
SEQ = 256
DEPTH = 2
DEC_BATCH = 2
DEC_SEQ = 1024
PAST_LEN = 512
GRID_W = 64
LRU_WIDTH = D_MODEL // 2
LRU_BLOCKS = 8
LRU_BLOCK = LRU_WIDTH // LRU_BLOCKS
CONV_W = 4
LRU_C = 8.0
WIN_HEADS = 8
WIN_KV_HEADS = 2
WIN_REP = WIN_HEADS // WIN_KV_HEADS
WIN_HEAD_DIM = 64
WINDOW = 128
WIN_Q = WIN_HEADS * WIN_HEAD_DIM
WIN_KV = WIN_KV_HEADS * WIN_HEAD_DIM
EVEN_IN = 2 * LRU_WIDTH + WIN_Q + 2 * WIN_KV
DIFF_HEADS = 8
DIFF_HEAD_DIM = 64
DIFF_QK = DIFF_HEADS * 2 * DIFF_HEAD_DIM
DIFF_V = DIFF_HEADS * 2 * DIFF_HEAD_DIM
ODD_IN = 2 * DIFF_QK + DIFF_V
N_EXPERTS = 16
EXPERT_FF = 2 * D_MODEL
CAPACITY_FACTOR = 2
ROPE_BASE = 10000.0
EPS = 1e-6
NEG_INF = -1e30

LANES = 128
SUBLANES = 8
VMEM_LIMIT_BYTES = 48 * 1024 * 1024

N_COND = 1 + DEC_BATCH
COND_ROWS = SUBLANES
TOKEN_TILE = 512
MIX_TILE = 1024
ROUTE_ROWS = 512
LOGIT_TILE = 256
FF_TILE = 1024
FFN_VMEM_LIMIT_BYTES = 56 * 1024 * 1024
LRU_CHUNK = 128
ATT_Q_TILE = 256
ATT_K_TILE = 256


class Stream:
    def __init__(self, n_req, seq, cond0, cond_step):
        self.n_req, self.seq, self.cond0, self.cond_step = n_req, seq, cond0, cond_step
        self.tokens = n_req * seq
        self.cap = CAPACITY_FACTOR * seq // N_EXPERTS
        self.req_per_step = max(1, ROUTE_ROWS // seq)

    def cond_of_row(self, row):
        return self.cond0 + self.cond_step * (row // self.seq)


CTX = Stream(BATCH, SEQ, 0, 0)
LAT = Stream(DEC_BATCH, DEC_SEQ, 1, 1)


def _params(n_axes, vmem_limit_bytes=VMEM_LIMIT_BYTES):
    return pltpu.CompilerParams(dimension_semantics=("arbitrary",) * n_axes,
                                vmem_limit_bytes=vmem_limit_bytes)


def _mod_spec(layer, stream, k, rows_per_step):
    return pl.BlockSpec(
        (1, 1, D_MODEL),
        lambda i, *_: (layer * COND_ROWS + stream.cond_of_row(i * rows_per_step), 0, k))


def _dot(a, b):
    return jnp.dot(a, b, preferred_element_type=F32)


def _dot_nt(a, b):
    return lax.dot_general(a, b, (((1,), (1,)), ((), ())), preferred_element_type=F32)


def _modnorm(x, g, shift, scale):
    y = x * lax.rsqrt(jnp.mean(x * x, axis=-1, keepdims=True) + EPS)
    return (y * g) * (1.0 + scale) + shift


def _lane_half_masks(shape):
    lane = lax.broadcasted_iota(I32, shape, len(shape) - 1)
    left = (lane & (LANES - 1)) < LANES // 2
    return left, jnp.logical_not(left)


def _rope(x, cos, sin):
    parts = []
    for c in range(x.shape[1] // LANES):
        xs = x[:, c * LANES:(c + 1) * LANES]
        lane = lax.broadcasted_iota(I32, xs.shape, 1)
        first = (lane & 31) < 16
        rot = jnp.where(first, -pltpu.roll(xs, LANES - 16, axis=1), pltpu.roll(xs, 16, axis=1))
        parts.append(xs * cos[:, c * LANES:(c + 1) * LANES] + rot * sin[:, c * LANES:(c + 1) * LANES])
    return parts[0] if len(parts) == 1 else jnp.concatenate(parts, axis=1)


def _adaln_kernel(cv_ref, w_ref, b_ref, o_ref):
    cv = cv_ref[...]
    s = cv * jax.nn.sigmoid(cv)
    w = w_ref[0]
    ridx = lax.broadcasted_iota(I32, (COND_ROWS, w.shape[1]), 0)
    out = jnp.zeros((COND_ROWS, w.shape[1]), F32)
    for r in range(N_COND):
        out = jnp.where(ridx == r, jnp.sum(w * s[:, r:r + 1], axis=0, keepdims=True), out)
    o_ref[0] = out + b_ref[0]


def _adaln(cv_t, ada_w, ada_b):
    tn = 1024
    return pl.pallas_call(
        _adaln_kernel,
        grid=(DEPTH, 6 * D_MODEL // tn),
        in_specs=[pl.BlockSpec((D_MODEL, COND_ROWS), lambda l, j: (0, 0)),
                  pl.BlockSpec((1, D_MODEL, tn), lambda l, j: (l, 0, j)),
                  pl.BlockSpec((1, 1, tn), lambda l, j: (l, 0, j))],
        out_specs=pl.BlockSpec((1, COND_ROWS, tn), lambda l, j: (l, 0, j)),
        out_shape=jax.ShapeDtypeStruct((DEPTH, COND_ROWS, 6 * D_MODEL), F32),
        compiler_params=_params(2),
        name="adaln",
    )(cv_t, ada_w, ada_b.reshape(DEPTH, 1, 6 * D_MODEL))


def _proj_kernel(x_ref, g_ref, sh_ref, sc_ref, w_ref, o_ref):
    h = _modnorm(x_ref[...], g_ref[...], sh_ref[0], sc_ref[0])
    o_ref[...] = _dot(h.astype(BF16), w_ref[...])


def _proj(x, g_row, mods, layer, stream, w):
    n_out = w.shape[1]
    tm = TOKEN_TILE
    return pl.pallas_call(
        _proj_kernel,
        grid=(stream.tokens // tm,),
        in_specs=[pl.BlockSpec((tm, D_MODEL), lambda i: (i, 0)),
                  pl.BlockSpec((1, D_MODEL), lambda i: (0, 0)),
                  _mod_spec(layer, stream, 0, tm),
                  _mod_spec(layer, stream, 1, tm),
                  pl.BlockSpec((D_MODEL, n_out), lambda i: (0, 0))],
        out_specs=pl.BlockSpec((tm, n_out), lambda i: (i, 0)),
        out_shape=jax.ShapeDtypeStruct((stream.tokens, n_out), F32),
        compiler_params=_params(1),
        name=f"proj_l{layer}_s{stream.seq}",
    )(x, g_row, mods, mods, w)


def _proj_heads_kernel(seq, x_ref, g_ref, sh_ref, sc_ref, w_ref, q_ref, k_ref, v_ref):
    h = _modnorm(x_ref[...], g_ref[...], sh_ref[0], sc_ref[0])
    res = _dot(h.astype(BF16), w_ref[...])
    q_ref[...] = res[:, 0:DIFF_QK]
    for r in range(x_ref.shape[0] // seq):
        rows = slice(r * seq, (r + 1) * seq)
        for hh in range(DIFF_HEADS):
            k_ref[r, hh] = res[rows, DIFF_QK + hh * LANES:DIFF_QK + (hh + 1) * LANES]
            v_ref[r, hh] = res[rows, 2 * DIFF_QK + hh * LANES:2 * DIFF_QK + (hh + 1) * LANES]


def _proj_heads(x, g_row, mods, layer, stream, w):
    tm, seq = TOKEN_TILE, stream.seq
    head_shape = (stream.n_req, DIFF_HEADS, seq, 2 * DIFF_HEAD_DIM)
    head_spec = pl.BlockSpec((tm // seq, DIFF_HEADS, seq, 2 * DIFF_HEAD_DIM), lambda i: (i, 0, 0, 0))
    return pl.pallas_call(
        functools.partial(_proj_heads_kernel, seq),
        grid=(stream.tokens // tm,),
        in_specs=[pl.BlockSpec((tm, D_MODEL), lambda i: (i, 0)),
                  pl.BlockSpec((1, D_MODEL), lambda i: (0, 0)),
                  _mod_spec(layer, stream, 0, tm),
                  _mod_spec(layer, stream, 1, tm),
                  pl.BlockSpec((D_MODEL, ODD_IN), lambda i: (0, 0))],
        out_specs=[pl.BlockSpec((tm, DIFF_QK), lambda i: (i, 0)), head_spec, head_spec],
        out_shape=[jax.ShapeDtypeStruct((stream.tokens, DIFF_QK), F32),
                   jax.ShapeDtypeStruct(head_shape, F32), jax.ShapeDtypeStruct(head_shape, F32)],
        compiler_params=_params(1),
        name=f"proj_heads_l{layer}_s{seq}",
    )(x, g_row, mods, mods, w)


def _lru_kernel(seq, xl_ref, gl_ref, cw_ref, cb_ref, wa_ref, wx_ref, ba_ref, bx_ref, lam_ref, h0_ref,
                y_ref, hfin_ref, xpad, af, bf, ab, bb):
    width = LRU_WIDTH
    ch = LRU_CHUNK
    halo = SUBLANES
    xpad[0:halo, :] = jnp.zeros((halo, width), F32)
    xpad[halo + seq:2 * halo + seq, :] = jnp.zeros((halo, width), F32)
    xpad[halo:halo + seq, :] = xl_ref[...]

    lam = lam_ref[...]
    z = -lam
    softplus = jnp.maximum(z, 0.0) + jnp.log1p(jnp.exp(-jnp.abs(z)))
    sub = lax.broadcasted_iota(I32, (ch // SUBLANES, SUBLANES, width), 1)
    cw = cw_ref[...]
    cb = cb_ref[...]

    def gates_chunk(c, carry):
        r0 = pl.multiple_of(c * ch, ch)
        win = xpad[pl.ds(r0, ch + 2 * halo), :]
        n_win = ch + 2 * halo

        def tap(j):
            return pltpu.roll(win, n_win - (halo - 2 + j), axis=0)[0:ch]

        xc = tap(0) * cw[0:1]
        for j in range(1, CONV_W):
            xc = xc + tap(j) * cw[j:j + 1]
        xc = xc + cb
        xcb = xc.astype(BF16)
        for d, (a_s, b_s) in enumerate(((af, bf), (ab, bb))):
            r = jax.nn.sigmoid(_dot(xcb, wa_ref[d]) + ba_ref[d:d + 1])
            ig = jax.nn.sigmoid(_dot(xcb, wx_ref[d]) + bx_ref[d:d + 1])
            log_a = (-LRU_C * r) * softplus[d:d + 1]
            a = jnp.exp(log_a)
            b = jnp.sqrt(jnp.tanh(-log_a) * (a * a + 1.0)) * (ig * xc)
            a = a.reshape(ch // SUBLANES, SUBLANES, width)
            b = b.reshape(ch // SUBLANES, SUBLANES, width)
            for s in (1, 2, 4):
                keep = (sub >= s) if d == 0 else (sub < SUBLANES - s)
                shift = s if d == 0 else SUBLANES - s
                a_sh = jnp.where(keep, pltpu.roll(a, shift, axis=1), 1.0)
                b_sh = jnp.where(keep, pltpu.roll(b, shift, axis=1), 0.0)
                b = a * b_sh + b
                a = a * a_sh
            a_s[pl.ds(r0, ch), :] = a.reshape(ch, width)
            b_s[pl.ds(r0, ch), :] = b.reshape(ch, width)
        return carry

    lax.fori_loop(0, seq // ch, gates_chunk, 0)

    n_tiles = seq // SUBLANES
    h0 = h0_ref[0]

    def tile_step(k, carry):
        cf, cbw = carry
        rf = pl.multiple_of(k * SUBLANES, SUBLANES)
        rb = pl.multiple_of((n_tiles - 1 - k) * SUBLANES, SUBLANES)
        hf = af[pl.ds(rf, SUBLANES), :] * cf + bf[pl.ds(rf, SUBLANES), :]
        bf[pl.ds(rf, SUBLANES), :] = hf
        hb = ab[pl.ds(rb, SUBLANES), :] * cbw + bb[pl.ds(rb, SUBLANES), :]
        bb[pl.ds(rb, SUBLANES), :] = hb
        return hf[SUBLANES - 1:SUBLANES], hb[0:1]

    cf, cbw = lax.fori_loop(0, n_tiles, tile_step, (h0[0:1], h0[1:2]))
    hfin_ref[0, 0:1, :] = cf
    hfin_ref[0, 1:2, :] = cbw

    def out_chunk(c, carry):
        r0 = pl.multiple_of(c * ch, ch)
        hsum = bf[pl.ds(r0, ch), :] + bb[pl.ds(r0, ch), :]
        y_ref[pl.ds(r0, ch), :] = (hsum * jax.nn.gelu(gl_ref[pl.ds(r0, ch), :])).astype(BF16)
        return carry

    lax.fori_loop(0, seq // ch, out_chunk, 0)


def _lru(proj, stream, conv_w, conv_b, wa, wx, ba, bx, lam, h0):
    seq, width = stream.seq, LRU_WIDTH
    full2 = lambda b: (0, 0)
    full3 = lambda b: (0, 0, 0)
    return pl.pallas_call(
        functools.partial(_lru_kernel, seq),
        grid=(stream.n_req,),
        in_specs=[pl.BlockSpec((seq, width), lambda b: (b, 0)),
                  pl.BlockSpec((seq, width), lambda b: (b, 1)),
                  pl.BlockSpec((CONV_W, width), full2),
                  pl.BlockSpec((1, width), full2),
                  pl.BlockSpec((2, width, width), full3),
                  pl.BlockSpec((2, width, width), full3),
                  pl.BlockSpec((2, width), full2),
                  pl.BlockSpec((2, width), full2),
                  pl.BlockSpec((2, width), full2),
                  pl.BlockSpec((1, 2, width), lambda b: (b, 0, 0))],
        out_specs=[pl.BlockSpec((seq, width), lambda b: (b, 0)),
                   pl.BlockSpec((1, 2, width), lambda b: (b, 0, 0))],
        out_shape=[jax.ShapeDtypeStruct((stream.tokens, width), BF16),
                   jax.ShapeDtypeStruct((stream.n_req, 2, width), F32)],
        scratch_shapes=[pltpu.VMEM((seq + 2 * SUBLANES, width), F32)] + [pltpu.VMEM((seq, width), F32)] * 4,
        compiler_params=_params(1),
        name=f"lru_s{seq}",
    )(proj, proj, conv_w, conv_b, wa, wx, ba, bx, lam, h0)


def _attend(q, chunks, s_ref):
    tile_max = None
    spans = []
    off = 0
    for keys, _, valid in chunks:
        s = _dot_nt(q, keys())
        if valid is not None:
            s = jnp.where(valid, s, NEG_INF)
        n = s.shape[1]
        s_ref[:, off:off + n] = s
        for c in range(n // LANES):
            t = s[:, c * LANES:(c + 1) * LANES]
            tile_max = t if tile_max is None else jnp.maximum(tile_max, t)
        spans.append((off, n))
        off += n
    m = jnp.max(tile_max, axis=-1, keepdims=True)
    acc = None
    for (_, values, _), (o, n) in zip(chunks, spans):
        part = _dot(jnp.exp(s_ref[:, o:o + n] - m).astype(BF16), values())
        acc = part if acc is None else acc + part
    return acc, m


def _split_groups(kk):
    left, right = _lane_half_masks(kk.shape)
    g0_l = jnp.where(left, kk, 0.0)
    g1_r = jnp.where(right, kk, 0.0)
    return ((g0_l, pltpu.roll(g0_l, LANES // 2, axis=1)), (pltpu.roll(g1_r, LANES // 2, axis=1), g1_r))


def _win_ctx_kernel(sink_ref, q_ref, kv_ref, o_ref):
    scale = WIN_HEAD_DIM ** -0.5
    seq = q_ref.shape[0]
    ks = _split_groups(kv_ref[:, 0:LANES])
    vs = _split_groups(kv_ref[:, LANES:2 * LANES])
    top = lax.broadcasted_iota(I32, (2 * seq, 1), 0) < seq
    outs = [None] * (WIN_HEADS // 2)
    for g in range(WIN_KV_HEADS):
        pairs = (2 * g, 2 * g + 1)
        qs = jnp.concatenate([q_ref[:, p * LANES:(p + 1) * LANES] for p in pairs], axis=0)
        qs = (qs * scale).astype(BF16)
        for side in range(2):
            sk = jnp.where(top, sink_ref[2 * pairs[0] + side], sink_ref[2 * pairs[1] + side])
            s = _dot_nt(qs, ks[g][side].astype(BF16))
            m = jnp.maximum(jnp.max(s, axis=-1, keepdims=True), sk)
            e = jnp.exp(s - m)
            den = jnp.sum(e, axis=-1, keepdims=True) + jnp.exp(sk - m)
            o = _dot(e.astype(BF16), vs[g][side].astype(BF16)) * (1.0 / den)
            for k, p in enumerate(pairs):
                part = o[k * seq:(k + 1) * seq]
                outs[p] = part if outs[p] is None else outs[p] + part
    for p in range(WIN_HEADS // 2):
        o_ref[:, p * LANES:(p + 1) * LANES] = outs[p].astype(BF16)


def _win_lat_kernel(sink_ref, q_ref, kv_ref, ck_ref, cv_ref, cos_ref, sin_ref, o_ref,
                    kl_s, vl_s, kc_s, vc_s):
    seq, wn = DEC_SEQ, WINDOW
    scale = WIN_HEAD_DIM ** -0.5
    kr = _rope(kv_ref[:, 0:LANES], cos_ref[:, 0:LANES], sin_ref[:, 0:LANES])
    for src, dst in ((_split_groups(kr), kl_s), (_split_groups(kv_ref[:, LANES:2 * LANES]), vl_s),
                     (_split_groups(ck_ref[0]), kc_s), (_split_groups(cv_ref[0]), vc_s)):
        for g in range(WIN_KV_HEADS):
            for side in range(2):
                dst[2 * g + side] = src[g][side].astype(BF16)

    top = lax.broadcasted_iota(I32, (2 * wn, 1), 0) < wn

    def q_block(i, carry):
        for sub in range(2):
            r0 = pl.multiple_of((2 * i + sub) * wn, wn)
            start = pl.multiple_of(jnp.clip((2 * i + sub - 1) * wn, 0, seq - 3 * wn), wn)
            qr = _rope(q_ref[pl.ds(r0, wn), :], cos_ref[pl.ds(r0, wn), :], sin_ref[pl.ds(r0, wn), :]) * scale
            qpos = r0 + (lax.broadcasted_iota(I32, (2 * wn, 3 * wn), 0) & (wn - 1))
            kpos = start + lax.broadcasted_iota(I32, (2 * wn, 3 * wn), 1)
            valid = jnp.abs(qpos - kpos) <= wn
            outs = [None] * (WIN_HEADS // 2)
            for g in range(WIN_KV_HEADS):
                pairs = (2 * g, 2 * g + 1)
                qs = jnp.concatenate([qr[:, p * LANES:(p + 1) * LANES] for p in pairs], axis=0).astype(BF16)
                for side in range(2):
                    idx = 2 * g + side
                    sk = jnp.where(top, sink_ref[2 * pairs[0] + side], sink_ref[2 * pairs[1] + side])
                    sl = _dot_nt(qs, kl_s[idx, pl.ds(start, 3 * wn), :])
                    sl = jnp.where(valid, sl, NEG_INF)
                    sc = _dot_nt(qs, kc_s[idx])
                    m = jnp.maximum(jnp.maximum(jnp.max(sl, axis=-1, keepdims=True),
                                                jnp.max(sc, axis=-1, keepdims=True)), sk)
                    el = jnp.exp(sl - m)
                    ec = jnp.exp(sc - m)
                    den = (jnp.sum(el, axis=-1, keepdims=True) + jnp.sum(ec, axis=-1, keepdims=True)
                           + jnp.exp(sk - m))
                    o = (_dot(el.astype(BF16), vl_s[idx, pl.ds(start, 3 * wn), :])
                         + _dot(ec.astype(BF16), vc_s[idx])) * (1.0 / den)
                    for k, p in enumerate(pairs):
                        part = o[k * wn:(k + 1) * wn]
                        outs[p] = part if outs[p] is None else outs[p] + part
            for p in range(WIN_HEADS // 2):
                o_ref[pl.ds(r0, wn), p * LANES:(p + 1) * LANES] = outs[p].astype(BF16)
        return carry

    lax.fori_loop(0, seq // (2 * wn), q_block, 0)


def _win_attn(proj, stream, sink, cache=None, rope=None):
    seq = stream.seq
    q_spec = pl.BlockSpec((seq, WIN_Q), lambda b: (b, 2 * LRU_WIDTH // WIN_Q))
    kv_spec = pl.BlockSpec((seq, 2 * WIN_KV), lambda b: (b, (2 * LRU_WIDTH + WIN_Q) // (2 * WIN_KV)))
    sink_spec = pl.BlockSpec(memory_space=pltpu.SMEM)
    out_spec = pl.BlockSpec((seq, WIN_Q), lambda b: (b, 0))
    out_shape = jax.ShapeDtypeStruct((stream.tokens, WIN_Q), BF16)
    if cache is None:
        return pl.pallas_call(
            _win_ctx_kernel, grid=(stream.n_req,),
            in_specs=[sink_spec, q_spec, kv_spec], out_specs=out_spec, out_shape=out_shape,
            compiler_params=_params(1), name="win_attn_ctx",
        )(sink, proj, proj)
    ck, cv = cache
    cos, sin = rope
    cache_spec = pl.BlockSpec((1, PAST_LEN, LANES), lambda b: (b, 0, 0))
    table_spec = pl.BlockSpec((seq, WIN_Q), lambda b: (0, 0))
    return pl.pallas_call(
        _win_lat_kernel, grid=(stream.n_req,),
        in_specs=[sink_spec, q_spec, kv_spec, cache_spec, cache_spec, table_spec, table_spec],
        out_specs=out_spec, out_shape=out_shape,
        scratch_shapes=[pltpu.VMEM((4, seq, LANES), BF16), pltpu.VMEM((4, seq, LANES), BF16),
                        pltpu.VMEM((4, PAST_LEN, LANES), BF16), pltpu.VMEM((4, PAST_LEN, LANES), BF16)],
        compiler_params=_params(1), name="win_attn_lat",
    )(sink, proj, proj, ck, cv, cos, sin)


def _diff_lambda(lq1_ref, lk1_ref, lq2_ref, lk2_ref, lambda_init):
    t1 = jnp.sum(lq1_ref[...] * lk1_ref[...], axis=-1, keepdims=True)
    t2 = jnp.sum(lq2_ref[...] * lk2_ref[...], axis=-1, keepdims=True)
    return jnp.exp(t1) - jnp.exp(t2) + lambda_init


def _subln(o, g_row, lambda_init):
    o = o * lax.rsqrt(jnp.mean(o * o, axis=-1, keepdims=True) + EPS) * g_row
    return o * (1.0 - lambda_init)


def _component_keys(k):
    left, right = _lane_half_masks(k.shape)
    return jnp.where(left, k, 0.0).astype(BF16), jnp.where(right, k, 0.0).astype(BF16)


def _values_with_ones(v):
    return jnp.concatenate([v.astype(BF16), jnp.ones(v.shape, BF16)], axis=1)


def _diff_combine(accs, lam):
    o1, o2 = accs[0][:, 0:LANES], accs[1][:, 0:LANES]
    return o1 * (1.0 / accs[0][:, LANES:]) - o2 * (lam * (1.0 / accs[1][:, LANES:]))


def _diff_ctx_kernel(lambda_init, lq1_ref, lk1_ref, lq2_ref, lk2_ref, sg_ref, q_ref, k_ref, v_ref, o_ref, s_ref):
    scale = DIFF_HEAD_DIM ** -0.5
    tq = ATT_Q_TILE
    lam = _diff_lambda(lq1_ref, lk1_ref, lq2_ref, lk2_ref, lambda_init)
    for h in range(DIFF_HEADS):
        cols = slice(h * LANES, (h + 1) * LANES)
        keys = _component_keys(k_ref[0, h])
        vals = _values_with_ones(v_ref[0, h])
        for qt in range(q_ref.shape[0] // tq):
            rows = slice(qt * tq, (qt + 1) * tq)
            q = (q_ref[rows, cols] * scale).astype(BF16)
            slots = [s_ref.at[(4 * h + 2 * qt + c) % s_ref.shape[0]] for c in range(2)]
            accs = [_attend(q, [(lambda: keys[c], lambda: vals, None)], slots[c])[0] for c in range(2)]
            o = _diff_combine(accs, lam)
            o_ref[rows, cols] = _subln(o, sg_ref[h:h + 1, :], lambda_init).astype(BF16)


def _diff_lat_kernel(lambda_init, lq1_ref, lk1_ref, lq2_ref, lk2_ref, sg_ref, q_ref, k_ref, v_ref,
                     ck_ref, cv_ref, cos_ref, sin_ref, o_ref, kl_s, kc_s, vl_s, vc_s, s_ref):
    seq, tq, tk = DEC_SEQ, ATT_Q_TILE, ATT_K_TILE
    scale = DIFF_HEAD_DIM ** -0.5
    lam = _diff_lambda(lq1_ref, lk1_ref, lq2_ref, lk2_ref, lambda_init)
    for src, dst in ((_rope(k_ref[...], cos_ref[...], sin_ref[...]), kl_s), (ck_ref[0, 0, 0], kc_s)):
        dst[0], dst[1] = _component_keys(src)
    vl_s[...] = _values_with_ones(v_ref[...])
    vc_s[...] = _values_with_ones(cv_ref[0, 0, 0])
    sg = sg_ref[0]

    for qt in range(seq // tq):
        rows = slice(qt * tq, (qt + 1) * tq)
        qr = (_rope(q_ref[rows, :], cos_ref[rows, :], sin_ref[rows, :]) * scale).astype(BF16)
        accs = []
        for comp in range(2):
            chunks = [(lambda j=j: kl_s[comp, j * tk:(j + 1) * tk, :],
                       lambda j=j: vl_s[j * tk:(j + 1) * tk, :], None) for j in range(seq // tk)]
            chunks += [(lambda j=j: kc_s[comp, j * tk:(j + 1) * tk, :],
                        lambda j=j: vc_s[j * tk:(j + 1) * tk, :], None) for j in range(PAST_LEN // tk)]
            accs.append(_attend(qr, chunks, s_ref.at[2 * qt + comp])[0])
        o_ref[rows, :] = _subln(_diff_combine(accs, lam), sg, lambda_init).astype(BF16)


def _diff_attn(proj, stream, lambda_init, lq1, lk1, lq2, lk2, subln_g, cache=None, rope=None):
    seq = stream.seq
    vec_spec = lambda nd: pl.BlockSpec((1, DIFF_HEAD_DIM), lambda *_: (0, 0))
    out_shape = jax.ShapeDtypeStruct((stream.tokens, DIFF_V), BF16)
    if cache is None:
        q, kh, vh = proj
        head_spec = pl.BlockSpec((1, DIFF_HEADS, seq, LANES), lambda b: (b, 0, 0, 0))
        return pl.pallas_call(
            functools.partial(_diff_ctx_kernel, lambda_init), grid=(stream.n_req,),
            in_specs=[vec_spec(1)] * 4 + [pl.BlockSpec((DIFF_HEADS, LANES), lambda b: (0, 0)),
                                          pl.BlockSpec((seq, DIFF_QK), lambda b: (b, 0)), head_spec, head_spec],
            out_specs=pl.BlockSpec((seq, DIFF_V), lambda b: (b, 0)), out_shape=out_shape,
            scratch_shapes=[pltpu.VMEM((8, ATT_Q_TILE, seq), F32)],
            compiler_params=_params(1), name="diff_attn_ctx",
        )(lq1, lk1, lq2, lk2, subln_g, q, kh, vh)
    ck, cv = cache
    cos, sin = rope
    blk = lambda c: pl.BlockSpec((seq, LANES), lambda b, h: (b, c * DIFF_HEADS + h))
    cache_spec = pl.BlockSpec((1, 1, 1, PAST_LEN, LANES), lambda b, h: (b, 0, h, 0, 0))
    table_spec = pl.BlockSpec((seq, LANES), lambda b, h: (0, 0))
    return pl.pallas_call(
        functools.partial(_diff_lat_kernel, lambda_init), grid=(stream.n_req, DIFF_HEADS),
        in_specs=[vec_spec(2)] * 4 + [pl.BlockSpec((1, 1, LANES), lambda b, h: (h, 0, 0)),
                                      blk(0), blk(1), blk(2), cache_spec, cache_spec, table_spec, table_spec],
        out_specs=pl.BlockSpec((seq, LANES), lambda b, h: (b, h)), out_shape=out_shape,
        scratch_shapes=[pltpu.VMEM((2, seq, LANES), BF16), pltpu.VMEM((2, PAST_LEN, LANES), BF16),
                        pltpu.VMEM((seq, 2 * LANES), BF16), pltpu.VMEM((PAST_LEN, 2 * LANES), BF16),
                        pltpu.VMEM((2 * seq // ATT_Q_TILE, ATT_Q_TILE, seq + PAST_LEN), F32)],
        compiler_params=_params(2), name="diff_attn_lat",
    )(lq1, lk1, lq2, lk2, subln_g.reshape(DIFF_HEADS, 1, LANES), proj, proj, proj, ck, cv, cos, sin)


def _mix_out_kernel(n_in, *refs):
    a_refs = refs[:n_in]
    w_ref, x_ref, gate_ref, g2_ref, sh2_ref, sc2_ref, wr_ref, xn_ref, h2_ref, lg_ref = refs[n_in:]
    kp = D_MODEL // n_in
    acc = None
    for k, a_ref in enumerate(a_refs):
        part = _dot(a_ref[...], w_ref[k * kp:(k + 1) * kp, :])
        acc = part if acc is None else acc + part
    xn = x_ref[...] + gate_ref[0] * acc
    xn_ref[...] = xn
    h2 = _modnorm(xn, g2_ref[...], sh2_ref[0], sc2_ref[0])
    h_hi = h2.astype(BF16)
    h2_ref[...] = h_hi
    h_lo = (h2 - h_hi.astype(F32)).astype(BF16)
    wr = wr_ref[...]
    w_hi = wr.astype(BF16)
    w_lo = (wr - w_hi.astype(F32)).astype(BF16)
    by_hi = _dot_nt(jnp.concatenate([w_hi, w_lo], axis=0), h_hi)
    lg = by_hi[0:N_EXPERTS] + (by_hi[N_EXPERTS:] + _dot_nt(w_hi, h_lo))
    for c in range(lg.shape[1] // LOGIT_TILE):
        lg_ref[c] = lg[:, c * LOGIT_TILE:(c + 1) * LOGIT_TILE]


def _mix_out(mixed, w_out, x, g2_row, mods, layer, stream, wr_t):
    tm = MIX_TILE
    n_in = len(mixed)
    kp = D_MODEL // n_in
    row_blk = lambda width: pl.BlockSpec((tm, width), lambda i: (i, 0))
    return pl.pallas_call(
        functools.partial(_mix_out_kernel, n_in),
        grid=(stream.tokens // tm,),
        in_specs=[row_blk(kp)] * n_in + [
            pl.BlockSpec((D_MODEL, D_MODEL), lambda i: (0, 0)),
            row_blk(D_MODEL),
            _mod_spec(layer, stream, 2, tm),
            pl.BlockSpec((1, D_MODEL), lambda i: (0, 0)),
            _mod_spec(layer, stream, 3, tm),
            _mod_spec(layer, stream, 4, tm),
            pl.BlockSpec((N_EXPERTS, D_MODEL), lambda i: (0, 0))],
        out_specs=[row_blk(D_MODEL), row_blk(D_MODEL),
                   pl.BlockSpec((tm // LOGIT_TILE, N_EXPERTS, LOGIT_TILE), lambda i: (i, 0, 0))],
        out_shape=[jax.ShapeDtypeStruct((stream.tokens, D_MODEL), F32),
                   jax.ShapeDtypeStruct((stream.tokens, D_MODEL), BF16),
                   jax.ShapeDtypeStruct((stream.tokens // LOGIT_TILE, N_EXPERTS, LOGIT_TILE), F32)],
        compiler_params=_params(1),
        name=f"mix_out_l{layer}_s{stream.seq}",
    )(*mixed, w_out, x, mods, g2_row, mods, mods, wr_t)


def _sort_desc_lanes(x):
    rows, n = x.shape
    tiles = [x[:, c * LANES:(c + 1) * LANES] for c in range(n // LANES)]
    lane = lax.broadcasted_iota(I32, (rows, LANES), 1)
    k = 2
    while k <= n:
        j = k // 2
        while j >= 1:
            if j < LANES:
                lower = (lane & j) == 0
                for c in range(len(tiles)):
                    t = tiles[c]
                    partner = jnp.where(lower, pltpu.roll(t, LANES - j, axis=1), pltpu.roll(t, j, axis=1))
                    desc = ((lane & k) == 0) if k < LANES else (((c * LANES) & k) == 0)
                    take_max = (lower == desc) if k < LANES else (lower if desc else jnp.logical_not(lower))
                    tiles[c] = jnp.where(take_max, jnp.maximum(t, partner), jnp.minimum(t, partner))
            else:
                jc = j // LANES
                new = list(tiles)
                for c in range(len(tiles)):
                    take_max = ((c & jc) == 0) == (((c * LANES) & k) == 0)
                    new[c] = (jnp.maximum if take_max else jnp.minimum)(tiles[c], tiles[c ^ jc])
                tiles = new
            j //= 2
        k *= 2
    return tiles


def _router_kernel(cap, lg_ref, pos_ref, g_ref):
    x = lg_ref[...]
    n_b, n_e, n_tok = x.shape
    e = jnp.exp(x - jnp.max(x, axis=1, keepdims=True))
    aff = (e / jnp.sum(e, axis=1, keepdims=True)).reshape(n_b * n_e, n_tok)
    srt = _sort_desc_lanes(aff)
    lane_k = (cap - 1) % LANES
    thr = srt[(cap - 1) // LANES][:, lane_k:lane_k + 1]
    gt = aff > thr
    eq = aff == thr
    n_gt = jnp.sum(gt.astype(F32), axis=1, keepdims=True)
    before = (lax.broadcasted_iota(I32, (n_tok, n_tok), 0)
              < lax.broadcasted_iota(I32, (n_tok, n_tok), 1)).astype(BF16)
    eq_rank = _dot(eq.astype(BF16), before)
    sel = gt | (eq & (eq_rank < cap - n_gt))
    slot = _dot(sel.astype(BF16), before).astype(I32)
    pos_ref[...] = jnp.where(sel, slot, -1).reshape(n_b, n_e, n_tok)
    g_ref[...] = jnp.where(sel, aff, 0.0).reshape(n_b, n_e, n_tok)


def _router(logits, stream):
    shape = (stream.n_req, N_EXPERTS, stream.seq)
    spec = pl.BlockSpec(shape, lambda: (0, 0, 0))
    return pl.pallas_call(
        functools.partial(_router_kernel, stream.cap),
        in_specs=[spec], out_specs=[spec, spec],
        out_shape=[jax.ShapeDtypeStruct(shape, I32), jax.ShapeDtypeStruct(shape, F32)],
        compiler_params=pltpu.CompilerParams(vmem_limit_bytes=VMEM_LIMIT_BYTES),
        name=f"router_s{stream.seq}",
    )(logits)


def _gather_kernel(cap, seq, h_ref, pos_ref, xs_ref):
    slot = lax.broadcasted_iota(I32, (cap, seq), 0)
    for r in range(pos_ref.shape[0]):
        onehot = jnp.concatenate([(slot == pos_ref[r, e:e + 1, :]).astype(BF16) for e in range(N_EXPERTS)], axis=0)
        xs = _dot(onehot, h_ref[r * seq:(r + 1) * seq, :]).astype(BF16)
        for e in range(N_EXPERTS):
            xs_ref[e, r * cap:(r + 1) * cap, :] = xs[e * cap:(e + 1) * cap]


def _gather(h2, pos, stream):
    cap, seq, per = stream.cap, stream.seq, stream.req_per_step
    return pl.pallas_call(
        functools.partial(_gather_kernel, cap, seq),
        grid=(stream.n_req // per,),
        in_specs=[pl.BlockSpec((per * seq, D_MODEL), lambda i: (i, 0)),
                  pl.BlockSpec((per, N_EXPERTS, seq), lambda i: (i, 0, 0))],
        out_specs=pl.BlockSpec((N_EXPERTS, per * cap, D_MODEL), lambda i: (0, i, 0)),
        out_shape=jax.ShapeDtypeStruct((N_EXPERTS, stream.n_req * cap, D_MODEL), BF16),
        compiler_params=_params(1),
        name=f"gather_s{seq}",
    )(h2, pos)


def _ffn_kernel(xa_ref, xb_ref, wg_ref, wu_ref, wd_ref, ya_ref, yb_ref, acc):
    j = pl.program_id(1)
    rows_a = xa_ref.shape[1]

    @pl.when(j == 0)
    def _():
        acc[...] = jnp.zeros_like(acc)

    x = jnp.concatenate([xa_ref[0], xb_ref[0]], axis=0)
    a = _dot(x, wg_ref[0, 0].astype(BF16))
    u = _dot(x, wu_ref[0, 0].astype(BF16))
    acc[...] += _dot(((a * jax.nn.sigmoid(a)) * u).astype(BF16), wd_ref[0, 0].astype(BF16))

    @pl.when(j == pl.num_programs(1) - 1)
    def _():
        ya_ref[0] = acc[0:rows_a, :].astype(BF16)
        yb_ref[0] = acc[rows_a:, :].astype(BF16)


def _ffn(xs_a, xs_b, layer, w_gate, w_up, w_down):
    tf = FF_TILE
    rows_a, rows_b = xs_a.shape[1], xs_b.shape[1]
    x_spec = lambda rows: pl.BlockSpec((1, rows, D_MODEL), lambda e, j: (e, 0, 0))
    return pl.pallas_call(
        _ffn_kernel,
        grid=(N_EXPERTS, EXPERT_FF // tf),
        in_specs=[x_spec(rows_a), x_spec(rows_b),
                  pl.BlockSpec((1, 1, D_MODEL, tf), lambda e, j: (layer, e, 0, j)),
                  pl.BlockSpec((1, 1, D_MODEL, tf), lambda e, j: (layer, e, 0, j)),
                  pl.BlockSpec((1, 1, tf, D_MODEL), lambda e, j: (layer, e, j, 0))],
        out_specs=[x_spec(rows_a), x_spec(rows_b)],
        out_shape=[jax.ShapeDtypeStruct(xs_a.shape, BF16), jax.ShapeDtypeStruct(xs_b.shape, BF16)],
        scratch_shapes=[pltpu.VMEM((rows_a + rows_b, D_MODEL), F32)],
        compiler_params=_params(2, FFN_VMEM_LIMIT_BYTES),
        name=f"ffn_l{layer}",
    )(xs_a, xs_b, w_gate, w_up, w_down)


def _scatter_kernel(cap, seq, final, y_ref, pos_ref, g_ref, x_ref, gate_ref, fg_ref, o_ref):
    slot = lax.broadcasted_iota(I32, (cap, seq), 0)
    for r in range(pos_ref.shape[0]):
        onehots, gated = [], []
        for e in range(N_EXPERTS):
            hit = slot == pos_ref[r, e:e + 1, :]
            gate = jnp.sum(jnp.where(hit, g_ref[r, e:e + 1, :], 0.0), axis=1, keepdims=True)
            gated.append((y_ref[e, r * cap:(r + 1) * cap, :].astype(F32) * gate).astype(BF16))
            onehots.append(hit.astype(BF16))
        moe = lax.dot_general(jnp.concatenate(onehots, axis=0), jnp.concatenate(gated, axis=0),
                              (((0,), (0,)), ((), ())), preferred_element_type=F32)
        rows = slice(r * seq, (r + 1) * seq)
        xn = x_ref[rows, :] + gate_ref[0] * moe
        if final:
            xn = xn * lax.rsqrt(jnp.mean(xn * xn, axis=-1, keepdims=True) + EPS) * fg_ref[...]
        o_ref[rows, :] = xn


def _scatter(y, pos, g, xn, mods, layer, stream, final, final_g_row):
    cap, seq, per = stream.cap, stream.seq, stream.req_per_step
    tok_blk = pl.BlockSpec((per * seq, D_MODEL), lambda i: (i, 0))
    sel_blk = pl.BlockSpec((per, N_EXPERTS, seq), lambda i: (i, 0, 0))
    return pl.pallas_call(
        functools.partial(_scatter_kernel, cap, seq, final),
        grid=(stream.n_req // per,),
        in_specs=[pl.BlockSpec((N_EXPERTS, per * cap, D_MODEL), lambda i: (0, i, 0)),
                  sel_blk, sel_blk, tok_blk,
                  _mod_spec(layer, stream, 5, per * seq),
                  pl.BlockSpec((1, D_MODEL), lambda i: (0, 0))],
        out_specs=tok_blk,
        out_shape=jax.ShapeDtypeStruct((stream.tokens, D_MODEL), F32),
        compiler_params=_params(1),
        name=f"scatter_l{layer}_s{seq}",
    )(y, pos, g, xn, mods, final_g_row)


def _axial_rope_tables(rows, head_dim):
    row = jnp.repeat(jnp.arange(rows, dtype=F32), GRID_W)
    col = jnp.tile(jnp.arange(GRID_W, dtype=F32), rows)
    nf = head_dim // 4
    inv = ROPE_BASE ** (-jnp.arange(nf, dtype=F32) / nf)
    ar = row[:, None] * inv[None]
    ac = col[:, None] * inv[None]
    ang = jnp.concatenate([ar, ar, ac, ac], axis=-1)
    return jnp.cos(ang), jnp.sin(ang)


def _block_diag(w):
    eye = jnp.eye(LRU_BLOCKS, dtype=w.dtype)
    return jnp.einsum('dnkj,nm->dnkmj', w, eye).reshape(2, LRU_WIDTH, LRU_WIDTH)


def kernel(x_prompt, x_sample, cache_win_k, cache_win_v, state_lru, cache_diff_k, cache_diff_v, c, c_ctx, ada_w, ada_b, norm_g, final_g, even_w_in, even_w_out, conv_w, conv_b, lru_wa, lru_ba, lru_wx, lru_bx, lru_lambda, win_sink, odd_w_in, odd_w_out, diff_lq1, diff_lk1, diff_lq2, diff_lk2, diff_subln_g, moe_router, moe_w_gate, moe_w_up, moe_w_down):
    cv_t = jnp.concatenate([c_ctx[None], c, jnp.zeros((COND_ROWS - N_COND, D_MODEL), F32)], axis=0).T
    mods = _adaln(cv_t, ada_w, ada_b).reshape(DEPTH * COND_ROWS, 1, 6 * D_MODEL)

    cos, sin = _axial_rope_tables(DEC_SEQ // GRID_W, WIN_HEAD_DIM)
    rope_win = (jnp.tile(cos, (1, WIN_HEADS)), jnp.tile(sin, (1, WIN_HEADS)))
    rope_diff = (jnp.tile(cos, (1, 2)), jnp.tile(sin, (1, 2)))

    streams = (CTX, LAT)
    xs = [x_prompt.reshape(CTX.tokens, D_MODEL), x_sample.reshape(LAT.tokens, D_MODEL)]
    final_g_row = final_g.reshape(1, D_MODEL)
    outs = {}

    for layer in range(DEPTH):
        idx = layer // 2
        even = layer % 2 == 0
        w_in = (even_w_in if even else odd_w_in)[idx].astype(BF16)
        w_out = (even_w_out if even else odd_w_out)[idx].astype(BF16)
        wr_t = moe_router[layer].T
        g1_row = norm_g[layer, 0].reshape(1, D_MODEL)
        g2_row = norm_g[layer, 1].reshape(1, D_MODEL)
        if even:
            wa = _block_diag(lru_wa[idx]).astype(BF16)
            wx = _block_diag(lru_wx[idx]).astype(BF16)

        routed = []
        for si, stream in enumerate(streams):
            is_ctx = stream is CTX
            if even or not is_ctx:
                proj = _proj(xs[si], g1_row, mods, layer, stream, w_in)
            if even:
                h0 = jnp.zeros((stream.n_req, 2, LRU_WIDTH), F32) if is_ctx else state_lru[:, idx]
                y_lru, h_fin = _lru(proj, stream, conv_w[idx], conv_b[idx].reshape(1, LRU_WIDTH), wa, wx,
                                    lru_ba[idx], lru_bx[idx], lru_lambda[idx], h0)
                if is_ctx:
                    o = _win_attn(proj, stream, win_sink[idx])
                    k0 = 2 * LRU_WIDTH + WIN_Q
                    to_heads = lambda t: t.reshape(BATCH, SEQ, WIN_KV_HEADS, WIN_HEAD_DIM).transpose(0, 2, 1, 3)[:, None]
                    outs["win_k"] = to_heads(proj[:, k0:k0 + WIN_KV])
                    outs["win_v"] = to_heads(proj[:, k0 + WIN_KV:k0 + 2 * WIN_KV])
                    outs["lru"] = h_fin[:, None]
                else:
                    pack = lambda t: t[:, idx].transpose(0, 2, 1, 3).reshape(DEC_BATCH, PAST_LEN, WIN_KV)
                    o = _win_attn(proj, stream, win_sink[idx],
                                  cache=(pack(cache_win_k), pack(cache_win_v)), rope=rope_win)
                mixed = [y_lru, o]
            else:
                lambda_init = 0.8 - 0.6 * math.exp(-0.3 * layer)
                vec = lambda t: t[idx].reshape(1, DIFF_HEAD_DIM)
                args = (vec(diff_lq1), vec(diff_lk1), vec(diff_lq2), vec(diff_lk2), diff_subln_g[idx])
                if is_ctx:
                    q, kh, vh = _proj_heads(xs[si], g1_row, mods, layer, stream, w_in)
                    o = _diff_attn((q, kh, vh), stream, lambda_init, *args)
                    outs["diff_k"] = kh[:, None]
                    outs["diff_v"] = vh[:, None]
                else:
                    o = _diff_attn(proj, stream, lambda_init, *args,
                                   cache=(cache_diff_k[:, idx:idx + 1], cache_diff_v[:, idx:idx + 1]), rope=rope_diff)
                mixed = [o]
            xn, h2, lg = _mix_out(mixed, w_out, xs[si], g2_row, mods, layer, stream, wr_t)
            per_req = stream.seq // LOGIT_TILE
            lg = lg.reshape(stream.n_req, per_req, N_EXPERTS, LOGIT_TILE).transpose(0, 2, 1, 3)
            pos, gate = _router(lg.reshape(stream.n_req, N_EXPERTS, stream.seq), stream)
            routed.append((xn, pos, gate, _gather(h2, pos, stream)))

        ys = _ffn(routed[0][3], routed[1][3], layer, moe_w_gate, moe_w_up, moe_w_down)
        final = layer == DEPTH - 1
        for si, stream in enumerate(streams):
            xn, pos, gate, _ = routed[si]
            xs[si] = _scatter(ys[si], pos, gate, xn, mods, layer, stream, final, final_g_row)

    y_prompt = xs[0].reshape(BATCH, SEQ, D_MODEL)
    y_sample = xs[1].reshape(DEC_BATCH, DEC_SEQ, D_MODEL)
    return (y_prompt, y_sample, outs["win_k"], outs["win_v"], outs["lru"], outs["diff_k"], outs["diff_v"])
```

```python
import dataclasses
import functools
import math
from typing import Callable, NamedTuple

import jax
import jax.numpy as jnp
from jax import lax
from jax.experimental import pallas as pl
from jax.experimental.pallas import tpu as pltpu

F32 = jnp.float32
BF16 = jnp.bfloat16
I32 = jnp.int32

D_MODEL = 1024
BATCH = 16
SEQ = 256
DEPTH = 2
DEC_BATCH = 2
DEC_SEQ = 1024
PAST_LEN = 512
GRID_W = 64
LRU_WIDTH = D_MODEL // 2
LRU_BLOCKS = 8
LRU_BLOCK = LRU_WIDTH // LRU_BLOCKS
CONV_W = 4
LRU_C = 8.0
WIN_HEADS = 8
WIN_KV_HEADS = 2
WIN_REP = WIN_HEADS // WIN_KV_HEADS
WIN_HEAD_DIM = 64
WINDOW = 128
WIN_Q = WIN_HEADS * WIN_HEAD_DIM
WIN_KV = WIN_KV_HEADS * WIN_HEAD_DIM
EVEN_IN = 2 * LRU_WIDTH + WIN_Q + 2 * WIN_KV
DIFF_HEADS = 8
DIFF_HEAD_DIM = 64
DIFF_QK = DIFF_HEADS * 2 * DIFF_HEAD_DIM
DIFF_V = DIFF_HEADS * 2 * DIFF_HEAD_DIM
ODD_IN = 2 * DIFF_QK + DIFF_V
N_EXPERTS = 16
EXPERT_FF = 2 * D_MODEL
CAPACITY_FACTOR = 2
ROPE_BASE = 10000.0
EPS = 1e-6
NEG_INF = -1e30

LANES = 128
SUBLANES = 8
VMEM_LIMIT_BYTES = 48 * 1024 * 1024

N_COND = 1 + DEC_BATCH
COND_ROWS = SUBLANES
TOKEN_TILE = 512
MIX_TILE = 512
ROUTE_ROWS = 512
LOGIT_TILE = 256
FF_TILE = 1024
FFN_VMEM_LIMIT_BYTES = 56 * 1024 * 1024
PAIR_VMEM_LIMIT_BYTES = 56 * 1024 * 1024
LRU_CHUNK = 128
ATT_Q_TILE = 256
ATT_K_TILE = 256


class Stream:
    def __init__(self, n_req, seq, cond0, cond_step):
        self.n_req, self.seq, self.cond0, self.cond_step = n_req, seq, cond0, cond_step
        self.tokens = n_req * seq
        self.cap = CAPACITY_FACTOR * seq // N_EXPERTS
        self.req_per_step = max(1, ROUTE_ROWS // seq)

    def cond_of_row(self, row):
        return self.cond0 + self.cond_step * (row // self.seq)


CTX = Stream(BATCH, SEQ, 0, 0)
LAT = Stream(DEC_BATCH, DEC_SEQ, 1, 1)


def _params(n_axes, vmem_limit_bytes=VMEM_LIMIT_BYTES):
    return pltpu.CompilerParams(dimension_semantics=("arbitrary",) * n_axes,
                                vmem_limit_bytes=vmem_limit_bytes)


class Call(NamedTuple):
    kernel: Callable
    steps: int
    in_specs: tuple
    out_specs: tuple
    out_shapes: tuple
    scratch: tuple
    args: tuple
    name: str


def _deferred(kernel, *, grid, in_specs, out_specs, out_shape, scratch_shapes=(), name):
    (steps,) = grid
    as_tuple = lambda v: tuple(v) if isinstance(v, (list, tuple)) else (v,)
    return lambda *args: Call(kernel, steps, tuple(in_specs), as_tuple(out_specs), as_tuple(out_shape),
                              tuple(scratch_shapes), args, name)


def _run(call):
    return pl.pallas_call(
        call.kernel, grid=(call.steps,), in_specs=list(call.in_specs), out_specs=list(call.out_specs),
        out_shape=list(call.out_shapes), scratch_shapes=list(call.scratch),
        compiler_params=_params(1), name=call.name)(*call.args)


def _shifted(spec, first, steps):
    if spec.index_map is None:
        return spec
    return dataclasses.replace(spec, index_map=lambda i: spec.index_map(jnp.clip(i - first, 0, steps - 1)))


def _pair_kernel(a, b, *refs):
    n_in, n_out = len(a.in_specs) + len(b.in_specs), len(a.out_specs) + len(b.out_specs)
    ins, outs, scr = refs[:n_in], refs[n_in:n_in + n_out], refs[n_in + n_out:]
    i = pl.program_id(0)

    @pl.when(i < a.steps)
    def _():
        a.kernel(*ins[:len(a.in_specs)], *outs[:len(a.out_specs)], *scr[:len(a.scratch)])

    @pl.when(i >= a.steps)
    def _():
        b.kernel(*ins[len(a.in_specs):], *outs[len(a.out_specs):], *scr[len(a.scratch):])


def _run_pair(a, b):
    specs = lambda sa, sb: ([_shifted(s, 0, a.steps) for s in sa] + [_shifted(s, a.steps, b.steps) for s in sb])
    outs = pl.pallas_call(
        functools.partial(_pair_kernel, a, b), grid=(a.steps + b.steps,),
        in_specs=specs(a.in_specs, b.in_specs), out_specs=specs(a.out_specs, b.out_specs),
        out_shape=list(a.out_shapes + b.out_shapes), scratch_shapes=list(a.scratch + b.scratch),
        compiler_params=_params(1, PAIR_VMEM_LIMIT_BYTES), name=f"{a.name}+{b.name}")(*a.args, *b.args)
    return outs[:len(a.out_specs)], outs[len(a.out_specs):]


def _resident_spec(shape):
    return pl.BlockSpec(shape, lambda i: (0, 0), pipeline_mode=pl.Buffered(1))


def _mod_spec(layer, stream, k, rows_per_step):
    return pl.BlockSpec(
        (1, 1, D_MODEL),
        lambda i, *_: (layer * COND_ROWS + stream.cond_of_row(i * rows_per_step), 0, k))


def _dot(a, b):
    return jnp.dot(a, b, preferred_element_type=F32)


def _dot_nt(a, b):
    return lax.dot_general(a, b, (((1,), (1,)), ((), ())), preferred_element_type=F32)


def _modnorm(x, g, shift, scale):
    y = x * lax.rsqrt(jnp.mean(x * x, axis=-1, keepdims=True) + EPS)
    return (y * g) * (1.0 + scale) + shift


def _lane_half_masks(shape):
    lane = lax.broadcasted_iota(I32, shape, len(shape) - 1)
    left = (lane & (LANES - 1)) < LANES // 2
    return left, jnp.logical_not(left)


def _rope(x, cos, sin):
    parts = []
    for c in range(x.shape[1] // LANES):
        xs = x[:, c * LANES:(c + 1) * LANES]
        lane = lax.broadcasted_iota(I32, xs.shape, 1)
        first = (lane & 31) < 16
        rot = jnp.where(first, -pltpu.roll(xs, LANES - 16, axis=1), pltpu.roll(xs, 16, axis=1))
        parts.append(xs * cos[:, c * LANES:(c + 1) * LANES] + rot * sin[:, c * LANES:(c + 1) * LANES])
    return parts[0] if len(parts) == 1 else jnp.concatenate(parts, axis=1)


def _adaln_kernel(cv_ref, w_ref, b_ref, o_ref):
    cv = cv_ref[...]
    s = cv * jax.nn.sigmoid(cv)
    w = w_ref[0]
    ridx = lax.broadcasted_iota(I32, (COND_ROWS, w.shape[1]), 0)
    out = jnp.zeros((COND_ROWS, w.shape[1]), F32)
    for r in range(N_COND):
        out = jnp.where(ridx == r, jnp.sum(w * s[:, r:r + 1], axis=0, keepdims=True), out)
    o_ref[0] = out + b_ref[0]


def _adaln(cv_t, ada_w, ada_b):
    tn = 1024
    return pl.pallas_call(
        _adaln_kernel,
        grid=(DEPTH, 6 * D_MODEL // tn),
        in_specs=[pl.BlockSpec((D_MODEL, COND_ROWS), lambda l, j: (0, 0)),
                  pl.BlockSpec((1, D_MODEL, tn), lambda l, j: (l, 0, j)),
                  pl.BlockSpec((1, 1, tn), lambda l, j: (l, 0, j))],
        out_specs=pl.BlockSpec((1, COND_ROWS, tn), lambda l, j: (l, 0, j)),
        out_shape=jax.ShapeDtypeStruct((DEPTH, COND_ROWS, 6 * D_MODEL), F32),
        compiler_params=_params(2),
        name="adaln",
    )(cv_t, ada_w, ada_b.reshape(DEPTH, 1, 6 * D_MODEL))


def _proj_kernel(x_ref, g_ref, sh_ref, sc_ref, w_ref, o_ref):
    h = _modnorm(x_ref[...], g_ref[...], sh_ref[0], sc_ref[0])
    o_ref[...] = _dot(h.astype(BF16), w_ref[...])


def _proj(x, g_row, mods, layer, stream, w):
    n_out = w.shape[1]
    tm = TOKEN_TILE
    return _deferred(
        _proj_kernel,
        grid=(stream.tokens // tm,),
        in_specs=[pl.BlockSpec((tm, D_MODEL), lambda i: (i, 0)),
                  pl.BlockSpec((1, D_MODEL), lambda i: (0, 0)),
                  _mod_spec(layer, stream, 0, tm),
                  _mod_spec(layer, stream, 1, tm),
                  _resident_spec((D_MODEL, n_out))],
        out_specs=pl.BlockSpec((tm, n_out), lambda i: (i, 0)),
        out_shape=jax.ShapeDtypeStruct((stream.tokens, n_out), F32),
        name=f"proj_l{layer}_s{stream.seq}",
    )(x, g_row, mods, mods, w)


def _proj_heads_kernel(seq, x_ref, g_ref, sh_ref, sc_ref, w_ref, q_ref, k_ref, v_ref):
    h = _modnorm(x_ref[...], g_ref[...], sh_ref[0], sc_ref[0])
    res = _dot(h.astype(BF16), w_ref[...])
    q_ref[...] = res[:, 0:DIFF_QK]
    for r in range(x_ref.shape[0] // seq):
        rows = slice(r * seq, (r + 1) * seq)
        for hh in range(DIFF_HEADS):
            k_ref[r, hh] = res[rows, DIFF_QK + hh * LANES:DIFF_QK + (hh + 1) * LANES]
            v_ref[r, hh] = res[rows, 2 * DIFF_QK + hh * LANES:2 * DIFF_QK + (hh + 1) * LANES]


def _proj_heads(x, g_row, mods, layer, stream, w):
    tm, seq = TOKEN_TILE, stream.seq
    head_shape = (stream.n_req, DIFF_HEADS, seq, 2 * DIFF_HEAD_DIM)
    head_spec = pl.BlockSpec((tm // seq, DIFF_HEADS, seq, 2 * DIFF_HEAD_DIM), lambda i: (i, 0, 0, 0))
    return _deferred(
        functools.partial(_proj_heads_kernel, seq),
        grid=(stream.tokens // tm,),
        in_specs=[pl.BlockSpec((tm, D_MODEL), lambda i: (i, 0)),
                  pl.BlockSpec((1, D_MODEL), lambda i: (0, 0)),
                  _mod_spec(layer, stream, 0, tm),
                  _mod_spec(layer, stream, 1, tm),
                  _resident_spec((D_MODEL, ODD_IN))],
        out_specs=[pl.BlockSpec((tm, DIFF_QK), lambda i: (i, 0)), head_spec, head_spec],
        out_shape=[jax.ShapeDtypeStruct((stream.tokens, DIFF_QK), F32),
                   jax.ShapeDtypeStruct(head_shape, F32), jax.ShapeDtypeStruct(head_shape, F32)],
        name=f"proj_heads_l{layer}_s{seq}",
    )(x, g_row, mods, mods, w)


def _lru_kernel(seq, xl_ref, gl_ref, cw_ref, cb_ref, wa_ref, wx_ref, ba_ref, bx_ref, lam_ref, h0_ref,
                y_ref, hfin_ref, xpad, af, bf, ab, bb):
    width = LRU_WIDTH
    ch = LRU_CHUNK
    halo = SUBLANES
    xpad[0:halo, :] = jnp.zeros((halo, width), F32)
    xpad[halo + seq:2 * halo + seq, :] = jnp.zeros((halo, width), F32)
    xpad[halo:halo + seq, :] = xl_ref[...]

    lam = lam_ref[...]
    z = -lam
    softplus = jnp.maximum(z, 0.0) + jnp.log1p(jnp.exp(-jnp.abs(z)))
    sub = lax.broadcasted_iota(I32, (ch // SUBLANES, SUBLANES, width), 1)
    cw = cw_ref[...]
    cb = cb_ref[...]

    def gates_chunk(c, carry):
        r0 = pl.multiple_of(c * ch, ch)
        win = xpad[pl.ds(r0, ch + 2 * halo), :]
        n_win = ch + 2 * halo

        def tap(j):
            return pltpu.roll(win, n_win - (halo - 2 + j), axis=0)[0:ch]

        xc = tap(0) * cw[0:1]
        for j in range(1, CONV_W):
            xc = xc + tap(j) * cw[j:j + 1]
        xc = xc + cb
        xcb = xc.astype(BF16)
        for d, (a_s, b_s) in enumerate(((af, bf), (ab, bb))):
            r = jax.nn.sigmoid(_dot(xcb, wa_ref[d]) + ba_ref[d:d + 1])
            ig = jax.nn.sigmoid(_dot(xcb, wx_ref[d]) + bx_ref[d:d + 1])
            log_a = (-LRU_C * r) * softplus[d:d + 1]
            a = jnp.exp(log_a)
            v = 1.0 - a * a
            b = jnp.where(v > 0.0, v * lax.rsqrt(v), 0.0) * (ig * xc)
            a = a.reshape(ch // SUBLANES, SUBLANES, width)
            b = b.reshape(ch // SUBLANES, SUBLANES, width)
            for s in (1, 2, 4):
                keep = (sub >= s) if d == 0 else (sub < SUBLANES - s)
                shift = s if d == 0 else SUBLANES - s
                a_sh = jnp.where(keep, pltpu.roll(a, shift, axis=1), 1.0)
                b_sh = jnp.where(keep, pltpu.roll(b, shift, axis=1), 0.0)
                b = a * b_sh + b
                a = a * a_sh
            a_s[pl.ds(r0, ch), :] = a.reshape(ch, width)
            b_s[pl.ds(r0, ch), :] = b.reshape(ch, width)
        return carry

    lax.fori_loop(0, seq // ch, gates_chunk, 0)

    n_tiles = seq // SUBLANES
    h0 = h0_ref[0]

    def tile_step(k, carry):
        cf, cbw = carry
        rf = pl.multiple_of(k * SUBLANES, SUBLANES)
        rb = pl.multiple_of((n_tiles - 1 - k) * SUBLANES, SUBLANES)
        hf = af[pl.ds(rf, SUBLANES), :] * cf + bf[pl.ds(rf, SUBLANES), :]
        bf[pl.ds(rf, SUBLANES), :] = hf
        hb = ab[pl.ds(rb, SUBLANES), :] * cbw + bb[pl.ds(rb, SUBLANES), :]
        bb[pl.ds(rb, SUBLANES), :] = hb
        return hf[SUBLANES - 1:SUBLANES], hb[0:1]

    cf, cbw = lax.fori_loop(0, n_tiles, tile_step, (h0[0:1], h0[1:2]))
    hfin_ref[0, 0:1, :] = cf
    hfin_ref[0, 1:2, :] = cbw

    def out_chunk(c, carry):
        r0 = pl.multiple_of(c * ch, ch)
        hsum = bf[pl.ds(r0, ch), :] + bb[pl.ds(r0, ch), :]
        y_ref[pl.ds(r0, ch), :] = (hsum * jax.nn.gelu(gl_ref[pl.ds(r0, ch), :])).astype(BF16)
        return carry

    lax.fori_loop(0, seq // ch, out_chunk, 0)


def _lru(proj, stream, conv_w, conv_b, wa, wx, ba, bx, lam, h0):
    seq, width = stream.seq, LRU_WIDTH
    full2 = lambda b: (0, 0)
    full3 = lambda b: (0, 0, 0)
    return _deferred(
        functools.partial(_lru_kernel, seq),
        grid=(stream.n_req,),
        in_specs=[pl.BlockSpec((seq, width), lambda b: (b, 0)),
                  pl.BlockSpec((seq, width), lambda b: (b, 1)),
                  pl.BlockSpec((CONV_W, width), full2),
                  pl.BlockSpec((1, width), full2),
                  pl.BlockSpec((2, width, width), full3),
                  pl.BlockSpec((2, width, width), full3),
                  pl.BlockSpec((2, width), full2),
                  pl.BlockSpec((2, width), full2),
                  pl.BlockSpec((2, width), full2),
                  pl.BlockSpec((1, 2, width), lambda b: (b, 0, 0))],
        out_specs=[pl.BlockSpec((seq, width), lambda b: (b, 0)),
                   pl.BlockSpec((1, 2, width), lambda b: (b, 0, 0))],
        out_shape=[jax.ShapeDtypeStruct((stream.tokens, width), BF16),
                   jax.ShapeDtypeStruct((stream.n_req, 2, width), F32)],
        scratch_shapes=[pltpu.VMEM((seq + 2 * SUBLANES, width), F32)] + [pltpu.VMEM((seq, width), F32)] * 4,
        name=f"lru_s{seq}",
    )(proj, proj, conv_w, conv_b, wa, wx, ba, bx, lam, h0)


def _attend(q, chunks, s_ref):
    tile_max = None
    spans = []
    off = 0
    for keys, _, valid in chunks:
        s = _dot_nt(q, keys())
        if valid is not None:
            s = jnp.where(valid, s, NEG_INF)
        n = s.shape[1]
        s_ref[:, off:off + n] = s
        for c in range(n // LANES):
            t = s[:, c * LANES:(c + 1) * LANES]
            tile_max = t if tile_max is None else jnp.maximum(tile_max, t)
        spans.append((off, n))
        off += n
    m = jnp.max(tile_max, axis=-1, keepdims=True)
    acc = None
    for (_, values, _), (o, n) in zip(chunks, spans):
        part = _dot(jnp.exp(s_ref[:, o:o + n] - m).astype(BF16), values())
        acc = part if acc is None else acc + part
    return acc, m


def _split_groups(kk):
    left, right = _lane_half_masks(kk.shape)
    g0_l = jnp.where(left, kk, 0.0)
    g1_r = jnp.where(right, kk, 0.0)
    return ((g0_l, pltpu.roll(g0_l, LANES // 2, axis=1)), (pltpu.roll(g1_r, LANES // 2, axis=1), g1_r))


def _win_ctx_kernel(sink_ref, q_ref, kv_ref, o_ref):
    scale = WIN_HEAD_DIM ** -0.5
    seq = q_ref.shape[0]
    ks = _split_groups(kv_ref[:, 0:LANES])
    vs = _split_groups(kv_ref[:, LANES:2 * LANES])
    top = lax.broadcasted_iota(I32, (2 * seq, 1), 0) < seq
    outs = [None] * (WIN_HEADS // 2)
    for g in range(WIN_KV_HEADS):
        pairs = (2 * g, 2 * g + 1)
        qs = jnp.concatenate([q_ref[:, p * LANES:(p + 1) * LANES] for p in pairs], axis=0)
        qs = (qs * scale).astype(BF16)
        for side in range(2):
            sk = jnp.where(top, sink_ref[2 * pairs[0] + side], sink_ref[2 * pairs[1] + side])
            s = _dot_nt(qs, ks[g][side].astype(BF16))
            m = jnp.maximum(jnp.max(s, axis=-1, keepdims=True), sk)
            e = jnp.exp(s - m)
            den = jnp.sum(e, axis=-1, keepdims=True) + jnp.exp(sk - m)
            o = _dot(e.astype(BF16), vs[g][side].astype(BF16)) * (1.0 / den)
            for k, p in enumerate(pairs):
                part = o[k * seq:(k + 1) * seq]
                outs[p] = part if outs[p] is None else outs[p] + part
    for p in range(WIN_HEADS // 2):
        o_ref[:, p * LANES:(p + 1) * LANES] = outs[p].astype(BF16)


def _win_lat_kernel(sink_ref, q_ref, kv_ref, ck_ref, cv_ref, cos_ref, sin_ref, o_ref,
                    kl_s, vl_s, kc_s, vc_s):
    seq, wn = DEC_SEQ, WINDOW
    scale = WIN_HEAD_DIM ** -0.5
    kr = _rope(kv_ref[:, 0:LANES], cos_ref[:, 0:LANES], sin_ref[:, 0:LANES])
    for src, dst in ((_split_groups(kr), kl_s), (_split_groups(kv_ref[:, LANES:2 * LANES]), vl_s),
                     (_split_groups(ck_ref[0]), kc_s), (_split_groups(cv_ref[0]), vc_s)):
        for g in range(WIN_KV_HEADS):
            for side in range(2):
                dst[2 * g + side] = src[g][side].astype(BF16)

    top = lax.broadcasted_iota(I32, (2 * wn, 1), 0) < wn

    def q_block(i, carry):
        for sub in range(2):
            r0 = pl.multiple_of((2 * i + sub) * wn, wn)
            start = pl.multiple_of(jnp.clip((2 * i + sub - 1) * wn, 0, seq - 3 * wn), wn)
            qr = _rope(q_ref[pl.ds(r0, wn), :], cos_ref[pl.ds(r0, wn), :], sin_ref[pl.ds(r0, wn), :]) * scale
            qpos = r0 + (lax.broadcasted_iota(I32, (2 * wn, 3 * wn), 0) & (wn - 1))
            kpos = start + lax.broadcasted_iota(I32, (2 * wn, 3 * wn), 1)
            valid = jnp.abs(qpos - kpos) <= wn
            outs = [None] * (WIN_HEADS // 2)
            for g in range(WIN_KV_HEADS):
                pairs = (2 * g, 2 * g + 1)
                qs = jnp.concatenate([qr[:, p * LANES:(p + 1) * LANES] for p in pairs], axis=0).astype(BF16)
                for side in range(2):
                    idx = 2 * g + side
                    sk = jnp.where(top, sink_ref[2 * pairs[0] + side], sink_ref[2 * pairs[1] + side])
                    sl = _dot_nt(qs, kl_s[idx, pl.ds(start, 3 * wn), :])
                    sl = jnp.where(valid, sl, NEG_INF)
                    sc = _dot_nt(qs, kc_s[idx])
                    m = jnp.maximum(jnp.maximum(jnp.max(sl, axis=-1, keepdims=True),
                                                jnp.max(sc, axis=-1, keepdims=True)), sk)
                    el = jnp.exp(sl - m)
                    ec = jnp.exp(sc - m)
                    den = (jnp.sum(el, axis=-1, keepdims=True) + jnp.sum(ec, axis=-1, keepdims=True)
                           + jnp.exp(sk - m))
                    o = (_dot(el.astype(BF16), vl_s[idx, pl.ds(start, 3 * wn), :])
                         + _dot(ec.astype(BF16), vc_s[idx])) * (1.0 / den)
                    for k, p in enumerate(pairs):
                        part = o[k * wn:(k + 1) * wn]
                        outs[p] = part if outs[p] is None else outs[p] + part
            for p in range(WIN_HEADS // 2):
                o_ref[pl.ds(r0, wn), p * LANES:(p + 1) * LANES] = outs[p].astype(BF16)
        return carry

    lax.fori_loop(0, seq // (2 * wn), q_block, 0)


def _win_attn(proj, stream, sink, cache=None, rope=None):
    seq = stream.seq
    q_spec = pl.BlockSpec((seq, WIN_Q), lambda b: (b, 2 * LRU_WIDTH // WIN_Q))
    kv_spec = pl.BlockSpec((seq, 2 * WIN_KV), lambda b: (b, (2 * LRU_WIDTH + WIN_Q) // (2 * WIN_KV)))
    sink_spec = pl.BlockSpec(memory_space=pltpu.SMEM)
    out_spec = pl.BlockSpec((seq, WIN_Q), lambda b: (b, 0))
    out_shape = jax.ShapeDtypeStruct((stream.tokens, WIN_Q), BF16)
    if cache is None:
        return _deferred(
            _win_ctx_kernel, grid=(stream.n_req,),
            in_specs=[sink_spec, q_spec, kv_spec], out_specs=out_spec, out_shape=out_shape,
            name="win_attn_ctx",
        )(sink, proj, proj)
    ck, cv = cache
    cos, sin = rope
    cache_spec = pl.BlockSpec((1, PAST_LEN, LANES), lambda b: (b, 0, 0))
    table_spec = pl.BlockSpec((seq, WIN_Q), lambda b: (0, 0))
    return _deferred(
        _win_lat_kernel, grid=(stream.n_req,),
        in_specs=[sink_spec, q_spec, kv_spec, cache_spec, cache_spec, table_spec, table_spec],
        out_specs=out_spec, out_shape=out_shape,
        scratch_shapes=[pltpu.VMEM((4, seq, LANES), BF16), pltpu.VMEM((4, seq, LANES), BF16),
                        pltpu.VMEM((4, PAST_LEN, LANES), BF16), pltpu.VMEM((4, PAST_LEN, LANES), BF16)],
        name="win_attn_lat",
    )(sink, proj, proj, ck, cv, cos, sin)


def _diff_lambda(lq1_ref, lk1_ref, lq2_ref, lk2_ref, lambda_init):
    t1 = jnp.sum(lq1_ref[...] * lk1_ref[...], axis=-1, keepdims=True)
    t2 = jnp.sum(lq2_ref[...] * lk2_ref[...], axis=-1, keepdims=True)
    return jnp.exp(t1) - jnp.exp(t2) + lambda_init


def _subln(o, g_row, lambda_init):
    o = o * lax.rsqrt(jnp.mean(o * o, axis=-1, keepdims=True) + EPS) * g_row
    return o * (1.0 - lambda_init)


def _component_keys(k):
    left, right = _lane_half_masks(k.shape)
    return jnp.where(left, k, 0.0).astype(BF16), jnp.where(right, k, 0.0).astype(BF16)


def _values_with_ones(v):
    return jnp.concatenate([v.astype(BF16), jnp.ones(v.shape, BF16)], axis=1)


def _diff_combine(accs, lam):
    o1, o2 = accs[0][:, 0:LANES], accs[1][:, 0:LANES]
    return o1 * (1.0 / accs[0][:, LANES:]) - o2 * (lam * (1.0 / accs[1][:, LANES:]))


def _diff_ctx_kernel(lambda_init, lq1_ref, lk1_ref, lq2_ref, lk2_ref, sg_ref, q_ref, k_ref, v_ref, o_ref, s_ref):
    scale = DIFF_HEAD_DIM ** -0.5
    tq = ATT_Q_TILE
    lam = _diff_lambda(lq1_ref, lk1_ref, lq2_ref, lk2_ref, lambda_init)
    for h in range(DIFF_HEADS):
        cols = slice(h * LANES, (h + 1) * LANES)
        keys = _component_keys(k_ref[0, h])
        vals = _values_with_ones(v_ref[0, h])
        for qt in range(q_ref.shape[0] // tq):
            rows = slice(qt * tq, (qt + 1) * tq)
            q = (q_ref[rows, cols] * scale).astype(BF16)
            slots = [s_ref.at[(4 * h + 2 * qt + c) % s_ref.shape[0]] for c in range(2)]
            accs = [_attend(q, [(lambda: keys[c], lambda: vals, None)], slots[c])[0] for c in range(2)]
            o = _diff_combine(accs, lam)
            o_ref[rows, cols] = _subln(o, sg_ref[h:h + 1, :], lambda_init).astype(BF16)


def _diff_lat_kernel(lambda_init, lq1_ref, lk1_ref, lq2_ref, lk2_ref, sg_ref, q_ref, k_ref, v_ref,
                     ck_ref, cv_ref, cos_ref, sin_ref, o_ref, kl_s, kc_s, vl_s, vc_s, s_ref):
    seq, tq, tk = DEC_SEQ, ATT_Q_TILE, ATT_K_TILE
    scale = DIFF_HEAD_DIM ** -0.5
    lam = _diff_lambda(lq1_ref, lk1_ref, lq2_ref, lk2_ref, lambda_init)
    for src, dst in ((_rope(k_ref[...], cos_ref[...], sin_ref[...]), kl_s), (ck_ref[0, 0, 0], kc_s)):
        dst[0], dst[1] = _component_keys(src)
    vl_s[...] = _values_with_ones(v_ref[...])
    vc_s[...] = _values_with_ones(cv_ref[0, 0, 0])
    sg = sg_ref[0]

    for qt in range(seq // tq):
        rows = slice(qt * tq, (qt + 1) * tq)
        qr = (_rope(q_ref[rows, :], cos_ref[rows, :], sin_ref[rows, :]) * scale).astype(BF16)
        accs = []
        for comp in range(2):
            chunks = [(lambda j=j: kl_s[comp, j * tk:(j + 1) * tk, :],
                       lambda j=j: vl_s[j * tk:(j + 1) * tk, :], None) for j in range(seq // tk)]
            chunks += [(lambda j=j: kc_s[comp, j * tk:(j + 1) * tk, :],
                        lambda j=j: vc_s[j * tk:(j + 1) * tk, :], None) for j in range(PAST_LEN // tk)]
            accs.append(_attend(qr, chunks, s_ref.at[2 * qt + comp])[0])
        o_ref[rows, :] = _subln(_diff_combine(accs, lam), sg, lambda_init).astype(BF16)


def _diff_attn(proj, stream, lambda_init, lq1, lk1, lq2, lk2, subln_g, cache=None, rope=None):
    seq = stream.seq
    vec_spec = lambda nd: pl.BlockSpec((1, DIFF_HEAD_DIM), lambda *_: (0, 0))
    out_shape = jax.ShapeDtypeStruct((stream.tokens, DIFF_V), BF16)
    if cache is None:
        q, kh, vh = proj
        head_spec = pl.BlockSpec((1, DIFF_HEADS, seq, LANES), lambda b: (b, 0, 0, 0))
        return _deferred(
            functools.partial(_diff_ctx_kernel, lambda_init), grid=(stream.n_req,),
            in_specs=[vec_spec(1)] * 4 + [pl.BlockSpec((DIFF_HEADS, LANES), lambda b: (0, 0)),
                                          pl.BlockSpec((seq, DIFF_QK), lambda b: (b, 0)), head_spec, head_spec],
            out_specs=pl.BlockSpec((seq, DIFF_V), lambda b: (b, 0)), out_shape=out_shape,
            scratch_shapes=[pltpu.VMEM((8, ATT_Q_TILE, seq), F32)],
            name="diff_attn_ctx",
        )(lq1, lk1, lq2, lk2, subln_g, q, kh, vh)
    ck, cv = cache
    cos, sin = rope
    nh = DIFF_HEADS
    blk = lambda c: pl.BlockSpec((seq, LANES), lambda i: (i // nh, c * nh + i % nh))
    cache_spec = pl.BlockSpec((1, 1, 1, PAST_LEN, LANES), lambda i: (i // nh, 0, i % nh, 0, 0))
    table_spec = pl.BlockSpec((seq, LANES), lambda i: (0, 0))
    return _deferred(
        functools.partial(_diff_lat_kernel, lambda_init), grid=(stream.n_req * nh,),
        in_specs=[vec_spec(2)] * 4 + [pl.BlockSpec((1, 1, LANES), lambda i: (i % nh, 0, 0)),
                                      blk(0), blk(1), blk(2), cache_spec, cache_spec, table_spec, table_spec],
        out_specs=pl.BlockSpec((seq, LANES), lambda i: (i // nh, i % nh)), out_shape=out_shape,
        scratch_shapes=[pltpu.VMEM((2, seq, LANES), BF16), pltpu.VMEM((2, PAST_LEN, LANES), BF16),
                        pltpu.VMEM((seq, 2 * LANES), BF16), pltpu.VMEM((PAST_LEN, 2 * LANES), BF16),
                        pltpu.VMEM((2 * seq // ATT_Q_TILE, ATT_Q_TILE, seq + PAST_LEN), F32)],
        name="diff_attn_lat",
    )(lq1, lk1, lq2, lk2, subln_g.reshape(DIFF_HEADS, 1, LANES), proj, proj, proj, ck, cv, cos, sin)


def _mix_out_kernel(n_in, *refs):
    a_refs = refs[:n_in]
    w_ref, x_ref, gate_ref, g2_ref, sh2_ref, sc2_ref, wr_ref, xn_ref, h2_ref, lg_ref = refs[n_in:]
    kp = D_MODEL // n_in
    wr = wr_ref[...]
    w_hi = wr.astype(BF16)
    w_lo = (wr - w_hi.astype(F32)).astype(BF16)
    acc = None
    for k, a_ref in enumerate(a_refs):
        part = _dot(a_ref[...], w_ref[k * kp:(k + 1) * kp, :])
        acc = part if acc is None else acc + part
    xn = x_ref[...] + gate_ref[0] * acc
    xn_ref[...] = xn
    h2 = _modnorm(xn, g2_ref[...], sh2_ref[0], sc2_ref[0])
    h_hi = h2.astype(BF16)
    h2_ref[...] = h_hi
    h_lo = (h2 - h_hi.astype(F32)).astype(BF16)
    by_hi = _dot_nt(jnp.concatenate([w_hi, w_lo], axis=0), h_hi)
    lg = by_hi[0:N_EXPERTS] + (by_hi[N_EXPERTS:] + _dot_nt(w_hi, h_lo))
    for c in range(lg.shape[1] // LOGIT_TILE):
        lg_ref[c] = lg[:, c * LOGIT_TILE:(c + 1) * LOGIT_TILE]


def _mix_out(mixed, w_out, x, g2_row, mods, layer, stream, wr_t):
    tm = MIX_TILE
    n_in = len(mixed)
    kp = D_MODEL // n_in
    row_blk = lambda width: pl.BlockSpec((tm, width), lambda i: (i, 0))
    return _deferred(
        functools.partial(_mix_out_kernel, n_in),
        grid=(stream.tokens // tm,),
        in_specs=[row_blk(kp)] * n_in + [
            _resident_spec((D_MODEL, D_MODEL)),
            row_blk(D_MODEL),
            _mod_spec(layer, stream, 2, tm),
            pl.BlockSpec((1, D_MODEL), lambda i: (0, 0)),
            _mod_spec(layer, stream, 3, tm),
            _mod_spec(layer, stream, 4, tm),
            pl.BlockSpec((N_EXPERTS, D_MODEL), lambda i: (0, 0))],
        out_specs=[row_blk(D_MODEL), row_blk(D_MODEL),
                   pl.BlockSpec((tm // LOGIT_TILE, N_EXPERTS, LOGIT_TILE), lambda i: (i, 0, 0))],
        out_shape=[jax.ShapeDtypeStruct((stream.tokens, D_MODEL), F32),
                   jax.ShapeDtypeStruct((stream.tokens, D_MODEL), BF16),
                   jax.ShapeDtypeStruct((stream.tokens // LOGIT_TILE, N_EXPERTS, LOGIT_TILE), F32)],
        name=f"mix_out_l{layer}_s{stream.seq}",
    )(*mixed, w_out, x, mods, g2_row, mods, mods, wr_t)


def _sort_desc_lanes(x):
    rows, n = x.shape
    tiles = [x[:, c * LANES:(c + 1) * LANES] for c in range(n // LANES)]
    lane = lax.broadcasted_iota(I32, (rows, LANES), 1)
    k = 2
    while k <= n:
        j = k // 2
        while j >= 1:
            if j < LANES:
                lower = (lane & j) == 0
                for c in range(len(tiles)):
                    t = tiles[c]
                    partner = jnp.where(lower, pltpu.roll(t, LANES - j, axis=1), pltpu.roll(t, j, axis=1))
                    desc = ((lane & k) == 0) if k < LANES else (((c * LANES) & k) == 0)
                    take_max = (lower == desc) if k < LANES else (lower if desc else jnp.logical_not(lower))
                    tiles[c] = jnp.where(take_max, jnp.maximum(t, partner), jnp.minimum(t, partner))
            else:
                jc = j // LANES
                new = list(tiles)
                for c in range(len(tiles)):
                    take_max = ((c & jc) == 0) == (((c * LANES) & k) == 0)
                    new[c] = (jnp.maximum if take_max else jnp.minimum)(tiles[c], tiles[c ^ jc])
                tiles = new
            j //= 2
        k *= 2
    return tiles


def _router_kernel(cap, lg_ref, pos_ref, g_ref):
    x = lg_ref[...]
    n_b, n_e, n_tok = x.shape
    e = jnp.exp(x - jnp.max(x, axis=1, keepdims=True))
    aff = (e / jnp.sum(e, axis=1, keepdims=True)).reshape(n_b * n_e, n_tok)
    srt = _sort_desc_lanes(aff)
    lane_k = (cap - 1) % LANES
    thr = srt[(cap - 1) // LANES][:, lane_k:lane_k + 1]
    gt = aff > thr
    eq = aff == thr
    n_gt = jnp.sum(gt.astype(F32), axis=1, keepdims=True)
    before = (lax.broadcasted_iota(I32, (n_tok, n_tok), 0)
              < lax.broadcasted_iota(I32, (n_tok, n_tok), 1)).astype(BF16)
    eq_rank = _dot(eq.astype(BF16), before)
    sel = gt | (eq & (eq_rank < cap - n_gt))
    slot = _dot(sel.astype(BF16), before).astype(I32)
    pos_ref[...] = jnp.where(sel, slot, -1).reshape(n_b, n_e, n_tok)
    g_ref[...] = jnp.where(sel, aff, 0.0).reshape(n_b, n_e, n_tok)


def _router(logits, stream):
    shape = (stream.n_req, N_EXPERTS, stream.seq)
    spec = pl.BlockSpec(shape, lambda i: (0, 0, 0))
    return _deferred(
        functools.partial(_router_kernel, stream.cap), grid=(1,),
        in_specs=[spec], out_specs=[spec, spec],
        out_shape=[jax.ShapeDtypeStruct(shape, I32), jax.ShapeDtypeStruct(shape, F32)],
        name=f"router_s{stream.seq}",
    )(logits)


def _gather_kernel(cap, seq, h_ref, pos_ref, xs_ref):
    slot = lax.broadcasted_iota(I32, (cap, seq), 0)
    for r in range(pos_ref.shape[0]):
        onehot = jnp.concatenate([(slot == pos_ref[r, e:e + 1, :]).astype(BF16) for e in range(N_EXPERTS)], axis=0)
        xs = _dot(onehot, h_ref[r * seq:(r + 1) * seq, :]).astype(BF16)
        for e in range(N_EXPERTS):
            xs_ref[e, r * cap:(r + 1) * cap, :] = xs[e * cap:(e + 1) * cap]


def _gather(h2, pos, stream):
    cap, seq, per = stream.cap, stream.seq, stream.req_per_step
    return _deferred(
        functools.partial(_gather_kernel, cap, seq),
        grid=(stream.n_req // per,),
        in_specs=[pl.BlockSpec((per * seq, D_MODEL), lambda i: (i, 0)),
                  pl.BlockSpec((per, N_EXPERTS, seq), lambda i: (i, 0, 0))],
        out_specs=pl.BlockSpec((N_EXPERTS, per * cap, D_MODEL), lambda i: (0, i, 0)),
        out_shape=jax.ShapeDtypeStruct((N_EXPERTS, stream.n_req * cap, D_MODEL), BF16),
        name=f"gather_s{seq}",
    )(h2, pos)


def _ffn_kernel(xa_ref, xb_ref, wg_ref, wu_ref, wd_ref, ya_ref, yb_ref, acc):
    j = pl.program_id(1)
    rows_a = xa_ref.shape[1]

    @pl.when(j == 0)
    def _():
        acc[...] = jnp.zeros_like(acc)

    x = jnp.concatenate([xa_ref[0], xb_ref[0]], axis=0)
    a = _dot(x, wg_ref[0, 0].astype(BF16))
    u = _dot(x, wu_ref[0, 0].astype(BF16))
    acc[...] += _dot(((a * jax.nn.sigmoid(a)) * u).astype(BF16), wd_ref[0, 0].astype(BF16))

    @pl.when(j == pl.num_programs(1) - 1)
    def _():
        ya_ref[0] = acc[0:rows_a, :].astype(BF16)
        yb_ref[0] = acc[rows_a:, :].astype(BF16)


def _ffn(xs_a, xs_b, layer, w_gate, w_up, w_down):
    tf = FF_TILE
    rows_a, rows_b = xs_a.shape[1], xs_b.shape[1]
    x_spec = lambda rows: pl.BlockSpec((1, rows, D_MODEL), lambda e, j: (e, 0, 0))
    return pl.pallas_call(
        _ffn_kernel,
        grid=(N_EXPERTS, EXPERT_FF // tf),
        in_specs=[x_spec(rows_a), x_spec(rows_b),
                  pl.BlockSpec((1, 1, D_MODEL, tf), lambda e, j: (layer, e, 0, j)),
                  pl.BlockSpec((1, 1, D_MODEL, tf), lambda e, j: (layer, e, 0, j)),
                  pl.BlockSpec((1, 1, tf, D_MODEL), lambda e, j: (layer, e, j, 0))],
        out_specs=[x_spec(rows_a), x_spec(rows_b)],
        out_shape=[jax.ShapeDtypeStruct(xs_a.shape, BF16), jax.ShapeDtypeStruct(xs_b.shape, BF16)],
        scratch_shapes=[pltpu.VMEM((rows_a + rows_b, D_MODEL), F32)],
        compiler_params=_params(2, FFN_VMEM_LIMIT_BYTES),
        name=f"ffn_l{layer}",
    )(xs_a, xs_b, w_gate, w_up, w_down)


def _scatter_kernel(cap, seq, final, y_ref, pos_ref, g_ref, x_ref, gate_ref, fg_ref, o_ref):
    slot = lax.broadcasted_iota(I32, (cap, seq), 0)
    for r in range(pos_ref.shape[0]):
        onehots, gated = [], []
        for e in range(N_EXPERTS):
            hit = slot == pos_ref[r, e:e + 1, :]
            gate = jnp.sum(jnp.where(hit, g_ref[r, e:e + 1, :], 0.0), axis=1, keepdims=True)
            gated.append((y_ref[e, r * cap:(r + 1) * cap, :].astype(F32) * gate).astype(BF16))
            onehots.append(hit.astype(BF16))
        moe = lax.dot_general(jnp.concatenate(onehots, axis=0), jnp.concatenate(gated, axis=0),
                              (((0,), (0,)), ((), ())), preferred_element_type=F32)
        rows = slice(r * seq, (r + 1) * seq)
        xn = x_ref[rows, :] + gate_ref[0] * moe
        if final:
            xn = xn * lax.rsqrt(jnp.mean(xn * xn, axis=-1, keepdims=True) + EPS) * fg_ref[...]
        o_ref[rows, :] = xn


def _scatter(y, pos, g, xn, mods, layer, stream, final, final_g_row):
    cap, seq, per = stream.cap, stream.seq, stream.req_per_step
    tok_blk = pl.BlockSpec((per * seq, D_MODEL), lambda i: (i, 0))
    sel_blk = pl.BlockSpec((per, N_EXPERTS, seq), lambda i: (i, 0, 0))
    return _deferred(
        functools.partial(_scatter_kernel, cap, seq, final),
        grid=(stream.n_req // per,),
        in_specs=[pl.BlockSpec((N_EXPERTS, per * cap, D_MODEL), lambda i: (0, i, 0)),
                  sel_blk, sel_blk, tok_blk,
                  _mod_spec(layer, stream, 5, per * seq),
                  pl.BlockSpec((1, D_MODEL), lambda i: (0, 0))],
        out_specs=tok_blk,
        out_shape=jax.ShapeDtypeStruct((stream.tokens, D_MODEL), F32),
        name=f"scatter_l{layer}_s{seq}",
    )(y, pos, g, xn, mods, final_g_row)


def _axial_rope_tables(rows, head_dim):
    row = jnp.repeat(jnp.arange(rows, dtype=F32), GRID_W)
    col = jnp.tile(jnp.arange(GRID_W, dtype=F32), rows)
    nf = head_dim // 4
    inv = ROPE_BASE ** (-jnp.arange(nf, dtype=F32) / nf)
    ar = row[:, None] * inv[None]
    ac = col[:, None] * inv[None]
    ang = jnp.concatenate([ar, ar, ac, ac], axis=-1)
    return jnp.cos(ang), jnp.sin(ang)


def _block_diag(w):
    eye = jnp.eye(LRU_BLOCKS, dtype=w.dtype)
    return jnp.einsum('dnkj,nm->dnkmj', w, eye).reshape(2, LRU_WIDTH, LRU_WIDTH)


def kernel(x_prompt, x_sample, cache_win_k, cache_win_v, state_lru, cache_diff_k, cache_diff_v, c, c_ctx, ada_w, ada_b, norm_g, final_g, even_w_in, even_w_out, conv_w, conv_b, lru_wa, lru_ba, lru_wx, lru_bx, lru_lambda, win_sink, odd_w_in, odd_w_out, diff_lq1, diff_lk1, diff_lq2, diff_lk2, diff_subln_g, moe_router, moe_w_gate, moe_w_up, moe_w_down):
    cv_t = jnp.concatenate([c_ctx[None], c, jnp.zeros((COND_ROWS - N_COND, D_MODEL), F32)], axis=0).T
    mods = _adaln(cv_t, ada_w, ada_b).reshape(DEPTH * COND_ROWS, 1, 6 * D_MODEL)

    cos, sin = _axial_rope_tables(DEC_SEQ // GRID_W, WIN_HEAD_DIM)
    rope_win = (jnp.tile(cos, (1, WIN_HEADS)), jnp.tile(sin, (1, WIN_HEADS)))
    rope_diff = (jnp.tile(cos, (1, 2)), jnp.tile(sin, (1, 2)))

    xs = [x_prompt.reshape(CTX.tokens, D_MODEL), x_sample.reshape(LAT.tokens, D_MODEL)]
    final_g_row = final_g.reshape(1, D_MODEL)
    outs = {}

    def both(make):
        return _run_pair(make(0, CTX), make(1, LAT))

    for layer in range(DEPTH):
        idx = layer // 2
        even = layer % 2 == 0
        w_in = (even_w_in if even else odd_w_in)[idx].astype(BF16)
        w_out = (even_w_out if even else odd_w_out)[idx].astype(BF16)
        wr_t = moe_router[layer].T
        g1_row = norm_g[layer, 0].reshape(1, D_MODEL)
        g2_row = norm_g[layer, 1].reshape(1, D_MODEL)
        if even:
            wa = _block_diag(lru_wa[idx]).astype(BF16)
            wx = _block_diag(lru_wx[idx]).astype(BF16)
            (proj_c,), (proj_d,) = both(lambda si, st: _proj(xs[si], g1_row, mods, layer, st, w_in))
            projs = (proj_c, proj_d)
            h0s = (jnp.zeros((CTX.n_req, 2, LRU_WIDTH), F32), state_lru[:, idx])
            (y_c, h_fin), (y_d, _) = both(lambda si, st: _lru(
                projs[si], st, conv_w[idx], conv_b[idx].reshape(1, LRU_WIDTH), wa, wx,
                lru_ba[idx], lru_bx[idx], lru_lambda[idx], h0s[si]))
            pack = lambda t: t[:, idx].transpose(0, 2, 1, 3).reshape(DEC_BATCH, PAST_LEN, WIN_KV)
            (o_c,), (o_d,) = _run_pair(
                _win_attn(proj_c, CTX, win_sink[idx]),
                _win_attn(proj_d, LAT, win_sink[idx], cache=(pack(cache_win_k), pack(cache_win_v)), rope=rope_win))
            k0 = 2 * LRU_WIDTH + WIN_Q
            to_heads = lambda t: t.reshape(BATCH, SEQ, WIN_KV_HEADS, WIN_HEAD_DIM).transpose(0, 2, 1, 3)[:, None]
            outs["win_k"] = to_heads(proj_c[:, k0:k0 + WIN_KV])
            outs["win_v"] = to_heads(proj_c[:, k0 + WIN_KV:k0 + 2 * WIN_KV])
            outs["lru"] = h_fin[:, None]
            mixed = ([y_c, o_c], [y_d, o_d])
        else:
            lambda_init = 0.8 - 0.6 * math.exp(-0.3 * layer)
            vec = lambda t: t[idx].reshape(1, DIFF_HEAD_DIM)
            args = (vec(diff_lq1), vec(diff_lk1), vec(diff_lq2), vec(diff_lk2), diff_subln_g[idx])
            (q, kh, vh), (proj_d,) = _run_pair(_proj_heads(xs[0], g1_row, mods, layer, CTX, w_in),
                                              _proj(xs[1], g1_row, mods, layer, LAT, w_in))
            (o_c,), (o_d,) = _run_pair(
                _diff_attn((q, kh, vh), CTX, lambda_init, *args),
                _diff_attn(proj_d, LAT, lambda_init, *args,
                           cache=(cache_diff_k[:, idx:idx + 1], cache_diff_v[:, idx:idx + 1]), rope=rope_diff))
            outs["diff_k"] = kh[:, None]
            outs["diff_v"] = vh[:, None]
            mixed = ([o_c], [o_d])

        mix = both(lambda si, st: _mix_out(mixed[si], w_out, xs[si], g2_row, mods, layer, st, wr_t))
        xns, h2s = (mix[0][0], mix[1][0]), (mix[0][1], mix[1][1])

        def request_major(lg, st):
            per_req = st.seq // LOGIT_TILE
            lg = lg.reshape(st.n_req, per_req, N_EXPERTS, LOGIT_TILE).transpose(0, 2, 1, 3)
            return lg.reshape(st.n_req, N_EXPERTS, st.seq)

        lgs = (request_major(mix[0][2], CTX), request_major(mix[1][2], LAT))
        routes = both(lambda si, st: _router(lgs[si], st))
        (rows_c,), (rows_d,) = both(lambda si, st: _gather(h2s[si], routes[si][0], st))
        ys = _ffn(rows_c, rows_d, layer, moe_w_gate, moe_w_up, moe_w_down)
        final = layer == DEPTH - 1
        (x_c,), (x_d,) = both(lambda si, st: _scatter(ys[si], routes[si][0], routes[si][1], xns[si], mods, layer, st,
                                                      final, final_g_row))
        xs = [x_c, x_d]


    y_prompt = xs[0].reshape(BATCH, SEQ, D_MODEL)
    y_sample = xs[1].reshape(DEC_BATCH, DEC_SEQ, D_MODEL)
    return (y_prompt, y_sample, outs["win_k"], outs["win_v"], outs["lru"], outs["diff_k"], outs["diff_v"])
```

```python
import dataclasses
import functools
import math
from typing import Callable, NamedTuple

import jax
import jax.numpy as jnp
from jax import lax
from jax.experimental import pallas as pl
from jax.experimental.pallas import tpu as pltpu

F32 = jnp.float32
BF16 = jnp.bfloat16
I32 = jnp.int32

D_MODEL = 1024
BATCH = 16
SEQ = 256
DEPTH = 2
DEC_BATCH = 2
DEC_SEQ = 1024
PAST_LEN = 512
GRID_W = 64
LRU_WIDTH = D_MODEL // 2
LRU_BLOCKS = 8
LRU_BLOCK = LRU_WIDTH // LRU_BLOCKS
CONV_W = 4
LRU_C = 8.0
WIN_HEADS = 8
WIN_KV_HEADS = 2
WIN_REP = WIN_HEADS // WIN_KV_HEADS
WIN_HEAD_DIM = 64
WINDOW = 128
WIN_Q = WIN_HEADS * WIN_HEAD_DIM
WIN_KV = WIN_KV_HEADS * WIN_HEAD_DIM
EVEN_IN = 2 * LRU_WIDTH + WIN_Q + 2 * WIN_KV
DIFF_HEADS = 8
DIFF_HEAD_DIM = 64
DIFF_QK = DIFF_HEADS * 2 * DIFF_HEAD_DIM
DIFF_V = DIFF_HEADS * 2 * DIFF_HEAD_DIM
ODD_IN = 2 * DIFF_QK + DIFF_V
N_EXPERTS = 16
EXPERT_FF = 2 * D_MODEL
CAPACITY_FACTOR = 2
ROPE_BASE = 10000.0
EPS = 1e-6
NEG_INF = -1e30

LANES = 128
SUBLANES = 8
VMEM_LIMIT_BYTES = 48 * 1024 * 1024

N_COND = 1 + DEC_BATCH
COND_ROWS = SUBLANES
TOKEN_TILE = 512
MIX_TILE = 512
ROUTE_ROWS = 512
LOGIT_TILE = 256
FF_TILE = 1024
FFN_SLOTS = 3
FFN_VMEM_LIMIT_BYTES = 58 * 1024 * 1024
PAIR_VMEM_LIMIT_BYTES = 56 * 1024 * 1024
LRU_CHUNK = 128
ATT_Q_TILE = 256
ATT_K_TILE = 256


class Stream:
    def __init__(self, n_req, seq, cond0, cond_step):
        self.n_req, self.seq, self.cond0, self.cond_step = n_req, seq, cond0, cond_step
        self.tokens = n_req * seq
        self.cap = CAPACITY_FACTOR * seq // N_EXPERTS
        self.req_per_step = max(1, ROUTE_ROWS // seq)

    def cond_of_row(self, row):
        return self.cond0 + self.cond_step * (row // self.seq)


CTX = Stream(BATCH, SEQ, 0, 0)
LAT = Stream(DEC_BATCH, DEC_SEQ, 1, 1)


def _params(n_axes, vmem_limit_bytes=VMEM_LIMIT_BYTES):
    return pltpu.CompilerParams(dimension_semantics=("arbitrary",) * n_axes,
                                vmem_limit_bytes=vmem_limit_bytes)


class Call(NamedTuple):
    kernel: Callable
    steps: int
    in_specs: tuple
    out_specs: tuple
    out_shapes: tuple
    scratch: tuple
    args: tuple
    name: str


def _deferred(kernel, *, grid, in_specs, out_specs, out_shape, scratch_shapes=(), name):
    (steps,) = grid
    as_tuple = lambda v: tuple(v) if isinstance(v, (list, tuple)) else (v,)
    return lambda *args: Call(kernel, steps, tuple(in_specs), as_tuple(out_specs), as_tuple(out_shape),
                              tuple(scratch_shapes), args, name)


def _run(call):
    return pl.pallas_call(
        call.kernel, grid=(call.steps,), in_specs=list(call.in_specs), out_specs=list(call.out_specs),
        out_shape=list(call.out_shapes), scratch_shapes=list(call.scratch),
        compiler_params=_params(1), name=call.name)(*call.args)


def _shifted(spec, first, steps):
    if spec.index_map is None:
        return spec
    return dataclasses.replace(spec, index_map=lambda i: spec.index_map(jnp.clip(i - first, 0, steps - 1)))


def _pair_kernel(a, b, *refs):
    n_in, n_out = len(a.in_specs) + len(b.in_specs), len(a.out_specs) + len(b.out_specs)
    ins, outs, scr = refs[:n_in], refs[n_in:n_in + n_out], refs[n_in + n_out:]
    i = pl.program_id(0)

    @pl.when(i < a.steps)
    def _():
        a.kernel(*ins[:len(a.in_specs)], *outs[:len(a.out_specs)], *scr[:len(a.scratch)])

    @pl.when(i >= a.steps)
    def _():
        b.kernel(*ins[len(a.in_specs):], *outs[len(a.out_specs):], *scr[len(a.scratch):])


def _run_pair(a, b):
    specs = lambda sa, sb: ([_shifted(s, 0, a.steps) for s in sa] + [_shifted(s, a.steps, b.steps) for s in sb])
    outs = pl.pallas_call(
        functools.partial(_pair_kernel, a, b), grid=(a.steps + b.steps,),
        in_specs=specs(a.in_specs, b.in_specs), out_specs=specs(a.out_specs, b.out_specs),
        out_shape=list(a.out_shapes + b.out_shapes), scratch_shapes=list(a.scratch + b.scratch),
        compiler_params=_params(1, PAIR_VMEM_LIMIT_BYTES), name=f"{a.name}+{b.name}")(*a.args, *b.args)
    return outs[:len(a.out_specs)], outs[len(a.out_specs):]


def _resident_spec(shape):
    return pl.BlockSpec(shape, lambda i: (0, 0), pipeline_mode=pl.Buffered(1))


def _mod_spec(layer, stream, k, rows_per_step):
    return pl.BlockSpec(
        (1, 1, D_MODEL),
        lambda i, *_: (layer * COND_ROWS + stream.cond_of_row(i * rows_per_step), 0, k))


def _dot(a, b):
    return jnp.dot(a, b, preferred_element_type=F32)


def _dot_nt(a, b):
    return lax.dot_general(a, b, (((1,), (1,)), ((), ())), preferred_element_type=F32)


def _modnorm(x, g, shift, scale):
    y = x * lax.rsqrt(jnp.mean(x * x, axis=-1, keepdims=True) + EPS)
    return (y * g) * (1.0 + scale) + shift


def _lane_half_masks(shape):
    lane = lax.broadcasted_iota(I32, shape, len(shape) - 1)
    left = (lane & (LANES - 1)) < LANES // 2
    return left, jnp.logical_not(left)


def _rope(x, cos, sin):
    parts = []
    for c in range(x.shape[1] // LANES):
        xs = x[:, c * LANES:(c + 1) * LANES]
        lane = lax.broadcasted_iota(I32, xs.shape, 1)
        first = (lane & 31) < 16
        rot = jnp.where(first, -pltpu.roll(xs, LANES - 16, axis=1), pltpu.roll(xs, 16, axis=1))
        parts.append(xs * cos[:, c * LANES:(c + 1) * LANES] + rot * sin[:, c * LANES:(c + 1) * LANES])
    return parts[0] if len(parts) == 1 else jnp.concatenate(parts, axis=1)


def _adaln_kernel(cv_ref, w_ref, b_ref, o_ref):
    cv = cv_ref[...]
    s = cv * jax.nn.sigmoid(cv)
    w = w_ref[0]
    ridx = lax.broadcasted_iota(I32, (COND_ROWS, w.shape[1]), 0)
    out = jnp.zeros((COND_ROWS, w.shape[1]), F32)
    for r in range(N_COND):
        out = jnp.where(ridx == r, jnp.sum(w * s[:, r:r + 1], axis=0, keepdims=True), out)
    o_ref[0] = out + b_ref[0]


def _adaln(cv_t, ada_w, ada_b):
    tn = 1024
    return pl.pallas_call(
        _adaln_kernel,
        grid=(DEPTH, 6 * D_MODEL // tn),
        in_specs=[pl.BlockSpec((D_MODEL, COND_ROWS), lambda l, j: (0, 0)),
                  pl.BlockSpec((1, D_MODEL, tn), lambda l, j: (l, 0, j)),
                  pl.BlockSpec((1, 1, tn), lambda l, j: (l, 0, j))],
        out_specs=pl.BlockSpec((1, COND_ROWS, tn), lambda l, j: (l, 0, j)),
        out_shape=jax.ShapeDtypeStruct((DEPTH, COND_ROWS, 6 * D_MODEL), F32),
        compiler_params=_params(2),
        name="adaln",
    )(cv_t, ada_w, ada_b.reshape(DEPTH, 1, 6 * D_MODEL))


def _proj_kernel(x_ref, g_ref, sh_ref, sc_ref, w_ref, o_ref):
    h = _modnorm(x_ref[...], g_ref[...], sh_ref[0], sc_ref[0])
    o_ref[...] = _dot(h.astype(BF16), w_ref[...])


def _proj(x, g_row, mods, layer, stream, w):
    n_out = w.shape[1]
    tm = TOKEN_TILE
    return _deferred(
        _proj_kernel,
        grid=(stream.tokens // tm,),
        in_specs=[pl.BlockSpec((tm, D_MODEL), lambda i: (i, 0)),
                  pl.BlockSpec((1, D_MODEL), lambda i: (0, 0)),
                  _mod_spec(layer, stream, 0, tm),
                  _mod_spec(layer, stream, 1, tm),
                  _resident_spec((D_MODEL, n_out))],
        out_specs=pl.BlockSpec((tm, n_out), lambda i: (i, 0)),
        out_shape=jax.ShapeDtypeStruct((stream.tokens, n_out), F32),
        name=f"proj_l{layer}_s{stream.seq}",
    )(x, g_row, mods, mods, w)


def _proj_heads_kernel(seq, x_ref, g_ref, sh_ref, sc_ref, w_ref, q_ref, k_ref, v_ref):
    h = _modnorm(x_ref[...], g_ref[...], sh_ref[0], sc_ref[0])
    res = _dot(h.astype(BF16), w_ref[...])
    q_ref[...] = res[:, 0:DIFF_QK]
    for r in range(x_ref.shape[0] // seq):
        rows = slice(r * seq, (r + 1) * seq)
        for hh in range(DIFF_HEADS):
            k_ref[r, hh] = res[rows, DIFF_QK + hh * LANES:DIFF_QK + (hh + 1) * LANES]
            v_ref[r, hh] = res[rows, 2 * DIFF_QK + hh * LANES:2 * DIFF_QK + (hh + 1) * LANES]


def _proj_heads(x, g_row, mods, layer, stream, w):
    tm, seq = TOKEN_TILE, stream.seq
    head_shape = (stream.n_req, DIFF_HEADS, seq, 2 * DIFF_HEAD_DIM)
    head_spec = pl.BlockSpec((tm // seq, DIFF_HEADS, seq, 2 * DIFF_HEAD_DIM), lambda i: (i, 0, 0, 0))
    return _deferred(
        functools.partial(_proj_heads_kernel, seq),
        grid=(stream.tokens // tm,),
        in_specs=[pl.BlockSpec((tm, D_MODEL), lambda i: (i, 0)),
                  pl.BlockSpec((1, D_MODEL), lambda i: (0, 0)),
                  _mod_spec(layer, stream, 0, tm),
                  _mod_spec(layer, stream, 1, tm),
                  _resident_spec((D_MODEL, ODD_IN))],
        out_specs=[pl.BlockSpec((tm, DIFF_QK), lambda i: (i, 0)), head_spec, head_spec],
        out_shape=[jax.ShapeDtypeStruct((stream.tokens, DIFF_QK), F32),
                   jax.ShapeDtypeStruct(head_shape, F32), jax.ShapeDtypeStruct(head_shape, F32)],
        name=f"proj_heads_l{layer}_s{seq}",
    )(x, g_row, mods, mods, w)


def _lru_kernel(seq, xl_ref, gl_ref, cw_ref, cb_ref, wa_ref, wx_ref, ba_ref, bx_ref, lam_ref, h0_ref,
                y_ref, hfin_ref, xpad, af, bf, ab, bb):
    width = LRU_WIDTH
    ch = LRU_CHUNK
    halo = SUBLANES
    xpad[0:halo, :] = jnp.zeros((halo, width), F32)
    xpad[halo + seq:2 * halo + seq, :] = jnp.zeros((halo, width), F32)
    xpad[halo:halo + seq, :] = xl_ref[...]

    lam = lam_ref[...]
    z = -lam
    softplus = jnp.maximum(z, 0.0) + jnp.log1p(jnp.exp(-jnp.abs(z)))
    sub = lax.broadcasted_iota(I32, (ch // SUBLANES, SUBLANES, width), 1)
    cw = cw_ref[...]
    cb = cb_ref[...]

    def gates_chunk(c, carry):
        r0 = pl.multiple_of(c * ch, ch)
        win = xpad[pl.ds(r0, ch + 2 * halo), :]
        n_win = ch + 2 * halo

        def tap(j):
            return pltpu.roll(win, n_win - (halo - 2 + j), axis=0)[0:ch]

        xc = tap(0) * cw[0:1]
        for j in range(1, CONV_W):
            xc = xc + tap(j) * cw[j:j + 1]
        xc = xc + cb
        xcb = xc.astype(BF16)
        for d, (a_s, b_s) in enumerate(((af, bf), (ab, bb))):
            r = jax.nn.sigmoid(_dot(xcb, wa_ref[d]) + ba_ref[d:d + 1])
            ig = jax.nn.sigmoid(_dot(xcb, wx_ref[d]) + bx_ref[d:d + 1])
            log_a = (-LRU_C * r) * softplus[d:d + 1]
            a = jnp.exp(log_a)
            v = 1.0 - a * a
            b = jnp.where(v > 0.0, v * lax.rsqrt(v), 0.0) * (ig * xc)
            a = a.reshape(ch // SUBLANES, SUBLANES, width)
            b = b.reshape(ch // SUBLANES, SUBLANES, width)
            for s in (1, 2, 4):
                keep = (sub >= s) if d == 0 else (sub < SUBLANES - s)
                shift = s if d == 0 else SUBLANES - s
                a_sh = jnp.where(keep, pltpu.roll(a, shift, axis=1), 1.0)
                b_sh = jnp.where(keep, pltpu.roll(b, shift, axis=1), 0.0)
                b = a * b_sh + b
                a = a * a_sh
            a_s[pl.ds(r0, ch), :] = a.reshape(ch, width)
            b_s[pl.ds(r0, ch), :] = b.reshape(ch, width)
        return carry

    lax.fori_loop(0, seq // ch, gates_chunk, 0)

    n_tiles = seq // SUBLANES
    h0 = h0_ref[0]

    def tile_step(k, carry):
        cf, cbw = carry
        rf = pl.multiple_of(k * SUBLANES, SUBLANES)
        rb = pl.multiple_of((n_tiles - 1 - k) * SUBLANES, SUBLANES)
        hf = af[pl.ds(rf, SUBLANES), :] * cf + bf[pl.ds(rf, SUBLANES), :]
        bf[pl.ds(rf, SUBLANES), :] = hf
        hb = ab[pl.ds(rb, SUBLANES), :] * cbw + bb[pl.ds(rb, SUBLANES), :]
        bb[pl.ds(rb, SUBLANES), :] = hb
        return hf[SUBLANES - 1:SUBLANES], hb[0:1]

    cf, cbw = lax.fori_loop(0, n_tiles, tile_step, (h0[0:1], h0[1:2]))
    hfin_ref[0, 0:1, :] = cf
    hfin_ref[0, 1:2, :] = cbw

    def out_chunk(c, carry):
        r0 = pl.multiple_of(c * ch, ch)
        hsum = bf[pl.ds(r0, ch), :] + bb[pl.ds(r0, ch), :]
        y_ref[pl.ds(r0, ch), :] = (hsum * jax.nn.gelu(gl_ref[pl.ds(r0, ch), :])).astype(BF16)
        return carry

    lax.fori_loop(0, seq // ch, out_chunk, 0)


def _lru(proj, stream, conv_w, conv_b, wa, wx, ba, bx, lam, h0):
    seq, width = stream.seq, LRU_WIDTH
    full2 = lambda b: (0, 0)
    full3 = lambda b: (0, 0, 0)
    return _deferred(
        functools.partial(_lru_kernel, seq),
        grid=(stream.n_req,),
        in_specs=[pl.BlockSpec((seq, width), lambda b: (b, 0)),
                  pl.BlockSpec((seq, width), lambda b: (b, 1)),
                  pl.BlockSpec((CONV_W, width), full2),
                  pl.BlockSpec((1, width), full2),
                  pl.BlockSpec((2, width, width), full3),
                  pl.BlockSpec((2, width, width), full3),
                  pl.BlockSpec((2, width), full2),
                  pl.BlockSpec((2, width), full2),
                  pl.BlockSpec((2, width), full2),
                  pl.BlockSpec((1, 2, width), lambda b: (b, 0, 0))],
        out_specs=[pl.BlockSpec((seq, width), lambda b: (b, 0)),
                   pl.BlockSpec((1, 2, width), lambda b: (b, 0, 0))],
        out_shape=[jax.ShapeDtypeStruct((stream.tokens, width), BF16),
                   jax.ShapeDtypeStruct((stream.n_req, 2, width), F32)],
        scratch_shapes=[pltpu.VMEM((seq + 2 * SUBLANES, width), F32)] + [pltpu.VMEM((seq, width), F32)] * 4,
        name=f"lru_s{seq}",
    )(proj, proj, conv_w, conv_b, wa, wx, ba, bx, lam, h0)


def _attend(q, chunks, s_ref):
    tile_max = None
    spans = []
    off = 0
    for keys, _, valid in chunks:
        s = _dot_nt(q, keys())
        if valid is not None:
            s = jnp.where(valid, s, NEG_INF)
        n = s.shape[1]
        s_ref[:, off:off + n] = s
        for c in range(n // LANES):
            t = s[:, c * LANES:(c + 1) * LANES]
            tile_max = t if tile_max is None else jnp.maximum(tile_max, t)
        spans.append((off, n))
        off += n
    m = jnp.max(tile_max, axis=-1, keepdims=True)
    acc = None
    for (_, values, _), (o, n) in zip(chunks, spans):
        part = _dot(jnp.exp(s_ref[:, o:o + n] - m).astype(BF16), values())
        acc = part if acc is None else acc + part
    return acc, m


def _split_groups(kk):
    left, right = _lane_half_masks(kk.shape)
    g0_l = jnp.where(left, kk, 0.0)
    g1_r = jnp.where(right, kk, 0.0)
    return ((g0_l, pltpu.roll(g0_l, LANES // 2, axis=1)), (pltpu.roll(g1_r, LANES // 2, axis=1), g1_r))


def _win_ctx_kernel(sink_ref, q_ref, kv_ref, o_ref):
    scale = WIN_HEAD_DIM ** -0.5
    seq = q_ref.shape[0]
    ks = _split_groups(kv_ref[:, 0:LANES])
    vs = _split_groups(kv_ref[:, LANES:2 * LANES])
    top = lax.broadcasted_iota(I32, (2 * seq, 1), 0) < seq
    outs = [None] * (WIN_HEADS // 2)
    for g in range(WIN_KV_HEADS):
        pairs = (2 * g, 2 * g + 1)
        qs = jnp.concatenate([q_ref[:, p * LANES:(p + 1) * LANES] for p in pairs], axis=0)
        qs = (qs * scale).astype(BF16)
        for side in range(2):
            sk = jnp.where(top, sink_ref[2 * pairs[0] + side], sink_ref[2 * pairs[1] + side])
            s = _dot_nt(qs, ks[g][side].astype(BF16))
            m = jnp.maximum(jnp.max(s, axis=-1, keepdims=True), sk)
            e = jnp.exp(s - m)
            den = jnp.sum(e, axis=-1, keepdims=True) + jnp.exp(sk - m)
            o = _dot(e.astype(BF16), vs[g][side].astype(BF16)) * (1.0 / den)
            for k, p in enumerate(pairs):
                part = o[k * seq:(k + 1) * seq]
                outs[p] = part if outs[p] is None else outs[p] + part
    for p in range(WIN_HEADS // 2):
        o_ref[:, p * LANES:(p + 1) * LANES] = outs[p].astype(BF16)


def _win_lat_kernel(sink_ref, q_ref, kv_ref, ck_ref, cv_ref, cos_ref, sin_ref, o_ref,
                    kl_s, vl_s, kc_s, vc_s):
    seq, wn = DEC_SEQ, WINDOW
    scale = WIN_HEAD_DIM ** -0.5
    kr = _rope(kv_ref[:, 0:LANES], cos_ref[:, 0:LANES], sin_ref[:, 0:LANES])
    for src, dst in ((_split_groups(kr), kl_s), (_split_groups(kv_ref[:, LANES:2 * LANES]), vl_s),
                     (_split_groups(ck_ref[0]), kc_s), (_split_groups(cv_ref[0]), vc_s)):
        for g in range(WIN_KV_HEADS):
            for side in range(2):
                dst[2 * g + side] = src[g][side].astype(BF16)

    top = lax.broadcasted_iota(I32, (2 * wn, 1), 0) < wn

    def q_block(i, carry):
        for sub in range(2):
            r0 = pl.multiple_of((2 * i + sub) * wn, wn)
            start = pl.multiple_of(jnp.clip((2 * i + sub - 1) * wn, 0, seq - 3 * wn), wn)
            qr = _rope(q_ref[pl.ds(r0, wn), :], cos_ref[pl.ds(r0, wn), :], sin_ref[pl.ds(r0, wn), :]) * scale
            qpos = r0 + (lax.broadcasted_iota(I32, (2 * wn, 3 * wn), 0) & (wn - 1))
            kpos = start + lax.broadcasted_iota(I32, (2 * wn, 3 * wn), 1)
            valid = jnp.abs(qpos - kpos) <= wn
            outs = [None] * (WIN_HEADS // 2)
            for g in range(WIN_KV_HEADS):
                pairs = (2 * g, 2 * g + 1)
                qs = jnp.concatenate([qr[:, p * LANES:(p + 1) * LANES] for p in pairs], axis=0).astype(BF16)
                for side in range(2):
                    idx = 2 * g + side
                    sk = jnp.where(top, sink_ref[2 * pairs[0] + side], sink_ref[2 * pairs[1] + side])
                    sl = _dot_nt(qs, kl_s[idx, pl.ds(start, 3 * wn), :])
                    sl = jnp.where(valid, sl, NEG_INF)
                    sc = _dot_nt(qs, kc_s[idx])
                    m = jnp.maximum(jnp.maximum(jnp.max(sl, axis=-1, keepdims=True),
                                                jnp.max(sc, axis=-1, keepdims=True)), sk)
                    el = jnp.exp(sl - m)
                    ec = jnp.exp(sc - m)
                    den = (jnp.sum(el, axis=-1, keepdims=True) + jnp.sum(ec, axis=-1, keepdims=True)
                           + jnp.exp(sk - m))
                    o = (_dot(el.astype(BF16), vl_s[idx, pl.ds(start, 3 * wn), :])
                         + _dot(ec.astype(BF16), vc_s[idx])) * (1.0 / den)
                    for k, p in enumerate(pairs):
                        part = o[k * wn:(k + 1) * wn]
                        outs[p] = part if outs[p] is None else outs[p] + part
            for p in range(WIN_HEADS // 2):
                o_ref[pl.ds(r0, wn), p * LANES:(p + 1) * LANES] = outs[p].astype(BF16)
        return carry

    lax.fori_loop(0, seq // (2 * wn), q_block, 0)


def _win_attn(proj, stream, sink, cache=None, rope=None):
    seq = stream.seq
    q_spec = pl.BlockSpec((seq, WIN_Q), lambda b: (b, 2 * LRU_WIDTH // WIN_Q))
    kv_spec = pl.BlockSpec((seq, 2 * WIN_KV), lambda b: (b, (2 * LRU_WIDTH + WIN_Q) // (2 * WIN_KV)))
    sink_spec = pl.BlockSpec(memory_space=pltpu.SMEM)
    out_spec = pl.BlockSpec((seq, WIN_Q), lambda b: (b, 0))
    out_shape = jax.ShapeDtypeStruct((stream.tokens, WIN_Q), BF16)
    if cache is None:
        return _deferred(
            _win_ctx_kernel, grid=(stream.n_req,),
            in_specs=[sink_spec, q_spec, kv_spec], out_specs=out_spec, out_shape=out_shape,
            name="win_attn_ctx",
        )(sink, proj, proj)
    ck, cv = cache
    cos, sin = rope
    cache_spec = pl.BlockSpec((1, PAST_LEN, LANES), lambda b: (b, 0, 0))
    table_spec = pl.BlockSpec((seq, WIN_Q), lambda b: (0, 0))
    return _deferred(
        _win_lat_kernel, grid=(stream.n_req,),
        in_specs=[sink_spec, q_spec, kv_spec, cache_spec, cache_spec, table_spec, table_spec],
        out_specs=out_spec, out_shape=out_shape,
        scratch_shapes=[pltpu.VMEM((4, seq, LANES), BF16), pltpu.VMEM((4, seq, LANES), BF16),
                        pltpu.VMEM((4, PAST_LEN, LANES), BF16), pltpu.VMEM((4, PAST_LEN, LANES), BF16)],
        name="win_attn_lat",
    )(sink, proj, proj, ck, cv, cos, sin)


def _diff_lambda(lq1_ref, lk1_ref, lq2_ref, lk2_ref, lambda_init):
    t1 = jnp.sum(lq1_ref[...] * lk1_ref[...], axis=-1, keepdims=True)
    t2 = jnp.sum(lq2_ref[...] * lk2_ref[...], axis=-1, keepdims=True)
    return jnp.exp(t1) - jnp.exp(t2) + lambda_init


def _subln(o, g_row, lambda_init):
    o = o * lax.rsqrt(jnp.mean(o * o, axis=-1, keepdims=True) + EPS) * g_row
    return o * (1.0 - lambda_init)


def _component_keys(k):
    left, right = _lane_half_masks(k.shape)
    return jnp.where(left, k, 0.0).astype(BF16), jnp.where(right, k, 0.0).astype(BF16)


def _values_with_ones(v):
    return jnp.concatenate([v.astype(BF16), jnp.ones(v.shape, BF16)], axis=1)


def _diff_combine(accs, lam):
    o1, o2 = accs[0][:, 0:LANES], accs[1][:, 0:LANES]
    return o1 * (1.0 / accs[0][:, LANES:]) - o2 * (lam * (1.0 / accs[1][:, LANES:]))


def _diff_ctx_kernel(lambda_init, lq1_ref, lk1_ref, lq2_ref, lk2_ref, sg_ref, q_ref, k_ref, v_ref, o_ref, s_ref):
    scale = DIFF_HEAD_DIM ** -0.5
    tq = ATT_Q_TILE
    lam = _diff_lambda(lq1_ref, lk1_ref, lq2_ref, lk2_ref, lambda_init)
    for h in range(DIFF_HEADS):
        cols = slice(h * LANES, (h + 1) * LANES)
        keys = _component_keys(k_ref[0, h])
        vals = _values_with_ones(v_ref[0, h])
        for qt in range(q_ref.shape[0] // tq):
            rows = slice(qt * tq, (qt + 1) * tq)
            q = (q_ref[rows, cols] * scale).astype(BF16)
            slots = [s_ref.at[(4 * h + 2 * qt + c) % s_ref.shape[0]] for c in range(2)]
            accs = [_attend(q, [(lambda: keys[c], lambda: vals, None)], slots[c])[0] for c in range(2)]
            o = _diff_combine(accs, lam)
            o_ref[rows, cols] = _subln(o, sg_ref[h:h + 1, :], lambda_init).astype(BF16)


def _diff_lat_kernel(lambda_init, lq1_ref, lk1_ref, lq2_ref, lk2_ref, sg_ref, q_ref, k_ref, v_ref,
                     ck_ref, cv_ref, cos_ref, sin_ref, o_ref, kl_s, kc_s, vl_s, vc_s, s_ref):
    seq, tq, tk = DEC_SEQ, ATT_Q_TILE, ATT_K_TILE
    scale = DIFF_HEAD_DIM ** -0.5
    lam = _diff_lambda(lq1_ref, lk1_ref, lq2_ref, lk2_ref, lambda_init)
    for src, dst in ((_rope(k_ref[...], cos_ref[...], sin_ref[...]), kl_s), (ck_ref[0, 0, 0], kc_s)):
        dst[0], dst[1] = _component_keys(src)
    vl_s[...] = _values_with_ones(v_ref[...])
    vc_s[...] = _values_with_ones(cv_ref[0, 0, 0])
    sg = sg_ref[0]

    for qt in range(seq // tq):
        rows = slice(qt * tq, (qt + 1) * tq)
        qr = (_rope(q_ref[rows, :], cos_ref[rows, :], sin_ref[rows, :]) * scale).astype(BF16)
        accs = []
        for comp in range(2):
            chunks = [(lambda j=j: kl_s[comp, j * tk:(j + 1) * tk, :],
                       lambda j=j: vl_s[j * tk:(j + 1) * tk, :], None) for j in range(seq // tk)]
            chunks += [(lambda j=j: kc_s[comp, j * tk:(j + 1) * tk, :],
                        lambda j=j: vc_s[j * tk:(j + 1) * tk, :], None) for j in range(PAST_LEN // tk)]
            accs.append(_attend(qr, chunks, s_ref.at[2 * qt + comp])[0])
        o_ref[rows, :] = _subln(_diff_combine(accs, lam), sg, lambda_init).astype(BF16)


def _diff_attn(proj, stream, lambda_init, lq1, lk1, lq2, lk2, subln_g, cache=None, rope=None):
    seq = stream.seq
    vec_spec = lambda nd: pl.BlockSpec((1, DIFF_HEAD_DIM), lambda *_: (0, 0))
    out_shape = jax.ShapeDtypeStruct((stream.tokens, DIFF_V), BF16)
    if cache is None:
        q, kh, vh = proj
        head_spec = pl.BlockSpec((1, DIFF_HEADS, seq, LANES), lambda b: (b, 0, 0, 0))
        return _deferred(
            functools.partial(_diff_ctx_kernel, lambda_init), grid=(stream.n_req,),
            in_specs=[vec_spec(1)] * 4 + [pl.BlockSpec((DIFF_HEADS, LANES), lambda b: (0, 0)),
                                          pl.BlockSpec((seq, DIFF_QK), lambda b: (b, 0)), head_spec, head_spec],
            out_specs=pl.BlockSpec((seq, DIFF_V), lambda b: (b, 0)), out_shape=out_shape,
            scratch_shapes=[pltpu.VMEM((8, ATT_Q_TILE, seq), F32)],
            name="diff_attn_ctx",
        )(lq1, lk1, lq2, lk2, subln_g, q, kh, vh)
    ck, cv = cache
    cos, sin = rope
    nh = DIFF_HEADS
    blk = lambda c: pl.BlockSpec((seq, LANES), lambda i: (i // nh, c * nh + i % nh))
    cache_spec = pl.BlockSpec((1, 1, 1, PAST_LEN, LANES), lambda i: (i // nh, 0, i % nh, 0, 0))
    table_spec = pl.BlockSpec((seq, LANES), lambda i: (0, 0))
    return _deferred(
        functools.partial(_diff_lat_kernel, lambda_init), grid=(stream.n_req * nh,),
        in_specs=[vec_spec(2)] * 4 + [pl.BlockSpec((1, 1, LANES), lambda i: (i % nh, 0, 0)),
                                      blk(0), blk(1), blk(2), cache_spec, cache_spec, table_spec, table_spec],
        out_specs=pl.BlockSpec((seq, LANES), lambda i: (i // nh, i % nh)), out_shape=out_shape,
        scratch_shapes=[pltpu.VMEM((2, seq, LANES), BF16), pltpu.VMEM((2, PAST_LEN, LANES), BF16),
                        pltpu.VMEM((seq, 2 * LANES), BF16), pltpu.VMEM((PAST_LEN, 2 * LANES), BF16),
                        pltpu.VMEM((2 * seq // ATT_Q_TILE, ATT_Q_TILE, seq + PAST_LEN), F32)],
        name="diff_attn_lat",
    )(lq1, lk1, lq2, lk2, subln_g.reshape(DIFF_HEADS, 1, LANES), proj, proj, proj, ck, cv, cos, sin)


def _mix_out_kernel(n_in, *refs):
    a_refs = refs[:n_in]
    w_ref, x_ref, gate_ref, g2_ref, sh2_ref, sc2_ref, wr_ref, xn_ref, h2_ref, lg_ref = refs[n_in:]
    kp = D_MODEL // n_in
    wr = wr_ref[...]
    w_hi = wr.astype(BF16)
    w_lo = (wr - w_hi.astype(F32)).astype(BF16)
    acc = None
    for k, a_ref in enumerate(a_refs):
        part = _dot(a_ref[...], w_ref[k * kp:(k + 1) * kp, :])
        acc = part if acc is None else acc + part
    xn = x_ref[...] + gate_ref[0] * acc
    xn_ref[...] = xn
    h2 = _modnorm(xn, g2_ref[...], sh2_ref[0], sc2_ref[0])
    h_hi = h2.astype(BF16)
    h2_ref[...] = h_hi
    h_lo = (h2 - h_hi.astype(F32)).astype(BF16)
    by_hi = _dot_nt(jnp.concatenate([w_hi, w_lo], axis=0), h_hi)
    lg = by_hi[0:N_EXPERTS] + (by_hi[N_EXPERTS:] + _dot_nt(w_hi, h_lo))
    for c in range(lg.shape[1] // LOGIT_TILE):
        lg_ref[c] = lg[:, c * LOGIT_TILE:(c + 1) * LOGIT_TILE]


def _mix_out(mixed, w_out, x, g2_row, mods, layer, stream, wr_t):
    tm = MIX_TILE
    n_in = len(mixed)
    kp = D_MODEL // n_in
    row_blk = lambda width: pl.BlockSpec((tm, width), lambda i: (i, 0))
    return _deferred(
        functools.partial(_mix_out_kernel, n_in),
        grid=(stream.tokens // tm,),
        in_specs=[row_blk(kp)] * n_in + [
            _resident_spec((D_MODEL, D_MODEL)),
            row_blk(D_MODEL),
            _mod_spec(layer, stream, 2, tm),
            pl.BlockSpec((1, D_MODEL), lambda i: (0, 0)),
            _mod_spec(layer, stream, 3, tm),
            _mod_spec(layer, stream, 4, tm),
            pl.BlockSpec((N_EXPERTS, D_MODEL), lambda i: (0, 0))],
        out_specs=[row_blk(D_MODEL), row_blk(D_MODEL),
                   pl.BlockSpec((tm // LOGIT_TILE, N_EXPERTS, LOGIT_TILE), lambda i: (i, 0, 0))],
        out_shape=[jax.ShapeDtypeStruct((stream.tokens, D_MODEL), F32),
                   jax.ShapeDtypeStruct((stream.tokens, D_MODEL), BF16),
                   jax.ShapeDtypeStruct((stream.tokens // LOGIT_TILE, N_EXPERTS, LOGIT_TILE), F32)],
        name=f"mix_out_l{layer}_s{stream.seq}",
    )(*mixed, w_out, x, mods, g2_row, mods, mods, wr_t)


def _sort_desc_lanes(x):
    rows, n = x.shape
    tiles = [x[:, c * LANES:(c + 1) * LANES] for c in range(n // LANES)]
    lane = lax.broadcasted_iota(I32, (rows, LANES), 1)
    k = 2
    while k <= n:
        j = k // 2
        while j >= 1:
            if j < LANES:
                lower = (lane & j) == 0
                for c in range(len(tiles)):
                    t = tiles[c]
                    partner = jnp.where(lower, pltpu.roll(t, LANES - j, axis=1), pltpu.roll(t, j, axis=1))
                    desc = ((lane & k) == 0) if k < LANES else (((c * LANES) & k) == 0)
                    take_max = (lower == desc) if k < LANES else (lower if desc else jnp.logical_not(lower))
                    tiles[c] = jnp.where(take_max, jnp.maximum(t, partner), jnp.minimum(t, partner))
            else:
                jc = j // LANES
                new = list(tiles)
                for c in range(len(tiles)):
                    take_max = ((c & jc) == 0) == (((c * LANES) & k) == 0)
                    new[c] = (jnp.maximum if take_max else jnp.minimum)(tiles[c], tiles[c ^ jc])
                tiles = new
            j //= 2
        k *= 2
    return tiles


def _router_kernel(cap, lg_ref, pos_ref, g_ref):
    x = lg_ref[...]
    n_b, n_e, n_tok = x.shape
    e = jnp.exp(x - jnp.max(x, axis=1, keepdims=True))
    aff = (e / jnp.sum(e, axis=1, keepdims=True)).reshape(n_b * n_e, n_tok)
    srt = _sort_desc_lanes(aff)
    lane_k = (cap - 1) % LANES
    thr = srt[(cap - 1) // LANES][:, lane_k:lane_k + 1]
    gt = aff > thr
    eq = aff == thr
    n_gt = jnp.sum(gt.astype(F32), axis=1, keepdims=True)
    before = (lax.broadcasted_iota(I32, (n_tok, n_tok), 0)
              < lax.broadcasted_iota(I32, (n_tok, n_tok), 1)).astype(BF16)
    eq_rank = _dot(eq.astype(BF16), before)
    sel = gt | (eq & (eq_rank < cap - n_gt))
    slot = _dot(sel.astype(BF16), before).astype(I32)
    pos_ref[...] = jnp.where(sel, slot, -1).reshape(n_b, n_e, n_tok)
    g_ref[...] = jnp.where(sel, aff, 0.0).reshape(n_b, n_e, n_tok)


def _router(logits, stream):
    shape = (stream.n_req, N_EXPERTS, stream.seq)
    spec = pl.BlockSpec(shape, lambda i: (0, 0, 0))
    return _deferred(
        functools.partial(_router_kernel, stream.cap), grid=(1,),
        in_specs=[spec], out_specs=[spec, spec],
        out_shape=[jax.ShapeDtypeStruct(shape, I32), jax.ShapeDtypeStruct(shape, F32)],
        name=f"router_s{stream.seq}",
    )(logits)


def _gather_kernel(cap, seq, h_ref, pos_ref, xs_ref):
    slot = lax.broadcasted_iota(I32, (cap, seq), 0)
    for r in range(pos_ref.shape[0]):
        onehot = jnp.concatenate([(slot == pos_ref[r, e:e + 1, :]).astype(BF16) for e in range(N_EXPERTS)], axis=0)
        xs = _dot(onehot, h_ref[r * seq:(r + 1) * seq, :]).astype(BF16)
        for e in range(N_EXPERTS):
            xs_ref[e, r * cap:(r + 1) * cap, :] = xs[e * cap:(e + 1) * cap]


def _gather(h2, pos, stream):
    cap, seq, per = stream.cap, stream.seq, stream.req_per_step
    return _deferred(
        functools.partial(_gather_kernel, cap, seq),
        grid=(stream.n_req // per,),
        in_specs=[pl.BlockSpec((per * seq, D_MODEL), lambda i: (i, 0)),
                  pl.BlockSpec((per, N_EXPERTS, seq), lambda i: (i, 0, 0))],
        out_specs=pl.BlockSpec((N_EXPERTS, per * cap, D_MODEL), lambda i: (0, i, 0)),
        out_shape=jax.ShapeDtypeStruct((N_EXPERTS, stream.n_req * cap, D_MODEL), BF16),
        name=f"gather_s{seq}",
    )(h2, pos)


def _ffn_kernel(layer, n_j, xa_ref, xb_ref, wg_hbm, wu_hbm, wd_hbm, ya_ref, yb_ref,
                acc, wg_buf, wu_buf, wd_buf, sems):
    j = pl.program_id(1)
    step = pl.program_id(0) * n_j + j
    total = pl.num_programs(0) * n_j
    tf = wd_buf.shape[1]
    rows_a = xa_ref.shape[1]
    ahead = FFN_SLOTS - 1

    def weight_copies(s):
        e, jj, slot = s // n_j, s % n_j, s % FFN_SLOTS
        cols = pl.ds(jj * tf, tf)
        return (pltpu.make_async_copy(wg_hbm.at[layer, e, :, cols], wg_buf.at[slot], sems.at[0, slot]),
                pltpu.make_async_copy(wu_hbm.at[layer, e, :, cols], wu_buf.at[slot], sems.at[1, slot]),
                pltpu.make_async_copy(wd_hbm.at[layer, e, cols, :], wd_buf.at[slot], sems.at[2, slot]))

    @pl.when(step == 0)
    def _():
        for s in range(ahead):
            for copy in weight_copies(s):
                copy.start()

    @pl.when(step + ahead < total)
    def _():
        for copy in weight_copies(step + ahead):
            copy.start()

    @pl.when(j == 0)
    def _():
        acc[...] = jnp.zeros_like(acc)

    for copy in weight_copies(step):
        copy.wait()
    slot = step % FFN_SLOTS
    x = jnp.concatenate([xa_ref[0], xb_ref[0]], axis=0)
    a = _dot(x, wg_buf[slot].astype(BF16))
    u = _dot(x, wu_buf[slot].astype(BF16))
    acc[...] += _dot(((a * jax.nn.sigmoid(a)) * u).astype(BF16), wd_buf[slot].astype(BF16))

    @pl.when(j == n_j - 1)
    def _():
        ya_ref[0] = acc[0:rows_a, :].astype(BF16)
        yb_ref[0] = acc[rows_a:, :].astype(BF16)


def _ffn(xs_a, xs_b, layer, w_gate, w_up, w_down):
    tf = FF_TILE
    rows_a, rows_b = xs_a.shape[1], xs_b.shape[1]
    x_spec = lambda rows: pl.BlockSpec((1, rows, D_MODEL), lambda e, j: (e, 0, 0))
    n_j = EXPERT_FF // tf
    hbm = pl.BlockSpec(memory_space=pl.ANY)
    return pl.pallas_call(
        functools.partial(_ffn_kernel, layer, n_j),
        grid=(N_EXPERTS, n_j),
        in_specs=[x_spec(rows_a), x_spec(rows_b), hbm, hbm, hbm],
        out_specs=[x_spec(rows_a), x_spec(rows_b)],
        out_shape=[jax.ShapeDtypeStruct(xs_a.shape, BF16), jax.ShapeDtypeStruct(xs_b.shape, BF16)],
        scratch_shapes=[pltpu.VMEM((rows_a + rows_b, D_MODEL), F32),
                        pltpu.VMEM((FFN_SLOTS, D_MODEL, tf), F32), pltpu.VMEM((FFN_SLOTS, D_MODEL, tf), F32),
                        pltpu.VMEM((FFN_SLOTS, tf, D_MODEL), F32), pltpu.SemaphoreType.DMA((3, FFN_SLOTS))],
        compiler_params=_params(2, FFN_VMEM_LIMIT_BYTES),
        name=f"ffn_l{layer}",
    )(xs_a, xs_b, w_gate, w_up, w_down)


def _scatter_kernel(cap, seq, final, y_ref, pos_ref, g_ref, x_ref, gate_ref, fg_ref, o_ref):
    slot = lax.broadcasted_iota(I32, (cap, seq), 0)
    for r in range(pos_ref.shape[0]):
        onehots, gated = [], []
        for e in range(N_EXPERTS):
            hit = slot == pos_ref[r, e:e + 1, :]
            gate = jnp.sum(jnp.where(hit, g_ref[r, e:e + 1, :], 0.0), axis=1, keepdims=True)
            gated.append((y_ref[e, r * cap:(r + 1) * cap, :].astype(F32) * gate).astype(BF16))
            onehots.append(hit.astype(BF16))
        moe = lax.dot_general(jnp.concatenate(onehots, axis=0), jnp.concatenate(gated, axis=0),
                              (((0,), (0,)), ((), ())), preferred_element_type=F32)
        rows = slice(r * seq, (r + 1) * seq)
        xn = x_ref[rows, :] + gate_ref[0] * moe
        if final:
            xn = xn * lax.rsqrt(jnp.mean(xn * xn, axis=-1, keepdims=True) + EPS) * fg_ref[...]
        o_ref[rows, :] = xn


def _scatter(y, pos, g, xn, mods, layer, stream, final, final_g_row):
    cap, seq, per = stream.cap, stream.seq, stream.req_per_step
    tok_blk = pl.BlockSpec((per * seq, D_MODEL), lambda i: (i, 0))
    sel_blk = pl.BlockSpec((per, N_EXPERTS, seq), lambda i: (i, 0, 0))
    return _deferred(
        functools.partial(_scatter_kernel, cap, seq, final),
        grid=(stream.n_req // per,),
        in_specs=[pl.BlockSpec((N_EXPERTS, per * cap, D_MODEL), lambda i: (0, i, 0)),
                  sel_blk, sel_blk, tok_blk,
                  _mod_spec(layer, stream, 5, per * seq),
                  pl.BlockSpec((1, D_MODEL), lambda i: (0, 0))],
        out_specs=tok_blk,
        out_shape=jax.ShapeDtypeStruct((stream.tokens, D_MODEL), F32),
        name=f"scatter_l{layer}_s{seq}",
    )(y, pos, g, xn, mods, final_g_row)


def _axial_rope_tables(rows, head_dim):
    row = jnp.repeat(jnp.arange(rows, dtype=F32), GRID_W)
    col = jnp.tile(jnp.arange(GRID_W, dtype=F32), rows)
    nf = head_dim // 4
    inv = ROPE_BASE ** (-jnp.arange(nf, dtype=F32) / nf)
    ar = row[:, None] * inv[None]
    ac = col[:, None] * inv[None]
    ang = jnp.concatenate([ar, ar, ac, ac], axis=-1)
    return jnp.cos(ang), jnp.sin(ang)


def _block_diag(w):
    eye = jnp.eye(LRU_BLOCKS, dtype=w.dtype)
    return jnp.einsum('dnkj,nm->dnkmj', w, eye).reshape(2, LRU_WIDTH, LRU_WIDTH)


def kernel(x_prompt, x_sample, cache_win_k, cache_win_v, state_lru, cache_diff_k, cache_diff_v, c, c_ctx, ada_w, ada_b, norm_g, final_g, even_w_in, even_w_out, conv_w, conv_b, lru_wa, lru_ba, lru_wx, lru_bx, lru_lambda, win_sink, odd_w_in, odd_w_out, diff_lq1, diff_lk1, diff_lq2, diff_lk2, diff_subln_g, moe_router, moe_w_gate, moe_w_up, moe_w_down):
    cv_t = jnp.concatenate([c_ctx[None], c, jnp.zeros((COND_ROWS - N_COND, D_MODEL), F32)], axis=0).T
    mods = _adaln(cv_t, ada_w, ada_b).reshape(DEPTH * COND_ROWS, 1, 6 * D_MODEL)

    cos, sin = _axial_rope_tables(DEC_SEQ // GRID_W, WIN_HEAD_DIM)
    rope_win = (jnp.tile(cos, (1, WIN_HEADS)), jnp.tile(sin, (1, WIN_HEADS)))
    rope_diff = (jnp.tile(cos, (1, 2)), jnp.tile(sin, (1, 2)))

    xs = [x_prompt.reshape(CTX.tokens, D_MODEL), x_sample.reshape(LAT.tokens, D_MODEL)]
    final_g_row = final_g.reshape(1, D_MODEL)
    outs = {}

    def both(make):
        return _run_pair(make(0, CTX), make(1, LAT))

    for layer in range(DEPTH):
        idx = layer // 2
        even = layer % 2 == 0
        w_in = (even_w_in if even else odd_w_in)[idx].astype(BF16)
        w_out = (even_w_out if even else odd_w_out)[idx].astype(BF16)
        wr_t = moe_router[layer].T
        g1_row = norm_g[layer, 0].reshape(1, D_MODEL)
        g2_row = norm_g[layer, 1].reshape(1, D_MODEL)
        if even:
            wa = _block_diag(lru_wa[idx]).astype(BF16)
            wx = _block_diag(lru_wx[idx]).astype(BF16)
            (proj_c,), (proj_d,) = both(lambda si, st: _proj(xs[si], g1_row, mods, layer, st, w_in))
            projs = (proj_c, proj_d)
            h0s = (jnp.zeros((CTX.n_req, 2, LRU_WIDTH), F32), state_lru[:, idx])
            (y_c, h_fin), (y_d, _) = both(lambda si, st: _lru(
                projs[si], st, conv_w[idx], conv_b[idx].reshape(1, LRU_WIDTH), wa, wx,
                lru_ba[idx], lru_bx[idx], lru_lambda[idx], h0s[si]))
            pack = lambda t: t[:, idx].transpose(0, 2, 1, 3).reshape(DEC_BATCH, PAST_LEN, WIN_KV)
            (o_c,), (o_d,) = _run_pair(
                _win_attn(proj_c, CTX, win_sink[idx]),
                _win_attn(proj_d, LAT, win_sink[idx], cache=(pack(cache_win_k), pack(cache_win_v)), rope=rope_win))
            k0 = 2 * LRU_WIDTH + WIN_Q
            to_heads = lambda t: t.reshape(BATCH, SEQ, WIN_KV_HEADS, WIN_HEAD_DIM).transpose(0, 2, 1, 3)[:, None]
            outs["win_k"] = to_heads(proj_c[:, k0:k0 + WIN_KV])
            outs["win_v"] = to_heads(proj_c[:, k0 + WIN_KV:k0 + 2 * WIN_KV])
            outs["lru"] = h_fin[:, None]
            mixed = ([y_c, o_c], [y_d, o_d])
        else:
            lambda_init = 0.8 - 0.6 * math.exp(-0.3 * layer)
            vec = lambda t: t[idx].reshape(1, DIFF_HEAD_DIM)
            args = (vec(diff_lq1), vec(diff_lk1), vec(diff_lq2), vec(diff_lk2), diff_subln_g[idx])
            (q, kh, vh), (proj_d,) = _run_pair(_proj_heads(xs[0], g1_row, mods, layer, CTX, w_in),
                                              _proj(xs[1], g1_row, mods, layer, LAT, w_in))
            (o_c,), (o_d,) = _run_pair(
                _diff_attn((q, kh, vh), CTX, lambda_init, *args),
                _diff_attn(proj_d, LAT, lambda_init, *args,
                           cache=(cache_diff_k[:, idx:idx + 1], cache_diff_v[:, idx:idx + 1]), rope=rope_diff))
            outs["diff_k"] = kh[:, None]
            outs["diff_v"] = vh[:, None]
            mixed = ([o_c], [o_d])

        mix = both(lambda si, st: _mix_out(mixed[si], w_out, xs[si], g2_row, mods, layer, st, wr_t))
        xns, h2s = (mix[0][0], mix[1][0]), (mix[0][1], mix[1][1])

        def request_major(lg, st):
            per_req = st.seq // LOGIT_TILE
            lg = lg.reshape(st.n_req, per_req, N_EXPERTS, LOGIT_TILE).transpose(0, 2, 1, 3)
            return lg.reshape(st.n_req, N_EXPERTS, st.seq)

        lgs = (request_major(mix[0][2], CTX), request_major(mix[1][2], LAT))
        routes = both(lambda si, st: _router(lgs[si], st))
        (rows_c,), (rows_d,) = both(lambda si, st: _gather(h2s[si], routes[si][0], st))
        ys = _ffn(rows_c, rows_d, layer, moe_w_gate, moe_w_up, moe_w_down)
        final = layer == DEPTH - 1
        (x_c,), (x_d,) = both(lambda si, st: _scatter(ys[si], routes[si][0], routes[si][1], xns[si], mods, layer, st,
                                                      final, final_g_row))
        xs = [x_c, x_d]


    y_prompt = xs[0].reshape(BATCH, SEQ, D_MODEL)
    y_sample = xs[1].reshape(DEC_BATCH, DEC_SEQ, D_MODEL)
    return (y_prompt, y_sample, outs["win_k"], outs["win_v"], outs["lru"], outs["diff_k"], outs["diff_v"])
```

```python
import dataclasses
import functools
import math
from typing import Callable, NamedTuple

import jax
import jax.numpy as jnp
from jax import lax
from jax.experimental import pallas as pl
from jax.experimental.pallas import tpu as pltpu

F32 = jnp.float32
BF16 = jnp.bfloat16
I32 = jnp.int32

D_MODEL = 1024
BATCH = 16
SEQ = 256
DEPTH = 2
DEC_BATCH = 2
DEC_SEQ = 1024
PAST_LEN = 512
GRID_W = 64
LRU_WIDTH = D_MODEL // 2
LRU_BLOCKS = 8
LRU_BLOCK = LRU_WIDTH // LRU_BLOCKS
CONV_W = 4
LRU_C = 8.0
WIN_HEADS = 8
WIN_KV_HEADS = 2
WIN_REP = WIN_HEADS // WIN_KV_HEADS
WIN_HEAD_DIM = 64
WINDOW = 128
WIN_Q = WIN_HEADS * WIN_HEAD_DIM
WIN_KV = WIN_KV_HEADS * WIN_HEAD_DIM
EVEN_IN = 2 * LRU_WIDTH + WIN_Q + 2 * WIN_KV
DIFF_HEADS = 8
DIFF_HEAD_DIM = 64
DIFF_QK = DIFF_HEADS * 2 * DIFF_HEAD_DIM
DIFF_V = DIFF_HEADS * 2 * DIFF_HEAD_DIM
ODD_IN = 2 * DIFF_QK + DIFF_V
N_EXPERTS = 16
EXPERT_FF = 2 * D_MODEL
CAPACITY_FACTOR = 2
ROPE_BASE = 10000.0
EPS = 1e-6
NEG_INF = -1e30

LANES = 128
SUBLANES = 8
VMEM_LIMIT_BYTES = 48 * 1024 * 1024

N_COND = 1 + DEC_BATCH
COND_ROWS = SUBLANES
TOKEN_TILE = 512
MIX_TILE = 512
ROUTE_ROWS = 512
LOGIT_TILE = 256
FF_TILE = 1024
FFN_VMEM_LIMIT_BYTES = 56 * 1024 * 1024
PAIR_VMEM_LIMIT_BYTES = 56 * 1024 * 1024
LRU_CHUNK = 128
ATT_Q_TILE = 256
ATT_K_TILE = 256


class Stream:
    def __init__(self, n_req, seq, cond0, cond_step):
        self.n_req, self.seq, self.cond0, self.cond_step = n_req, seq, cond0, cond_step
        self.tokens = n_req * seq
        self.cap = CAPACITY_FACTOR * seq // N_EXPERTS
        self.req_per_step = max(1, ROUTE_ROWS // seq)

    def cond_of_row(self, row):
        return self.cond0 + self.cond_step * (row // self.seq)


CTX = Stream(BATCH, SEQ, 0, 0)
LAT = Stream(DEC_BATCH, DEC_SEQ, 1, 1)


def _params(n_axes, vmem_limit_bytes=VMEM_LIMIT_BYTES):
    return pltpu.CompilerParams(dimension_semantics=("arbitrary",) * n_axes,
                                vmem_limit_bytes=vmem_limit_bytes)


class Call(NamedTuple):
    kernel: Callable
    steps: int
    in_specs: tuple
    out_specs: tuple
    out_shapes: tuple
    scratch: tuple
    args: tuple
    name: str


def _deferred(kernel, *, grid, in_specs, out_specs, out_shape, scratch_shapes=(), name):
    (steps,) = grid
    as_tuple = lambda v: tuple(v) if isinstance(v, (list, tuple)) else (v,)
    return lambda *args: Call(kernel, steps, tuple(in_specs), as_tuple(out_specs), as_tuple(out_shape),
                              tuple(scratch_shapes), args, name)


def _run(call):
    return pl.pallas_call(
        call.kernel, grid=(call.steps,), in_specs=list(call.in_specs), out_specs=list(call.out_specs),
        out_shape=list(call.out_shapes), scratch_shapes=list(call.scratch),
        compiler_params=_params(1), name=call.name)(*call.args)


def _shifted(spec, first, steps):
    if spec.index_map is None:
        return spec
    return dataclasses.replace(spec, index_map=lambda i: spec.index_map(jnp.clip(i - first, 0, steps - 1)))


def _pair_kernel(a, b, *refs):
    n_in, n_out = len(a.in_specs) + len(b.in_specs), len(a.out_specs) + len(b.out_specs)
    ins, outs, scr = refs[:n_in], refs[n_in:n_in + n_out], refs[n_in + n_out:]
    i = pl.program_id(0)

    @pl.when(i < a.steps)
    def _():
        a.kernel(*ins[:len(a.in_specs)], *outs[:len(a.out_specs)], *scr[:len(a.scratch)])

    @pl.when(i >= a.steps)
    def _():
        b.kernel(*ins[len(a.in_specs):], *outs[len(a.out_specs):], *scr[len(a.scratch):])


def _run_pair(a, b):
    specs = lambda sa, sb: ([_shifted(s, 0, a.steps) for s in sa] + [_shifted(s, a.steps, b.steps) for s in sb])
    outs = pl.pallas_call(
        functools.partial(_pair_kernel, a, b), grid=(a.steps + b.steps,),
        in_specs=specs(a.in_specs, b.in_specs), out_specs=specs(a.out_specs, b.out_specs),
        out_shape=list(a.out_shapes + b.out_shapes), scratch_shapes=list(a.scratch + b.scratch),
        compiler_params=_params(1, PAIR_VMEM_LIMIT_BYTES), name=f"{a.name}+{b.name}")(*a.args, *b.args)
    return outs[:len(a.out_specs)], outs[len(a.out_specs):]


def _resident_spec(shape):
    return pl.BlockSpec(shape, lambda i: (0, 0), pipeline_mode=pl.Buffered(1))


def _mod_spec(layer, stream, k, rows_per_step):
    return pl.BlockSpec(
        (1, 1, D_MODEL),
        lambda i, *_: (layer * COND_ROWS + stream.cond_of_row(i * rows_per_step), 0, k))


def _dot(a, b):
    return jnp.dot(a, b, preferred_element_type=F32)


def _dot_nt(a, b):
    return lax.dot_general(a, b, (((1,), (1,)), ((), ())), preferred_element_type=F32)


def _modnorm(x, g, shift, scale):
    y = x * lax.rsqrt(jnp.mean(x * x, axis=-1, keepdims=True) + EPS)
    return (y * g) * (1.0 + scale) + shift


def _lane_half_masks(shape):
    lane = lax.broadcasted_iota(I32, shape, len(shape) - 1)
    left = (lane & (LANES - 1)) < LANES // 2
    return left, jnp.logical_not(left)


def _rope(x, cos, sin):
    parts = []
    for c in range(x.shape[1] // LANES):
        xs = x[:, c * LANES:(c + 1) * LANES]
        lane = lax.broadcasted_iota(I32, xs.shape, 1)
        first = (lane & 31) < 16
        rot = jnp.where(first, -pltpu.roll(xs, LANES - 16, axis=1), pltpu.roll(xs, 16, axis=1))
        parts.append(xs * cos[:, c * LANES:(c + 1) * LANES] + rot * sin[:, c * LANES:(c + 1) * LANES])
    return parts[0] if len(parts) == 1 else jnp.concatenate(parts, axis=1)


def _adaln_kernel(cv_ref, w_ref, b_ref, o_ref):
    cv = cv_ref[...]
    s = cv * jax.nn.sigmoid(cv)
    w = w_ref[0]
    ridx = lax.broadcasted_iota(I32, (COND_ROWS, w.shape[1]), 0)
    out = jnp.zeros((COND_ROWS, w.shape[1]), F32)
    for r in range(N_COND):
        out = jnp.where(ridx == r, jnp.sum(w * s[:, r:r + 1], axis=0, keepdims=True), out)
    o_ref[0] = out + b_ref[0]


def _adaln(cv_t, ada_w, ada_b):
    tn = 1024
    return pl.pallas_call(
        _adaln_kernel,
        grid=(DEPTH, 6 * D_MODEL // tn),
        in_specs=[pl.BlockSpec((D_MODEL, COND_ROWS), lambda l, j: (0, 0)),
                  pl.BlockSpec((1, D_MODEL, tn), lambda l, j: (l, 0, j)),
                  pl.BlockSpec((1, 1, tn), lambda l, j: (l, 0, j))],
        out_specs=pl.BlockSpec((1, COND_ROWS, tn), lambda l, j: (l, 0, j)),
        out_shape=jax.ShapeDtypeStruct((DEPTH, COND_ROWS, 6 * D_MODEL), F32),
        compiler_params=_params(2),
        name="adaln",
    )(cv_t, ada_w, ada_b.reshape(DEPTH, 1, 6 * D_MODEL))


def _proj_kernel(x_ref, g_ref, sh_ref, sc_ref, w_ref, o_ref):
    h = _modnorm(x_ref[...], g_ref[...], sh_ref[0], sc_ref[0])
    o_ref[...] = _dot(h.astype(BF16), w_ref[...])


def _proj(x, g_row, mods, layer, stream, w):
    n_out = w.shape[1]
    tm = TOKEN_TILE
    return _deferred(
        _proj_kernel,
        grid=(stream.tokens // tm,),
        in_specs=[pl.BlockSpec((tm, D_MODEL), lambda i: (i, 0)),
                  pl.BlockSpec((1, D_MODEL), lambda i: (0, 0)),
                  _mod_spec(layer, stream, 0, tm),
                  _mod_spec(layer, stream, 1, tm),
                  _resident_spec((D_MODEL, n_out))],
        out_specs=pl.BlockSpec((tm, n_out), lambda i: (i, 0)),
        out_shape=jax.ShapeDtypeStruct((stream.tokens, n_out), F32),
        name=f"proj_l{layer}_s{stream.seq}",
    )(x, g_row, mods, mods, w)


def _proj_heads_kernel(seq, x_ref, g_ref, sh_ref, sc_ref, w_ref, q_ref, k_ref, v_ref):
    h = _modnorm(x_ref[...], g_ref[...], sh_ref[0], sc_ref[0])
    res = _dot(h.astype(BF16), w_ref[...])
    q_ref[...] = res[:, 0:DIFF_QK]
    for r in range(x_ref.shape[0] // seq):
        rows = slice(r * seq, (r + 1) * seq)
        for hh in range(DIFF_HEADS):
            k_ref[r, hh] = res[rows, DIFF_QK + hh * LANES:DIFF_QK + (hh + 1) * LANES]
            v_ref[r, hh] = res[rows, 2 * DIFF_QK + hh * LANES:2 * DIFF_QK + (hh + 1) * LANES]


def _proj_heads(x, g_row, mods, layer, stream, w):
    tm, seq = TOKEN_TILE, stream.seq
    head_shape = (stream.n_req, DIFF_HEADS, seq, 2 * DIFF_HEAD_DIM)
    head_spec = pl.BlockSpec((tm // seq, DIFF_HEADS, seq, 2 * DIFF_HEAD_DIM), lambda i: (i, 0, 0, 0))
    return _deferred(
        functools.partial(_proj_heads_kernel, seq),
        grid=(stream.tokens // tm,),
        in_specs=[pl.BlockSpec((tm, D_MODEL), lambda i: (i, 0)),
                  pl.BlockSpec((1, D_MODEL), lambda i: (0, 0)),
                  _mod_spec(layer, stream, 0, tm),
                  _mod_spec(layer, stream, 1, tm),
                  _resident_spec((D_MODEL, ODD_IN))],
        out_specs=[pl.BlockSpec((tm, DIFF_QK), lambda i: (i, 0)), head_spec, head_spec],
        out_shape=[jax.ShapeDtypeStruct((stream.tokens, DIFF_QK), F32),
                   jax.ShapeDtypeStruct(head_shape, F32), jax.ShapeDtypeStruct(head_shape, F32)],
        name=f"proj_heads_l{layer}_s{seq}",
    )(x, g_row, mods, mods, w)


def _lru_kernel(seq, xl_ref, gl_ref, cw_ref, cb_ref, wa_ref, wx_ref, ba_ref, bx_ref, lam_ref, h0_ref,
                y_ref, hfin_ref, xe, af, bf, ab, bb):
    width = LRU_WIDTH
    ch = LRU_CHUNK
    seg = seq // SUBLANES
    lead = (CONV_W // 2) * SUBLANES

    def to_segment_major(x):
        return jnp.transpose(x.reshape(SUBLANES, seg, LANES), (1, 0, 2)).reshape(seq, LANES)

    def to_time_major(x):
        return jnp.transpose(x.reshape(seg, SUBLANES, LANES), (1, 0, 2)).reshape(seq, LANES)

    for c in range(width // LANES):
        cols = slice(c * LANES, (c + 1) * LANES)
        xe[lead:lead + seq, cols] = to_segment_major(xl_ref[:, cols])
    sub = lax.broadcasted_iota(I32, (SUBLANES, width), 0)
    for k in range(CONV_W // 2):
        prev = xe[lead + (seg - 2 + k) * SUBLANES:lead + (seg - 1 + k) * SUBLANES, :]
        xe[k * SUBLANES:(k + 1) * SUBLANES, :] = jnp.where(sub >= 1, pltpu.roll(prev, 1, axis=0), 0.0)
    first = xe[lead:lead + SUBLANES, :]
    xe[lead + seq:lead + seq + SUBLANES, :] = jnp.where(sub < SUBLANES - 1,
                                                        pltpu.roll(first, SUBLANES - 1, axis=0), 0.0)

    lam = lam_ref[...]
    z = -lam
    softplus = jnp.maximum(z, 0.0) + jnp.log1p(jnp.exp(-jnp.abs(z)))
    cw = cw_ref[...]
    cb = cb_ref[...]

    def gates_chunk(c, carry):
        r0 = pl.multiple_of(c * ch, ch)
        xc = xe[pl.ds(r0, ch), :] * cw[0:1]
        for j in range(1, CONV_W):
            xc = xc + xe[pl.ds(pl.multiple_of(r0 + j * SUBLANES, SUBLANES), ch), :] * cw[j:j + 1]
        xc = xc + cb
        xcb = xc.astype(BF16)
        for d, (a_s, b_s) in enumerate(((af, bf), (ab, bb))):
            r = jax.nn.sigmoid(_dot(xcb, wa_ref[d]) + ba_ref[d:d + 1])
            ig = jax.nn.sigmoid(_dot(xcb, wx_ref[d]) + bx_ref[d:d + 1])
            log_a = (-LRU_C * r) * softplus[d:d + 1]
            a = jnp.exp(log_a)
            v = 1.0 - a * a
            a_s[pl.ds(r0, ch), :] = a
            b_s[pl.ds(r0, ch), :] = jnp.where(v > 0.0, v * lax.rsqrt(v), 0.0) * (ig * xc)
        return carry

    lax.fori_loop(0, seq // ch, gates_chunk, 0)

    def scan_step(k, carry):
        hf, pf, hb, pb = carry
        rf = pl.multiple_of(k * SUBLANES, SUBLANES)
        rb = pl.multiple_of((seg - 1 - k) * SUBLANES, SUBLANES)
        a = af[pl.ds(rf, SUBLANES), :]
        hf = a * hf + bf[pl.ds(rf, SUBLANES), :]
        pf = a * pf
        bf[pl.ds(rf, SUBLANES), :] = hf
        af[pl.ds(rf, SUBLANES), :] = pf
        a = ab[pl.ds(rb, SUBLANES), :]
        hb = a * hb + bb[pl.ds(rb, SUBLANES), :]
        pb = a * pb
        bb[pl.ds(rb, SUBLANES), :] = hb
        ab[pl.ds(rb, SUBLANES), :] = pb
        return hf, pf, hb, pb

    zeros = jnp.zeros((SUBLANES, width), F32)
    ones = jnp.ones((SUBLANES, width), F32)
    hf, pf, hb, pb = lax.fori_loop(0, seg, scan_step, (zeros, ones, zeros, ones))

    h0 = h0_ref[0]
    state, carry_f = h0[0:1], zeros
    for g in range(SUBLANES):
        carry_f = jnp.where(sub == g, state, carry_f)
        state = pf[g:g + 1] * state + hf[g:g + 1]
    hfin_ref[0, 0:1, :] = state
    state, carry_b = h0[1:2], zeros
    for g in reversed(range(SUBLANES)):
        carry_b = jnp.where(sub == g, state, carry_b)
        state = pb[g:g + 1] * state + hb[g:g + 1]
    hfin_ref[0, 1:2, :] = state

    groups = (ch // SUBLANES, SUBLANES, width)

    def combine_chunk(c, carry):
        r0 = pl.multiple_of(c * ch, ch)
        rows = pl.ds(r0, ch)
        h_fwd = bf[rows, :].reshape(groups) + af[rows, :].reshape(groups) * carry_f
        h_bwd = bb[rows, :].reshape(groups) + ab[rows, :].reshape(groups) * carry_b
        bf[rows, :] = (h_fwd + h_bwd).reshape(ch, width)
        return carry

    lax.fori_loop(0, seq // ch, combine_chunk, 0)

    for c in range(width // LANES):
        cols = slice(c * LANES, (c + 1) * LANES)
        y_ref[:, cols] = (to_time_major(bf[:, cols]) * jax.nn.gelu(gl_ref[:, cols])).astype(BF16)


def _lru(proj, stream, conv_w, conv_b, wa, wx, ba, bx, lam, h0):
    seq, width = stream.seq, LRU_WIDTH
    full2 = lambda b: (0, 0)
    full3 = lambda b: (0, 0, 0)
    return _deferred(
        functools.partial(_lru_kernel, seq),
        grid=(stream.n_req,),
        in_specs=[pl.BlockSpec((seq, width), lambda b: (b, 0)),
                  pl.BlockSpec((seq, width), lambda b: (b, 1)),
                  pl.BlockSpec((CONV_W, width), full2),
                  pl.BlockSpec((1, width), full2),
                  pl.BlockSpec((2, width, width), full3),
                  pl.BlockSpec((2, width, width), full3),
                  pl.BlockSpec((2, width), full2),
                  pl.BlockSpec((2, width), full2),
                  pl.BlockSpec((2, width), full2),
                  pl.BlockSpec((1, 2, width), lambda b: (b, 0, 0))],
        out_specs=[pl.BlockSpec((seq, width), lambda b: (b, 0)),
                   pl.BlockSpec((1, 2, width), lambda b: (b, 0, 0))],
        out_shape=[jax.ShapeDtypeStruct((stream.tokens, width), BF16),
                   jax.ShapeDtypeStruct((stream.n_req, 2, width), F32)],
        scratch_shapes=[pltpu.VMEM((seq + (CONV_W - 1) * SUBLANES, width), F32)] + [pltpu.VMEM((seq, width), F32)] * 4,
        name=f"lru_s{seq}",
    )(proj, proj, conv_w, conv_b, wa, wx, ba, bx, lam, h0)


def _attend(q, chunks, s_ref):
    tile_max = None
    spans = []
    off = 0
    for keys, _, valid in chunks:
        s = _dot_nt(q, keys())
        if valid is not None:
            s = jnp.where(valid, s, NEG_INF)
        n = s.shape[1]
        s_ref[:, off:off + n] = s
        for c in range(n // LANES):
            t = s[:, c * LANES:(c + 1) * LANES]
            tile_max = t if tile_max is None else jnp.maximum(tile_max, t)
        spans.append((off, n))
        off += n
    m = jnp.max(tile_max, axis=-1, keepdims=True)
    acc = None
    for (_, values, _), (o, n) in zip(chunks, spans):
        part = _dot(jnp.exp(s_ref[:, o:o + n] - m).astype(BF16), values())
        acc = part if acc is None else acc + part
    return acc, m


def _split_groups(kk):
    left, right = _lane_half_masks(kk.shape)
    g0_l = jnp.where(left, kk, 0.0)
    g1_r = jnp.where(right, kk, 0.0)
    return ((g0_l, pltpu.roll(g0_l, LANES // 2, axis=1)), (pltpu.roll(g1_r, LANES // 2, axis=1), g1_r))


def _win_ctx_kernel(sink_ref, q_ref, kv_ref, o_ref):
    scale = WIN_HEAD_DIM ** -0.5
    seq = q_ref.shape[0]
    ks = _split_groups(kv_ref[:, 0:LANES])
    vs = _split_groups(kv_ref[:, LANES:2 * LANES])
    top = lax.broadcasted_iota(I32, (2 * seq, 1), 0) < seq
    outs = [None] * (WIN_HEADS // 2)
    for g in range(WIN_KV_HEADS):
        pairs = (2 * g, 2 * g + 1)
        qs = jnp.concatenate([q_ref[:, p * LANES:(p + 1) * LANES] for p in pairs], axis=0)
        qs = (qs * scale).astype(BF16)
        for side in range(2):
            sk = jnp.where(top, sink_ref[2 * pairs[0] + side], sink_ref[2 * pairs[1] + side])
            s = _dot_nt(qs, ks[g][side].astype(BF16))
            m = jnp.maximum(jnp.max(s, axis=-1, keepdims=True), sk)
            e = jnp.exp(s - m)
            den = jnp.sum(e, axis=-1, keepdims=True) + jnp.exp(sk - m)
            o = _dot(e.astype(BF16), vs[g][side].astype(BF16)) * (1.0 / den)
            for k, p in enumerate(pairs):
                part = o[k * seq:(k + 1) * seq]
                outs[p] = part if outs[p] is None else outs[p] + part
    for p in range(WIN_HEADS // 2):
        o_ref[:, p * LANES:(p + 1) * LANES] = outs[p].astype(BF16)


def _win_lat_kernel(sink_ref, q_ref, kv_ref, ck_ref, cv_ref, cos_ref, sin_ref, o_ref,
                    kl_s, vl_s, kc_s, vc_s):
    seq, wn = DEC_SEQ, WINDOW
    scale = WIN_HEAD_DIM ** -0.5
    kr = _rope(kv_ref[:, 0:LANES], cos_ref[:, 0:LANES], sin_ref[:, 0:LANES])
    for src, dst in ((_split_groups(kr), kl_s), (_split_groups(kv_ref[:, LANES:2 * LANES]), vl_s),
                     (_split_groups(ck_ref[0]), kc_s), (_split_groups(cv_ref[0]), vc_s)):
        for g in range(WIN_KV_HEADS):
            for side in range(2):
                dst[2 * g + side] = src[g][side].astype(BF16)

    top = lax.broadcasted_iota(I32, (2 * wn, 1), 0) < wn

    def q_block(i, carry):
        for sub in range(2):
            r0 = pl.multiple_of((2 * i + sub) * wn, wn)
            start = pl.multiple_of(jnp.clip((2 * i + sub - 1) * wn, 0, seq - 3 * wn), wn)
            qr = _rope(q_ref[pl.ds(r0, wn), :], cos_ref[pl.ds(r0, wn), :], sin_ref[pl.ds(r0, wn), :]) * scale
            qpos = r0 + (lax.broadcasted_iota(I32, (2 * wn, 3 * wn), 0) & (wn - 1))
            kpos = start + lax.broadcasted_iota(I32, (2 * wn, 3 * wn), 1)
            valid = jnp.abs(qpos - kpos) <= wn
            outs = [None] * (WIN_HEADS // 2)
            for g in range(WIN_KV_HEADS):
                pairs = (2 * g, 2 * g + 1)
                qs = jnp.concatenate([qr[:, p * LANES:(p + 1) * LANES] for p in pairs], axis=0).astype(BF16)
                for side in range(2):
                    idx = 2 * g + side
                    sk = jnp.where(top, sink_ref[2 * pairs[0] + side], sink_ref[2 * pairs[1] + side])
                    sl = _dot_nt(qs, kl_s[idx, pl.ds(start, 3 * wn), :])
                    sl = jnp.where(valid, sl, NEG_INF)
                    sc = _dot_nt(qs, kc_s[idx])
                    m = jnp.maximum(jnp.maximum(jnp.max(sl, axis=-1, keepdims=True),
                                                jnp.max(sc, axis=-1, keepdims=True)), sk)
                    el = jnp.exp(sl - m)
                    ec = jnp.exp(sc - m)
                    den = (jnp.sum(el, axis=-1, keepdims=True) + jnp.sum(ec, axis=-1, keepdims=True)
                           + jnp.exp(sk - m))
                    o = (_dot(el.astype(BF16), vl_s[idx, pl.ds(start, 3 * wn), :])
                         + _dot(ec.astype(BF16), vc_s[idx])) * (1.0 / den)
                    for k, p in enumerate(pairs):
                        part = o[k * wn:(k + 1) * wn]
                        outs[p] = part if outs[p] is None else outs[p] + part
            for p in range(WIN_HEADS // 2):
                o_ref[pl.ds(r0, wn), p * LANES:(p + 1) * LANES] = outs[p].astype(BF16)
        return carry

    lax.fori_loop(0, seq // (2 * wn), q_block, 0)


def _win_attn(proj, stream, sink, cache=None, rope=None):
    seq = stream.seq
    q_spec = pl.BlockSpec((seq, WIN_Q), lambda b: (b, 2 * LRU_WIDTH // WIN_Q))
    kv_spec = pl.BlockSpec((seq, 2 * WIN_KV), lambda b: (b, (2 * LRU_WIDTH + WIN_Q) // (2 * WIN_KV)))
    sink_spec = pl.BlockSpec(memory_space=pltpu.SMEM)
    out_spec = pl.BlockSpec((seq, WIN_Q), lambda b: (b, 0))
    out_shape = jax.ShapeDtypeStruct((stream.tokens, WIN_Q), BF16)
    if cache is None:
        return _deferred(
            _win_ctx_kernel, grid=(stream.n_req,),
            in_specs=[sink_spec, q_spec, kv_spec], out_specs=out_spec, out_shape=out_shape,
            name="win_attn_ctx",
        )(sink, proj, proj)
    ck, cv = cache
    cos, sin = rope
    cache_spec = pl.BlockSpec((1, PAST_LEN, LANES), lambda b: (b, 0, 0))
    table_spec = pl.BlockSpec((seq, WIN_Q), lambda b: (0, 0))
    return _deferred(
        _win_lat_kernel, grid=(stream.n_req,),
        in_specs=[sink_spec, q_spec, kv_spec, cache_spec, cache_spec, table_spec, table_spec],
        out_specs=out_spec, out_shape=out_shape,
        scratch_shapes=[pltpu.VMEM((4, seq, LANES), BF16), pltpu.VMEM((4, seq, LANES), BF16),
                        pltpu.VMEM((4, PAST_LEN, LANES), BF16), pltpu.VMEM((4, PAST_LEN, LANES), BF16)],
        name="win_attn_lat",
    )(sink, proj, proj, ck, cv, cos, sin)


def _diff_lambda(lq1_ref, lk1_ref, lq2_ref, lk2_ref, lambda_init):
    t1 = jnp.sum(lq1_ref[...] * lk1_ref[...], axis=-1, keepdims=True)
    t2 = jnp.sum(lq2_ref[...] * lk2_ref[...], axis=-1, keepdims=True)
    return jnp.exp(t1) - jnp.exp(t2) + lambda_init


def _subln(o, g_row, lambda_init):
    o = o * lax.rsqrt(jnp.mean(o * o, axis=-1, keepdims=True) + EPS) * g_row
    return o * (1.0 - lambda_init)


def _component_keys(k):
    left, right = _lane_half_masks(k.shape)
    return jnp.where(left, k, 0.0).astype(BF16), jnp.where(right, k, 0.0).astype(BF16)


def _values_with_ones(v):
    return jnp.concatenate([v.astype(BF16), jnp.ones(v.shape, BF16)], axis=1)


def _diff_combine(accs, lam):
    o1, o2 = accs[0][:, 0:LANES], accs[1][:, 0:LANES]
    return o1 * (1.0 / accs[0][:, LANES:]) - o2 * (lam * (1.0 / accs[1][:, LANES:]))


def _diff_ctx_kernel(lambda_init, lq1_ref, lk1_ref, lq2_ref, lk2_ref, sg_ref, q_ref, k_ref, v_ref, o_ref, s_ref):
    scale = DIFF_HEAD_DIM ** -0.5
    tq = ATT_Q_TILE
    lam = _diff_lambda(lq1_ref, lk1_ref, lq2_ref, lk2_ref, lambda_init)
    for h in range(DIFF_HEADS):
        cols = slice(h * LANES, (h + 1) * LANES)
        keys = _component_keys(k_ref[0, h])
        vals = _values_with_ones(v_ref[0, h])
        for qt in range(q_ref.shape[0] // tq):
            rows = slice(qt * tq, (qt + 1) * tq)
            q = (q_ref[rows, cols] * scale).astype(BF16)
            slots = [s_ref.at[(4 * h + 2 * qt + c) % s_ref.shape[0]] for c in range(2)]
            accs = [_attend(q, [(lambda: keys[c], lambda: vals, None)], slots[c])[0] for c in range(2)]
            o = _diff_combine(accs, lam)
            o_ref[rows, cols] = _subln(o, sg_ref[h:h + 1, :], lambda_init).astype(BF16)


def _diff_lat_kernel(lambda_init, lq1_ref, lk1_ref, lq2_ref, lk2_ref, sg_ref, q_ref, k_ref, v_ref,
                     ck_ref, cv_ref, cos_ref, sin_ref, o_ref, kl_s, kc_s, vl_s, vc_s, s_ref):
    seq, tq, tk = DEC_SEQ, ATT_Q_TILE, ATT_K_TILE
    scale = DIFF_HEAD_DIM ** -0.5
    lam = _diff_lambda(lq1_ref, lk1_ref, lq2_ref, lk2_ref, lambda_init)
    for src, dst in ((_rope(k_ref[...], cos_ref[...], sin_ref[...]), kl_s), (ck_ref[0, 0, 0], kc_s)):
        dst[0], dst[1] = _component_keys(src)
    vl_s[...] = _values_with_ones(v_ref[...])
    vc_s[...] = _values_with_ones(cv_ref[0, 0, 0])
    sg = sg_ref[0]

    for qt in range(seq // tq):
        rows = slice(qt * tq, (qt + 1) * tq)
        qr = (_rope(q_ref[rows, :], cos_ref[rows, :], sin_ref[rows, :]) * scale).astype(BF16)
        accs = []
        for comp in range(2):
            chunks = [(lambda j=j: kl_s[comp, j * tk:(j + 1) * tk, :],
                       lambda j=j: vl_s[j * tk:(j + 1) * tk, :], None) for j in range(seq // tk)]
            chunks += [(lambda j=j: kc_s[comp, j * tk:(j + 1) * tk, :],
                        lambda j=j: vc_s[j * tk:(j + 1) * tk, :], None) for j in range(PAST_LEN // tk)]
            accs.append(_attend(qr, chunks, s_ref.at[2 * qt + comp])[0])
        o_ref[rows, :] = _subln(_diff_combine(accs, lam), sg, lambda_init).astype(BF16)


def _diff_attn(proj, stream, lambda_init, lq1, lk1, lq2, lk2, subln_g, cache=None, rope=None):
    seq = stream.seq
    vec_spec = lambda nd: pl.BlockSpec((1, DIFF_HEAD_DIM), lambda *_: (0, 0))
    out_shape = jax.ShapeDtypeStruct((stream.tokens, DIFF_V), BF16)
    if cache is None:
        q, kh, vh = proj
        head_spec = pl.BlockSpec((1, DIFF_HEADS, seq, LANES), lambda b: (b, 0, 0, 0))
        return _deferred(
            functools.partial(_diff_ctx_kernel, lambda_init), grid=(stream.n_req,),
            in_specs=[vec_spec(1)] * 4 + [pl.BlockSpec((DIFF_HEADS, LANES), lambda b: (0, 0)),
                                          pl.BlockSpec((seq, DIFF_QK), lambda b: (b, 0)), head_spec, head_spec],
            out_specs=pl.BlockSpec((seq, DIFF_V), lambda b: (b, 0)), out_shape=out_shape,
            scratch_shapes=[pltpu.VMEM((8, ATT_Q_TILE, seq), F32)],
            name="diff_attn_ctx",
        )(lq1, lk1, lq2, lk2, subln_g, q, kh, vh)
    ck, cv = cache
    cos, sin = rope
    nh = DIFF_HEADS
    blk = lambda c: pl.BlockSpec((seq, LANES), lambda i: (i // nh, c * nh + i % nh))
    cache_spec = pl.BlockSpec((1, 1, 1, PAST_LEN, LANES), lambda i: (i // nh, 0, i % nh, 0, 0))
    table_spec = pl.BlockSpec((seq, LANES), lambda i: (0, 0))
    return _deferred(
        functools.partial(_diff_lat_kernel, lambda_init), grid=(stream.n_req * nh,),
        in_specs=[vec_spec(2)] * 4 + [pl.BlockSpec((1, 1, LANES), lambda i: (i % nh, 0, 0)),
                                      blk(0), blk(1), blk(2), cache_spec, cache_spec, table_spec, table_spec],
        out_specs=pl.BlockSpec((seq, LANES), lambda i: (i // nh, i % nh)), out_shape=out_shape,
        scratch_shapes=[pltpu.VMEM((2, seq, LANES), BF16), pltpu.VMEM((2, PAST_LEN, LANES), BF16),
                        pltpu.VMEM((seq, 2 * LANES), BF16), pltpu.VMEM((PAST_LEN, 2 * LANES), BF16),
                        pltpu.VMEM((2 * seq // ATT_Q_TILE, ATT_Q_TILE, seq + PAST_LEN), F32)],
        name="diff_attn_lat",
    )(lq1, lk1, lq2, lk2, subln_g.reshape(DIFF_HEADS, 1, LANES), proj, proj, proj, ck, cv, cos, sin)


def _mix_out_kernel(n_in, *refs):
    a_refs = refs[:n_in]
    w_ref, x_ref, gate_ref, g2_ref, sh2_ref, sc2_ref, wr_ref, xn_ref, h2_ref, lg_ref = refs[n_in:]
    kp = D_MODEL // n_in
    wr = wr_ref[...]
    w_hi = wr.astype(BF16)
    w_lo = (wr - w_hi.astype(F32)).astype(BF16)
    acc = None
    for k, a_ref in enumerate(a_refs):
        part = _dot(a_ref[...], w_ref[k * kp:(k + 1) * kp, :])
        acc = part if acc is None else acc + part
    xn = x_ref[...] + gate_ref[0] * acc
    xn_ref[...] = xn
    h2 = _modnorm(xn, g2_ref[...], sh2_ref[0], sc2_ref[0])
    h_hi = h2.astype(BF16)
    h2_ref[...] = h_hi
    h_lo = (h2 - h_hi.astype(F32)).astype(BF16)
    by_hi = _dot_nt(jnp.concatenate([w_hi, w_lo], axis=0), h_hi)
    lg = by_hi[0:N_EXPERTS] + (by_hi[N_EXPERTS:] + _dot_nt(w_hi, h_lo))
    for c in range(lg.shape[1] // LOGIT_TILE):
        lg_ref[c] = lg[:, c * LOGIT_TILE:(c + 1) * LOGIT_TILE]


def _mix_out(mixed, w_out, x, g2_row, mods, layer, stream, wr_t):
    tm = MIX_TILE
    n_in = len(mixed)
    kp = D_MODEL // n_in
    row_blk = lambda width: pl.BlockSpec((tm, width), lambda i: (i, 0))
    return _deferred(
        functools.partial(_mix_out_kernel, n_in),
        grid=(stream.tokens // tm,),
        in_specs=[row_blk(kp)] * n_in + [
            _resident_spec((D_MODEL, D_MODEL)),
            row_blk(D_MODEL),
            _mod_spec(layer, stream, 2, tm),
            pl.BlockSpec((1, D_MODEL), lambda i: (0, 0)),
            _mod_spec(layer, stream, 3, tm),
            _mod_spec(layer, stream, 4, tm),
            pl.BlockSpec((N_EXPERTS, D_MODEL), lambda i: (0, 0))],
        out_specs=[row_blk(D_MODEL), row_blk(D_MODEL),
                   pl.BlockSpec((tm // LOGIT_TILE, N_EXPERTS, LOGIT_TILE), lambda i: (i, 0, 0))],
        out_shape=[jax.ShapeDtypeStruct((stream.tokens, D_MODEL), F32),
                   jax.ShapeDtypeStruct((stream.tokens, D_MODEL), BF16),
                   jax.ShapeDtypeStruct((stream.tokens // LOGIT_TILE, N_EXPERTS, LOGIT_TILE), F32)],
        name=f"mix_out_l{layer}_s{stream.seq}",
    )(*mixed, w_out, x, mods, g2_row, mods, mods, wr_t)


def _sort_desc_lanes(x):
    rows, n = x.shape
    tiles = [x[:, c * LANES:(c + 1) * LANES] for c in range(n // LANES)]
    lane = lax.broadcasted_iota(I32, (rows, LANES), 1)
    k = 2
    while k <= n:
        j = k // 2
        while j >= 1:
            if j < LANES:
                lower = (lane & j) == 0
                for c in range(len(tiles)):
                    t = tiles[c]
                    partner = jnp.where(lower, pltpu.roll(t, LANES - j, axis=1), pltpu.roll(t, j, axis=1))
                    desc = ((lane & k) == 0) if k < LANES else (((c * LANES) & k) == 0)
                    take_max = (lower == desc) if k < LANES else (lower if desc else jnp.logical_not(lower))
                    tiles[c] = jnp.where(take_max, jnp.maximum(t, partner), jnp.minimum(t, partner))
            else:
                jc = j // LANES
                new = list(tiles)
                for c in range(len(tiles)):
                    take_max = ((c & jc) == 0) == (((c * LANES) & k) == 0)
                    new[c] = (jnp.maximum if take_max else jnp.minimum)(tiles[c], tiles[c ^ jc])
                tiles = new
            j //= 2
        k *= 2
    return tiles


def _router_kernel(cap, lg_ref, pos_ref, g_ref):
    x = lg_ref[...]
    n_b, n_e, n_tok = x.shape
    e = jnp.exp(x - jnp.max(x, axis=1, keepdims=True))
    aff = (e / jnp.sum(e, axis=1, keepdims=True)).reshape(n_b * n_e, n_tok)
    srt = _sort_desc_lanes(aff)
    lane_k = (cap - 1) % LANES
    thr = srt[(cap - 1) // LANES][:, lane_k:lane_k + 1]
    gt = aff > thr
    eq = aff == thr
    n_gt = jnp.sum(gt.astype(F32), axis=1, keepdims=True)
    before = (lax.broadcasted_iota(I32, (n_tok, n_tok), 0)
              < lax.broadcasted_iota(I32, (n_tok, n_tok), 1)).astype(BF16)
    eq_rank = _dot(eq.astype(BF16), before)
    sel = gt | (eq & (eq_rank < cap - n_gt))
    slot = _dot(sel.astype(BF16), before).astype(I32)
    pos_ref[...] = jnp.where(sel, slot, -1).reshape(n_b, n_e, n_tok)
    g_ref[...] = jnp.where(sel, aff, 0.0).reshape(n_b, n_e, n_tok)


def _router(logits, stream):
    shape = (stream.n_req, N_EXPERTS, stream.seq)
    spec = pl.BlockSpec(shape, lambda i: (0, 0, 0))
    return _deferred(
        functools.partial(_router_kernel, stream.cap), grid=(1,),
        in_specs=[spec], out_specs=[spec, spec],
        out_shape=[jax.ShapeDtypeStruct(shape, I32), jax.ShapeDtypeStruct(shape, F32)],
        name=f"router_s{stream.seq}",
    )(logits)


def _gather_kernel(cap, seq, h_ref, pos_ref, xs_ref):
    slot = lax.broadcasted_iota(I32, (cap, seq), 0)
    for r in range(pos_ref.shape[0]):
        onehot = jnp.concatenate([(slot == pos_ref[r, e:e + 1, :]).astype(BF16) for e in range(N_EXPERTS)], axis=0)
        xs = _dot(onehot, h_ref[r * seq:(r + 1) * seq, :]).astype(BF16)
        for e in range(N_EXPERTS):
            xs_ref[e, r * cap:(r + 1) * cap, :] = xs[e * cap:(e + 1) * cap]


def _gather(h2, pos, stream):
    cap, seq, per = stream.cap, stream.seq, stream.req_per_step
    return _deferred(
        functools.partial(_gather_kernel, cap, seq),
        grid=(stream.n_req // per,),
        in_specs=[pl.BlockSpec((per * seq, D_MODEL), lambda i: (i, 0)),
                  pl.BlockSpec((per, N_EXPERTS, seq), lambda i: (i, 0, 0))],
        out_specs=pl.BlockSpec((N_EXPERTS, per * cap, D_MODEL), lambda i: (0, i, 0)),
        out_shape=jax.ShapeDtypeStruct((N_EXPERTS, stream.n_req * cap, D_MODEL), BF16),
        name=f"gather_s{seq}",
    )(h2, pos)


def _ffn_kernel(xa_ref, xb_ref, wg_ref, wu_ref, wd_ref, ya_ref, yb_ref, acc):
    j = pl.program_id(1)
    rows_a = xa_ref.shape[1]

    @pl.when(j == 0)
    def _():
        acc[...] = jnp.zeros_like(acc)

    x = jnp.concatenate([xa_ref[0], xb_ref[0]], axis=0)
    a = _dot(x, wg_ref[0, 0].astype(BF16))
    u = _dot(x, wu_ref[0, 0].astype(BF16))
    acc[...] += _dot(((a * jax.nn.sigmoid(a)) * u).astype(BF16), wd_ref[0, 0].astype(BF16))

    @pl.when(j == pl.num_programs(1) - 1)
    def _():
        ya_ref[0] = acc[0:rows_a, :].astype(BF16)
        yb_ref[0] = acc[rows_a:, :].astype(BF16)


def _ffn(xs_a, xs_b, layer, w_gate, w_up, w_down):
    tf = FF_TILE
    rows_a, rows_b = xs_a.shape[1], xs_b.shape[1]
    x_spec = lambda rows: pl.BlockSpec((1, rows, D_MODEL), lambda e, j: (e, 0, 0))
    return pl.pallas_call(
        _ffn_kernel,
        grid=(N_EXPERTS, EXPERT_FF // tf),
        in_specs=[x_spec(rows_a), x_spec(rows_b),
                  pl.BlockSpec((1, 1, D_MODEL, tf), lambda e, j: (layer, e, 0, j)),
                  pl.BlockSpec((1, 1, D_MODEL, tf), lambda e, j: (layer, e, 0, j)),
                  pl.BlockSpec((1, 1, tf, D_MODEL), lambda e, j: (layer, e, j, 0))],
        out_specs=[x_spec(rows_a), x_spec(rows_b)],
        out_shape=[jax.ShapeDtypeStruct(xs_a.shape, BF16), jax.ShapeDtypeStruct(xs_b.shape, BF16)],
        scratch_shapes=[pltpu.VMEM((rows_a + rows_b, D_MODEL), F32)],
        compiler_params=_params(2, FFN_VMEM_LIMIT_BYTES),
        name=f"ffn_l{layer}",
    )(xs_a, xs_b, w_gate, w_up, w_down)


def _scatter_kernel(cap, seq, final, y_ref, pos_ref, g_ref, x_ref, gate_ref, fg_ref, o_ref):
    slot = lax.broadcasted_iota(I32, (cap, seq), 0)
    for r in range(pos_ref.shape[0]):
        onehots, gated = [], []
        for e in range(N_EXPERTS):
            hit = slot == pos_ref[r, e:e + 1, :]
            gate = jnp.sum(jnp.where(hit, g_ref[r, e:e + 1, :], 0.0), axis=1, keepdims=True)
            gated.append((y_ref[e, r * cap:(r + 1) * cap, :].astype(F32) * gate).astype(BF16))
            onehots.append(hit.astype(BF16))
        moe = lax.dot_general(jnp.concatenate(onehots, axis=0), jnp.concatenate(gated, axis=0),
                              (((0,), (0,)), ((), ())), preferred_element_type=F32)
        rows = slice(r * seq, (r + 1) * seq)
        xn = x_ref[rows, :] + gate_ref[0] * moe
        if final:
            xn = xn * lax.rsqrt(jnp.mean(xn * xn, axis=-1, keepdims=True) + EPS) * fg_ref[...]
        o_ref[rows, :] = xn


def _scatter(y, pos, g, xn, mods, layer, stream, final, final_g_row):
    cap, seq, per = stream.cap, stream.seq, stream.req_per_step
    tok_blk = pl.BlockSpec((per * seq, D_MODEL), lambda i: (i, 0))
    sel_blk = pl.BlockSpec((per, N_EXPERTS, seq), lambda i: (i, 0, 0))
    return _deferred(
        functools.partial(_scatter_kernel, cap, seq, final),
        grid=(stream.n_req // per,),
        in_specs=[pl.BlockSpec((N_EXPERTS, per * cap, D_MODEL), lambda i: (0, i, 0)),
                  sel_blk, sel_blk, tok_blk,
                  _mod_spec(layer, stream, 5, per * seq),
                  pl.BlockSpec((1, D_MODEL), lambda i: (0, 0))],
        out_specs=tok_blk,
        out_shape=jax.ShapeDtypeStruct((stream.tokens, D_MODEL), F32),
        name=f"scatter_l{layer}_s{seq}",
    )(y, pos, g, xn, mods, final_g_row)


def _axial_rope_tables(rows, head_dim):
    row = jnp.repeat(jnp.arange(rows, dtype=F32), GRID_W)
    col = jnp.tile(jnp.arange(GRID_W, dtype=F32), rows)
    nf = head_dim // 4
    inv = ROPE_BASE ** (-jnp.arange(nf, dtype=F32) / nf)
    ar = row[:, None] * inv[None]
    ac = col[:, None] * inv[None]
    ang = jnp.concatenate([ar, ar, ac, ac], axis=-1)
    return jnp.cos(ang), jnp.sin(ang)


def _block_diag(w):
    eye = jnp.eye(LRU_BLOCKS, dtype=w.dtype)
    return jnp.einsum('dnkj,nm->dnkmj', w, eye).reshape(2, LRU_WIDTH, LRU_WIDTH)


def kernel(x_prompt, x_sample, cache_win_k, cache_win_v, state_lru, cache_diff_k, cache_diff_v, c, c_ctx, ada_w, ada_b, norm_g, final_g, even_w_in, even_w_out, conv_w, conv_b, lru_wa, lru_ba, lru_wx, lru_bx, lru_lambda, win_sink, odd_w_in, odd_w_out, diff_lq1, diff_lk1, diff_lq2, diff_lk2, diff_subln_g, moe_router, moe_w_gate, moe_w_up, moe_w_down):
    cv_t = jnp.concatenate([c_ctx[None], c, jnp.zeros((COND_ROWS - N_COND, D_MODEL), F32)], axis=0).T
    mods = _adaln(cv_t, ada_w, ada_b).reshape(DEPTH * COND_ROWS, 1, 6 * D_MODEL)

    cos, sin = _axial_rope_tables(DEC_SEQ // GRID_W, WIN_HEAD_DIM)
    rope_win = (jnp.tile(cos, (1, WIN_HEADS)), jnp.tile(sin, (1, WIN_HEADS)))
    rope_diff = (jnp.tile(cos, (1, 2)), jnp.tile(sin, (1, 2)))

    xs = [x_prompt.reshape(CTX.tokens, D_MODEL), x_sample.reshape(LAT.tokens, D_MODEL)]
    final_g_row = final_g.reshape(1, D_MODEL)
    outs = {}

    def both(make):
        return _run_pair(make(0, CTX), make(1, LAT))

    for layer in range(DEPTH):
        idx = layer // 2
        even = layer % 2 == 0
        w_in = (even_w_in if even else odd_w_in)[idx].astype(BF16)
        w_out = (even_w_out if even else odd_w_out)[idx].astype(BF16)
        wr_t = moe_router[layer].T
        g1_row = norm_g[layer, 0].reshape(1, D_MODEL)
        g2_row = norm_g[layer, 1].reshape(1, D_MODEL)
        if even:
            wa = _block_diag(lru_wa[idx]).astype(BF16)
            wx = _block_diag(lru_wx[idx]).astype(BF16)
            (proj_c,), (proj_d,) = both(lambda si, st: _proj(xs[si], g1_row, mods, layer, st, w_in))
            projs = (proj_c, proj_d)
            h0s = (jnp.zeros((CTX.n_req, 2, LRU_WIDTH), F32), state_lru[:, idx])
            (y_c, h_fin), (y_d, _) = both(lambda si, st: _lru(
                projs[si], st, conv_w[idx], conv_b[idx].reshape(1, LRU_WIDTH), wa, wx,
                lru_ba[idx], lru_bx[idx], lru_lambda[idx], h0s[si]))
            pack = lambda t: t[:, idx].transpose(0, 2, 1, 3).reshape(DEC_BATCH, PAST_LEN, WIN_KV)
            (o_c,), (o_d,) = _run_pair(
                _win_attn(proj_c, CTX, win_sink[idx]),
                _win_attn(proj_d, LAT, win_sink[idx], cache=(pack(cache_win_k), pack(cache_win_v)), rope=rope_win))
            k0 = 2 * LRU_WIDTH + WIN_Q
            to_heads = lambda t: t.reshape(BATCH, SEQ, WIN_KV_HEADS, WIN_HEAD_DIM).transpose(0, 2, 1, 3)[:, None]
            outs["win_k"] = to_heads(proj_c[:, k0:k0 + WIN_KV])
            outs["win_v"] = to_heads(proj_c[:, k0 + WIN_KV:k0 + 2 * WIN_KV])
            outs["lru"] = h_fin[:, None]
            mixed = ([y_c, o_c], [y_d, o_d])
        else:
            lambda_init = 0.8 - 0.6 * math.exp(-0.3 * layer)
            vec = lambda t: t[idx].reshape(1, DIFF_HEAD_DIM)
            args = (vec(diff_lq1), vec(diff_lk1), vec(diff_lq2), vec(diff_lk2), diff_subln_g[idx])
            (q, kh, vh), (proj_d,) = _run_pair(_proj_heads(xs[0], g1_row, mods, layer, CTX, w_in),
                                              _proj(xs[1], g1_row, mods, layer, LAT, w_in))
            (o_c,), (o_d,) = _run_pair(
                _diff_attn((q, kh, vh), CTX, lambda_init, *args),
                _diff_attn(proj_d, LAT, lambda_init, *args,
                           cache=(cache_diff_k[:, idx:idx + 1], cache_diff_v[:, idx:idx + 1]), rope=rope_diff))
            outs["diff_k"] = kh[:, None]
            outs["diff_v"] = vh[:, None]
            mixed = ([o_c], [o_d])

        mix = both(lambda si, st: _mix_out(mixed[si], w_out, xs[si], g2_row, mods, layer, st, wr_t))
        xns, h2s = (mix[0][0], mix[1][0]), (mix[0][1], mix[1][1])

        def request_major(lg, st):
            per_req = st.seq // LOGIT_TILE
            lg = lg.reshape(st.n_req, per_req, N_EXPERTS, LOGIT_TILE).transpose(0, 2, 1, 3)
            return lg.reshape(st.n_req, N_EXPERTS, st.seq)

        lgs = (request_major(mix[0][2], CTX), request_major(mix[1][2], LAT))
        routes = both(lambda si, st: _router(lgs[si], st))
        (rows_c,), (rows_d,) = both(lambda si, st: _gather(h2s[si], routes[si][0], st))
        ys = _ffn(rows_c, rows_d, layer, moe_w_gate, moe_w_up, moe_w_down)
        final = layer == DEPTH - 1
        (x_c,), (x_d,) = both(lambda si, st: _scatter(ys[si], routes[si][0], routes[si][1], xns[si], mods, layer, st,
                                                      final, final_g_row))
        xs = [x_c, x_d]


    y_prompt = xs[0].reshape(BATCH, SEQ, D_MODEL)
    y_sample = xs[1].reshape(DEC_BATCH, DEC_SEQ, D_MODEL)
    return (y_prompt, y_sample, outs["win_k"], outs["win_v"], outs["lru"], outs["diff_k"], outs["diff_v"])
```

```python
import functools
import math
from typing import Callable, NamedTuple

import jax
import jax.numpy as jnp
from jax import lax
from jax.experimental import pallas as pl
from jax.experimental.pallas import tpu as pltpu

F32 = jnp.float32
BF16 = jnp.bfloat16
I32 = jnp.int32

D_MODEL = 1024
BATCH = 16
SEQ = 256
DEPTH = 2
DEC_BATCH = 2
DEC_SEQ = 1024
PAST_LEN = 512
GRID_W = 64
LRU_WIDTH = D_MODEL // 2
LRU_BLOCKS = 8
LRU_BLOCK = LRU_WIDTH // LRU_BLOCKS
CONV_W = 4
LRU_C = 8.0
WIN_HEADS = 8
WIN_KV_HEADS = 2
WIN_REP = WIN_HEADS // WIN_KV_HEADS
WIN_HEAD_DIM = 64
WINDOW = 128
WIN_Q = WIN_HEADS * WIN_HEAD_DIM
WIN_KV = WIN_KV_HEADS * WIN_HEAD_DIM
EVEN_IN = 2 * LRU_WIDTH + WIN_Q + 2 * WIN_KV
DIFF_HEADS = 8
DIFF_HEAD_DIM = 64
DIFF_QK = DIFF_HEADS * 2 * DIFF_HEAD_DIM
DIFF_V = DIFF_HEADS * 2 * DIFF_HEAD_DIM
ODD_IN = 2 * DIFF_QK + DIFF_V
N_EXPERTS = 16
EXPERT_FF = 2 * D_MODEL
CAPACITY_FACTOR = 2
ROPE_BASE = 10000.0
EPS = 1e-6
NEG_INF = -1e30

LANES = 128
SUBLANES = 8
VMEM_LIMIT_BYTES = 48 * 1024 * 1024

N_COND = 1 + DEC_BATCH
COND_ROWS = SUBLANES
TOKEN_TILE = 512
MIX_TILE = 1024
ROUTE_ROWS = 1024
LOGIT_TILE = 256
FF_TILE = 1024
FFN_VMEM_LIMIT_BYTES = 56 * 1024 * 1024
LRU_CHUNK = 256
WIN_BLOCKS_PER_ITER = 2
ATT_Q_TILE = 256
ATT_K_TILE = 256


class Stream:
    def __init__(self, n_req, seq, cond0, cond_step):
        self.n_req, self.seq, self.cond0, self.cond_step = n_req, seq, cond0, cond_step
        self.tokens = n_req * seq
        self.cap = CAPACITY_FACTOR * seq // N_EXPERTS
        self.req_per_step = max(1, ROUTE_ROWS // seq)

    def cond_of_row(self, row):
        return self.cond0 + self.cond_step * (row // self.seq)


CTX = Stream(BATCH, SEQ, 0, 0)
LAT = Stream(DEC_BATCH, DEC_SEQ, 1, 1)


def _params(n_axes, vmem_limit_bytes=VMEM_LIMIT_BYTES):
    return pltpu.CompilerParams(dimension_semantics=("arbitrary",) * n_axes,
                                vmem_limit_bytes=vmem_limit_bytes)


class Call(NamedTuple):
    kernel: Callable
    steps: int
    in_specs: tuple
    out_specs: tuple
    out_shapes: tuple
    scratch: tuple
    args: tuple
    name: str


def _deferred(kernel, *, grid, in_specs, out_specs, out_shape, scratch_shapes=(), name):
    (steps,) = grid
    as_tuple = lambda v: tuple(v) if isinstance(v, (list, tuple)) else (v,)
    return lambda *args: Call(kernel, steps, tuple(in_specs), as_tuple(out_specs), as_tuple(out_shape),
                              tuple(scratch_shapes), args, name)


def _run(call):
    return pl.pallas_call(
        call.kernel, grid=(call.steps,), in_specs=list(call.in_specs), out_specs=list(call.out_specs),
        out_shape=list(call.out_shapes), scratch_shapes=list(call.scratch),
        compiler_params=_params(1), name=call.name)(*call.args)


def _run_each(a, b):
    return _run(a), _run(b)


def _resident_spec(shape):
    return pl.BlockSpec(shape, lambda i: (0, 0), pipeline_mode=pl.Buffered(1))


def _mod_spec(layer, stream, k, rows_per_step):
    return pl.BlockSpec(
        (1, 1, D_MODEL),
        lambda i, *_: (layer * COND_ROWS + stream.cond_of_row(i * rows_per_step), 0, k))


def _dot(a, b):
    return jnp.dot(a, b, preferred_element_type=F32)


def _dot_nt(a, b):
    return lax.dot_general(a, b, (((1,), (1,)), ((), ())), preferred_element_type=F32)


def _modnorm(x, g, shift, scale):
    y = x * lax.rsqrt(jnp.mean(x * x, axis=-1, keepdims=True) + EPS)
    return (y * g) * (1.0 + scale) + shift


def _lane_half_masks(shape):
    lane = lax.broadcasted_iota(I32, shape, len(shape) - 1)
    left = (lane & (LANES - 1)) < LANES // 2
    return left, jnp.logical_not(left)


def _rope(x, cos, sin):
    parts = []
    for c in range(x.shape[1] // LANES):
        xs = x[:, c * LANES:(c + 1) * LANES]
        lane = lax.broadcasted_iota(I32, xs.shape, 1)
        first = (lane & 31) < 16
        rot = jnp.where(first, -pltpu.roll(xs, LANES - 16, axis=1), pltpu.roll(xs, 16, axis=1))
        parts.append(xs * cos[:, c * LANES:(c + 1) * LANES] + rot * sin[:, c * LANES:(c + 1) * LANES])
    return parts[0] if len(parts) == 1 else jnp.concatenate(parts, axis=1)


def _adaln_kernel(cv_ref, w_ref, b_ref, o_ref):
    cv = cv_ref[...]
    s = cv * jax.nn.sigmoid(cv)
    w = w_ref[0]
    ridx = lax.broadcasted_iota(I32, (COND_ROWS, w.shape[1]), 0)
    out = jnp.zeros((COND_ROWS, w.shape[1]), F32)
    for r in range(N_COND):
        out = jnp.where(ridx == r, jnp.sum(w * s[:, r:r + 1], axis=0, keepdims=True), out)
    o_ref[0] = out + b_ref[0]


def _adaln(cv_t, ada_w, ada_b):
    tn = 1024
    return pl.pallas_call(
        _adaln_kernel,
        grid=(DEPTH, 6 * D_MODEL // tn),
        in_specs=[pl.BlockSpec((D_MODEL, COND_ROWS), lambda l, j: (0, 0)),
                  pl.BlockSpec((1, D_MODEL, tn), lambda l, j: (l, 0, j)),
                  pl.BlockSpec((1, 1, tn), lambda l, j: (l, 0, j))],
        out_specs=pl.BlockSpec((1, COND_ROWS, tn), lambda l, j: (l, 0, j)),
        out_shape=jax.ShapeDtypeStruct((DEPTH, COND_ROWS, 6 * D_MODEL), F32),
        compiler_params=_params(2),
        name="adaln",
    )(cv_t, ada_w, ada_b.reshape(DEPTH, 1, 6 * D_MODEL))


def _proj_kernel(x_ref, g_ref, sh_ref, sc_ref, w_ref, o_ref):
    h = _modnorm(x_ref[...], g_ref[...], sh_ref[0], sc_ref[0])
    o_ref[...] = _dot(h.astype(BF16), w_ref[...])


def _proj(x, g_row, mods, layer, stream, w):
    n_out = w.shape[1]
    tm = TOKEN_TILE
    return _deferred(
        _proj_kernel,
        grid=(stream.tokens // tm,),
        in_specs=[pl.BlockSpec((tm, D_MODEL), lambda i: (i, 0)),
                  pl.BlockSpec((1, D_MODEL), lambda i: (0, 0)),
                  _mod_spec(layer, stream, 0, tm),
                  _mod_spec(layer, stream, 1, tm),
                  _resident_spec((D_MODEL, n_out))],
        out_specs=pl.BlockSpec((tm, n_out), lambda i: (i, 0)),
        out_shape=jax.ShapeDtypeStruct((stream.tokens, n_out), F32),
        name=f"proj_l{layer}_s{stream.seq}",
    )(x, g_row, mods, mods, w)


def _proj_heads_kernel(seq, x_ref, g_ref, sh_ref, sc_ref, w_ref, q_ref, k_ref, v_ref):
    h = _modnorm(x_ref[...], g_ref[...], sh_ref[0], sc_ref[0])
    res = _dot(h.astype(BF16), w_ref[...])
    q_ref[...] = res[:, 0:DIFF_QK]
    for r in range(x_ref.shape[0] // seq):
        rows = slice(r * seq, (r + 1) * seq)
        for hh in range(DIFF_HEADS):
            k_ref[r, hh] = res[rows, DIFF_QK + hh * LANES:DIFF_QK + (hh + 1) * LANES]
            v_ref[r, hh] = res[rows, 2 * DIFF_QK + hh * LANES:2 * DIFF_QK + (hh + 1) * LANES]


def _proj_heads(x, g_row, mods, layer, stream, w):
    tm, seq = TOKEN_TILE, stream.seq
    head_shape = (stream.n_req, DIFF_HEADS, seq, 2 * DIFF_HEAD_DIM)
    head_spec = pl.BlockSpec((tm // seq, DIFF_HEADS, seq, 2 * DIFF_HEAD_DIM), lambda i: (i, 0, 0, 0))
    return _deferred(
        functools.partial(_proj_heads_kernel, seq),
        grid=(stream.tokens // tm,),
        in_specs=[pl.BlockSpec((tm, D_MODEL), lambda i: (i, 0)),
                  pl.BlockSpec((1, D_MODEL), lambda i: (0, 0)),
                  _mod_spec(layer, stream, 0, tm),
                  _mod_spec(layer, stream, 1, tm),
                  _resident_spec((D_MODEL, ODD_IN))],
        out_specs=[pl.BlockSpec((tm, DIFF_QK), lambda i: (i, 0)), head_spec, head_spec],
        out_shape=[jax.ShapeDtypeStruct((stream.tokens, DIFF_QK), F32),
                   jax.ShapeDtypeStruct(head_shape, F32), jax.ShapeDtypeStruct(head_shape, F32)],
        name=f"proj_heads_l{layer}_s{seq}",
    )(x, g_row, mods, mods, w)


def _lru_kernel(seq, xl_ref, gl_ref, cw_ref, cb_ref, wa_ref, wx_ref, ba_ref, bx_ref, lam_ref, h0_ref,
                y_ref, hfin_ref, xe, af, bf, ab, bb):
    width = LRU_WIDTH
    ch = LRU_CHUNK
    seg = seq // SUBLANES
    lead = (CONV_W // 2) * SUBLANES

    def to_segment_major(x):
        return jnp.transpose(x.reshape(SUBLANES, seg, LANES), (1, 0, 2)).reshape(seq, LANES)

    def to_time_major(x):
        return jnp.transpose(x.reshape(seg, SUBLANES, LANES), (1, 0, 2)).reshape(seq, LANES)

    for c in range(width // LANES):
        cols = slice(c * LANES, (c + 1) * LANES)
        xe[lead:lead + seq, cols] = to_segment_major(xl_ref[:, cols])
    sub = lax.broadcasted_iota(I32, (SUBLANES, width), 0)
    for k in range(CONV_W // 2):
        prev = xe[lead + (seg - 2 + k) * SUBLANES:lead + (seg - 1 + k) * SUBLANES, :]
        xe[k * SUBLANES:(k + 1) * SUBLANES, :] = jnp.where(sub >= 1, pltpu.roll(prev, 1, axis=0), 0.0)
    first = xe[lead:lead + SUBLANES, :]
    xe[lead + seq:lead + seq + SUBLANES, :] = jnp.where(sub < SUBLANES - 1,
                                                        pltpu.roll(first, SUBLANES - 1, axis=0), 0.0)

    lam = lam_ref[...]
    z = -lam
    softplus = jnp.maximum(z, 0.0) + jnp.log1p(jnp.exp(-jnp.abs(z)))
    cw = cw_ref[...]
    cb = cb_ref[...]

    def gates_chunk(c, carry):
        r0 = pl.multiple_of(c * ch, ch)
        xc = xe[pl.ds(r0, ch), :] * cw[0:1]
        for j in range(1, CONV_W):
            xc = xc + xe[pl.ds(pl.multiple_of(r0 + j * SUBLANES, SUBLANES), ch), :] * cw[j:j + 1]
        xc = xc + cb
        xcb = xc.astype(BF16)
        for d, (a_s, b_s) in enumerate(((af, bf), (ab, bb))):
            r = jax.nn.sigmoid(_dot(xcb, wa_ref[d]) + ba_ref[d:d + 1])
            ig = jax.nn.sigmoid(_dot(xcb, wx_ref[d]) + bx_ref[d:d + 1])
            log_a = (-LRU_C * r) * softplus[d:d + 1]
            a = jnp.exp(log_a)
            v = 1.0 - a * a
            a_s[pl.ds(r0, ch), :] = a
            b_s[pl.ds(r0, ch), :] = jnp.where(v > 0.0, v * lax.rsqrt(v), 0.0) * (ig * xc)
        return carry

    lax.fori_loop(0, seq // ch, gates_chunk, 0)

    def scan_step(k, carry):
        hf, pf, hb, pb = carry
        rf = pl.multiple_of(k * SUBLANES, SUBLANES)
        rb = pl.multiple_of((seg - 1 - k) * SUBLANES, SUBLANES)
        a = af[pl.ds(rf, SUBLANES), :]
        hf = a * hf + bf[pl.ds(rf, SUBLANES), :]
        pf = a * pf
        bf[pl.ds(rf, SUBLANES), :] = hf
        af[pl.ds(rf, SUBLANES), :] = pf
        a = ab[pl.ds(rb, SUBLANES), :]
        hb = a * hb + bb[pl.ds(rb, SUBLANES), :]
        pb = a * pb
        bb[pl.ds(rb, SUBLANES), :] = hb
        ab[pl.ds(rb, SUBLANES), :] = pb
        return hf, pf, hb, pb

    zeros = jnp.zeros((SUBLANES, width), F32)
    ones = jnp.ones((SUBLANES, width), F32)
    hf, pf, hb, pb = lax.fori_loop(0, seg, scan_step, (zeros, ones, zeros, ones))

    h0 = h0_ref[0]
    state, carry_f = h0[0:1], zeros
    for g in range(SUBLANES):
        carry_f = jnp.where(sub == g, state, carry_f)
        state = pf[g:g + 1] * state + hf[g:g + 1]
    hfin_ref[0, 0:1, :] = state
    state, carry_b = h0[1:2], zeros
    for g in reversed(range(SUBLANES)):
        carry_b = jnp.where(sub == g, state, carry_b)
        state = pb[g:g + 1] * state + hb[g:g + 1]
    hfin_ref[0, 1:2, :] = state

    groups = (ch // SUBLANES, SUBLANES, width)

    def combine_chunk(c, carry):
        r0 = pl.multiple_of(c * ch, ch)
        rows = pl.ds(r0, ch)
        h_fwd = bf[rows, :].reshape(groups) + af[rows, :].reshape(groups) * carry_f
        h_bwd = bb[rows, :].reshape(groups) + ab[rows, :].reshape(groups) * carry_b
        bf[rows, :] = (h_fwd + h_bwd).reshape(ch, width)
        return carry

    lax.fori_loop(0, seq // ch, combine_chunk, 0)

    for c in range(width // LANES):
        cols = slice(c * LANES, (c + 1) * LANES)
        y_ref[:, cols] = (to_time_major(bf[:, cols]) * jax.nn.gelu(gl_ref[:, cols])).astype(BF16)


def _lru(proj, stream, conv_w, conv_b, wa, wx, ba, bx, lam, h0):
    seq, width = stream.seq, LRU_WIDTH
    full2 = lambda b: (0, 0)
    full3 = lambda b: (0, 0, 0)
    return _deferred(
        functools.partial(_lru_kernel, seq),
        grid=(stream.n_req,),
        in_specs=[pl.BlockSpec((seq, width), lambda b: (b, 0)),
                  pl.BlockSpec((seq, width), lambda b: (b, 1)),
                  pl.BlockSpec((CONV_W, width), full2),
                  pl.BlockSpec((1, width), full2),
                  pl.BlockSpec((2, width, width), full3),
                  pl.BlockSpec((2, width, width), full3),
                  pl.BlockSpec((2, width), full2),
                  pl.BlockSpec((2, width), full2),
                  pl.BlockSpec((2, width), full2),
                  pl.BlockSpec((1, 2, width), lambda b: (b, 0, 0))],
        out_specs=[pl.BlockSpec((seq, width), lambda b: (b, 0)),
                   pl.BlockSpec((1, 2, width), lambda b: (b, 0, 0))],
        out_shape=[jax.ShapeDtypeStruct((stream.tokens, width), BF16),
                   jax.ShapeDtypeStruct((stream.n_req, 2, width), F32)],
        scratch_shapes=[pltpu.VMEM((seq + (CONV_W - 1) * SUBLANES, width), F32)] + [pltpu.VMEM((seq, width), F32)] * 4,
        name=f"lru_s{seq}",
    )(proj, proj, conv_w, conv_b, wa, wx, ba, bx, lam, h0)


def _attend(q, chunks, s_ref):
    tile_max = None
    spans = []
    off = 0
    for keys, _, valid in chunks:
        s = _dot_nt(q, keys())
        if valid is not None:
            s = jnp.where(valid, s, NEG_INF)
        n = s.shape[1]
        s_ref[:, off:off + n] = s
        for c in range(n // LANES):
            t = s[:, c * LANES:(c + 1) * LANES]
            tile_max = t if tile_max is None else jnp.maximum(tile_max, t)
        spans.append((off, n))
        off += n
    m = jnp.max(tile_max, axis=-1, keepdims=True)
    acc = None
    for (_, values, _), (o, n) in zip(chunks, spans):
        part = _dot(jnp.exp(s_ref[:, o:o + n] - m).astype(BF16), values())
        acc = part if acc is None else acc + part
    return acc, m


def _split_groups(kk):
    left, right = _lane_half_masks(kk.shape)
    g0_l = jnp.where(left, kk, 0.0)
    g1_r = jnp.where(right, kk, 0.0)
    return ((g0_l, pltpu.roll(g0_l, LANES // 2, axis=1)), (pltpu.roll(g1_r, LANES // 2, axis=1), g1_r))


def _win_ctx_kernel(sink_ref, q_ref, kv_ref, o_ref):
    scale = WIN_HEAD_DIM ** -0.5
    seq = q_ref.shape[0]
    ks = _split_groups(kv_ref[:, 0:LANES])
    vs = _split_groups(kv_ref[:, LANES:2 * LANES])
    top = lax.broadcasted_iota(I32, (2 * seq, 1), 0) < seq
    outs = [None] * (WIN_HEADS // 2)
    for g in range(WIN_KV_HEADS):
        pairs = (2 * g, 2 * g + 1)
        qs = jnp.concatenate([q_ref[:, p * LANES:(p + 1) * LANES] for p in pairs], axis=0)
        qs = (qs * scale).astype(BF16)
        for side in range(2):
            sk = jnp.where(top, sink_ref[2 * pairs[0] + side], sink_ref[2 * pairs[1] + side])
            s = _dot_nt(qs, ks[g][side].astype(BF16))
            m = jnp.maximum(jnp.max(s, axis=-1, keepdims=True), sk)
            e = jnp.exp(s - m)
            den = jnp.sum(e, axis=-1, keepdims=True) + jnp.exp(sk - m)
            o = _dot(e.astype(BF16), vs[g][side].astype(BF16)) * (1.0 / den)
            for k, p in enumerate(pairs):
                part = o[k * seq:(k + 1) * seq]
                outs[p] = part if outs[p] is None else outs[p] + part
    for p in range(WIN_HEADS // 2):
        o_ref[:, p * LANES:(p + 1) * LANES] = outs[p].astype(BF16)


def _win_lat_kernel(sink_ref, q_ref, kv_ref, ck_ref, cv_ref, cos_ref, sin_ref, o_ref,
                    kl_s, vl_s, kc_s, vc_s):
    seq, wn = DEC_SEQ, WINDOW
    scale = WIN_HEAD_DIM ** -0.5
    kr = _rope(kv_ref[:, 0:LANES], cos_ref[:, 0:LANES], sin_ref[:, 0:LANES])
    for src, dst in ((_split_groups(kr), kl_s), (_split_groups(kv_ref[:, LANES:2 * LANES]), vl_s),
                     (_split_groups(ck_ref[0]), kc_s), (_split_groups(cv_ref[0]), vc_s)):
        for g in range(WIN_KV_HEADS):
            for side in range(2):
                dst[2 * g + side] = src[g][side].astype(BF16)

    top = lax.broadcasted_iota(I32, (2 * wn, 1), 0) < wn

    def q_block(i, carry):
        for sub in range(WIN_BLOCKS_PER_ITER):
            blk = WIN_BLOCKS_PER_ITER * i + sub
            r0 = pl.multiple_of(blk * wn, wn)
            start = pl.multiple_of(jnp.clip((blk - 1) * wn, 0, seq - 3 * wn), wn)
            qr = _rope(q_ref[pl.ds(r0, wn), :], cos_ref[pl.ds(r0, wn), :], sin_ref[pl.ds(r0, wn), :]) * scale
            qpos = r0 + (lax.broadcasted_iota(I32, (2 * wn, 3 * wn), 0) & (wn - 1))
            kpos = start + lax.broadcasted_iota(I32, (2 * wn, 3 * wn), 1)
            valid = jnp.abs(qpos - kpos) <= wn
            outs = [None] * (WIN_HEADS // 2)
            for g in range(WIN_KV_HEADS):
                pairs = (2 * g, 2 * g + 1)
                qs = jnp.concatenate([qr[:, p * LANES:(p + 1) * LANES] for p in pairs], axis=0).astype(BF16)
                for side in range(2):
                    idx = 2 * g + side
                    sk = jnp.where(top, sink_ref[2 * pairs[0] + side], sink_ref[2 * pairs[1] + side])
                    sl = _dot_nt(qs, kl_s[idx, pl.ds(start, 3 * wn), :])
                    sl = jnp.where(valid, sl, NEG_INF)
                    sc = _dot_nt(qs, kc_s[idx])
                    m = jnp.maximum(jnp.maximum(jnp.max(sl, axis=-1, keepdims=True),
                                                jnp.max(sc, axis=-1, keepdims=True)), sk)
                    el = jnp.exp(sl - m)
                    ec = jnp.exp(sc - m)
                    den = (jnp.sum(el, axis=-1, keepdims=True) + jnp.sum(ec, axis=-1, keepdims=True)
                           + jnp.exp(sk - m))
                    o = (_dot(el.astype(BF16), vl_s[idx, pl.ds(start, 3 * wn), :])
                         + _dot(ec.astype(BF16), vc_s[idx])) * (1.0 / den)
                    for k, p in enumerate(pairs):
                        part = o[k * wn:(k + 1) * wn]
                        outs[p] = part if outs[p] is None else outs[p] + part
            for p in range(WIN_HEADS // 2):
                o_ref[pl.ds(r0, wn), p * LANES:(p + 1) * LANES] = outs[p].astype(BF16)
        return carry

    lax.fori_loop(0, seq // (WIN_BLOCKS_PER_ITER * wn), q_block, 0)


def _win_attn(proj, stream, sink, cache=None, rope=None):
    seq = stream.seq
    q_spec = pl.BlockSpec((seq, WIN_Q), lambda b: (b, 2 * LRU_WIDTH // WIN_Q))
    kv_spec = pl.BlockSpec((seq, 2 * WIN_KV), lambda b: (b, (2 * LRU_WIDTH + WIN_Q) // (2 * WIN_KV)))
    sink_spec = pl.BlockSpec(memory_space=pltpu.SMEM)
    out_spec = pl.BlockSpec((seq, WIN_Q), lambda b: (b, 0))
    out_shape = jax.ShapeDtypeStruct((stream.tokens, WIN_Q), BF16)
    if cache is None:
        return _deferred(
            _win_ctx_kernel, grid=(stream.n_req,),
            in_specs=[sink_spec, q_spec, kv_spec], out_specs=out_spec, out_shape=out_shape,
            name="win_attn_ctx",
        )(sink, proj, proj)
    ck, cv = cache
    cos, sin = rope
    cache_spec = pl.BlockSpec((1, PAST_LEN, LANES), lambda b: (b, 0, 0))
    table_spec = pl.BlockSpec((seq, WIN_Q), lambda b: (0, 0))
    return _deferred(
        _win_lat_kernel, grid=(stream.n_req,),
        in_specs=[sink_spec, q_spec, kv_spec, cache_spec, cache_spec, table_spec, table_spec],
        out_specs=out_spec, out_shape=out_shape,
        scratch_shapes=[pltpu.VMEM((4, seq, LANES), BF16), pltpu.VMEM((4, seq, LANES), BF16),
                        pltpu.VMEM((4, PAST_LEN, LANES), BF16), pltpu.VMEM((4, PAST_LEN, LANES), BF16)],
        name="win_attn_lat",
    )(sink, proj, proj, ck, cv, cos, sin)


def _diff_lambda(lq1_ref, lk1_ref, lq2_ref, lk2_ref, lambda_init):
    t1 = jnp.sum(lq1_ref[...] * lk1_ref[...], axis=-1, keepdims=True)
    t2 = jnp.sum(lq2_ref[...] * lk2_ref[...], axis=-1, keepdims=True)
    return jnp.exp(t1) - jnp.exp(t2) + lambda_init


def _subln(o, g_row, lambda_init):
    o = o * lax.rsqrt(jnp.mean(o * o, axis=-1, keepdims=True) + EPS) * g_row
    return o * (1.0 - lambda_init)


def _component_keys(k):
    left, right = _lane_half_masks(k.shape)
    return jnp.where(left, k, 0.0).astype(BF16), jnp.where(right, k, 0.0).astype(BF16)


def _values_with_ones(v):
    return jnp.concatenate([v.astype(BF16), jnp.ones(v.shape, BF16)], axis=1)


def _diff_combine(accs, lam):
    o1, o2 = accs[0][:, 0:LANES], accs[1][:, 0:LANES]
    return o1 * (1.0 / accs[0][:, LANES:]) - o2 * (lam * (1.0 / accs[1][:, LANES:]))


def _diff_ctx_kernel(lambda_init, lq1_ref, lk1_ref, lq2_ref, lk2_ref, sg_ref, q_ref, k_ref, v_ref, o_ref, s_ref):
    scale = DIFF_HEAD_DIM ** -0.5
    tq = ATT_Q_TILE
    lam = _diff_lambda(lq1_ref, lk1_ref, lq2_ref, lk2_ref, lambda_init)
    for h in range(DIFF_HEADS):
        cols = slice(h * LANES, (h + 1) * LANES)
        keys = _component_keys(k_ref[0, h])
        vals = _values_with_ones(v_ref[0, h])
        for qt in range(q_ref.shape[0] // tq):
            rows = slice(qt * tq, (qt + 1) * tq)
            q = (q_ref[rows, cols] * scale).astype(BF16)
            slots = [s_ref.at[(4 * h + 2 * qt + c) % s_ref.shape[0]] for c in range(2)]
            accs = [_attend(q, [(lambda: keys[c], lambda: vals, None)], slots[c])[0] for c in range(2)]
            o = _diff_combine(accs, lam)
            o_ref[rows, cols] = _subln(o, sg_ref[h:h + 1, :], lambda_init).astype(BF16)


def _diff_lat_kernel(lambda_init, lq1_ref, lk1_ref, lq2_ref, lk2_ref, sg_ref, q_ref, k_ref, v_ref,
                     ck_ref, cv_ref, cos_ref, sin_ref, o_ref, kl_s, kc_s, vl_s, vc_s, s_ref):
    seq, tq, tk = DEC_SEQ, ATT_Q_TILE, ATT_K_TILE
    scale = DIFF_HEAD_DIM ** -0.5
    lam = _diff_lambda(lq1_ref, lk1_ref, lq2_ref, lk2_ref, lambda_init)
    for src, dst in ((_rope(k_ref[...], cos_ref[...], sin_ref[...]), kl_s), (ck_ref[0, 0, 0], kc_s)):
        dst[0], dst[1] = _component_keys(src)
    vl_s[...] = _values_with_ones(v_ref[...])
    vc_s[...] = _values_with_ones(cv_ref[0, 0, 0])
    sg = sg_ref[0]

    for qt in range(seq // tq):
        rows = slice(qt * tq, (qt + 1) * tq)
        qr = (_rope(q_ref[rows, :], cos_ref[rows, :], sin_ref[rows, :]) * scale).astype(BF16)
        accs = []
        for comp in range(2):
            chunks = [(lambda j=j: kl_s[comp, j * tk:(j + 1) * tk, :],
                       lambda j=j: vl_s[j * tk:(j + 1) * tk, :], None) for j in range(seq // tk)]
            chunks += [(lambda j=j: kc_s[comp, j * tk:(j + 1) * tk, :],
                        lambda j=j: vc_s[j * tk:(j + 1) * tk, :], None) for j in range(PAST_LEN // tk)]
            accs.append(_attend(qr, chunks, s_ref.at[2 * qt + comp])[0])
        o_ref[rows, :] = _subln(_diff_combine(accs, lam), sg, lambda_init).astype(BF16)


def _diff_attn(proj, stream, lambda_init, lq1, lk1, lq2, lk2, subln_g, cache=None, rope=None):
    seq = stream.seq
    vec_spec = lambda nd: pl.BlockSpec((1, DIFF_HEAD_DIM), lambda *_: (0, 0))
    out_shape = jax.ShapeDtypeStruct((stream.tokens, DIFF_V), BF16)
    if cache is None:
        q, kh, vh = proj
        head_spec = pl.BlockSpec((1, DIFF_HEADS, seq, LANES), lambda b: (b, 0, 0, 0))
        return _deferred(
            functools.partial(_diff_ctx_kernel, lambda_init), grid=(stream.n_req,),
            in_specs=[vec_spec(1)] * 4 + [pl.BlockSpec((DIFF_HEADS, LANES), lambda b: (0, 0)),
                                          pl.BlockSpec((seq, DIFF_QK), lambda b: (b, 0)), head_spec, head_spec],
            out_specs=pl.BlockSpec((seq, DIFF_V), lambda b: (b, 0)), out_shape=out_shape,
            scratch_shapes=[pltpu.VMEM((8, ATT_Q_TILE, seq), F32)],
            name="diff_attn_ctx",
        )(lq1, lk1, lq2, lk2, subln_g, q, kh, vh)
    ck, cv = cache
    cos, sin = rope
    nh = DIFF_HEADS
    blk = lambda c: pl.BlockSpec((seq, LANES), lambda i: (i // nh, c * nh + i % nh))
    cache_spec = pl.BlockSpec((1, 1, 1, PAST_LEN, LANES), lambda i: (i // nh, 0, i % nh, 0, 0))
    table_spec = pl.BlockSpec((seq, LANES), lambda i: (0, 0))
    return _deferred(
        functools.partial(_diff_lat_kernel, lambda_init), grid=(stream.n_req * nh,),
        in_specs=[vec_spec(2)] * 4 + [pl.BlockSpec((1, 1, LANES), lambda i: (i % nh, 0, 0)),
                                      blk(0), blk(1), blk(2), cache_spec, cache_spec, table_spec, table_spec],
        out_specs=pl.BlockSpec((seq, LANES), lambda i: (i // nh, i % nh)), out_shape=out_shape,
        scratch_shapes=[pltpu.VMEM((2, seq, LANES), BF16), pltpu.VMEM((2, PAST_LEN, LANES), BF16),
                        pltpu.VMEM((seq, 2 * LANES), BF16), pltpu.VMEM((PAST_LEN, 2 * LANES), BF16),
                        pltpu.VMEM((2 * seq // ATT_Q_TILE, ATT_Q_TILE, seq + PAST_LEN), F32)],
        name="diff_attn_lat",
    )(lq1, lk1, lq2, lk2, subln_g.reshape(DIFF_HEADS, 1, LANES), proj, proj, proj, ck, cv, cos, sin)


def _mix_out_kernel(n_in, *refs):
    a_refs = refs[:n_in]
    w_ref, x_ref, gate_ref, g2_ref, sh2_ref, sc2_ref, wr_ref, xn_ref, h2_ref, lg_ref = refs[n_in:]
    kp = D_MODEL // n_in
    wr = wr_ref[...]
    w_hi = wr.astype(BF16)
    w_lo = (wr - w_hi.astype(F32)).astype(BF16)
    acc = None
    for k, a_ref in enumerate(a_refs):
        part = _dot(a_ref[...], w_ref[k * kp:(k + 1) * kp, :])
        acc = part if acc is None else acc + part
    xn = x_ref[...] + gate_ref[0] * acc
    xn_ref[...] = xn
    h2 = _modnorm(xn, g2_ref[...], sh2_ref[0], sc2_ref[0])
    h_hi = h2.astype(BF16)
    h2_ref[...] = h_hi
    h_lo = (h2 - h_hi.astype(F32)).astype(BF16)
    by_hi = _dot_nt(jnp.concatenate([w_hi, w_lo], axis=0), h_hi)
    lg = by_hi[0:N_EXPERTS] + (by_hi[N_EXPERTS:] + _dot_nt(w_hi, h_lo))
    for c in range(lg.shape[1] // LOGIT_TILE):
        lg_ref[c] = lg[:, c * LOGIT_TILE:(c + 1) * LOGIT_TILE]


def _mix_out(mixed, w_out, x, g2_row, mods, layer, stream, wr_t):
    tm = MIX_TILE
    n_in = len(mixed)
    kp = D_MODEL // n_in
    row_blk = lambda width: pl.BlockSpec((tm, width), lambda i: (i, 0))
    return _deferred(
        functools.partial(_mix_out_kernel, n_in),
        grid=(stream.tokens // tm,),
        in_specs=[row_blk(kp)] * n_in + [
            _resident_spec((D_MODEL, D_MODEL)),
            row_blk(D_MODEL),
            _mod_spec(layer, stream, 2, tm),
            pl.BlockSpec((1, D_MODEL), lambda i: (0, 0)),
            _mod_spec(layer, stream, 3, tm),
            _mod_spec(layer, stream, 4, tm),
            pl.BlockSpec((N_EXPERTS, D_MODEL), lambda i: (0, 0))],
        out_specs=[row_blk(D_MODEL), row_blk(D_MODEL),
                   pl.BlockSpec((tm // LOGIT_TILE, N_EXPERTS, LOGIT_TILE), lambda i: (i, 0, 0))],
        out_shape=[jax.ShapeDtypeStruct((stream.tokens, D_MODEL), F32),
                   jax.ShapeDtypeStruct((stream.tokens, D_MODEL), BF16),
                   jax.ShapeDtypeStruct((stream.tokens // LOGIT_TILE, N_EXPERTS, LOGIT_TILE), F32)],
        name=f"mix_out_l{layer}_s{stream.seq}",
    )(*mixed, w_out, x, mods, g2_row, mods, mods, wr_t)


def _sort_desc_lanes(x):
    rows, n = x.shape
    tiles = [x[:, c * LANES:(c + 1) * LANES] for c in range(n // LANES)]
    lane = lax.broadcasted_iota(I32, (rows, LANES), 1)
    k = 2
    while k <= n:
        j = k // 2
        while j >= 1:
            if j < LANES:
                lower = (lane & j) == 0
                for c in range(len(tiles)):
                    t = tiles[c]
                    partner = jnp.where(lower, pltpu.roll(t, LANES - j, axis=1), pltpu.roll(t, j, axis=1))
                    desc = ((lane & k) == 0) if k < LANES else (((c * LANES) & k) == 0)
                    take_max = (lower == desc) if k < LANES else (lower if desc else jnp.logical_not(lower))
                    tiles[c] = jnp.where(take_max, jnp.maximum(t, partner), jnp.minimum(t, partner))
            else:
                jc = j // LANES
                new = list(tiles)
                for c in range(len(tiles)):
                    take_max = ((c & jc) == 0) == (((c * LANES) & k) == 0)
                    new[c] = (jnp.maximum if take_max else jnp.minimum)(tiles[c], tiles[c ^ jc])
                tiles = new
            j //= 2
        k *= 2
    return tiles


def _router_kernel(cap, lg_ref, pos_ref, g_ref):
    x = lg_ref[...]
    n_b, n_e, n_tok = x.shape
    e = jnp.exp(x - jnp.max(x, axis=1, keepdims=True))
    aff = (e / jnp.sum(e, axis=1, keepdims=True)).reshape(n_b * n_e, n_tok)
    srt = _sort_desc_lanes(aff)
    lane_k = (cap - 1) % LANES
    thr = srt[(cap - 1) // LANES][:, lane_k:lane_k + 1]
    gt = aff > thr
    eq = aff == thr
    n_gt = jnp.sum(gt.astype(F32), axis=1, keepdims=True)
    before = (lax.broadcasted_iota(I32, (n_tok, n_tok), 0)
              < lax.broadcasted_iota(I32, (n_tok, n_tok), 1)).astype(BF16)
    eq_rank = _dot(eq.astype(BF16), before)
    sel = gt | (eq & (eq_rank < cap - n_gt))
    slot = _dot(sel.astype(BF16), before).astype(I32)
    pos_ref[...] = jnp.where(sel, slot, -1).reshape(n_b, n_e, n_tok)
    g_ref[...] = jnp.where(sel, aff, 0.0).reshape(n_b, n_e, n_tok)


def _router(logits, stream):
    shape = (stream.n_req, N_EXPERTS, stream.seq)
    spec = pl.BlockSpec(shape, lambda i: (0, 0, 0))
    return _deferred(
        functools.partial(_router_kernel, stream.cap), grid=(1,),
        in_specs=[spec], out_specs=[spec, spec],
        out_shape=[jax.ShapeDtypeStruct(shape, I32), jax.ShapeDtypeStruct(shape, F32)],
        name=f"router_s{stream.seq}",
    )(logits)


def _gather_kernel(cap, seq, h_ref, pos_ref, xs_ref):
    slot = lax.broadcasted_iota(I32, (cap, seq), 0)
    for r in range(pos_ref.shape[0]):
        onehot = jnp.concatenate([(slot == pos_ref[r, e:e + 1, :]).astype(BF16) for e in range(N_EXPERTS)], axis=0)
        xs = _dot(onehot, h_ref[r * seq:(r + 1) * seq, :]).astype(BF16)
        for e in range(N_EXPERTS):
            xs_ref[e, r * cap:(r + 1) * cap, :] = xs[e * cap:(e + 1) * cap]


def _gather(h2, pos, stream):
    cap, seq, per = stream.cap, stream.seq, stream.req_per_step
    return _deferred(
        functools.partial(_gather_kernel, cap, seq),
        grid=(stream.n_req // per,),
        in_specs=[pl.BlockSpec((per * seq, D_MODEL), lambda i: (i, 0)),
                  pl.BlockSpec((per, N_EXPERTS, seq), lambda i: (i, 0, 0))],
        out_specs=pl.BlockSpec((N_EXPERTS, per * cap, D_MODEL), lambda i: (0, i, 0)),
        out_shape=jax.ShapeDtypeStruct((N_EXPERTS, stream.n_req * cap, D_MODEL), BF16),
        name=f"gather_s{seq}",
    )(h2, pos)


def _ffn_kernel(xa_ref, xb_ref, wg_ref, wu_ref, wd_ref, ya_ref, yb_ref, acc):
    j = pl.program_id(1)
    rows_a = xa_ref.shape[1]

    @pl.when(j == 0)
    def _():
        acc[...] = jnp.zeros_like(acc)

    x = jnp.concatenate([xa_ref[0], xb_ref[0]], axis=0)
    a = _dot(x, wg_ref[0, 0].astype(BF16))
    u = _dot(x, wu_ref[0, 0].astype(BF16))
    acc[...] += _dot(((a * jax.nn.sigmoid(a)) * u).astype(BF16), wd_ref[0, 0].astype(BF16))

    @pl.when(j == pl.num_programs(1) - 1)
    def _():
        ya_ref[0] = acc[0:rows_a, :].astype(BF16)
        yb_ref[0] = acc[rows_a:, :].astype(BF16)


def _ffn(xs_a, xs_b, layer, w_gate, w_up, w_down):
    tf = FF_TILE
    rows_a, rows_b = xs_a.shape[1], xs_b.shape[1]
    x_spec = lambda rows: pl.BlockSpec((1, rows, D_MODEL), lambda e, j: (e, 0, 0))
    return pl.pallas_call(
        _ffn_kernel,
        grid=(N_EXPERTS, EXPERT_FF // tf),
        in_specs=[x_spec(rows_a), x_spec(rows_b),
                  pl.BlockSpec((1, 1, D_MODEL, tf), lambda e, j: (layer, e, 0, j)),
                  pl.BlockSpec((1, 1, D_MODEL, tf), lambda e, j: (layer, e, 0, j)),
                  pl.BlockSpec((1, 1, tf, D_MODEL), lambda e, j: (layer, e, j, 0))],
        out_specs=[x_spec(rows_a), x_spec(rows_b)],
        out_shape=[jax.ShapeDtypeStruct(xs_a.shape, BF16), jax.ShapeDtypeStruct(xs_b.shape, BF16)],
        scratch_shapes=[pltpu.VMEM((rows_a + rows_b, D_MODEL), F32)],
        compiler_params=_params(2, FFN_VMEM_LIMIT_BYTES),
        name=f"ffn_l{layer}",
    )(xs_a, xs_b, w_gate, w_up, w_down)


def _scatter_kernel(cap, seq, final, y_ref, pos_ref, g_ref, x_ref, gate_ref, fg_ref, o_ref):
    slot = lax.broadcasted_iota(I32, (cap, seq), 0)
    for r in range(pos_ref.shape[0]):
        onehots, gated = [], []
        for e in range(N_EXPERTS):
            hit = slot == pos_ref[r, e:e + 1, :]
            gate = jnp.sum(jnp.where(hit, g_ref[r, e:e + 1, :], 0.0), axis=1, keepdims=True)
            gated.append((y_ref[e, r * cap:(r + 1) * cap, :].astype(F32) * gate).astype(BF16))
            onehots.append(hit.astype(BF16))
        moe = lax.dot_general(jnp.concatenate(onehots, axis=0), jnp.concatenate(gated, axis=0),
                              (((0,), (0,)), ((), ())), preferred_element_type=F32)
        rows = slice(r * seq, (r + 1) * seq)
        xn = x_ref[rows, :] + gate_ref[0] * moe
        if final:
            xn = xn * lax.rsqrt(jnp.mean(xn * xn, axis=-1, keepdims=True) + EPS) * fg_ref[...]
        o_ref[rows, :] = xn


def _scatter(y, pos, g, xn, mods, layer, stream, final, final_g_row):
    cap, seq, per = stream.cap, stream.seq, stream.req_per_step
    tok_blk = pl.BlockSpec((per * seq, D_MODEL), lambda i: (i, 0))
    sel_blk = pl.BlockSpec((per, N_EXPERTS, seq), lambda i: (i, 0, 0))
    return _deferred(
        functools.partial(_scatter_kernel, cap, seq, final),
        grid=(stream.n_req // per,),
        in_specs=[pl.BlockSpec((N_EXPERTS, per * cap, D_MODEL), lambda i: (0, i, 0)),
                  sel_blk, sel_blk, tok_blk,
                  _mod_spec(layer, stream, 5, per * seq),
                  pl.BlockSpec((1, D_MODEL), lambda i: (0, 0))],
        out_specs=tok_blk,
        out_shape=jax.ShapeDtypeStruct((stream.tokens, D_MODEL), F32),
        name=f"scatter_l{layer}_s{seq}",
    )(y, pos, g, xn, mods, final_g_row)


def _axial_rope_tables(rows, head_dim):
    row = jnp.repeat(jnp.arange(rows, dtype=F32), GRID_W)
    col = jnp.tile(jnp.arange(GRID_W, dtype=F32), rows)
    nf = head_dim // 4
    inv = ROPE_BASE ** (-jnp.arange(nf, dtype=F32) / nf)
    ar = row[:, None] * inv[None]
    ac = col[:, None] * inv[None]
    ang = jnp.concatenate([ar, ar, ac, ac], axis=-1)
    return jnp.cos(ang), jnp.sin(ang)


def _block_diag(w):
    eye = jnp.eye(LRU_BLOCKS, dtype=w.dtype)
    return jnp.einsum('dnkj,nm->dnkmj', w, eye).reshape(2, LRU_WIDTH, LRU_WIDTH)


def kernel(x_prompt, x_sample, cache_win_k, cache_win_v, state_lru, cache_diff_k, cache_diff_v, c, c_ctx, ada_w, ada_b, norm_g, final_g, even_w_in, even_w_out, conv_w, conv_b, lru_wa, lru_ba, lru_wx, lru_bx, lru_lambda, win_sink, odd_w_in, odd_w_out, diff_lq1, diff_lk1, diff_lq2, diff_lk2, diff_subln_g, moe_router, moe_w_gate, moe_w_up, moe_w_down):
    cv_t = jnp.concatenate([c_ctx[None], c, jnp.zeros((COND_ROWS - N_COND, D_MODEL), F32)], axis=0).T
    mods = _adaln(cv_t, ada_w, ada_b).reshape(DEPTH * COND_ROWS, 1, 6 * D_MODEL)

    cos, sin = _axial_rope_tables(DEC_SEQ // GRID_W, WIN_HEAD_DIM)
    rope_win = (jnp.tile(cos, (1, WIN_HEADS)), jnp.tile(sin, (1, WIN_HEADS)))
    rope_diff = (jnp.tile(cos, (1, 2)), jnp.tile(sin, (1, 2)))

    xs = [x_prompt.reshape(CTX.tokens, D_MODEL), x_sample.reshape(LAT.tokens, D_MODEL)]
    final_g_row = final_g.reshape(1, D_MODEL)
    outs = {}

    def both(make):
        return _run_each(make(0, CTX), make(1, LAT))

    for layer in range(DEPTH):
        idx = layer // 2
        even = layer % 2 == 0
        w_in = (even_w_in if even else odd_w_in)[idx].astype(BF16)
        w_out = (even_w_out if even else odd_w_out)[idx].astype(BF16)
        wr_t = moe_router[layer].T
        g1_row = norm_g[layer, 0].reshape(1, D_MODEL)
        g2_row = norm_g[layer, 1].reshape(1, D_MODEL)
        if even:
            wa = _block_diag(lru_wa[idx]).astype(BF16)
            wx = _block_diag(lru_wx[idx]).astype(BF16)
            (proj_c,), (proj_d,) = both(lambda si, st: _proj(xs[si], g1_row, mods, layer, st, w_in))
            projs = (proj_c, proj_d)
            h0s = (jnp.zeros((CTX.n_req, 2, LRU_WIDTH), F32), state_lru[:, idx])
            (y_c, h_fin), (y_d, _) = both(lambda si, st: _lru(
                projs[si], st, conv_w[idx], conv_b[idx].reshape(1, LRU_WIDTH), wa, wx,
                lru_ba[idx], lru_bx[idx], lru_lambda[idx], h0s[si]))
            pack = lambda t: t[:, idx].transpose(0, 2, 1, 3).reshape(DEC_BATCH, PAST_LEN, WIN_KV)
            (o_c,), (o_d,) = _run_each(
                _win_attn(proj_c, CTX, win_sink[idx]),
                _win_attn(proj_d, LAT, win_sink[idx], cache=(pack(cache_win_k), pack(cache_win_v)), rope=rope_win))
            k0 = 2 * LRU_WIDTH + WIN_Q
            to_heads = lambda t: t.reshape(BATCH, SEQ, WIN_KV_HEADS, WIN_HEAD_DIM).transpose(0, 2, 1, 3)[:, None]
            outs["win_k"] = to_heads(proj_c[:, k0:k0 + WIN_KV])
            outs["win_v"] = to_heads(proj_c[:, k0 + WIN_KV:k0 + 2 * WIN_KV])
            outs["lru"] = h_fin[:, None]
            mixed = ([y_c, o_c], [y_d, o_d])
        else:
            lambda_init = 0.8 - 0.6 * math.exp(-0.3 * layer)
            vec = lambda t: t[idx].reshape(1, DIFF_HEAD_DIM)
            args = (vec(diff_lq1), vec(diff_lk1), vec(diff_lq2), vec(diff_lk2), diff_subln_g[idx])
            (q, kh, vh), (proj_d,) = _run_each(_proj_heads(xs[0], g1_row, mods, layer, CTX, w_in),
                                              _proj(xs[1], g1_row, mods, layer, LAT, w_in))
            (o_c,), (o_d,) = _run_each(
                _diff_attn((q, kh, vh), CTX, lambda_init, *args),
                _diff_attn(proj_d, LAT, lambda_init, *args,
                           cache=(cache_diff_k[:, idx:idx + 1], cache_diff_v[:, idx:idx + 1]), rope=rope_diff))
            outs["diff_k"] = kh[:, None]
            outs["diff_v"] = vh[:, None]
            mixed = ([o_c], [o_d])

        mix = both(lambda si, st: _mix_out(mixed[si], w_out, xs[si], g2_row, mods, layer, st, wr_t))
        xns, h2s = (mix[0][0], mix[1][0]), (mix[0][1], mix[1][1])

        def request_major(lg, st):
            per_req = st.seq // LOGIT_TILE
            lg = lg.reshape(st.n_req, per_req, N_EXPERTS, LOGIT_TILE).transpose(0, 2, 1, 3)
            return lg.reshape(st.n_req, N_EXPERTS, st.seq)

        lgs = (request_major(mix[0][2], CTX), request_major(mix[1][2], LAT))
        routes = both(lambda si, st: _router(lgs[si], st))
        (rows_c,), (rows_d,) = both(lambda si, st: _gather(h2s[si], routes[si][0], st))
        ys = _ffn(rows_c, rows_d, layer, moe_w_gate, moe_w_up, moe_w_down)
        final = layer == DEPTH - 1
        (x_c,), (x_d,) = both(lambda si, st: _scatter(ys[si], routes[si][0], routes[si][1], xns[si], mods, layer, st,
                                                      final, final_g_row))
        xs = [x_c, x_d]


    y_prompt = xs[0].reshape(BATCH, SEQ, D_MODEL)
    y_sample = xs[1].reshape(DEC_BATCH, DEC_SEQ, D_MODEL)
    return (y_prompt, y_sample, outs["win_k"], outs["win_v"], outs["lru"], outs["diff_k"], outs["diff_v"])
```

```python
import functools
import math
from typing import Callable, NamedTuple

import jax
import jax.numpy as jnp
import numpy as np
from jax import lax
from jax.experimental import pallas as pl
from jax.experimental.pallas import tpu as pltpu

F32 = jnp.float32
BF16 = jnp.bfloat16
I32 = jnp.int32

D_MODEL = 1024
BATCH = 16
SEQ = 256
DEPTH = 2
DEC_BATCH = 2
DEC_SEQ = 1024
PAST_LEN = 512
GRID_W = 64
LRU_WIDTH = D_MODEL // 2
LRU_BLOCKS = 8
LRU_BLOCK = LRU_WIDTH // LRU_BLOCKS
CONV_W = 4
LRU_C = 8.0
WIN_HEADS = 8
WIN_KV_HEADS = 2
WIN_REP = WIN_HEADS // WIN_KV_HEADS
WIN_HEAD_DIM = 64
WINDOW = 128
WIN_Q = WIN_HEADS * WIN_HEAD_DIM
WIN_KV = WIN_KV_HEADS * WIN_HEAD_DIM
EVEN_IN = 2 * LRU_WIDTH + WIN_Q + 2 * WIN_KV
DIFF_HEADS = 8
DIFF_HEAD_DIM = 64
DIFF_QK = DIFF_HEADS * 2 * DIFF_HEAD_DIM
DIFF_V = DIFF_HEADS * 2 * DIFF_HEAD_DIM
ODD_IN = 2 * DIFF_QK + DIFF_V
N_EXPERTS = 16
EXPERT_FF = 2 * D_MODEL
CAPACITY_FACTOR = 2
ROPE_BASE = 10000.0
EPS = 1e-6
NEG_INF = -1e30

LANES = 128
SUBLANES = 8
VMEM_LIMIT_BYTES = 48 * 1024 * 1024

N_COND = 1 + DEC_BATCH
COND_ROWS = SUBLANES
TOKEN_TILE = 512
MIX_TILE = 1024
ROUTE_ROWS = 1024
LOGIT_TILE = 256
FF_TILE = 1024
FFN_VMEM_LIMIT_BYTES = 56 * 1024 * 1024
LRU_CHUNK = 256
WIN_BLOCKS_PER_ITER = 2
ATT_Q_TILE = 256
ATT_K_TILE = 256


class Stream:
    def __init__(self, n_req, seq, cond0, cond_step):
        self.n_req, self.seq, self.cond0, self.cond_step = n_req, seq, cond0, cond_step
        self.tokens = n_req * seq
        self.cap = CAPACITY_FACTOR * seq // N_EXPERTS
        self.req_per_step = max(1, ROUTE_ROWS // seq)

    def cond_of_row(self, row):
        return self.cond0 + self.cond_step * (row // self.seq)


CTX = Stream(BATCH, SEQ, 0, 0)
LAT = Stream(DEC_BATCH, DEC_SEQ, 1, 1)


def _params(n_axes, vmem_limit_bytes=VMEM_LIMIT_BYTES):
    return pltpu.CompilerParams(dimension_semantics=("arbitrary",) * n_axes,
                                vmem_limit_bytes=vmem_limit_bytes)


class Call(NamedTuple):
    kernel: Callable
    steps: int
    in_specs: tuple
    out_specs: tuple
    out_shapes: tuple
    scratch: tuple
    args: tuple
    name: str


def _deferred(kernel, *, grid, in_specs, out_specs, out_shape, scratch_shapes=(), name):
    (steps,) = grid
    as_tuple = lambda v: tuple(v) if isinstance(v, (list, tuple)) else (v,)
    return lambda *args: Call(kernel, steps, tuple(in_specs), as_tuple(out_specs), as_tuple(out_shape),
                              tuple(scratch_shapes), args, name)


def _run(call):
    return pl.pallas_call(
        call.kernel, grid=(call.steps,), in_specs=list(call.in_specs), out_specs=list(call.out_specs),
        out_shape=list(call.out_shapes), scratch_shapes=list(call.scratch),
        compiler_params=_params(1), name=call.name)(*call.args)


def _run_each(a, b):
    return _run(a), _run(b)


def _resident_spec(shape):
    return pl.BlockSpec(shape, lambda i: (0, 0), pipeline_mode=pl.Buffered(1))


def _mod_spec(layer, stream, k, rows_per_step):
    return pl.BlockSpec(
        (1, 1, D_MODEL),
        lambda i, *_: (layer * COND_ROWS + stream.cond_of_row(i * rows_per_step), 0, k))


def _dot(a, b):
    return jnp.dot(a, b, preferred_element_type=F32)


def _dot_nt(a, b):
    return lax.dot_general(a, b, (((1,), (1,)), ((), ())), preferred_element_type=F32)


def _modnorm(x, g, shift, scale):
    y = x * lax.rsqrt(jnp.mean(x * x, axis=-1, keepdims=True) + EPS)
    return (y * g) * (1.0 + scale) + shift


def _lane_half_masks(shape):
    lane = lax.broadcasted_iota(I32, shape, len(shape) - 1)
    left = (lane & (LANES - 1)) < LANES // 2
    return left, jnp.logical_not(left)


def _rope(x, cos, sin):
    parts = []
    for c in range(x.shape[1] // LANES):
        xs = x[:, c * LANES:(c + 1) * LANES]
        lane = lax.broadcasted_iota(I32, xs.shape, 1)
        first = (lane & 31) < 16
        rot = jnp.where(first, -pltpu.roll(xs, LANES - 16, axis=1), pltpu.roll(xs, 16, axis=1))
        parts.append(xs * cos + rot * sin)
    return parts[0] if len(parts) == 1 else jnp.concatenate(parts, axis=1)


def _adaln_kernel(cv_ref, w_ref, b_ref, o_ref):
    cv = cv_ref[...]
    s = cv * jax.nn.sigmoid(cv)
    w = w_ref[0]
    ridx = lax.broadcasted_iota(I32, (COND_ROWS, w.shape[1]), 0)
    out = jnp.zeros((COND_ROWS, w.shape[1]), F32)
    for r in range(N_COND):
        out = jnp.where(ridx == r, jnp.sum(w * s[:, r:r + 1], axis=0, keepdims=True), out)
    o_ref[0] = out + b_ref[0]


def _adaln(cv_t, ada_w, ada_b):
    tn = 1024
    return pl.pallas_call(
        _adaln_kernel,
        grid=(DEPTH, 6 * D_MODEL // tn),
        in_specs=[pl.BlockSpec((D_MODEL, COND_ROWS), lambda l, j: (0, 0)),
                  pl.BlockSpec((1, D_MODEL, tn), lambda l, j: (l, 0, j)),
                  pl.BlockSpec((1, 1, tn), lambda l, j: (l, 0, j))],
        out_specs=pl.BlockSpec((1, COND_ROWS, tn), lambda l, j: (l, 0, j)),
        out_shape=jax.ShapeDtypeStruct((DEPTH, COND_ROWS, 6 * D_MODEL), F32),
        compiler_params=_params(2),
        name="adaln",
    )(cv_t, ada_w, ada_b.reshape(DEPTH, 1, 6 * D_MODEL))


def _proj_kernel(x_ref, g_ref, sh_ref, sc_ref, w_ref, o_ref):
    h = _modnorm(x_ref[...], g_ref[...], sh_ref[0], sc_ref[0])
    o_ref[...] = _dot(h.astype(BF16), w_ref[...].astype(BF16))


def _proj(x, g_row, mods, layer, stream, w):
    n_out = w.shape[1]
    tm = TOKEN_TILE
    return _deferred(
        _proj_kernel,
        grid=(stream.tokens // tm,),
        in_specs=[pl.BlockSpec((tm, D_MODEL), lambda i: (i, 0)),
                  pl.BlockSpec((1, D_MODEL), lambda i: (0, 0)),
                  _mod_spec(layer, stream, 0, tm),
                  _mod_spec(layer, stream, 1, tm),
                  _resident_spec((D_MODEL, n_out))],
        out_specs=pl.BlockSpec((tm, n_out), lambda i: (i, 0)),
        out_shape=jax.ShapeDtypeStruct((stream.tokens, n_out), F32),
        name=f"proj_l{layer}_s{stream.seq}",
    )(x, g_row, mods, mods, w)


def _proj_heads_kernel(seq, x_ref, g_ref, sh_ref, sc_ref, w_ref, q_ref, k_ref, v_ref):
    h = _modnorm(x_ref[...], g_ref[...], sh_ref[0], sc_ref[0])
    res = _dot(h.astype(BF16), w_ref[...].astype(BF16))
    q_ref[...] = res[:, 0:DIFF_QK]
    for r in range(x_ref.shape[0] // seq):
        rows = slice(r * seq, (r + 1) * seq)
        for hh in range(DIFF_HEADS):
            k_ref[r, hh] = res[rows, DIFF_QK + hh * LANES:DIFF_QK + (hh + 1) * LANES]
            v_ref[r, hh] = res[rows, 2 * DIFF_QK + hh * LANES:2 * DIFF_QK + (hh + 1) * LANES]


def _proj_heads(x, g_row, mods, layer, stream, w):
    tm, seq = TOKEN_TILE, stream.seq
    head_shape = (stream.n_req, DIFF_HEADS, seq, 2 * DIFF_HEAD_DIM)
    head_spec = pl.BlockSpec((tm // seq, DIFF_HEADS, seq, 2 * DIFF_HEAD_DIM), lambda i: (i, 0, 0, 0))
    return _deferred(
        functools.partial(_proj_heads_kernel, seq),
        grid=(stream.tokens // tm,),
        in_specs=[pl.BlockSpec((tm, D_MODEL), lambda i: (i, 0)),
                  pl.BlockSpec((1, D_MODEL), lambda i: (0, 0)),
                  _mod_spec(layer, stream, 0, tm),
                  _mod_spec(layer, stream, 1, tm),
                  _resident_spec((D_MODEL, ODD_IN))],
        out_specs=[pl.BlockSpec((tm, DIFF_QK), lambda i: (i, 0)), head_spec, head_spec],
        out_shape=[jax.ShapeDtypeStruct((stream.tokens, DIFF_QK), F32),
                   jax.ShapeDtypeStruct(head_shape, F32), jax.ShapeDtypeStruct(head_shape, F32)],
        name=f"proj_heads_l{layer}_s{seq}",
    )(x, g_row, mods, mods, w)


def _proj_win_kernel(seq, x_ref, g_ref, sh_ref, sc_ref, w_ref, o_ref, k_ref, v_ref):
    h = _modnorm(x_ref[...], g_ref[...], sh_ref[0], sc_ref[0])
    res = _dot(h.astype(BF16), w_ref[...].astype(BF16))
    o_ref[...] = res
    k0 = 2 * LRU_WIDTH + WIN_Q
    for r in range(x_ref.shape[0] // seq):
        rows = slice(r * seq, (r + 1) * seq)
        for g in range(WIN_KV_HEADS):
            k_ref[r, g] = res[rows, k0 + g * WIN_HEAD_DIM:k0 + (g + 1) * WIN_HEAD_DIM]
            v_ref[r, g] = res[rows, k0 + WIN_KV + g * WIN_HEAD_DIM:k0 + WIN_KV + (g + 1) * WIN_HEAD_DIM]


def _proj_win(x, g_row, mods, layer, stream, w):
    tm, seq = TOKEN_TILE, stream.seq
    head_shape = (stream.n_req, WIN_KV_HEADS, seq, WIN_HEAD_DIM)
    head_spec = pl.BlockSpec((tm // seq, WIN_KV_HEADS, seq, WIN_HEAD_DIM), lambda i: (i, 0, 0, 0))
    return _deferred(
        functools.partial(_proj_win_kernel, seq),
        grid=(stream.tokens // tm,),
        in_specs=[pl.BlockSpec((tm, D_MODEL), lambda i: (i, 0)),
                  pl.BlockSpec((1, D_MODEL), lambda i: (0, 0)),
                  _mod_spec(layer, stream, 0, tm),
                  _mod_spec(layer, stream, 1, tm),
                  _resident_spec((D_MODEL, EVEN_IN))],
        out_specs=[pl.BlockSpec((tm, EVEN_IN), lambda i: (i, 0)), head_spec, head_spec],
        out_shape=[jax.ShapeDtypeStruct((stream.tokens, EVEN_IN), F32),
                   jax.ShapeDtypeStruct(head_shape, F32), jax.ShapeDtypeStruct(head_shape, F32)],
        name=f"proj_win_l{layer}_s{seq}",
    )(x, g_row, mods, mods, w)


def _lru_kernel(seq, xl_ref, gl_ref, cw_ref, cb_ref, wa_ref, wx_ref, ba_ref, bx_ref, lam_ref, h0_ref,
                y_ref, hfin_ref, xe, af, bf, ab, bb):
    width = LRU_WIDTH
    ch = LRU_CHUNK
    seg = seq // SUBLANES
    lead = (CONV_W // 2) * SUBLANES

    def to_segment_major(x):
        return jnp.transpose(x.reshape(SUBLANES, seg, LANES), (1, 0, 2)).reshape(seq, LANES)

    def to_time_major(x):
        return jnp.transpose(x.reshape(seg, SUBLANES, LANES), (1, 0, 2)).reshape(seq, LANES)

    for c in range(width // LANES):
        cols = slice(c * LANES, (c + 1) * LANES)
        xe[lead:lead + seq, cols] = to_segment_major(xl_ref[:, cols])
    sub = lax.broadcasted_iota(I32, (SUBLANES, width), 0)
    for k in range(CONV_W // 2):
        prev = xe[lead + (seg - 2 + k) * SUBLANES:lead + (seg - 1 + k) * SUBLANES, :]
        xe[k * SUBLANES:(k + 1) * SUBLANES, :] = jnp.where(sub >= 1, pltpu.roll(prev, 1, axis=0), 0.0)
    first = xe[lead:lead + SUBLANES, :]
    xe[lead + seq:lead + seq + SUBLANES, :] = jnp.where(sub < SUBLANES - 1,
                                                        pltpu.roll(first, SUBLANES - 1, axis=0), 0.0)

    lam = lam_ref[...]
    z = -lam
    softplus = jnp.maximum(z, 0.0) + jnp.log1p(jnp.exp(-jnp.abs(z)))
    cw = cw_ref[...]
    cb = cb_ref[...]

    def gates_chunk(c, carry):
        r0 = pl.multiple_of(c * ch, ch)
        xc = xe[pl.ds(r0, ch), :] * cw[0:1]
        for j in range(1, CONV_W):
            xc = xc + xe[pl.ds(pl.multiple_of(r0 + j * SUBLANES, SUBLANES), ch), :] * cw[j:j + 1]
        xc = xc + cb
        xcb = xc.astype(BF16)
        for d, (a_s, b_s) in enumerate(((af, bf), (ab, bb))):
            r = jax.nn.sigmoid(_dot(xcb, wa_ref[d]) + ba_ref[d:d + 1])
            ig = jax.nn.sigmoid(_dot(xcb, wx_ref[d]) + bx_ref[d:d + 1])
            log_a = (-LRU_C * r) * softplus[d:d + 1]
            a = jnp.exp(log_a)
            v = 1.0 - a * a
            a_s[pl.ds(r0, ch), :] = a
            b_s[pl.ds(r0, ch), :] = jnp.where(v > 0.0, v * lax.rsqrt(v), 0.0) * (ig * xc)
        return carry

    lax.fori_loop(0, seq // ch, gates_chunk, 0)

    def scan_step(k, carry):
        hf, pf, hb, pb = carry
        rf = pl.multiple_of(k * SUBLANES, SUBLANES)
        rb = pl.multiple_of((seg - 1 - k) * SUBLANES, SUBLANES)
        a = af[pl.ds(rf, SUBLANES), :]
        hf = a * hf + bf[pl.ds(rf, SUBLANES), :]
        pf = a * pf
        bf[pl.ds(rf, SUBLANES), :] = hf
        af[pl.ds(rf, SUBLANES), :] = pf
        a = ab[pl.ds(rb, SUBLANES), :]
        hb = a * hb + bb[pl.ds(rb, SUBLANES), :]
        pb = a * pb
        bb[pl.ds(rb, SUBLANES), :] = hb
        ab[pl.ds(rb, SUBLANES), :] = pb
        return hf, pf, hb, pb

    zeros = jnp.zeros((SUBLANES, width), F32)
    ones = jnp.ones((SUBLANES, width), F32)
    hf, pf, hb, pb = lax.fori_loop(0, seg, scan_step, (zeros, ones, zeros, ones))

    h0 = h0_ref[0]
    state, carry_f = h0[0:1], zeros
    for g in range(SUBLANES):
        carry_f = jnp.where(sub == g, state, carry_f)
        state = pf[g:g + 1] * state + hf[g:g + 1]
    hfin_ref[0, 0:1, :] = state
    state, carry_b = h0[1:2], zeros
    for g in reversed(range(SUBLANES)):
        carry_b = jnp.where(sub == g, state, carry_b)
        state = pb[g:g + 1] * state + hb[g:g + 1]
    hfin_ref[0, 1:2, :] = state

    groups = (ch // SUBLANES, SUBLANES, width)

    def combine_chunk(c, carry):
        r0 = pl.multiple_of(c * ch, ch)
        rows = pl.ds(r0, ch)
        h_fwd = bf[rows, :].reshape(groups) + af[rows, :].reshape(groups) * carry_f
        h_bwd = bb[rows, :].reshape(groups) + ab[rows, :].reshape(groups) * carry_b
        bf[rows, :] = (h_fwd + h_bwd).reshape(ch, width)
        return carry

    lax.fori_loop(0, seq // ch, combine_chunk, 0)

    for c in range(width // LANES):
        cols = slice(c * LANES, (c + 1) * LANES)
        y_ref[:, cols] = (to_time_major(bf[:, cols]) * jax.nn.gelu(gl_ref[:, cols])).astype(BF16)


def _lru(proj, stream, conv_w, conv_b, wa, wx, ba, bx, lam, h0):
    seq, width = stream.seq, LRU_WIDTH
    full2 = lambda b: (0, 0)
    full3 = lambda b: (0, 0, 0)
    return _deferred(
        functools.partial(_lru_kernel, seq),
        grid=(stream.n_req,),
        in_specs=[pl.BlockSpec((seq, width), lambda b: (b, 0)),
                  pl.BlockSpec((seq, width), lambda b: (b, 1)),
                  pl.BlockSpec((CONV_W, width), full2),
                  pl.BlockSpec((1, width), full2),
                  pl.BlockSpec((2, width, width), full3),
                  pl.BlockSpec((2, width, width), full3),
                  pl.BlockSpec((2, width), full2),
                  pl.BlockSpec((2, width), full2),
                  pl.BlockSpec((2, width), full2),
                  pl.BlockSpec((1, 2, width), lambda b: (b, 0, 0))],
        out_specs=[pl.BlockSpec((seq, width), lambda b: (b, 0)),
                   pl.BlockSpec((1, 2, width), lambda b: (b, 0, 0))],
        out_shape=[jax.ShapeDtypeStruct((stream.tokens, width), BF16),
                   jax.ShapeDtypeStruct((stream.n_req, 2, width), F32)],
        scratch_shapes=[pltpu.VMEM((seq + (CONV_W - 1) * SUBLANES, width), F32)] + [pltpu.VMEM((seq, width), F32)] * 4,
        name=f"lru_s{seq}",
    )(proj, proj, conv_w, conv_b, wa, wx, ba, bx, lam, h0)


def _attend(q, chunks, s_ref):
    tile_max = None
    spans = []
    off = 0
    for keys, _, valid in chunks:
        s = _dot_nt(q, keys())
        if valid is not None:
            s = jnp.where(valid, s, NEG_INF)
        n = s.shape[1]
        s_ref[:, off:off + n] = s
        for c in range(n // LANES):
            t = s[:, c * LANES:(c + 1) * LANES]
            tile_max = t if tile_max is None else jnp.maximum(tile_max, t)
        spans.append((off, n))
        off += n
    m = jnp.max(tile_max, axis=-1, keepdims=True)
    acc = None
    for (_, values, _), (o, n) in zip(chunks, spans):
        part = _dot(jnp.exp(s_ref[:, o:o + n] - m).astype(BF16), values())
        acc = part if acc is None else acc + part
    return acc, m


def _split_groups(kk):
    left, right = _lane_half_masks(kk.shape)
    g0_l = jnp.where(left, kk, 0.0)
    g1_r = jnp.where(right, kk, 0.0)
    return ((g0_l, pltpu.roll(g0_l, LANES // 2, axis=1)), (pltpu.roll(g1_r, LANES // 2, axis=1), g1_r))


def _win_ctx_kernel(sink_ref, q_ref, kv_ref, o_ref):
    scale = WIN_HEAD_DIM ** -0.5
    seq = q_ref.shape[0]
    ks = _split_groups(kv_ref[:, 0:LANES])
    vs = _split_groups(kv_ref[:, LANES:2 * LANES])
    top = lax.broadcasted_iota(I32, (2 * seq, 1), 0) < seq
    outs = [None] * (WIN_HEADS // 2)
    for g in range(WIN_KV_HEADS):
        pairs = (2 * g, 2 * g + 1)
        qs = jnp.concatenate([q_ref[:, p * LANES:(p + 1) * LANES] for p in pairs], axis=0)
        qs = (qs * scale).astype(BF16)
        for side in range(2):
            sk = jnp.where(top, sink_ref[2 * pairs[0] + side], sink_ref[2 * pairs[1] + side])
            s = _dot_nt(qs, ks[g][side].astype(BF16))
            m = jnp.maximum(jnp.max(s, axis=-1, keepdims=True), sk)
            e = jnp.exp(s - m)
            den = jnp.sum(e, axis=-1, keepdims=True) + jnp.exp(sk - m)
            o = _dot(e.astype(BF16), vs[g][side].astype(BF16)) * (1.0 / den)
            for k, p in enumerate(pairs):
                part = o[k * seq:(k + 1) * seq]
                outs[p] = part if outs[p] is None else outs[p] + part
    for p in range(WIN_HEADS // 2):
        o_ref[:, p * LANES:(p + 1) * LANES] = outs[p].astype(BF16)


def _win_lat_kernel(sink_ref, q_ref, kv_ref, ck_ref, cv_ref, cos_ref, sin_ref, o_ref,
                    kl_s, vl_s, kc_s, vc_s):
    seq, wn = DEC_SEQ, WINDOW
    scale = WIN_HEAD_DIM ** -0.5
    kr = _rope(kv_ref[:, 0:LANES], cos_ref[...], sin_ref[...])
    for src, dst in ((_split_groups(kr), kl_s), (_split_groups(kv_ref[:, LANES:2 * LANES]), vl_s),
                     (_split_groups(ck_ref[0]), kc_s), (_split_groups(cv_ref[0]), vc_s)):
        for g in range(WIN_KV_HEADS):
            for side in range(2):
                dst[2 * g + side] = src[g][side].astype(BF16)

    top = lax.broadcasted_iota(I32, (2 * wn, 1), 0) < wn

    def q_block(i, carry):
        for sub in range(WIN_BLOCKS_PER_ITER):
            blk = WIN_BLOCKS_PER_ITER * i + sub
            r0 = pl.multiple_of(blk * wn, wn)
            start = pl.multiple_of(jnp.clip((blk - 1) * wn, 0, seq - 3 * wn), wn)
            qr = _rope(q_ref[pl.ds(r0, wn), :], cos_ref[pl.ds(r0, wn), :], sin_ref[pl.ds(r0, wn), :]) * scale
            qpos = r0 + (lax.broadcasted_iota(I32, (2 * wn, 3 * wn), 0) & (wn - 1))
            kpos = start + lax.broadcasted_iota(I32, (2 * wn, 3 * wn), 1)
            valid = jnp.abs(qpos - kpos) <= wn
            outs = [None] * (WIN_HEADS // 2)
            for g in range(WIN_KV_HEADS):
                pairs = (2 * g, 2 * g + 1)
                qs = jnp.concatenate([qr[:, p * LANES:(p + 1) * LANES] for p in pairs], axis=0).astype(BF16)
                for side in range(2):
                    idx = 2 * g + side
                    sk = jnp.where(top, sink_ref[2 * pairs[0] + side], sink_ref[2 * pairs[1] + side])
                    sl = _dot_nt(qs, kl_s[idx, pl.ds(start, 3 * wn), :])
                    sl = jnp.where(valid, sl, NEG_INF)
                    sc = _dot_nt(qs, kc_s[idx])
                    m = jnp.maximum(jnp.maximum(jnp.max(sl, axis=-1, keepdims=True),
                                                jnp.max(sc, axis=-1, keepdims=True)), sk)
                    el = jnp.exp(sl - m)
                    ec = jnp.exp(sc - m)
                    den = (jnp.sum(el, axis=-1, keepdims=True) + jnp.sum(ec, axis=-1, keepdims=True)
                           + jnp.exp(sk - m))
                    o = (_dot(el.astype(BF16), vl_s[idx, pl.ds(start, 3 * wn), :])
                         + _dot(ec.astype(BF16), vc_s[idx])) * (1.0 / den)
                    for k, p in enumerate(pairs):
                        part = o[k * wn:(k + 1) * wn]
                        outs[p] = part if outs[p] is None else outs[p] + part
            for p in range(WIN_HEADS // 2):
                o_ref[pl.ds(r0, wn), p * LANES:(p + 1) * LANES] = outs[p].astype(BF16)
        return carry

    lax.fori_loop(0, seq // (WIN_BLOCKS_PER_ITER * wn), q_block, 0)


def _win_attn(proj, stream, sink, cache=None, rope=None):
    seq = stream.seq
    q_spec = pl.BlockSpec((seq, WIN_Q), lambda b: (b, 2 * LRU_WIDTH // WIN_Q))
    kv_spec = pl.BlockSpec((seq, 2 * WIN_KV), lambda b: (b, (2 * LRU_WIDTH + WIN_Q) // (2 * WIN_KV)))
    sink_spec = pl.BlockSpec(memory_space=pltpu.SMEM)
    out_spec = pl.BlockSpec((seq, WIN_Q), lambda b: (b, 0))
    out_shape = jax.ShapeDtypeStruct((stream.tokens, WIN_Q), BF16)
    if cache is None:
        return _deferred(
            _win_ctx_kernel, grid=(stream.n_req,),
            in_specs=[sink_spec, q_spec, kv_spec], out_specs=out_spec, out_shape=out_shape,
            name="win_attn_ctx",
        )(sink, proj, proj)
    ck, cv = cache
    cos, sin = rope
    cache_spec = pl.BlockSpec((1, PAST_LEN, LANES), lambda b: (b, 0, 0))
    table_spec = pl.BlockSpec((seq, LANES), lambda b: (0, 0))
    return _deferred(
        _win_lat_kernel, grid=(stream.n_req,),
        in_specs=[sink_spec, q_spec, kv_spec, cache_spec, cache_spec, table_spec, table_spec],
        out_specs=out_spec, out_shape=out_shape,
        scratch_shapes=[pltpu.VMEM((4, seq, LANES), BF16), pltpu.VMEM((4, seq, LANES), BF16),
                        pltpu.VMEM((4, PAST_LEN, LANES), BF16), pltpu.VMEM((4, PAST_LEN, LANES), BF16)],
        name="win_attn_lat",
    )(sink, proj, proj, ck, cv, cos, sin)


def _diff_lambda(lq1_ref, lk1_ref, lq2_ref, lk2_ref, lambda_init):
    t1 = jnp.sum(lq1_ref[...] * lk1_ref[...], axis=-1, keepdims=True)
    t2 = jnp.sum(lq2_ref[...] * lk2_ref[...], axis=-1, keepdims=True)
    return jnp.exp(t1) - jnp.exp(t2) + lambda_init


def _subln(o, g_row, lambda_init):
    o = o * lax.rsqrt(jnp.mean(o * o, axis=-1, keepdims=True) + EPS) * g_row
    return o * (1.0 - lambda_init)


def _component_keys(k):
    left, right = _lane_half_masks(k.shape)
    return jnp.where(left, k, 0.0).astype(BF16), jnp.where(right, k, 0.0).astype(BF16)


def _values_with_ones(v):
    return jnp.concatenate([v.astype(BF16), jnp.ones(v.shape, BF16)], axis=1)


def _diff_combine(accs, lam):
    o1, o2 = accs[0][:, 0:LANES], accs[1][:, 0:LANES]
    return o1 * (1.0 / accs[0][:, LANES:]) - o2 * (lam * (1.0 / accs[1][:, LANES:]))


def _diff_ctx_kernel(lambda_init, lq1_ref, lk1_ref, lq2_ref, lk2_ref, sg_ref, q_ref, k_ref, v_ref, o_ref, s_ref):
    scale = DIFF_HEAD_DIM ** -0.5
    tq = ATT_Q_TILE
    lam = _diff_lambda(lq1_ref, lk1_ref, lq2_ref, lk2_ref, lambda_init)
    for h in range(DIFF_HEADS):
        cols = slice(h * LANES, (h + 1) * LANES)
        keys = _component_keys(k_ref[0, h])
        vals = _values_with_ones(v_ref[0, h])
        for qt in range(q_ref.shape[0] // tq):
            rows = slice(qt * tq, (qt + 1) * tq)
            q = (q_ref[rows, cols] * scale).astype(BF16)
            slots = [s_ref.at[(4 * h + 2 * qt + c) % s_ref.shape[0]] for c in range(2)]
            accs = [_attend(q, [(lambda: keys[c], lambda: vals, None)], slots[c])[0] for c in range(2)]
            o = _diff_combine(accs, lam)
            o_ref[rows, cols] = _subln(o, sg_ref[h:h + 1, :], lambda_init).astype(BF16)


def _diff_lat_kernel(lambda_init, lq1_ref, lk1_ref, lq2_ref, lk2_ref, sg_ref, q_ref, k_ref, v_ref,
                     ck_ref, cv_ref, cos_ref, sin_ref, o_ref, kl_s, kc_s, vl_s, vc_s, s_ref):
    seq, tq, tk = DEC_SEQ, ATT_Q_TILE, ATT_K_TILE
    scale = DIFF_HEAD_DIM ** -0.5
    lam = _diff_lambda(lq1_ref, lk1_ref, lq2_ref, lk2_ref, lambda_init)
    for src, dst in ((_rope(k_ref[...], cos_ref[...], sin_ref[...]), kl_s), (ck_ref[0, 0, 0], kc_s)):
        dst[0], dst[1] = _component_keys(src)
    vl_s[...] = _values_with_ones(v_ref[...])
    vc_s[...] = _values_with_ones(cv_ref[0, 0, 0])
    sg = sg_ref[0]

    for qt in range(seq // tq):
        rows = slice(qt * tq, (qt + 1) * tq)
        qr = (_rope(q_ref[rows, :], cos_ref[rows, :], sin_ref[rows, :]) * scale).astype(BF16)
        accs = []
        for comp in range(2):
            chunks = [(lambda j=j: kl_s[comp, j * tk:(j + 1) * tk, :],
                       lambda j=j: vl_s[j * tk:(j + 1) * tk, :], None) for j in range(seq // tk)]
            chunks += [(lambda j=j: kc_s[comp, j * tk:(j + 1) * tk, :],
                        lambda j=j: vc_s[j * tk:(j + 1) * tk, :], None) for j in range(PAST_LEN // tk)]
            accs.append(_attend(qr, chunks, s_ref.at[2 * qt + comp])[0])
        o_ref[rows, :] = _subln(_diff_combine(accs, lam), sg, lambda_init).astype(BF16)


def _diff_attn(proj, stream, lambda_init, lq1, lk1, lq2, lk2, subln_g, cache=None, rope=None):
    seq = stream.seq
    vec_spec = lambda nd: pl.BlockSpec((1, DIFF_HEAD_DIM), lambda *_: (0, 0))
    out_shape = jax.ShapeDtypeStruct((stream.tokens, DIFF_V), BF16)
    if cache is None:
        q, kh, vh = proj
        head_spec = pl.BlockSpec((1, DIFF_HEADS, seq, LANES), lambda b: (b, 0, 0, 0))
        return _deferred(
            functools.partial(_diff_ctx_kernel, lambda_init), grid=(stream.n_req,),
            in_specs=[vec_spec(1)] * 4 + [pl.BlockSpec((DIFF_HEADS, LANES), lambda b: (0, 0)),
                                          pl.BlockSpec((seq, DIFF_QK), lambda b: (b, 0)), head_spec, head_spec],
            out_specs=pl.BlockSpec((seq, DIFF_V), lambda b: (b, 0)), out_shape=out_shape,
            scratch_shapes=[pltpu.VMEM((8, ATT_Q_TILE, seq), F32)],
            name="diff_attn_ctx",
        )(lq1, lk1, lq2, lk2, subln_g, q, kh, vh)
    ck, cv = cache
    cos, sin = rope
    nh = DIFF_HEADS
    blk = lambda c: pl.BlockSpec((seq, LANES), lambda i: (i // nh, c * nh + i % nh))
    cache_spec = pl.BlockSpec((1, 1, 1, PAST_LEN, LANES), lambda i: (i // nh, 0, i % nh, 0, 0))
    table_spec = pl.BlockSpec((seq, LANES), lambda i: (0, 0))
    return _deferred(
        functools.partial(_diff_lat_kernel, lambda_init), grid=(stream.n_req * nh,),
        in_specs=[vec_spec(2)] * 4 + [pl.BlockSpec((1, 1, LANES), lambda i: (i % nh, 0, 0)),
                                      blk(0), blk(1), blk(2), cache_spec, cache_spec, table_spec, table_spec],
        out_specs=pl.BlockSpec((seq, LANES), lambda i: (i // nh, i % nh)), out_shape=out_shape,
        scratch_shapes=[pltpu.VMEM((2, seq, LANES), BF16), pltpu.VMEM((2, PAST_LEN, LANES), BF16),
                        pltpu.VMEM((seq, 2 * LANES), BF16), pltpu.VMEM((PAST_LEN, 2 * LANES), BF16),
                        pltpu.VMEM((2 * seq // ATT_Q_TILE, ATT_Q_TILE, seq + PAST_LEN), F32)],
        name="diff_attn_lat",
    )(lq1, lk1, lq2, lk2, subln_g.reshape(DIFF_HEADS, 1, LANES), proj, proj, proj, ck, cv, cos, sin)


def _mix_out_kernel(n_in, *refs):
    a_refs = refs[:n_in]
    w_ref, x_ref, gate_ref, g2_ref, sh2_ref, sc2_ref, wr_ref, xn_ref, h2_ref, lg_ref = refs[n_in:]
    kp = D_MODEL // n_in
    wr = wr_ref[...]
    w_hi = wr.astype(BF16)
    w_lo = (wr - w_hi.astype(F32)).astype(BF16)
    acc = None
    for k, a_ref in enumerate(a_refs):
        part = _dot(a_ref[...], w_ref[k * kp:(k + 1) * kp, :].astype(BF16))
        acc = part if acc is None else acc + part
    xn = x_ref[...] + gate_ref[0] * acc
    xn_ref[...] = xn
    h2 = _modnorm(xn, g2_ref[...], sh2_ref[0], sc2_ref[0])
    h_hi = h2.astype(BF16)
    h2_ref[...] = h_hi
    h_lo = (h2 - h_hi.astype(F32)).astype(BF16)
    by_hi = _dot_nt(jnp.concatenate([w_hi, w_lo], axis=0), h_hi)
    lg = by_hi[0:N_EXPERTS] + (by_hi[N_EXPERTS:] + _dot_nt(w_hi, h_lo))
    for c in range(lg.shape[1] // LOGIT_TILE):
        lg_ref[c] = lg[:, c * LOGIT_TILE:(c + 1) * LOGIT_TILE]


def _mix_out(mixed, w_out, x, g2_row, mods, layer, stream, wr_t):
    tm = MIX_TILE
    n_in = len(mixed)
    kp = D_MODEL // n_in
    row_blk = lambda width: pl.BlockSpec((tm, width), lambda i: (i, 0))
    return _deferred(
        functools.partial(_mix_out_kernel, n_in),
        grid=(stream.tokens // tm,),
        in_specs=[row_blk(kp)] * n_in + [
            _resident_spec((D_MODEL, D_MODEL)),
            row_blk(D_MODEL),
            _mod_spec(layer, stream, 2, tm),
            pl.BlockSpec((1, D_MODEL), lambda i: (0, 0)),
            _mod_spec(layer, stream, 3, tm),
            _mod_spec(layer, stream, 4, tm),
            pl.BlockSpec((N_EXPERTS, D_MODEL), lambda i: (0, 0))],
        out_specs=[row_blk(D_MODEL), row_blk(D_MODEL),
                   pl.BlockSpec((tm // LOGIT_TILE, N_EXPERTS, LOGIT_TILE), lambda i: (i, 0, 0))],
        out_shape=[jax.ShapeDtypeStruct((stream.tokens, D_MODEL), F32),
                   jax.ShapeDtypeStruct((stream.tokens, D_MODEL), BF16),
                   jax.ShapeDtypeStruct((stream.tokens // LOGIT_TILE, N_EXPERTS, LOGIT_TILE), F32)],
        name=f"mix_out_l{layer}_s{stream.seq}",
    )(*mixed, w_out, x, mods, g2_row, mods, mods, wr_t)


def _sort_desc_lanes(x):
    rows, n = x.shape
    tiles = [x[:, c * LANES:(c + 1) * LANES] for c in range(n // LANES)]
    lane = lax.broadcasted_iota(I32, (rows, LANES), 1)
    k = 2
    while k <= n:
        j = k // 2
        while j >= 1:
            if j < LANES:
                lower = (lane & j) == 0
                for c in range(len(tiles)):
                    t = tiles[c]
                    partner = jnp.where(lower, pltpu.roll(t, LANES - j, axis=1), pltpu.roll(t, j, axis=1))
                    desc = ((lane & k) == 0) if k < LANES else (((c * LANES) & k) == 0)
                    take_max = (lower == desc) if k < LANES else (lower if desc else jnp.logical_not(lower))
                    tiles[c] = jnp.where(take_max, jnp.maximum(t, partner), jnp.minimum(t, partner))
            else:
                jc = j // LANES
                new = list(tiles)
                for c in range(len(tiles)):
                    take_max = ((c & jc) == 0) == (((c * LANES) & k) == 0)
                    new[c] = (jnp.maximum if take_max else jnp.minimum)(tiles[c], tiles[c ^ jc])
                tiles = new
            j //= 2
        k *= 2
    return tiles


def _router_kernel(cap, lg_ref, pos_ref, g_ref):
    x = lg_ref[...]
    n_b, n_e, n_tok = x.shape
    e = jnp.exp(x - jnp.max(x, axis=1, keepdims=True))
    aff = (e / jnp.sum(e, axis=1, keepdims=True)).reshape(n_b * n_e, n_tok)
    srt = _sort_desc_lanes(aff)
    lane_k = (cap - 1) % LANES
    thr = srt[(cap - 1) // LANES][:, lane_k:lane_k + 1]
    gt = aff > thr
    eq = aff == thr
    n_gt = jnp.sum(gt.astype(F32), axis=1, keepdims=True)
    before = (lax.broadcasted_iota(I32, (n_tok, n_tok), 0)
              < lax.broadcasted_iota(I32, (n_tok, n_tok), 1)).astype(BF16)
    eq_rank = _dot(eq.astype(BF16), before)
    sel = gt | (eq & (eq_rank < cap - n_gt))
    slot = _dot(sel.astype(BF16), before).astype(I32)
    pos_ref[...] = jnp.where(sel, slot, -1).reshape(n_b, n_e, n_tok)
    g_ref[...] = jnp.where(sel, aff, 0.0).reshape(n_b, n_e, n_tok)


def _router(logits, stream):
    shape = (stream.n_req, N_EXPERTS, stream.seq)
    spec = pl.BlockSpec(shape, lambda i: (0, 0, 0))
    return _deferred(
        functools.partial(_router_kernel, stream.cap), grid=(1,),
        in_specs=[spec], out_specs=[spec, spec],
        out_shape=[jax.ShapeDtypeStruct(shape, I32), jax.ShapeDtypeStruct(shape, F32)],
        name=f"router_s{stream.seq}",
    )(logits)


def _gather_kernel(cap, seq, h_ref, pos_ref, xs_ref):
    slot = lax.broadcasted_iota(I32, (cap, seq), 0)
    for r in range(pos_ref.shape[0]):
        onehot = jnp.concatenate([(slot == pos_ref[r, e:e + 1, :]).astype(BF16) for e in range(N_EXPERTS)], axis=0)
        xs = _dot(onehot, h_ref[r * seq:(r + 1) * seq, :]).astype(BF16)
        for e in range(N_EXPERTS):
            xs_ref[e, r * cap:(r + 1) * cap, :] = xs[e * cap:(e + 1) * cap]


def _gather(h2, pos, stream):
    cap, seq, per = stream.cap, stream.seq, stream.req_per_step
    return _deferred(
        functools.partial(_gather_kernel, cap, seq),
        grid=(stream.n_req // per,),
        in_specs=[pl.BlockSpec((per * seq, D_MODEL), lambda i: (i, 0)),
                  pl.BlockSpec((per, N_EXPERTS, seq), lambda i: (i, 0, 0))],
        out_specs=pl.BlockSpec((N_EXPERTS, per * cap, D_MODEL), lambda i: (0, i, 0)),
        out_shape=jax.ShapeDtypeStruct((N_EXPERTS, stream.n_req * cap, D_MODEL), BF16),
        name=f"gather_s{seq}",
    )(h2, pos)


def _ffn_kernel(xa_ref, xb_ref, wg_ref, wu_ref, wd_ref, ya_ref, yb_ref, acc):
    j = pl.program_id(1)
    rows_a = xa_ref.shape[1]

    @pl.when(j == 0)
    def _():
        acc[...] = jnp.zeros_like(acc)

    x = jnp.concatenate([xa_ref[0], xb_ref[0]], axis=0)
    a = _dot(x, wg_ref[0, 0].astype(BF16))
    u = _dot(x, wu_ref[0, 0].astype(BF16))
    acc[...] += _dot(((a * jax.nn.sigmoid(a)) * u).astype(BF16), wd_ref[0, 0].astype(BF16))

    @pl.when(j == pl.num_programs(1) - 1)
    def _():
        ya_ref[0] = acc[0:rows_a, :].astype(BF16)
        yb_ref[0] = acc[rows_a:, :].astype(BF16)


def _ffn(xs_a, xs_b, layer, w_gate, w_up, w_down):
    tf = FF_TILE
    rows_a, rows_b = xs_a.shape[1], xs_b.shape[1]
    x_spec = lambda rows: pl.BlockSpec((1, rows, D_MODEL), lambda e, j: (e, 0, 0))
    return pl.pallas_call(
        _ffn_kernel,
        grid=(N_EXPERTS, EXPERT_FF // tf),
        in_specs=[x_spec(rows_a), x_spec(rows_b),
                  pl.BlockSpec((1, 1, D_MODEL, tf), lambda e, j: (layer, e, 0, j)),
                  pl.BlockSpec((1, 1, D_MODEL, tf), lambda e, j: (layer, e, 0, j)),
                  pl.BlockSpec((1, 1, tf, D_MODEL), lambda e, j: (layer, e, j, 0))],
        out_specs=[x_spec(rows_a), x_spec(rows_b)],
        out_shape=[jax.ShapeDtypeStruct(xs_a.shape, BF16), jax.ShapeDtypeStruct(xs_b.shape, BF16)],
        scratch_shapes=[pltpu.VMEM((rows_a + rows_b, D_MODEL), F32)],
        compiler_params=_params(2, FFN_VMEM_LIMIT_BYTES),
        name=f"ffn_l{layer}",
    )(xs_a, xs_b, w_gate, w_up, w_down)


def _scatter_kernel(cap, seq, final, y_ref, pos_ref, g_ref, x_ref, gate_ref, fg_ref, o_ref):
    slot = lax.broadcasted_iota(I32, (cap, seq), 0)
    for r in range(pos_ref.shape[0]):
        onehots, gated = [], []
        for e in range(N_EXPERTS):
            hit = slot == pos_ref[r, e:e + 1, :]
            gate = jnp.sum(jnp.where(hit, g_ref[r, e:e + 1, :], 0.0), axis=1, keepdims=True)
            gated.append((y_ref[e, r * cap:(r + 1) * cap, :].astype(F32) * gate).astype(BF16))
            onehots.append(hit.astype(BF16))
        moe = lax.dot_general(jnp.concatenate(onehots, axis=0), jnp.concatenate(gated, axis=0),
                              (((0,), (0,)), ((), ())), preferred_element_type=F32)
        rows = slice(r * seq, (r + 1) * seq)
        xn = x_ref[rows, :] + gate_ref[0] * moe
        if final:
            xn = xn * lax.rsqrt(jnp.mean(xn * xn, axis=-1, keepdims=True) + EPS) * fg_ref[...]
        o_ref[rows, :] = xn


def _scatter(y, pos, g, xn, mods, layer, stream, final, final_g_row):
    cap, seq, per = stream.cap, stream.seq, stream.req_per_step
    tok_blk = pl.BlockSpec((per * seq, D_MODEL), lambda i: (i, 0))
    sel_blk = pl.BlockSpec((per, N_EXPERTS, seq), lambda i: (i, 0, 0))
    return _deferred(
        functools.partial(_scatter_kernel, cap, seq, final),
        grid=(stream.n_req // per,),
        in_specs=[pl.BlockSpec((N_EXPERTS, per * cap, D_MODEL), lambda i: (0, i, 0)),
                  sel_blk, sel_blk, tok_blk,
                  _mod_spec(layer, stream, 5, per * seq),
                  pl.BlockSpec((1, D_MODEL), lambda i: (0, 0))],
        out_specs=tok_blk,
        out_shape=jax.ShapeDtypeStruct((stream.tokens, D_MODEL), F32),
        name=f"scatter_l{layer}_s{seq}",
    )(y, pos, g, xn, mods, final_g_row)


def _axial_rope_tables(rows, head_dim):
    f32 = np.float32
    row = np.repeat(np.arange(rows, dtype=f32), GRID_W)
    col = np.tile(np.arange(GRID_W, dtype=f32), rows)
    nf = head_dim // 4
    inv = (f32(ROPE_BASE) ** (-np.arange(nf, dtype=f32) / f32(nf))).astype(f32)
    ar = row[:, None] * inv[None]
    ac = col[:, None] * inv[None]
    ang = np.concatenate([ar, ar, ac, ac], axis=-1).astype(f32)
    return np.cos(ang).astype(f32), np.sin(ang).astype(f32)


def _block_diag(w):
    eye = jnp.eye(LRU_BLOCKS, dtype=w.dtype)
    return jnp.einsum('dnkj,nm->dnkmj', w, eye).reshape(2, LRU_WIDTH, LRU_WIDTH)


def kernel(x_prompt, x_sample, cache_win_k, cache_win_v, state_lru, cache_diff_k, cache_diff_v, c, c_ctx, ada_w, ada_b, norm_g, final_g, even_w_in, even_w_out, conv_w, conv_b, lru_wa, lru_ba, lru_wx, lru_bx, lru_lambda, win_sink, odd_w_in, odd_w_out, diff_lq1, diff_lk1, diff_lq2, diff_lk2, diff_subln_g, moe_router, moe_w_gate, moe_w_up, moe_w_down):
    cv_t = jnp.concatenate([c_ctx[None], c, jnp.zeros((COND_ROWS - N_COND, D_MODEL), F32)], axis=0).T
    mods = _adaln(cv_t, ada_w, ada_b).reshape(DEPTH * COND_ROWS, 1, 6 * D_MODEL)

    cos, sin = _axial_rope_tables(DEC_SEQ // GRID_W, WIN_HEAD_DIM)
    rope_win = rope_diff = (jnp.asarray(np.tile(cos, (1, 2))), jnp.asarray(np.tile(sin, (1, 2))))

    xs = [x_prompt.reshape(CTX.tokens, D_MODEL), x_sample.reshape(LAT.tokens, D_MODEL)]
    final_g_row = final_g.reshape(1, D_MODEL)
    outs = {}

    def both(make):
        return _run_each(make(0, CTX), make(1, LAT))

    for layer in range(DEPTH):
        idx = layer // 2
        even = layer % 2 == 0
        w_in = (even_w_in if even else odd_w_in)[idx]
        w_out = (even_w_out if even else odd_w_out)[idx]
        wr_t = moe_router[layer].T
        g1_row = norm_g[layer, 0].reshape(1, D_MODEL)
        g2_row = norm_g[layer, 1].reshape(1, D_MODEL)
        if even:
            wa = _block_diag(lru_wa[idx]).astype(BF16)
            wx = _block_diag(lru_wx[idx]).astype(BF16)
            (proj_c, win_k, win_v), (proj_d,) = _run_each(_proj_win(xs[0], g1_row, mods, layer, CTX, w_in),
                                                         _proj(xs[1], g1_row, mods, layer, LAT, w_in))
            projs = (proj_c, proj_d)
            h0s = (jnp.zeros((CTX.n_req, 2, LRU_WIDTH), F32), state_lru[:, idx])
            (y_c, h_fin), (y_d, _) = both(lambda si, st: _lru(
                projs[si], st, conv_w[idx], conv_b[idx].reshape(1, LRU_WIDTH), wa, wx,
                lru_ba[idx], lru_bx[idx], lru_lambda[idx], h0s[si]))
            pack = lambda t: t[:, idx].transpose(0, 2, 1, 3).reshape(DEC_BATCH, PAST_LEN, WIN_KV)
            (o_c,), (o_d,) = _run_each(
                _win_attn(proj_c, CTX, win_sink[idx]),
                _win_attn(proj_d, LAT, win_sink[idx], cache=(pack(cache_win_k), pack(cache_win_v)), rope=rope_win))
            outs["win_k"] = win_k[:, None]
            outs["win_v"] = win_v[:, None]
            outs["lru"] = h_fin[:, None]
            mixed = ([y_c, o_c], [y_d, o_d])
        else:
            lambda_init = 0.8 - 0.6 * math.exp(-0.3 * layer)
            vec = lambda t: t[idx].reshape(1, DIFF_HEAD_DIM)
            args = (vec(diff_lq1), vec(diff_lk1), vec(diff_lq2), vec(diff_lk2), diff_subln_g[idx])
            (q, kh, vh), (proj_d,) = _run_each(_proj_heads(xs[0], g1_row, mods, layer, CTX, w_in),
                                              _proj(xs[1], g1_row, mods, layer, LAT, w_in))
            (o_c,), (o_d,) = _run_each(
                _diff_attn((q, kh, vh), CTX, lambda_init, *args),
                _diff_attn(proj_d, LAT, lambda_init, *args,
                           cache=(cache_diff_k[:, idx:idx + 1], cache_diff_v[:, idx:idx + 1]), rope=rope_diff))
            outs["diff_k"] = kh[:, None]
            outs["diff_v"] = vh[:, None]
            mixed = ([o_c], [o_d])

        mix = both(lambda si, st: _mix_out(mixed[si], w_out, xs[si], g2_row, mods, layer, st, wr_t))
        xns, h2s = (mix[0][0], mix[1][0]), (mix[0][1], mix[1][1])

        def request_major(lg, st):
            per_req = st.seq // LOGIT_TILE
            lg = lg.reshape(st.n_req, per_req, N_EXPERTS, LOGIT_TILE).transpose(0, 2, 1, 3)
            return lg.reshape(st.n_req, N_EXPERTS, st.seq)

        lgs = (request_major(mix[0][2], CTX), request_major(mix[1][2], LAT))
        routes = both(lambda si, st: _router(lgs[si], st))
        (rows_c,), (rows_d,) = both(lambda si, st: _gather(h2s[si], routes[si][0], st))
        ys = _ffn(rows_c, rows_d, layer, moe_w_gate, moe_w_up, moe_w_down)
        final = layer == DEPTH - 1
        (x_c,), (x_d,) = both(lambda si, st: _scatter(ys[si], routes[si][0], routes[si][1], xns[si], mods, layer, st,
                                                      final, final_g_row))
        xs = [x_c, x_d]


    y_prompt = xs[0].reshape(BATCH, SEQ, D_MODEL)
    y_sample = xs[1].reshape(DEC_BATCH, DEC_SEQ, D_MODEL)
    return (y_prompt, y_sample, outs["win_k"], outs["win_v"], outs["lru"], outs["diff_k"], outs["diff_v"])
```

```python
import functools
import math
from typing import Callable, NamedTuple

import jax
import jax.numpy as jnp
import numpy as np
from jax import lax
from jax.experimental import pallas as pl
from jax.experimental.pallas import tpu as pltpu

F32 = jnp.float32
BF16 = jnp.bfloat16
I32 = jnp.int32

D_MODEL = 1024
BATCH = 16
SEQ = 256
DEPTH = 2
DEC_BATCH = 2
DEC_SEQ = 1024
PAST_LEN = 512
GRID_W = 64
LRU_WIDTH = D_MODEL // 2
LRU_BLOCKS = 8
LRU_BLOCK = LRU_WIDTH // LRU_BLOCKS
CONV_W = 4
LRU_C = 8.0
WIN_HEADS = 8
WIN_KV_HEADS = 2
WIN_REP = WIN_HEADS // WIN_KV_HEADS
WIN_HEAD_DIM = 64
WINDOW = 128
WIN_Q = WIN_HEADS * WIN_HEAD_DIM
WIN_KV = WIN_KV_HEADS * WIN_HEAD_DIM
EVEN_IN = 2 * LRU_WIDTH + WIN_Q + 2 * WIN_KV
DIFF_HEADS = 8
DIFF_HEAD_DIM = 64
DIFF_QK = DIFF_HEADS * 2 * DIFF_HEAD_DIM
DIFF_V = DIFF_HEADS * 2 * DIFF_HEAD_DIM
ODD_IN = 2 * DIFF_QK + DIFF_V
N_EXPERTS = 16
EXPERT_FF = 2 * D_MODEL
CAPACITY_FACTOR = 2
ROPE_BASE = 10000.0
EPS = 1e-6
NEG_INF = -1e30

LANES = 128
SUBLANES = 8
VMEM_LIMIT_BYTES = 48 * 1024 * 1024

N_COND = 1 + DEC_BATCH
COND_ROWS = SUBLANES
ADALN_TILE = 512
TOKEN_TILE = 512
MIX_TILE = 1024
ROUTE_ROWS = 1024
LOGIT_TILE = 256
FF_TILE = 1024
FFN_VMEM_LIMIT_BYTES = 56 * 1024 * 1024
LRU_CHUNK = 256
WIN_BLOCKS_PER_ITER = 2
ATT_Q_TILE = 256
ATT_K_TILE = 256


class Stream:
    def __init__(self, n_req, seq, cond0, cond_step):
        self.n_req, self.seq, self.cond0, self.cond_step = n_req, seq, cond0, cond_step
        self.tokens = n_req * seq
        self.cap = CAPACITY_FACTOR * seq // N_EXPERTS
        self.req_per_step = max(1, ROUTE_ROWS // seq)

    def cond_of_row(self, row):
        return self.cond0 + self.cond_step * (row // self.seq)


CTX = Stream(BATCH, SEQ, 0, 0)
LAT = Stream(DEC_BATCH, DEC_SEQ, 1, 1)


def _params(n_axes, vmem_limit_bytes=VMEM_LIMIT_BYTES):
    return pltpu.CompilerParams(dimension_semantics=("arbitrary",) * n_axes,
                                vmem_limit_bytes=vmem_limit_bytes)


class Call(NamedTuple):
    kernel: Callable
    steps: int
    in_specs: tuple
    out_specs: tuple
    out_shapes: tuple
    scratch: tuple
    args: tuple
    name: str


def _deferred(kernel, *, grid, in_specs, out_specs, out_shape, scratch_shapes=(), name):
    (steps,) = grid
    as_tuple = lambda v: tuple(v) if isinstance(v, (list, tuple)) else (v,)
    return lambda *args: Call(kernel, steps, tuple(in_specs), as_tuple(out_specs), as_tuple(out_shape),
                              tuple(scratch_shapes), args, name)


def _run(call):
    return pl.pallas_call(
        call.kernel, grid=(call.steps,), in_specs=list(call.in_specs), out_specs=list(call.out_specs),
        out_shape=list(call.out_shapes), scratch_shapes=list(call.scratch),
        compiler_params=_params(1), name=call.name)(*call.args)


def _run_each(a, b):
    return _run(a), _run(b)


def _resident_spec(shape):
    return pl.BlockSpec(shape, lambda i: (0, 0), pipeline_mode=pl.Buffered(1))


def _mod_spec(layer, stream, k, rows_per_step):
    return pl.BlockSpec(
        (1, 1, D_MODEL),
        lambda i, *_: (layer * COND_ROWS + stream.cond_of_row(i * rows_per_step), 0, k))


def _dot(a, b):
    return jnp.dot(a, b, preferred_element_type=F32)


def _dot_nt(a, b):
    return lax.dot_general(a, b, (((1,), (1,)), ((), ())), preferred_element_type=F32)


def _modnorm(x, g, shift, scale):
    y = x * lax.rsqrt(jnp.mean(x * x, axis=-1, keepdims=True) + EPS)
    return (y * g) * (1.0 + scale) + shift


def _lane_half_masks(shape):
    lane = lax.broadcasted_iota(I32, shape, len(shape) - 1)
    left = (lane & (LANES - 1)) < LANES // 2
    return left, jnp.logical_not(left)


def _rope(x, cos, sin):
    parts = []
    for c in range(x.shape[1] // LANES):
        xs = x[:, c * LANES:(c + 1) * LANES]
        lane = lax.broadcasted_iota(I32, xs.shape, 1)
        first = (lane & 31) < 16
        rot = jnp.where(first, -pltpu.roll(xs, LANES - 16, axis=1), pltpu.roll(xs, 16, axis=1))
        parts.append(xs * cos + rot * sin)
    return parts[0] if len(parts) == 1 else jnp.concatenate(parts, axis=1)


def _adaln_kernel(cv_ref, w_ref, b_ref, o_ref):
    cv = cv_ref[...]
    s = cv * jax.nn.sigmoid(cv)
    w = w_ref[0]
    ridx = lax.broadcasted_iota(I32, (COND_ROWS, w.shape[1]), 0)
    out = jnp.zeros((COND_ROWS, w.shape[1]), F32)
    for r in range(N_COND):
        out = jnp.where(ridx == r, jnp.sum(w * s[:, r:r + 1], axis=0, keepdims=True), out)
    o_ref[0] = out + b_ref[0]


def _adaln(cv_t, ada_w, ada_b):
    tn = ADALN_TILE
    return pl.pallas_call(
        _adaln_kernel,
        grid=(DEPTH, 6 * D_MODEL // tn),
        in_specs=[pl.BlockSpec((D_MODEL, COND_ROWS), lambda l, j: (0, 0)),
                  pl.BlockSpec((1, D_MODEL, tn), lambda l, j: (l, 0, j)),
                  pl.BlockSpec((1, 1, tn), lambda l, j: (l, 0, j))],
        out_specs=pl.BlockSpec((1, COND_ROWS, tn), lambda l, j: (l, 0, j)),
        out_shape=jax.ShapeDtypeStruct((DEPTH, COND_ROWS, 6 * D_MODEL), F32),
        compiler_params=_params(2),
        name="adaln",
    )(cv_t, ada_w, ada_b.reshape(DEPTH, 1, 6 * D_MODEL))


def _proj_kernel(x_ref, g_ref, sh_ref, sc_ref, w_ref, o_ref):
    h = _modnorm(x_ref[...], g_ref[...], sh_ref[0], sc_ref[0])
    o_ref[...] = _dot(h.astype(BF16), w_ref[...].astype(BF16))


def _proj(x, g_row, mods, layer, stream, w):
    n_out = w.shape[1]
    tm = TOKEN_TILE
    return _deferred(
        _proj_kernel,
        grid=(stream.tokens // tm,),
        in_specs=[pl.BlockSpec((tm, D_MODEL), lambda i: (i, 0)),
                  pl.BlockSpec((1, D_MODEL), lambda i: (0, 0)),
                  _mod_spec(layer, stream, 0, tm),
                  _mod_spec(layer, stream, 1, tm),
                  _resident_spec((D_MODEL, n_out))],
        out_specs=pl.BlockSpec((tm, n_out), lambda i: (i, 0)),
        out_shape=jax.ShapeDtypeStruct((stream.tokens, n_out), F32),
        name=f"proj_l{layer}_s{stream.seq}",
    )(x, g_row, mods, mods, w)


def _proj_heads_kernel(seq, x_ref, g_ref, sh_ref, sc_ref, w_ref, q_ref, k_ref, v_ref):
    h = _modnorm(x_ref[...], g_ref[...], sh_ref[0], sc_ref[0])
    res = _dot(h.astype(BF16), w_ref[...].astype(BF16))
    q_ref[...] = res[:, 0:DIFF_QK].astype(BF16)
    for r in range(x_ref.shape[0] // seq):
        rows = slice(r * seq, (r + 1) * seq)
        for hh in range(DIFF_HEADS):
            k_ref[r, hh] = res[rows, DIFF_QK + hh * LANES:DIFF_QK + (hh + 1) * LANES]
            v_ref[r, hh] = res[rows, 2 * DIFF_QK + hh * LANES:2 * DIFF_QK + (hh + 1) * LANES]


def _proj_heads(x, g_row, mods, layer, stream, w):
    tm, seq = TOKEN_TILE, stream.seq
    head_shape = (stream.n_req, DIFF_HEADS, seq, 2 * DIFF_HEAD_DIM)
    head_spec = pl.BlockSpec((tm // seq, DIFF_HEADS, seq, 2 * DIFF_HEAD_DIM), lambda i: (i, 0, 0, 0))
    return _deferred(
        functools.partial(_proj_heads_kernel, seq),
        grid=(stream.tokens // tm,),
        in_specs=[pl.BlockSpec((tm, D_MODEL), lambda i: (i, 0)),
                  pl.BlockSpec((1, D_MODEL), lambda i: (0, 0)),
                  _mod_spec(layer, stream, 0, tm),
                  _mod_spec(layer, stream, 1, tm),
                  _resident_spec((D_MODEL, ODD_IN))],
        out_specs=[pl.BlockSpec((tm, DIFF_QK), lambda i: (i, 0)), head_spec, head_spec],
        out_shape=[jax.ShapeDtypeStruct((stream.tokens, DIFF_QK), BF16),
                   jax.ShapeDtypeStruct(head_shape, F32), jax.ShapeDtypeStruct(head_shape, F32)],
        name=f"proj_heads_l{layer}_s{seq}",
    )(x, g_row, mods, mods, w)


def _proj_win_kernel(seq, x_ref, g_ref, sh_ref, sc_ref, w_ref, o_ref, k_ref, v_ref):
    h = _modnorm(x_ref[...], g_ref[...], sh_ref[0], sc_ref[0])
    res = _dot(h.astype(BF16), w_ref[...].astype(BF16))
    o_ref[...] = res
    k0 = 2 * LRU_WIDTH + WIN_Q
    for r in range(x_ref.shape[0] // seq):
        rows = slice(r * seq, (r + 1) * seq)
        for g in range(WIN_KV_HEADS):
            k_ref[r, g] = res[rows, k0 + g * WIN_HEAD_DIM:k0 + (g + 1) * WIN_HEAD_DIM]
            v_ref[r, g] = res[rows, k0 + WIN_KV + g * WIN_HEAD_DIM:k0 + WIN_KV + (g + 1) * WIN_HEAD_DIM]


def _proj_win(x, g_row, mods, layer, stream, w):
    tm, seq = TOKEN_TILE, stream.seq
    head_shape = (stream.n_req, WIN_KV_HEADS, seq, WIN_HEAD_DIM)
    head_spec = pl.BlockSpec((tm // seq, WIN_KV_HEADS, seq, WIN_HEAD_DIM), lambda i: (i, 0, 0, 0))
    return _deferred(
        functools.partial(_proj_win_kernel, seq),
        grid=(stream.tokens // tm,),
        in_specs=[pl.BlockSpec((tm, D_MODEL), lambda i: (i, 0)),
                  pl.BlockSpec((1, D_MODEL), lambda i: (0, 0)),
                  _mod_spec(layer, stream, 0, tm),
                  _mod_spec(layer, stream, 1, tm),
                  _resident_spec((D_MODEL, EVEN_IN))],
        out_specs=[pl.BlockSpec((tm, EVEN_IN), lambda i: (i, 0)), head_spec, head_spec],
        out_shape=[jax.ShapeDtypeStruct((stream.tokens, EVEN_IN), F32),
                   jax.ShapeDtypeStruct(head_shape, F32), jax.ShapeDtypeStruct(head_shape, F32)],
        name=f"proj_win_l{layer}_s{seq}",
    )(x, g_row, mods, mods, w)


def _lru_kernel(seq, xl_ref, gl_ref, cw_ref, cb_ref, wa_ref, wx_ref, ba_ref, bx_ref, lam_ref, h0_ref,
                y_ref, hfin_ref, xe, af, bf, ab, bb):
    width = LRU_WIDTH
    ch = LRU_CHUNK
    seg = seq // SUBLANES
    lead = (CONV_W // 2) * SUBLANES

    def to_segment_major(x):
        return jnp.transpose(x.reshape(SUBLANES, seg, LANES), (1, 0, 2)).reshape(seq, LANES)

    def to_time_major(x):
        return jnp.transpose(x.reshape(seg, SUBLANES, LANES), (1, 0, 2)).reshape(seq, LANES)

    for c in range(width // LANES):
        cols = slice(c * LANES, (c + 1) * LANES)
        xe[lead:lead + seq, cols] = to_segment_major(xl_ref[:, cols])
    sub = lax.broadcasted_iota(I32, (SUBLANES, width), 0)
    for k in range(CONV_W // 2):
        prev = xe[lead + (seg - 2 + k) * SUBLANES:lead + (seg - 1 + k) * SUBLANES, :]
        xe[k * SUBLANES:(k + 1) * SUBLANES, :] = jnp.where(sub >= 1, pltpu.roll(prev, 1, axis=0), 0.0)
    first = xe[lead:lead + SUBLANES, :]
    xe[lead + seq:lead + seq + SUBLANES, :] = jnp.where(sub < SUBLANES - 1,
                                                        pltpu.roll(first, SUBLANES - 1, axis=0), 0.0)

    lam = lam_ref[...]
    z = -lam
    softplus = jnp.maximum(z, 0.0) + jnp.log1p(jnp.exp(-jnp.abs(z)))
    cw = cw_ref[...]
    cb = cb_ref[...]

    def gates_chunk(c, carry):
        r0 = pl.multiple_of(c * ch, ch)
        xc = xe[pl.ds(r0, ch), :] * cw[0:1]
        for j in range(1, CONV_W):
            xc = xc + xe[pl.ds(pl.multiple_of(r0 + j * SUBLANES, SUBLANES), ch), :] * cw[j:j + 1]
        xc = xc + cb
        xcb = xc.astype(BF16)
        for d, (a_s, b_s) in enumerate(((af, bf), (ab, bb))):
            r = jax.nn.sigmoid(_dot(xcb, wa_ref[d]) + ba_ref[d:d + 1])
            ig = jax.nn.sigmoid(_dot(xcb, wx_ref[d]) + bx_ref[d:d + 1])
            log_a = (-LRU_C * r) * softplus[d:d + 1]
            a = jnp.exp(log_a)
            v = 1.0 - a * a
            a_s[pl.ds(r0, ch), :] = a
            b_s[pl.ds(r0, ch), :] = jnp.where(v > 0.0, v * lax.rsqrt(v), 0.0) * (ig * xc)
        return carry

    lax.fori_loop(0, seq // ch, gates_chunk, 0)

    def scan_step(k, carry):
        hf, pf, hb, pb = carry
        rf = pl.multiple_of(k * SUBLANES, SUBLANES)
        rb = pl.multiple_of((seg - 1 - k) * SUBLANES, SUBLANES)
        a = af[pl.ds(rf, SUBLANES), :]
        hf = a * hf + bf[pl.ds(rf, SUBLANES), :]
        pf = a * pf
        bf[pl.ds(rf, SUBLANES), :] = hf
        af[pl.ds(rf, SUBLANES), :] = pf
        a = ab[pl.ds(rb, SUBLANES), :]
        hb = a * hb + bb[pl.ds(rb, SUBLANES), :]
        pb = a * pb
        bb[pl.ds(rb, SUBLANES), :] = hb
        ab[pl.ds(rb, SUBLANES), :] = pb
        return hf, pf, hb, pb

    zeros = jnp.zeros((SUBLANES, width), F32)
    ones = jnp.ones((SUBLANES, width), F32)
    hf, pf, hb, pb = lax.fori_loop(0, seg, scan_step, (zeros, ones, zeros, ones))

    h0 = h0_ref[0]
    state, carry_f = h0[0:1], zeros
    for g in range(SUBLANES):
        carry_f = jnp.where(sub == g, state, carry_f)
        state = pf[g:g + 1] * state + hf[g:g + 1]
    hfin_ref[0, 0:1, :] = state
    state, carry_b = h0[1:2], zeros
    for g in reversed(range(SUBLANES)):
        carry_b = jnp.where(sub == g, state, carry_b)
        state = pb[g:g + 1] * state + hb[g:g + 1]
    hfin_ref[0, 1:2, :] = state

    groups = (ch // SUBLANES, SUBLANES, width)

    def combine_chunk(c, carry):
        r0 = pl.multiple_of(c * ch, ch)
        rows = pl.ds(r0, ch)
        h_fwd = bf[rows, :].reshape(groups) + af[rows, :].reshape(groups) * carry_f
        h_bwd = bb[rows, :].reshape(groups) + ab[rows, :].reshape(groups) * carry_b
        bf[rows, :] = (h_fwd + h_bwd).reshape(ch, width)
        return carry

    lax.fori_loop(0, seq // ch, combine_chunk, 0)

    for c in range(width // LANES):
        cols = slice(c * LANES, (c + 1) * LANES)
        y_ref[:, cols] = (to_time_major(bf[:, cols]) * jax.nn.gelu(gl_ref[:, cols])).astype(BF16)


def _lru(proj, stream, conv_w, conv_b, wa, wx, ba, bx, lam, h0):
    seq, width = stream.seq, LRU_WIDTH
    full2 = lambda b: (0, 0)
    full3 = lambda b: (0, 0, 0)
    return _deferred(
        functools.partial(_lru_kernel, seq),
        grid=(stream.n_req,),
        in_specs=[pl.BlockSpec((seq, width), lambda b: (b, 0)),
                  pl.BlockSpec((seq, width), lambda b: (b, 1)),
                  pl.BlockSpec((CONV_W, width), full2),
                  pl.BlockSpec((1, width), full2),
                  pl.BlockSpec((2, width, width), full3),
                  pl.BlockSpec((2, width, width), full3),
                  pl.BlockSpec((2, width), full2),
                  pl.BlockSpec((2, width), full2),
                  pl.BlockSpec((2, width), full2),
                  pl.BlockSpec((1, 2, width), lambda b: (b, 0, 0))],
        out_specs=[pl.BlockSpec((seq, width), lambda b: (b, 0)),
                   pl.BlockSpec((1, 2, width), lambda b: (b, 0, 0))],
        out_shape=[jax.ShapeDtypeStruct((stream.tokens, width), BF16),
                   jax.ShapeDtypeStruct((stream.n_req, 2, width), F32)],
        scratch_shapes=[pltpu.VMEM((seq + (CONV_W - 1) * SUBLANES, width), F32)] + [pltpu.VMEM((seq, width), F32)] * 4,
        name=f"lru_s{seq}",
    )(proj, proj, conv_w, conv_b, wa, wx, ba, bx, lam, h0)


def _attend(q, chunks, s_ref):
    tile_max = None
    spans = []
    off = 0
    for keys, _, valid in chunks:
        s = _dot_nt(q, keys())
        if valid is not None:
            s = jnp.where(valid, s, NEG_INF)
        n = s.shape[1]
        s_ref[:, off:off + n] = s
        for c in range(n // LANES):
            t = s[:, c * LANES:(c + 1) * LANES]
            tile_max = t if tile_max is None else jnp.maximum(tile_max, t)
        spans.append((off, n))
        off += n
    m = jnp.max(tile_max, axis=-1, keepdims=True)
    acc = None
    for (_, values, _), (o, n) in zip(chunks, spans):
        part = _dot(jnp.exp(s_ref[:, o:o + n] - m).astype(BF16), values())
        acc = part if acc is None else acc + part
    return acc, m


def _split_groups(kk):
    left, right = _lane_half_masks(kk.shape)
    g0_l = jnp.where(left, kk, 0.0)
    g1_r = jnp.where(right, kk, 0.0)
    return ((g0_l, pltpu.roll(g0_l, LANES // 2, axis=1)), (pltpu.roll(g1_r, LANES // 2, axis=1), g1_r))


def _win_ctx_kernel(sink_ref, q_ref, kv_ref, o_ref):
    scale = WIN_HEAD_DIM ** -0.5
    seq = q_ref.shape[0]
    ks = _split_groups(kv_ref[:, 0:LANES])
    vs = _split_groups(kv_ref[:, LANES:2 * LANES])
    top = lax.broadcasted_iota(I32, (2 * seq, 1), 0) < seq
    outs = [None] * (WIN_HEADS // 2)
    for g in range(WIN_KV_HEADS):
        pairs = (2 * g, 2 * g + 1)
        qs = jnp.concatenate([q_ref[:, p * LANES:(p + 1) * LANES] for p in pairs], axis=0)
        qs = (qs * scale).astype(BF16)
        for side in range(2):
            sk = jnp.where(top, sink_ref[2 * pairs[0] + side], sink_ref[2 * pairs[1] + side])
            s = _dot_nt(qs, ks[g][side].astype(BF16))
            m = jnp.maximum(jnp.max(s, axis=-1, keepdims=True), sk)
            e = jnp.exp(s - m)
            den = jnp.sum(e, axis=-1, keepdims=True) + jnp.exp(sk - m)
            o = _dot(e.astype(BF16), vs[g][side].astype(BF16)) * (1.0 / den)
            for k, p in enumerate(pairs):
                part = o[k * seq:(k + 1) * seq]
                outs[p] = part if outs[p] is None else outs[p] + part
    for p in range(WIN_HEADS // 2):
        o_ref[:, p * LANES:(p + 1) * LANES] = outs[p].astype(BF16)


def _win_lat_kernel(sink_ref, q_ref, kv_ref, ck_ref, cv_ref, cos_ref, sin_ref, o_ref,
                    kl_s, vl_s, kc_s, vc_s):
    seq, wn = DEC_SEQ, WINDOW
    scale = WIN_HEAD_DIM ** -0.5
    kr = _rope(kv_ref[:, 0:LANES], cos_ref[...], sin_ref[...])
    for src, dst in ((_split_groups(kr), kl_s), (_split_groups(kv_ref[:, LANES:2 * LANES]), vl_s),
                     (_split_groups(ck_ref[0]), kc_s), (_split_groups(cv_ref[0]), vc_s)):
        for g in range(WIN_KV_HEADS):
            for side in range(2):
                dst[2 * g + side] = src[g][side].astype(BF16)

    top = lax.broadcasted_iota(I32, (2 * wn, 1), 0) < wn

    def q_block(i, carry):
        for sub in range(WIN_BLOCKS_PER_ITER):
            blk = WIN_BLOCKS_PER_ITER * i + sub
            r0 = pl.multiple_of(blk * wn, wn)
            start = pl.multiple_of(jnp.clip((blk - 1) * wn, 0, seq - 3 * wn), wn)
            qr = _rope(q_ref[pl.ds(r0, wn), :], cos_ref[pl.ds(r0, wn), :], sin_ref[pl.ds(r0, wn), :]) * scale
            qpos = r0 + (lax.broadcasted_iota(I32, (2 * wn, 3 * wn), 0) & (wn - 1))
            kpos = start + lax.broadcasted_iota(I32, (2 * wn, 3 * wn), 1)
            valid = jnp.abs(qpos - kpos) <= wn
            outs = [None] * (WIN_HEADS // 2)
            for g in range(WIN_KV_HEADS):
                pairs = (2 * g, 2 * g + 1)
                qs = jnp.concatenate([qr[:, p * LANES:(p + 1) * LANES] for p in pairs], axis=0).astype(BF16)
                for side in range(2):
                    idx = 2 * g + side
                    sk = jnp.where(top, sink_ref[2 * pairs[0] + side], sink_ref[2 * pairs[1] + side])
                    sl = _dot_nt(qs, kl_s[idx, pl.ds(start, 3 * wn), :])
                    sl = jnp.where(valid, sl, NEG_INF)
                    sc = _dot_nt(qs, kc_s[idx])
                    m = jnp.maximum(jnp.maximum(jnp.max(sl, axis=-1, keepdims=True),
                                                jnp.max(sc, axis=-1, keepdims=True)), sk)
                    el = jnp.exp(sl - m)
                    ec = jnp.exp(sc - m)
                    den = (jnp.sum(el, axis=-1, keepdims=True) + jnp.sum(ec, axis=-1, keepdims=True)
                           + jnp.exp(sk - m))
                    o = (_dot(el.astype(BF16), vl_s[idx, pl.ds(start, 3 * wn), :])
                         + _dot(ec.astype(BF16), vc_s[idx])) * (1.0 / den)
                    for k, p in enumerate(pairs):
                        part = o[k * wn:(k + 1) * wn]
                        outs[p] = part if outs[p] is None else outs[p] + part
            for p in range(WIN_HEADS // 2):
                o_ref[pl.ds(r0, wn), p * LANES:(p + 1) * LANES] = outs[p].astype(BF16)
        return carry

    lax.fori_loop(0, seq // (WIN_BLOCKS_PER_ITER * wn), q_block, 0)


def _win_attn(proj, stream, sink, cache=None, rope=None):
    seq = stream.seq
    q_spec = pl.BlockSpec((seq, WIN_Q), lambda b: (b, 2 * LRU_WIDTH // WIN_Q))
    kv_spec = pl.BlockSpec((seq, 2 * WIN_KV), lambda b: (b, (2 * LRU_WIDTH + WIN_Q) // (2 * WIN_KV)))
    sink_spec = pl.BlockSpec(memory_space=pltpu.SMEM)
    out_spec = pl.BlockSpec((seq, WIN_Q), lambda b: (b, 0))
    out_shape = jax.ShapeDtypeStruct((stream.tokens, WIN_Q), BF16)
    if cache is None:
        return _deferred(
            _win_ctx_kernel, grid=(stream.n_req,),
            in_specs=[sink_spec, q_spec, kv_spec], out_specs=out_spec, out_shape=out_shape,
            name="win_attn_ctx",
        )(sink, proj, proj)
    ck, cv = cache
    cos, sin = rope
    cache_spec = pl.BlockSpec((1, PAST_LEN, LANES), lambda b: (b, 0, 0))
    table_spec = pl.BlockSpec((seq, LANES), lambda b: (0, 0))
    return _deferred(
        _win_lat_kernel, grid=(stream.n_req,),
        in_specs=[sink_spec, q_spec, kv_spec, cache_spec, cache_spec, table_spec, table_spec],
        out_specs=out_spec, out_shape=out_shape,
        scratch_shapes=[pltpu.VMEM((4, seq, LANES), BF16), pltpu.VMEM((4, seq, LANES), BF16),
                        pltpu.VMEM((4, PAST_LEN, LANES), BF16), pltpu.VMEM((4, PAST_LEN, LANES), BF16)],
        name="win_attn_lat",
    )(sink, proj, proj, ck, cv, cos, sin)


def _diff_lambda(lq1_ref, lk1_ref, lq2_ref, lk2_ref, lambda_init):
    t1 = jnp.sum(lq1_ref[...] * lk1_ref[...], axis=-1, keepdims=True)
    t2 = jnp.sum(lq2_ref[...] * lk2_ref[...], axis=-1, keepdims=True)
    return jnp.exp(t1) - jnp.exp(t2) + lambda_init


def _subln(o, g_row, lambda_init):
    o = o * lax.rsqrt(jnp.mean(o * o, axis=-1, keepdims=True) + EPS) * g_row
    return o * (1.0 - lambda_init)


def _component_keys(k):
    left, right = _lane_half_masks(k.shape)
    return jnp.where(left, k, 0.0).astype(BF16), jnp.where(right, k, 0.0).astype(BF16)


def _values_with_ones(v):
    return jnp.concatenate([v.astype(BF16), jnp.ones(v.shape, BF16)], axis=1)


def _diff_combine(accs, lam):
    o1, o2 = accs[0][:, 0:LANES], accs[1][:, 0:LANES]
    return o1 * (1.0 / accs[0][:, LANES:]) - o2 * (lam * (1.0 / accs[1][:, LANES:]))


def _diff_ctx_kernel(lambda_init, lq1_ref, lk1_ref, lq2_ref, lk2_ref, sg_ref, q_ref, k_ref, v_ref, o_ref, s_ref):
    scale = DIFF_HEAD_DIM ** -0.5
    tq = ATT_Q_TILE
    lam = _diff_lambda(lq1_ref, lk1_ref, lq2_ref, lk2_ref, lambda_init)
    for h in range(DIFF_HEADS):
        cols = slice(h * LANES, (h + 1) * LANES)
        keys = _component_keys(k_ref[0, h])
        vals = _values_with_ones(v_ref[0, h])
        for qt in range(q_ref.shape[0] // tq):
            rows = slice(qt * tq, (qt + 1) * tq)
            q = (q_ref[rows, cols] * scale).astype(BF16)
            slots = [s_ref.at[(4 * h + 2 * qt + c) % s_ref.shape[0]] for c in range(2)]
            accs = [_attend(q, [(lambda: keys[c], lambda: vals, None)], slots[c])[0] for c in range(2)]
            o = _diff_combine(accs, lam)
            o_ref[rows, cols] = _subln(o, sg_ref[h:h + 1, :], lambda_init).astype(BF16)


def _diff_lat_kernel(lambda_init, lq1_ref, lk1_ref, lq2_ref, lk2_ref, sg_ref, q_ref, k_ref, v_ref,
                     ck_ref, cv_ref, cos_ref, sin_ref, o_ref, kl_s, kc_s, vl_s, vc_s, s_ref):
    seq, tq, tk = DEC_SEQ, ATT_Q_TILE, ATT_K_TILE
    scale = DIFF_HEAD_DIM ** -0.5
    lam = _diff_lambda(lq1_ref, lk1_ref, lq2_ref, lk2_ref, lambda_init)
    for src, dst in ((_rope(k_ref[...], cos_ref[...], sin_ref[...]), kl_s), (ck_ref[0, 0, 0], kc_s)):
        dst[0], dst[1] = _component_keys(src)
    vl_s[...] = _values_with_ones(v_ref[...])
    vc_s[...] = _values_with_ones(cv_ref[0, 0, 0])
    sg = sg_ref[0]

    for qt in range(seq // tq):
        rows = slice(qt * tq, (qt + 1) * tq)
        qr = (_rope(q_ref[rows, :], cos_ref[rows, :], sin_ref[rows, :]) * scale).astype(BF16)
        accs = []
        for comp in range(2):
            chunks = [(lambda j=j: kl_s[comp, j * tk:(j + 1) * tk, :],
                       lambda j=j: vl_s[j * tk:(j + 1) * tk, :], None) for j in range(seq // tk)]
            chunks += [(lambda j=j: kc_s[comp, j * tk:(j + 1) * tk, :],
                        lambda j=j: vc_s[j * tk:(j + 1) * tk, :], None) for j in range(PAST_LEN // tk)]
            accs.append(_attend(qr, chunks, s_ref.at[2 * qt + comp])[0])
        o_ref[rows, :] = _subln(_diff_combine(accs, lam), sg, lambda_init).astype(BF16)


def _diff_attn(proj, stream, lambda_init, lq1, lk1, lq2, lk2, subln_g, cache=None, rope=None):
    seq = stream.seq
    vec_spec = lambda nd: pl.BlockSpec((1, DIFF_HEAD_DIM), lambda *_: (0, 0))
    out_shape = jax.ShapeDtypeStruct((stream.tokens, DIFF_V), BF16)
    if cache is None:
        q, kh, vh = proj
        head_spec = pl.BlockSpec((1, DIFF_HEADS, seq, LANES), lambda b: (b, 0, 0, 0))
        return _deferred(
            functools.partial(_diff_ctx_kernel, lambda_init), grid=(stream.n_req,),
            in_specs=[vec_spec(1)] * 4 + [pl.BlockSpec((DIFF_HEADS, LANES), lambda b: (0, 0)),
                                          pl.BlockSpec((seq, DIFF_QK), lambda b: (b, 0)), head_spec, head_spec],
            out_specs=pl.BlockSpec((seq, DIFF_V), lambda b: (b, 0)), out_shape=out_shape,
            scratch_shapes=[pltpu.VMEM((8, ATT_Q_TILE, seq), F32)],
            name="diff_attn_ctx",
        )(lq1, lk1, lq2, lk2, subln_g, q, kh, vh)
    ck, cv = cache
    cos, sin = rope
    nh = DIFF_HEADS
    blk = lambda c: pl.BlockSpec((seq, LANES), lambda i: (i // nh, c * nh + i % nh))
    cache_spec = pl.BlockSpec((1, 1, 1, PAST_LEN, LANES), lambda i: (i // nh, 0, i % nh, 0, 0))
    table_spec = pl.BlockSpec((seq, LANES), lambda i: (0, 0))
    return _deferred(
        functools.partial(_diff_lat_kernel, lambda_init), grid=(stream.n_req * nh,),
        in_specs=[vec_spec(2)] * 4 + [pl.BlockSpec((1, 1, LANES), lambda i: (i % nh, 0, 0)),
                                      blk(0), blk(1), blk(2), cache_spec, cache_spec, table_spec, table_spec],
        out_specs=pl.BlockSpec((seq, LANES), lambda i: (i // nh, i % nh)), out_shape=out_shape,
        scratch_shapes=[pltpu.VMEM((2, seq, LANES), BF16), pltpu.VMEM((2, PAST_LEN, LANES), BF16),
                        pltpu.VMEM((seq, 2 * LANES), BF16), pltpu.VMEM((PAST_LEN, 2 * LANES), BF16),
                        pltpu.VMEM((2 * seq // ATT_Q_TILE, ATT_Q_TILE, seq + PAST_LEN), F32)],
        name="diff_attn_lat",
    )(lq1, lk1, lq2, lk2, subln_g.reshape(DIFF_HEADS, 1, LANES), proj, proj, proj, ck, cv, cos, sin)


def _mix_out_kernel(n_in, *refs):
    a_refs = refs[:n_in]
    w_ref, x_ref, gate_ref, g2_ref, sh2_ref, sc2_ref, wr_ref, xn_ref, h2_ref, lg_ref = refs[n_in:]
    kp = D_MODEL // n_in
    wr = wr_ref[...]
    w_hi = wr.astype(BF16)
    w_lo = (wr - w_hi.astype(F32)).astype(BF16)
    acc = None
    for k, a_ref in enumerate(a_refs):
        part = _dot(a_ref[...], w_ref[k * kp:(k + 1) * kp, :].astype(BF16))
        acc = part if acc is None else acc + part
    xn = x_ref[...] + gate_ref[0] * acc
    xn_ref[...] = xn
    h2 = _modnorm(xn, g2_ref[...], sh2_ref[0], sc2_ref[0])
    h_hi = h2.astype(BF16)
    h2_ref[...] = h_hi
    h_lo = (h2 - h_hi.astype(F32)).astype(BF16)
    by_hi = _dot_nt(jnp.concatenate([w_hi, w_lo], axis=0), h_hi)
    lg = by_hi[0:N_EXPERTS] + (by_hi[N_EXPERTS:] + _dot_nt(w_hi, h_lo))
    for c in range(lg.shape[1] // LOGIT_TILE):
        lg_ref[c] = lg[:, c * LOGIT_TILE:(c + 1) * LOGIT_TILE]


def _mix_out(mixed, w_out, x, g2_row, mods, layer, stream, wr_t):
    tm = MIX_TILE
    n_in = len(mixed)
    kp = D_MODEL // n_in
    row_blk = lambda width: pl.BlockSpec((tm, width), lambda i: (i, 0))
    return _deferred(
        functools.partial(_mix_out_kernel, n_in),
        grid=(stream.tokens // tm,),
        in_specs=[row_blk(kp)] * n_in + [
            _resident_spec((D_MODEL, D_MODEL)),
            row_blk(D_MODEL),
            _mod_spec(layer, stream, 2, tm),
            pl.BlockSpec((1, D_MODEL), lambda i: (0, 0)),
            _mod_spec(layer, stream, 3, tm),
            _mod_spec(layer, stream, 4, tm),
            pl.BlockSpec((N_EXPERTS, D_MODEL), lambda i: (0, 0))],
        out_specs=[row_blk(D_MODEL), row_blk(D_MODEL),
                   pl.BlockSpec((tm // LOGIT_TILE, N_EXPERTS, LOGIT_TILE), lambda i: (i, 0, 0))],
        out_shape=[jax.ShapeDtypeStruct((stream.tokens, D_MODEL), F32),
                   jax.ShapeDtypeStruct((stream.tokens, D_MODEL), BF16),
                   jax.ShapeDtypeStruct((stream.tokens // LOGIT_TILE, N_EXPERTS, LOGIT_TILE), F32)],
        name=f"mix_out_l{layer}_s{stream.seq}",
    )(*mixed, w_out, x, mods, g2_row, mods, mods, wr_t)


def _sort_desc_lanes(x):
    rows, n = x.shape
    tiles = [x[:, c * LANES:(c + 1) * LANES] for c in range(n // LANES)]
    lane = lax.broadcasted_iota(I32, (rows, LANES), 1)
    k = 2
    while k <= n:
        j = k // 2
        while j >= 1:
            if j < LANES:
                lower = (lane & j) == 0
                for c in range(len(tiles)):
                    t = tiles[c]
                    partner = jnp.where(lower, pltpu.roll(t, LANES - j, axis=1), pltpu.roll(t, j, axis=1))
                    desc = ((lane & k) == 0) if k < LANES else (((c * LANES) & k) == 0)
                    take_max = (lower == desc) if k < LANES else (lower if desc else jnp.logical_not(lower))
                    tiles[c] = jnp.where(take_max, jnp.maximum(t, partner), jnp.minimum(t, partner))
            else:
                jc = j // LANES
                new = list(tiles)
                for c in range(len(tiles)):
                    take_max = ((c & jc) == 0) == (((c * LANES) & k) == 0)
                    new[c] = (jnp.maximum if take_max else jnp.minimum)(tiles[c], tiles[c ^ jc])
                tiles = new
            j //= 2
        k *= 2
    return tiles


def _router_kernel(cap, lg_ref, pos_ref, g_ref):
    x = lg_ref[...]
    n_b, n_e, n_tok = x.shape
    e = jnp.exp(x - jnp.max(x, axis=1, keepdims=True))
    aff = (e / jnp.sum(e, axis=1, keepdims=True)).reshape(n_b * n_e, n_tok)
    srt = _sort_desc_lanes(aff)
    lane_k = (cap - 1) % LANES
    thr = srt[(cap - 1) // LANES][:, lane_k:lane_k + 1]
    gt = aff > thr
    eq = aff == thr
    n_gt = jnp.sum(gt.astype(F32), axis=1, keepdims=True)
    before = (lax.broadcasted_iota(I32, (n_tok, n_tok), 0)
              < lax.broadcasted_iota(I32, (n_tok, n_tok), 1)).astype(BF16)
    eq_rank = _dot(eq.astype(BF16), before)
    sel = gt | (eq & (eq_rank < cap - n_gt))
    slot = _dot(sel.astype(BF16), before).astype(I32)
    pos_ref[...] = jnp.where(sel, slot, -1).reshape(n_b, n_e, n_tok)
    g_ref[...] = jnp.where(sel, aff, 0.0).reshape(n_b, n_e, n_tok)


def _router(logits, stream):
    shape = (stream.n_req, N_EXPERTS, stream.seq)
    spec = pl.BlockSpec(shape, lambda i: (0, 0, 0))
    return _deferred(
        functools.partial(_router_kernel, stream.cap), grid=(1,),
        in_specs=[spec], out_specs=[spec, spec],
        out_shape=[jax.ShapeDtypeStruct(shape, I32), jax.ShapeDtypeStruct(shape, F32)],
        name=f"router_s{stream.seq}",
    )(logits)


def _gather_kernel(cap, seq, h_ref, pos_ref, xs_ref):
    slot = lax.broadcasted_iota(I32, (cap, seq), 0)
    for r in range(pos_ref.shape[0]):
        onehot = jnp.concatenate([(slot == pos_ref[r, e:e + 1, :]).astype(BF16) for e in range(N_EXPERTS)], axis=0)
        xs = _dot(onehot, h_ref[r * seq:(r + 1) * seq, :]).astype(BF16)
        for e in range(N_EXPERTS):
            xs_ref[e, r * cap:(r + 1) * cap, :] = xs[e * cap:(e + 1) * cap]


def _gather(h2, pos, stream):
    cap, seq, per = stream.cap, stream.seq, stream.req_per_step
    return _deferred(
        functools.partial(_gather_kernel, cap, seq),
        grid=(stream.n_req // per,),
        in_specs=[pl.BlockSpec((per * seq, D_MODEL), lambda i: (i, 0)),
                  pl.BlockSpec((per, N_EXPERTS, seq), lambda i: (i, 0, 0))],
        out_specs=pl.BlockSpec((N_EXPERTS, per * cap, D_MODEL), lambda i: (0, i, 0)),
        out_shape=jax.ShapeDtypeStruct((N_EXPERTS, stream.n_req * cap, D_MODEL), BF16),
        name=f"gather_s{seq}",
    )(h2, pos)


def _ffn_kernel(xa_ref, xb_ref, wg_ref, wu_ref, wd_ref, ya_ref, yb_ref, acc):
    j = pl.program_id(1)
    rows_a = xa_ref.shape[1]

    @pl.when(j == 0)
    def _():
        acc[...] = jnp.zeros_like(acc)

    x = jnp.concatenate([xa_ref[0], xb_ref[0]], axis=0)
    a = _dot(x, wg_ref[0, 0].astype(BF16))
    u = _dot(x, wu_ref[0, 0].astype(BF16))
    acc[...] += _dot(((a * jax.nn.sigmoid(a)) * u).astype(BF16), wd_ref[0, 0].astype(BF16))

    @pl.when(j == pl.num_programs(1) - 1)
    def _():
        ya_ref[0] = acc[0:rows_a, :].astype(BF16)
        yb_ref[0] = acc[rows_a:, :].astype(BF16)


def _ffn(xs_a, xs_b, layer, w_gate, w_up, w_down):
    tf = FF_TILE
    rows_a, rows_b = xs_a.shape[1], xs_b.shape[1]
    x_spec = lambda rows: pl.BlockSpec((1, rows, D_MODEL), lambda e, j: (e, 0, 0))
    return pl.pallas_call(
        _ffn_kernel,
        grid=(N_EXPERTS, EXPERT_FF // tf),
        in_specs=[x_spec(rows_a), x_spec(rows_b),
                  pl.BlockSpec((1, 1, D_MODEL, tf), lambda e, j: (layer, e, 0, j)),
                  pl.BlockSpec((1, 1, D_MODEL, tf), lambda e, j: (layer, e, 0, j)),
                  pl.BlockSpec((1, 1, tf, D_MODEL), lambda e, j: (layer, e, j, 0))],
        out_specs=[x_spec(rows_a), x_spec(rows_b)],
        out_shape=[jax.ShapeDtypeStruct(xs_a.shape, BF16), jax.ShapeDtypeStruct(xs_b.shape, BF16)],
        scratch_shapes=[pltpu.VMEM((rows_a + rows_b, D_MODEL), F32)],
        compiler_params=_params(2, FFN_VMEM_LIMIT_BYTES),
        name=f"ffn_l{layer}",
    )(xs_a, xs_b, w_gate, w_up, w_down)


def _scatter_kernel(cap, seq, final, y_ref, pos_ref, g_ref, x_ref, gate_ref, fg_ref, o_ref):
    slot = lax.broadcasted_iota(I32, (cap, seq), 0)
    for r in range(pos_ref.shape[0]):
        onehots, gated = [], []
        for e in range(N_EXPERTS):
            hit = slot == pos_ref[r, e:e + 1, :]
            gate = jnp.sum(jnp.where(hit, g_ref[r, e:e + 1, :], 0.0), axis=1, keepdims=True)
            gated.append((y_ref[e, r * cap:(r + 1) * cap, :].astype(F32) * gate).astype(BF16))
            onehots.append(hit.astype(BF16))
        moe = lax.dot_general(jnp.concatenate(onehots, axis=0), jnp.concatenate(gated, axis=0),
                              (((0,), (0,)), ((), ())), preferred_element_type=F32)
        rows = slice(r * seq, (r + 1) * seq)
        xn = x_ref[rows, :] + gate_ref[0] * moe
        if final:
            xn = xn * lax.rsqrt(jnp.mean(xn * xn, axis=-1, keepdims=True) + EPS) * fg_ref[...]
        o_ref[rows, :] = xn


def _scatter(y, pos, g, xn, mods, layer, stream, final, final_g_row):
    cap, seq, per = stream.cap, stream.seq, stream.req_per_step
    tok_blk = pl.BlockSpec((per * seq, D_MODEL), lambda i: (i, 0))
    sel_blk = pl.BlockSpec((per, N_EXPERTS, seq), lambda i: (i, 0, 0))
    return _deferred(
        functools.partial(_scatter_kernel, cap, seq, final),
        grid=(stream.n_req // per,),
        in_specs=[pl.BlockSpec((N_EXPERTS, per * cap, D_MODEL), lambda i: (0, i, 0)),
                  sel_blk, sel_blk, tok_blk,
                  _mod_spec(layer, stream, 5, per * seq),
                  pl.BlockSpec((1, D_MODEL), lambda i: (0, 0))],
        out_specs=tok_blk,
        out_shape=jax.ShapeDtypeStruct((stream.tokens, D_MODEL), F32),
        name=f"scatter_l{layer}_s{seq}",
    )(y, pos, g, xn, mods, final_g_row)


def _axial_rope_tables(rows, head_dim):
    f32 = np.float32
    row = np.repeat(np.arange(rows, dtype=f32), GRID_W)
    col = np.tile(np.arange(GRID_W, dtype=f32), rows)
    nf = head_dim // 4
    inv = (f32(ROPE_BASE) ** (-np.arange(nf, dtype=f32) / f32(nf))).astype(f32)
    ar = row[:, None] * inv[None]
    ac = col[:, None] * inv[None]
    ang = np.concatenate([ar, ar, ac, ac], axis=-1).astype(f32)
    return np.cos(ang).astype(f32), np.sin(ang).astype(f32)


def _block_diag(w):
    eye = jnp.eye(LRU_BLOCKS, dtype=w.dtype)
    return jnp.einsum('dnkj,nm->dnkmj', w, eye).reshape(2, LRU_WIDTH, LRU_WIDTH)


def kernel(x_prompt, x_sample, cache_win_k, cache_win_v, state_lru, cache_diff_k, cache_diff_v, c, c_ctx, ada_w, ada_b, norm_g, final_g, even_w_in, even_w_out, conv_w, conv_b, lru_wa, lru_ba, lru_wx, lru_bx, lru_lambda, win_sink, odd_w_in, odd_w_out, diff_lq1, diff_lk1, diff_lq2, diff_lk2, diff_subln_g, moe_router, moe_w_gate, moe_w_up, moe_w_down):
    cv_t = jnp.concatenate([c_ctx[None], c, jnp.zeros((COND_ROWS - N_COND, D_MODEL), F32)], axis=0).T
    mods = _adaln(cv_t, ada_w, ada_b).reshape(DEPTH * COND_ROWS, 1, 6 * D_MODEL)

    cos, sin = _axial_rope_tables(DEC_SEQ // GRID_W, WIN_HEAD_DIM)
    rope_win = rope_diff = (jnp.asarray(np.tile(cos, (1, 2))), jnp.asarray(np.tile(sin, (1, 2))))

    xs = [x_prompt.reshape(CTX.tokens, D_MODEL), x_sample.reshape(LAT.tokens, D_MODEL)]
    final_g_row = final_g.reshape(1, D_MODEL)
    outs = {}

    def both(make):
        return _run_each(make(0, CTX), make(1, LAT))

    for layer in range(DEPTH):
        idx = layer // 2
        even = layer % 2 == 0
        w_in = (even_w_in if even else odd_w_in)[idx]
        w_out = (even_w_out if even else odd_w_out)[idx]
        wr_t = moe_router[layer].T
        g1_row = norm_g[layer, 0].reshape(1, D_MODEL)
        g2_row = norm_g[layer, 1].reshape(1, D_MODEL)
        if even:
            wa = _block_diag(lru_wa[idx]).astype(BF16)
            wx = _block_diag(lru_wx[idx]).astype(BF16)
            (proj_c, win_k, win_v), (proj_d,) = _run_each(_proj_win(xs[0], g1_row, mods, layer, CTX, w_in),
                                                         _proj(xs[1], g1_row, mods, layer, LAT, w_in))
            projs = (proj_c, proj_d)
            h0s = (jnp.zeros((CTX.n_req, 2, LRU_WIDTH), F32), state_lru[:, idx])
            (y_c, h_fin), (y_d, _) = both(lambda si, st: _lru(
                projs[si], st, conv_w[idx], conv_b[idx].reshape(1, LRU_WIDTH), wa, wx,
                lru_ba[idx], lru_bx[idx], lru_lambda[idx], h0s[si]))
            pack = lambda t: t[:, idx].transpose(0, 2, 1, 3).reshape(DEC_BATCH, PAST_LEN, WIN_KV)
            (o_c,), (o_d,) = _run_each(
                _win_attn(proj_c, CTX, win_sink[idx]),
                _win_attn(proj_d, LAT, win_sink[idx], cache=(pack(cache_win_k), pack(cache_win_v)), rope=rope_win))
            outs["win_k"] = win_k[:, None]
            outs["win_v"] = win_v[:, None]
            outs["lru"] = h_fin[:, None]
            mixed = ([y_c, o_c], [y_d, o_d])
        else:
            lambda_init = 0.8 - 0.6 * math.exp(-0.3 * layer)
            vec = lambda t: t[idx].reshape(1, DIFF_HEAD_DIM)
            args = (vec(diff_lq1), vec(diff_lk1), vec(diff_lq2), vec(diff_lk2), diff_subln_g[idx])
            (q, kh, vh), (proj_d,) = _run_each(_proj_heads(xs[0], g1_row, mods, layer, CTX, w_in),
                                              _proj(xs[1], g1_row, mods, layer, LAT, w_in))
            (o_c,), (o_d,) = _run_each(
                _diff_attn((q, kh, vh), CTX, lambda_init, *args),
                _diff_attn(proj_d, LAT, lambda_init, *args,
                           cache=(cache_diff_k[:, idx:idx + 1], cache_diff_v[:, idx:idx + 1]), rope=rope_diff))
            outs["diff_k"] = kh[:, None]
            outs["diff_v"] = vh[:, None]
            mixed = ([o_c], [o_d])

        mix = both(lambda si, st: _mix_out(mixed[si], w_out, xs[si], g2_row, mods, layer, st, wr_t))
        xns, h2s = (mix[0][0], mix[1][0]), (mix[0][1], mix[1][1])

        def request_major(lg, st):
            per_req = st.seq // LOGIT_TILE
            lg = lg.reshape(st.n_req, per_req, N_EXPERTS, LOGIT_TILE).transpose(0, 2, 1, 3)
            return lg.reshape(st.n_req, N_EXPERTS, st.seq)

        lgs = (request_major(mix[0][2], CTX), request_major(mix[1][2], LAT))
        routes = both(lambda si, st: _router(lgs[si], st))
        (rows_c,), (rows_d,) = both(lambda si, st: _gather(h2s[si], routes[si][0], st))
        ys = _ffn(rows_c, rows_d, layer, moe_w_gate, moe_w_up, moe_w_down)
        final = layer == DEPTH - 1
        (x_c,), (x_d,) = both(lambda si, st: _scatter(ys[si], routes[si][0], routes[si][1], xns[si], mods, layer, st,
                                                      final, final_g_row))
        xs = [x_c, x_d]


    y_prompt = xs[0].reshape(BATCH, SEQ, D_MODEL)
    y_sample = xs[1].reshape(DEC_BATCH, DEC_SEQ, D_MODEL)
    return (y_prompt, y_sample, outs["win_k"], outs["win_v"], outs["lru"], outs["diff_k"], outs["diff_v"])
```

```python
import functools
import math
from typing import Callable, NamedTuple

import jax
import jax.numpy as jnp
import numpy as np
from jax import lax
from jax.experimental import pallas as pl
from jax.experimental.pallas import tpu as pltpu

F32 = jnp.float32
BF16 = jnp.bfloat16
I32 = jnp.int32

D_MODEL = 1024
BATCH = 16
SEQ = 256
DEPTH = 2
DEC_BATCH = 2
DEC_SEQ = 1024
PAST_LEN = 512
GRID_W = 64
LRU_WIDTH = D_MODEL // 2
LRU_BLOCKS = 8
LRU_BLOCK = LRU_WIDTH // LRU_BLOCKS
CONV_W = 4
LRU_C = 8.0
WIN_HEADS = 8
WIN_KV_HEADS = 2
WIN_REP = WIN_HEADS // WIN_KV_HEADS
WIN_HEAD_DIM = 64
WINDOW = 128
WIN_Q = WIN_HEADS * WIN_HEAD_DIM
WIN_KV = WIN_KV_HEADS * WIN_HEAD_DIM
EVEN_IN = 2 * LRU_WIDTH + WIN_Q + 2 * WIN_KV
DIFF_HEADS = 8
DIFF_HEAD_DIM = 64
DIFF_QK = DIFF_HEADS * 2 * DIFF_HEAD_DIM
DIFF_V = DIFF_HEADS * 2 * DIFF_HEAD_DIM
ODD_IN = 2 * DIFF_QK + DIFF_V
N_EXPERTS = 16
EXPERT_FF = 2 * D_MODEL
CAPACITY_FACTOR = 2
ROPE_BASE = 10000.0
EPS = 1e-6
NEG_INF = -1e30

LANES = 128
SUBLANES = 8
VMEM_LIMIT_BYTES = 48 * 1024 * 1024

N_COND = 1 + DEC_BATCH
COND_ROWS = SUBLANES
ADALN_TILE = 2048
TOKEN_TILE = 512
MIX_TILE = 1024
ROUTE_ROWS = 1024
LOGIT_TILE = 256
FF_TILE = 1024
FFN_VMEM_LIMIT_BYTES = 56 * 1024 * 1024
LRU_CHUNK = 256
WIN_BLOCKS_PER_ITER = 2
ATT_Q_TILE = 256
ATT_K_TILE = 256


class Stream:
    def __init__(self, n_req, seq, cond0, cond_step):
        self.n_req, self.seq, self.cond0, self.cond_step = n_req, seq, cond0, cond_step
        self.tokens = n_req * seq
        self.cap = CAPACITY_FACTOR * seq // N_EXPERTS
        self.req_per_step = max(1, ROUTE_ROWS // seq)

    def cond_of_row(self, row):
        return self.cond0 + self.cond_step * (row // self.seq)


CTX = Stream(BATCH, SEQ, 0, 0)
LAT = Stream(DEC_BATCH, DEC_SEQ, 1, 1)


def _params(n_axes, vmem_limit_bytes=VMEM_LIMIT_BYTES):
    return pltpu.CompilerParams(dimension_semantics=("arbitrary",) * n_axes,
                                vmem_limit_bytes=vmem_limit_bytes)


class Call(NamedTuple):
    kernel: Callable
    steps: int
    in_specs: tuple
    out_specs: tuple
    out_shapes: tuple
    scratch: tuple
    args: tuple
    name: str


def _deferred(kernel, *, grid, in_specs, out_specs, out_shape, scratch_shapes=(), name):
    (steps,) = grid
    as_tuple = lambda v: tuple(v) if isinstance(v, (list, tuple)) else (v,)
    return lambda *args: Call(kernel, steps, tuple(in_specs), as_tuple(out_specs), as_tuple(out_shape),
                              tuple(scratch_shapes), args, name)


def _run(call):
    return pl.pallas_call(
        call.kernel, grid=(call.steps,), in_specs=list(call.in_specs), out_specs=list(call.out_specs),
        out_shape=list(call.out_shapes), scratch_shapes=list(call.scratch),
        compiler_params=_params(1), name=call.name)(*call.args)


def _run_each(a, b):
    return _run(a), _run(b)


def _resident_spec(shape):
    return pl.BlockSpec(shape, lambda i: (0, 0), pipeline_mode=pl.Buffered(1))


def _mod_spec(layer, stream, k, rows_per_step):
    return pl.BlockSpec(
        (1, 1, D_MODEL),
        lambda i, *_: (layer * COND_ROWS + stream.cond_of_row(i * rows_per_step), 0, k))


def _dot(a, b):
    return jnp.dot(a, b, preferred_element_type=F32)


def _dot_nt(a, b):
    return lax.dot_general(a, b, (((1,), (1,)), ((), ())), preferred_element_type=F32)


def _modnorm(x, g, shift, scale):
    y = x * lax.rsqrt(jnp.mean(x * x, axis=-1, keepdims=True) + EPS)
    return (y * g) * (1.0 + scale) + shift


def _lane_half_masks(shape):
    lane = lax.broadcasted_iota(I32, shape, len(shape) - 1)
    left = (lane & (LANES - 1)) < LANES // 2
    return left, jnp.logical_not(left)


def _rope(x, cos, sin):
    parts = []
    for c in range(x.shape[1] // LANES):
        xs = x[:, c * LANES:(c + 1) * LANES]
        lane = lax.broadcasted_iota(I32, xs.shape, 1)
        first = (lane & 31) < 16
        rot = jnp.where(first, -pltpu.roll(xs, LANES - 16, axis=1), pltpu.roll(xs, 16, axis=1))
        parts.append(xs * cos + rot * sin)
    return parts[0] if len(parts) == 1 else jnp.concatenate(parts, axis=1)


def _adaln_kernel(cv_ref, w_ref, b_ref, o_ref):
    cv = cv_ref[...]
    s = cv * jax.nn.sigmoid(cv)
    w = w_ref[0]
    ridx = lax.broadcasted_iota(I32, (COND_ROWS, w.shape[1]), 0)
    out = jnp.zeros((COND_ROWS, w.shape[1]), F32)
    for r in range(N_COND):
        out = jnp.where(ridx == r, jnp.sum(w * s[:, r:r + 1], axis=0, keepdims=True), out)
    o_ref[0] = out + b_ref[0]


def _adaln(cv_t, ada_w, ada_b):
    tn = ADALN_TILE
    return pl.pallas_call(
        _adaln_kernel,
        grid=(DEPTH, 6 * D_MODEL // tn),
        in_specs=[pl.BlockSpec((D_MODEL, COND_ROWS), lambda l, j: (0, 0)),
                  pl.BlockSpec((1, D_MODEL, tn), lambda l, j: (l, 0, j)),
                  pl.BlockSpec((1, 1, tn), lambda l, j: (l, 0, j))],
        out_specs=pl.BlockSpec((1, COND_ROWS, tn), lambda l, j: (l, 0, j)),
        out_shape=jax.ShapeDtypeStruct((DEPTH, COND_ROWS, 6 * D_MODEL), F32),
        compiler_params=_params(2),
        name="adaln",
    )(cv_t, ada_w, ada_b.reshape(DEPTH, 1, 6 * D_MODEL))


def _proj_kernel(x_ref, g_ref, sh_ref, sc_ref, w_ref, o_ref):
    h = _modnorm(x_ref[...], g_ref[...], sh_ref[0], sc_ref[0])
    o_ref[...] = _dot(h.astype(BF16), w_ref[...].astype(BF16))


def _proj(x, g_row, mods, layer, stream, w):
    n_out = w.shape[1]
    tm = TOKEN_TILE
    return _deferred(
        _proj_kernel,
        grid=(stream.tokens // tm,),
        in_specs=[pl.BlockSpec((tm, D_MODEL), lambda i: (i, 0)),
                  pl.BlockSpec((1, D_MODEL), lambda i: (0, 0)),
                  _mod_spec(layer, stream, 0, tm),
                  _mod_spec(layer, stream, 1, tm),
                  _resident_spec((D_MODEL, n_out))],
        out_specs=pl.BlockSpec((tm, n_out), lambda i: (i, 0)),
        out_shape=jax.ShapeDtypeStruct((stream.tokens, n_out), F32),
        name=f"proj_l{layer}_s{stream.seq}",
    )(x, g_row, mods, mods, w)


def _proj_heads_kernel(seq, x_ref, g_ref, sh_ref, sc_ref, w_ref, q_ref, k_ref, v_ref):
    h = _modnorm(x_ref[...], g_ref[...], sh_ref[0], sc_ref[0])
    res = _dot(h.astype(BF16), w_ref[...].astype(BF16))
    q_ref[...] = res[:, 0:DIFF_QK].astype(BF16)
    for r in range(x_ref.shape[0] // seq):
        rows = slice(r * seq, (r + 1) * seq)
        for hh in range(DIFF_HEADS):
            k_ref[r, hh] = res[rows, DIFF_QK + hh * LANES:DIFF_QK + (hh + 1) * LANES]
            v_ref[r, hh] = res[rows, 2 * DIFF_QK + hh * LANES:2 * DIFF_QK + (hh + 1) * LANES]


def _proj_heads(x, g_row, mods, layer, stream, w):
    tm, seq = TOKEN_TILE, stream.seq
    head_shape = (stream.n_req, DIFF_HEADS, seq, 2 * DIFF_HEAD_DIM)
    head_spec = pl.BlockSpec((tm // seq, DIFF_HEADS, seq, 2 * DIFF_HEAD_DIM), lambda i: (i, 0, 0, 0))
    return _deferred(
        functools.partial(_proj_heads_kernel, seq),
        grid=(stream.tokens // tm,),
        in_specs=[pl.BlockSpec((tm, D_MODEL), lambda i: (i, 0)),
                  pl.BlockSpec((1, D_MODEL), lambda i: (0, 0)),
                  _mod_spec(layer, stream, 0, tm),
                  _mod_spec(layer, stream, 1, tm),
                  _resident_spec((D_MODEL, ODD_IN))],
        out_specs=[pl.BlockSpec((tm, DIFF_QK), lambda i: (i, 0)), head_spec, head_spec],
        out_shape=[jax.ShapeDtypeStruct((stream.tokens, DIFF_QK), BF16),
                   jax.ShapeDtypeStruct(head_shape, F32), jax.ShapeDtypeStruct(head_shape, F32)],
        name=f"proj_heads_l{layer}_s{seq}",
    )(x, g_row, mods, mods, w)


def _proj_win_kernel(seq, x_ref, g_ref, sh_ref, sc_ref, w_ref, o_ref, k_ref, v_ref):
    h = _modnorm(x_ref[...], g_ref[...], sh_ref[0], sc_ref[0])
    res = _dot(h.astype(BF16), w_ref[...].astype(BF16))
    o_ref[...] = res
    k0 = 2 * LRU_WIDTH + WIN_Q
    for r in range(x_ref.shape[0] // seq):
        rows = slice(r * seq, (r + 1) * seq)
        for g in range(WIN_KV_HEADS):
            k_ref[r, g] = res[rows, k0 + g * WIN_HEAD_DIM:k0 + (g + 1) * WIN_HEAD_DIM]
            v_ref[r, g] = res[rows, k0 + WIN_KV + g * WIN_HEAD_DIM:k0 + WIN_KV + (g + 1) * WIN_HEAD_DIM]


def _proj_win(x, g_row, mods, layer, stream, w):
    tm, seq = TOKEN_TILE, stream.seq
    head_shape = (stream.n_req, WIN_KV_HEADS, seq, WIN_HEAD_DIM)
    head_spec = pl.BlockSpec((tm // seq, WIN_KV_HEADS, seq, WIN_HEAD_DIM), lambda i: (i, 0, 0, 0))
    return _deferred(
        functools.partial(_proj_win_kernel, seq),
        grid=(stream.tokens // tm,),
        in_specs=[pl.BlockSpec((tm, D_MODEL), lambda i: (i, 0)),
                  pl.BlockSpec((1, D_MODEL), lambda i: (0, 0)),
                  _mod_spec(layer, stream, 0, tm),
                  _mod_spec(layer, stream, 1, tm),
                  _resident_spec((D_MODEL, EVEN_IN))],
        out_specs=[pl.BlockSpec((tm, EVEN_IN), lambda i: (i, 0)), head_spec, head_spec],
        out_shape=[jax.ShapeDtypeStruct((stream.tokens, EVEN_IN), F32),
                   jax.ShapeDtypeStruct(head_shape, F32), jax.ShapeDtypeStruct(head_shape, F32)],
        name=f"proj_win_l{layer}_s{seq}",
    )(x, g_row, mods, mods, w)


def _lru_kernel(seq, xl_ref, gl_ref, cw_ref, cb_ref, wa_ref, wx_ref, ba_ref, bx_ref, lam_ref, h0_ref,
                y_ref, hfin_ref, xe, af, bf, ab, bb):
    width = LRU_WIDTH
    ch = LRU_CHUNK
    seg = seq // SUBLANES
    lead = (CONV_W // 2) * SUBLANES

    def to_segment_major(x):
        return jnp.transpose(x.reshape(SUBLANES, seg, LANES), (1, 0, 2)).reshape(seq, LANES)

    def to_time_major(x):
        return jnp.transpose(x.reshape(seg, SUBLANES, LANES), (1, 0, 2)).reshape(seq, LANES)

    for c in range(width // LANES):
        cols = slice(c * LANES, (c + 1) * LANES)
        xe[lead:lead + seq, cols] = to_segment_major(xl_ref[:, cols])
    sub = lax.broadcasted_iota(I32, (SUBLANES, width), 0)
    for k in range(CONV_W // 2):
        prev = xe[lead + (seg - 2 + k) * SUBLANES:lead + (seg - 1 + k) * SUBLANES, :]
        xe[k * SUBLANES:(k + 1) * SUBLANES, :] = jnp.where(sub >= 1, pltpu.roll(prev, 1, axis=0), 0.0)
    first = xe[lead:lead + SUBLANES, :]
    xe[lead + seq:lead + seq + SUBLANES, :] = jnp.where(sub < SUBLANES - 1,
                                                        pltpu.roll(first, SUBLANES - 1, axis=0), 0.0)

    lam = lam_ref[...]
    z = -lam
    softplus = jnp.maximum(z, 0.0) + jnp.log1p(jnp.exp(-jnp.abs(z)))
    cw = cw_ref[...]
    cb = cb_ref[...]

    def gates_chunk(c, carry):
        r0 = pl.multiple_of(c * ch, ch)
        xc = xe[pl.ds(r0, ch), :] * cw[0:1]
        for j in range(1, CONV_W):
            xc = xc + xe[pl.ds(pl.multiple_of(r0 + j * SUBLANES, SUBLANES), ch), :] * cw[j:j + 1]
        xc = xc + cb
        xcb = xc.astype(BF16)
        for d, (a_s, b_s) in enumerate(((af, bf), (ab, bb))):
            r = jax.nn.sigmoid(_dot(xcb, wa_ref[d]) + ba_ref[d:d + 1])
            ig = jax.nn.sigmoid(_dot(xcb, wx_ref[d]) + bx_ref[d:d + 1])
            log_a = (-LRU_C * r) * softplus[d:d + 1]
            a = jnp.exp(log_a)
            v = 1.0 - a * a
            a_s[pl.ds(r0, ch), :] = a
            b_s[pl.ds(r0, ch), :] = jnp.where(v > 0.0, v * lax.rsqrt(v), 0.0) * (ig * xc)
        return carry

    lax.fori_loop(0, seq // ch, gates_chunk, 0)

    def scan_step(k, carry):
        hf, pf, hb, pb = carry
        rf = pl.multiple_of(k * SUBLANES, SUBLANES)
        rb = pl.multiple_of((seg - 1 - k) * SUBLANES, SUBLANES)
        a = af[pl.ds(rf, SUBLANES), :]
        hf = a * hf + bf[pl.ds(rf, SUBLANES), :]
        pf = a * pf
        bf[pl.ds(rf, SUBLANES), :] = hf
        af[pl.ds(rf, SUBLANES), :] = pf
        a = ab[pl.ds(rb, SUBLANES), :]
        hb = a * hb + bb[pl.ds(rb, SUBLANES), :]
        pb = a * pb
        bb[pl.ds(rb, SUBLANES), :] = hb
        ab[pl.ds(rb, SUBLANES), :] = pb
        return hf, pf, hb, pb

    zeros = jnp.zeros((SUBLANES, width), F32)
    ones = jnp.ones((SUBLANES, width), F32)
    hf, pf, hb, pb = lax.fori_loop(0, seg, scan_step, (zeros, ones, zeros, ones))

    h0 = h0_ref[0]
    state, carry_f = h0[0:1], zeros
    for g in range(SUBLANES):
        carry_f = jnp.where(sub == g, state, carry_f)
        state = pf[g:g + 1] * state + hf[g:g + 1]
    hfin_ref[0, 0:1, :] = state
    state, carry_b = h0[1:2], zeros
    for g in reversed(range(SUBLANES)):
        carry_b = jnp.where(sub == g, state, carry_b)
        state = pb[g:g + 1] * state + hb[g:g + 1]
    hfin_ref[0, 1:2, :] = state

    groups = (ch // SUBLANES, SUBLANES, width)

    def combine_chunk(c, carry):
        r0 = pl.multiple_of(c * ch, ch)
        rows = pl.ds(r0, ch)
        h_fwd = bf[rows, :].reshape(groups) + af[rows, :].reshape(groups) * carry_f
        h_bwd = bb[rows, :].reshape(groups) + ab[rows, :].reshape(groups) * carry_b
        bf[rows, :] = (h_fwd + h_bwd).reshape(ch, width)
        return carry

    lax.fori_loop(0, seq // ch, combine_chunk, 0)

    for c in range(width // LANES):
        cols = slice(c * LANES, (c + 1) * LANES)
        y_ref[:, cols] = (to_time_major(bf[:, cols]) * jax.nn.gelu(gl_ref[:, cols])).astype(BF16)


def _lru(proj, stream, conv_w, conv_b, wa, wx, ba, bx, lam, h0):
    seq, width = stream.seq, LRU_WIDTH
    full2 = lambda b: (0, 0)
    full3 = lambda b: (0, 0, 0)
    return _deferred(
        functools.partial(_lru_kernel, seq),
        grid=(stream.n_req,),
        in_specs=[pl.BlockSpec((seq, width), lambda b: (b, 0)),
                  pl.BlockSpec((seq, width), lambda b: (b, 1)),
                  pl.BlockSpec((CONV_W, width), full2),
                  pl.BlockSpec((1, width), full2),
                  pl.BlockSpec((2, width, width), full3),
                  pl.BlockSpec((2, width, width), full3),
                  pl.BlockSpec((2, width), full2),
                  pl.BlockSpec((2, width), full2),
                  pl.BlockSpec((2, width), full2),
                  pl.BlockSpec((1, 2, width), lambda b: (b, 0, 0))],
        out_specs=[pl.BlockSpec((seq, width), lambda b: (b, 0)),
                   pl.BlockSpec((1, 2, width), lambda b: (b, 0, 0))],
        out_shape=[jax.ShapeDtypeStruct((stream.tokens, width), BF16),
                   jax.ShapeDtypeStruct((stream.n_req, 2, width), F32)],
        scratch_shapes=[pltpu.VMEM((seq + (CONV_W - 1) * SUBLANES, width), F32)] + [pltpu.VMEM((seq, width), F32)] * 4,
        name=f"lru_s{seq}",
    )(proj, proj, conv_w, conv_b, wa, wx, ba, bx, lam, h0)


def _attend(q, chunks, s_ref):
    tile_max = None
    spans = []
    off = 0
    for keys, _ in chunks:
        s = _dot_nt(q, keys())
        n = s.shape[1]
        s_ref[:, off:off + n] = s
        for c in range(n // LANES):
            t = s[:, c * LANES:(c + 1) * LANES]
            tile_max = t if tile_max is None else jnp.maximum(tile_max, t)
        spans.append((off, n))
        off += n
    m = jnp.max(tile_max, axis=-1, keepdims=True)
    acc = None
    for (_, values), (o, n) in zip(chunks, spans):
        part = _dot(jnp.exp(s_ref[:, o:o + n] - m).astype(BF16), values())
        acc = part if acc is None else acc + part
    return acc


def _split_groups(kk):
    left, right = _lane_half_masks(kk.shape)
    g0_l = jnp.where(left, kk, 0.0)
    g1_r = jnp.where(right, kk, 0.0)
    return ((g0_l, pltpu.roll(g0_l, LANES // 2, axis=1)), (pltpu.roll(g1_r, LANES // 2, axis=1), g1_r))


def _win_ctx_kernel(sink_ref, q_ref, kv_ref, o_ref):
    scale = WIN_HEAD_DIM ** -0.5
    seq = q_ref.shape[0]
    ks = _split_groups(kv_ref[:, 0:LANES])
    vs = _split_groups(kv_ref[:, LANES:2 * LANES])
    top = lax.broadcasted_iota(I32, (2 * seq, 1), 0) < seq
    outs = [None] * (WIN_HEADS // 2)
    for g in range(WIN_KV_HEADS):
        pairs = (2 * g, 2 * g + 1)
        qs = jnp.concatenate([q_ref[:, p * LANES:(p + 1) * LANES] for p in pairs], axis=0)
        qs = (qs * scale).astype(BF16)
        for side in range(2):
            sk = jnp.where(top, sink_ref[2 * pairs[0] + side], sink_ref[2 * pairs[1] + side])
            s = _dot_nt(qs, ks[g][side].astype(BF16))
            m = jnp.maximum(jnp.max(s, axis=-1, keepdims=True), sk)
            e = jnp.exp(s - m)
            den = jnp.sum(e, axis=-1, keepdims=True) + jnp.exp(sk - m)
            o = _dot(e.astype(BF16), vs[g][side].astype(BF16)) * (1.0 / den)
            for k, p in enumerate(pairs):
                part = o[k * seq:(k + 1) * seq]
                outs[p] = part if outs[p] is None else outs[p] + part
    for p in range(WIN_HEADS // 2):
        o_ref[:, p * LANES:(p + 1) * LANES] = outs[p].astype(BF16)


def _win_lat_kernel(sink_ref, q_ref, kv_ref, ck_ref, cv_ref, cos_ref, sin_ref, o_ref,
                    kl_s, vl_s, kc_s, vc_s):
    seq, wn = DEC_SEQ, WINDOW
    scale = WIN_HEAD_DIM ** -0.5
    kr = _rope(kv_ref[:, 0:LANES], cos_ref[...], sin_ref[...])
    for src, dst in ((_split_groups(kr), kl_s), (_split_groups(kv_ref[:, LANES:2 * LANES]), vl_s),
                     (_split_groups(ck_ref[0]), kc_s), (_split_groups(cv_ref[0]), vc_s)):
        for g in range(WIN_KV_HEADS):
            for side in range(2):
                dst[2 * g + side] = src[g][side].astype(BF16)

    top = lax.broadcasted_iota(I32, (2 * wn, 1), 0) < wn

    def q_block(i, carry):
        for sub in range(WIN_BLOCKS_PER_ITER):
            blk = WIN_BLOCKS_PER_ITER * i + sub
            r0 = pl.multiple_of(blk * wn, wn)
            start = pl.multiple_of(jnp.clip((blk - 1) * wn, 0, seq - 3 * wn), wn)
            qr = _rope(q_ref[pl.ds(r0, wn), :], cos_ref[pl.ds(r0, wn), :], sin_ref[pl.ds(r0, wn), :]) * scale
            qpos = r0 + (lax.broadcasted_iota(I32, (2 * wn, 3 * wn), 0) & (wn - 1))
            kpos = start + lax.broadcasted_iota(I32, (2 * wn, 3 * wn), 1)
            valid = jnp.abs(qpos - kpos) <= wn
            outs = [None] * (WIN_HEADS // 2)
            for g in range(WIN_KV_HEADS):
                pairs = (2 * g, 2 * g + 1)
                qs = jnp.concatenate([qr[:, p * LANES:(p + 1) * LANES] for p in pairs], axis=0).astype(BF16)
                for side in range(2):
                    idx = 2 * g + side
                    sk = jnp.where(top, sink_ref[2 * pairs[0] + side], sink_ref[2 * pairs[1] + side])
                    sl = _dot_nt(qs, kl_s[idx, pl.ds(start, 3 * wn), :])
                    sl = jnp.where(valid, sl, NEG_INF)
                    sc = _dot_nt(qs, kc_s[idx])
                    m = jnp.maximum(jnp.maximum(jnp.max(sl, axis=-1, keepdims=True),
                                                jnp.max(sc, axis=-1, keepdims=True)), sk)
                    el = jnp.exp(sl - m)
                    ec = jnp.exp(sc - m)
                    den = (jnp.sum(el, axis=-1, keepdims=True) + jnp.sum(ec, axis=-1, keepdims=True)
                           + jnp.exp(sk - m))
                    o = (_dot(el.astype(BF16), vl_s[idx, pl.ds(start, 3 * wn), :])
                         + _dot(ec.astype(BF16), vc_s[idx])) * (1.0 / den)
                    for k, p in enumerate(pairs):
                        part = o[k * wn:(k + 1) * wn]
                        outs[p] = part if outs[p] is None else outs[p] + part
            for p in range(WIN_HEADS // 2):
                o_ref[pl.ds(r0, wn), p * LANES:(p + 1) * LANES] = outs[p].astype(BF16)
        return carry

    lax.fori_loop(0, seq // (WIN_BLOCKS_PER_ITER * wn), q_block, 0)


def _win_attn(proj, stream, sink, cache=None, rope=None):
    seq = stream.seq
    q_spec = pl.BlockSpec((seq, WIN_Q), lambda b: (b, 2 * LRU_WIDTH // WIN_Q))
    kv_spec = pl.BlockSpec((seq, 2 * WIN_KV), lambda b: (b, (2 * LRU_WIDTH + WIN_Q) // (2 * WIN_KV)))
    sink_spec = pl.BlockSpec(memory_space=pltpu.SMEM)
    out_spec = pl.BlockSpec((seq, WIN_Q), lambda b: (b, 0))
    out_shape = jax.ShapeDtypeStruct((stream.tokens, WIN_Q), BF16)
    if cache is None:
        return _deferred(
            _win_ctx_kernel, grid=(stream.n_req,),
            in_specs=[sink_spec, q_spec, kv_spec], out_specs=out_spec, out_shape=out_shape,
            name="win_attn_ctx",
        )(sink, proj, proj)
    ck, cv = cache
    cos, sin = rope
    cache_spec = pl.BlockSpec((1, PAST_LEN, LANES), lambda b: (b, 0, 0))
    table_spec = pl.BlockSpec((seq, LANES), lambda b: (0, 0))
    return _deferred(
        _win_lat_kernel, grid=(stream.n_req,),
        in_specs=[sink_spec, q_spec, kv_spec, cache_spec, cache_spec, table_spec, table_spec],
        out_specs=out_spec, out_shape=out_shape,
        scratch_shapes=[pltpu.VMEM((4, seq, LANES), BF16), pltpu.VMEM((4, seq, LANES), BF16),
                        pltpu.VMEM((4, PAST_LEN, LANES), BF16), pltpu.VMEM((4, PAST_LEN, LANES), BF16)],
        name="win_attn_lat",
    )(sink, proj, proj, ck, cv, cos, sin)


def _diff_lambda(lq1_ref, lk1_ref, lq2_ref, lk2_ref, lambda_init):
    t1 = jnp.sum(lq1_ref[...] * lk1_ref[...], axis=-1, keepdims=True)
    t2 = jnp.sum(lq2_ref[...] * lk2_ref[...], axis=-1, keepdims=True)
    return jnp.exp(t1) - jnp.exp(t2) + lambda_init


def _subln(o, g_row, lambda_init):
    o = o * lax.rsqrt(jnp.mean(o * o, axis=-1, keepdims=True) + EPS) * g_row
    return o * (1.0 - lambda_init)


def _component_keys(k):
    left, right = _lane_half_masks(k.shape)
    return jnp.where(left, k, 0.0).astype(BF16), jnp.where(right, k, 0.0).astype(BF16)


def _values_with_ones(v):
    return jnp.concatenate([v.astype(BF16), jnp.ones(v.shape, BF16)], axis=1)


def _diff_combine(accs, lam):
    o1, o2 = accs[0][:, 0:LANES], accs[1][:, 0:LANES]
    return o1 * (1.0 / accs[0][:, LANES:]) - o2 * (lam * (1.0 / accs[1][:, LANES:]))


def _diff_ctx_kernel(lambda_init, lq1_ref, lk1_ref, lq2_ref, lk2_ref, sg_ref, q_ref, k_ref, v_ref, o_ref, s_ref):
    scale = DIFF_HEAD_DIM ** -0.5
    tq = ATT_Q_TILE
    lam = _diff_lambda(lq1_ref, lk1_ref, lq2_ref, lk2_ref, lambda_init)
    for h in range(DIFF_HEADS):
        cols = slice(h * LANES, (h + 1) * LANES)
        keys = _component_keys(k_ref[0, h])
        vals = _values_with_ones(v_ref[0, h])
        for qt in range(q_ref.shape[0] // tq):
            rows = slice(qt * tq, (qt + 1) * tq)
            q = (q_ref[rows, cols] * scale).astype(BF16)
            slots = [s_ref.at[(4 * h + 2 * qt + c) % s_ref.shape[0]] for c in range(2)]
            accs = [_attend(q, [(lambda: keys[c], lambda: vals)], slots[c]) for c in range(2)]
            o = _diff_combine(accs, lam)
            o_ref[rows, cols] = _subln(o, sg_ref[h:h + 1, :], lambda_init).astype(BF16)


def _diff_lat_kernel(lambda_init, lq1_ref, lk1_ref, lq2_ref, lk2_ref, sg_ref, q_ref, k_ref, v_ref,
                     ck_ref, cv_ref, cos_ref, sin_ref, o_ref, kl_s, kc_s, vl_s, vc_s, s_ref):
    seq, tq, tk = DEC_SEQ, ATT_Q_TILE, ATT_K_TILE
    scale = DIFF_HEAD_DIM ** -0.5
    lam = _diff_lambda(lq1_ref, lk1_ref, lq2_ref, lk2_ref, lambda_init)
    for src, dst in ((_rope(k_ref[...], cos_ref[...], sin_ref[...]), kl_s), (ck_ref[0, 0, 0], kc_s)):
        dst[0], dst[1] = _component_keys(src)
    vl_s[...] = _values_with_ones(v_ref[...])
    vc_s[...] = _values_with_ones(cv_ref[0, 0, 0])
    sg = sg_ref[0]

    for qt in range(seq // tq):
        rows = slice(qt * tq, (qt + 1) * tq)
        qr = (_rope(q_ref[rows, :], cos_ref[rows, :], sin_ref[rows, :]) * scale).astype(BF16)
        accs = []
        for comp in range(2):
            chunks = [(lambda j=j: kl_s[comp, j * tk:(j + 1) * tk, :],
                       lambda j=j: vl_s[j * tk:(j + 1) * tk, :]) for j in range(seq // tk)]
            chunks += [(lambda j=j: kc_s[comp, j * tk:(j + 1) * tk, :],
                        lambda j=j: vc_s[j * tk:(j + 1) * tk, :]) for j in range(PAST_LEN // tk)]
            accs.append(_attend(qr, chunks, s_ref.at[2 * qt + comp]))
        o_ref[rows, :] = _subln(_diff_combine(accs, lam), sg, lambda_init).astype(BF16)


def _diff_attn(proj, stream, lambda_init, lq1, lk1, lq2, lk2, subln_g, cache=None, rope=None):
    seq = stream.seq
    vec_spec = lambda nd: pl.BlockSpec((1, DIFF_HEAD_DIM), lambda *_: (0, 0))
    out_shape = jax.ShapeDtypeStruct((stream.tokens, DIFF_V), BF16)
    if cache is None:
        q, kh, vh = proj
        head_spec = pl.BlockSpec((1, DIFF_HEADS, seq, LANES), lambda b: (b, 0, 0, 0))
        return _deferred(
            functools.partial(_diff_ctx_kernel, lambda_init), grid=(stream.n_req,),
            in_specs=[vec_spec(1)] * 4 + [pl.BlockSpec((DIFF_HEADS, LANES), lambda b: (0, 0)),
                                          pl.BlockSpec((seq, DIFF_QK), lambda b: (b, 0)), head_spec, head_spec],
            out_specs=pl.BlockSpec((seq, DIFF_V), lambda b: (b, 0)), out_shape=out_shape,
            scratch_shapes=[pltpu.VMEM((8, ATT_Q_TILE, seq), F32)],
            name="diff_attn_ctx",
        )(lq1, lk1, lq2, lk2, subln_g, q, kh, vh)
    ck, cv = cache
    cos, sin = rope
    nh = DIFF_HEADS
    blk = lambda c: pl.BlockSpec((seq, LANES), lambda i: (i // nh, c * nh + i % nh))
    cache_spec = pl.BlockSpec((1, 1, 1, PAST_LEN, LANES), lambda i: (i // nh, 0, i % nh, 0, 0))
    table_spec = pl.BlockSpec((seq, LANES), lambda i: (0, 0))
    return _deferred(
        functools.partial(_diff_lat_kernel, lambda_init), grid=(stream.n_req * nh,),
        in_specs=[vec_spec(2)] * 4 + [pl.BlockSpec((1, 1, LANES), lambda i: (i % nh, 0, 0)),
                                      blk(0), blk(1), blk(2), cache_spec, cache_spec, table_spec, table_spec],
        out_specs=pl.BlockSpec((seq, LANES), lambda i: (i // nh, i % nh)), out_shape=out_shape,
        scratch_shapes=[pltpu.VMEM((2, seq, LANES), BF16), pltpu.VMEM((2, PAST_LEN, LANES), BF16),
                        pltpu.VMEM((seq, 2 * LANES), BF16), pltpu.VMEM((PAST_LEN, 2 * LANES), BF16),
                        pltpu.VMEM((2 * seq // ATT_Q_TILE, ATT_Q_TILE, seq + PAST_LEN), F32)],
        name="diff_attn_lat",
    )(lq1, lk1, lq2, lk2, subln_g.reshape(DIFF_HEADS, 1, LANES), proj, proj, proj, ck, cv, cos, sin)


def _mix_out_kernel(n_in, *refs):
    a_refs = refs[:n_in]
    w_ref, x_ref, gate_ref, g2_ref, sh2_ref, sc2_ref, wr_ref, xn_ref, h2_ref, lg_ref = refs[n_in:]
    kp = D_MODEL // n_in
    wr = wr_ref[...]
    w_hi = wr.astype(BF16)
    w_lo = (wr - w_hi.astype(F32)).astype(BF16)
    acc = None
    for k, a_ref in enumerate(a_refs):
        part = _dot(a_ref[...], w_ref[k * kp:(k + 1) * kp, :].astype(BF16))
        acc = part if acc is None else acc + part
    xn = x_ref[...] + gate_ref[0] * acc
    xn_ref[...] = xn
    h2 = _modnorm(xn, g2_ref[...], sh2_ref[0], sc2_ref[0])
    h_hi = h2.astype(BF16)
    h2_ref[...] = h_hi
    h_lo = (h2 - h_hi.astype(F32)).astype(BF16)
    by_hi = _dot_nt(jnp.concatenate([w_hi, w_lo], axis=0), h_hi)
    lg = by_hi[0:N_EXPERTS] + (by_hi[N_EXPERTS:] + _dot_nt(w_hi, h_lo))
    for c in range(lg.shape[1] // LOGIT_TILE):
        lg_ref[c] = lg[:, c * LOGIT_TILE:(c + 1) * LOGIT_TILE]


def _mix_out(mixed, w_out, x, g2_row, mods, layer, stream, wr_t):
    tm = MIX_TILE
    n_in = len(mixed)
    kp = D_MODEL // n_in
    row_blk = lambda width: pl.BlockSpec((tm, width), lambda i: (i, 0))
    return _deferred(
        functools.partial(_mix_out_kernel, n_in),
        grid=(stream.tokens // tm,),
        in_specs=[row_blk(kp)] * n_in + [
            _resident_spec((D_MODEL, D_MODEL)),
            row_blk(D_MODEL),
            _mod_spec(layer, stream, 2, tm),
            pl.BlockSpec((1, D_MODEL), lambda i: (0, 0)),
            _mod_spec(layer, stream, 3, tm),
            _mod_spec(layer, stream, 4, tm),
            pl.BlockSpec((N_EXPERTS, D_MODEL), lambda i: (0, 0))],
        out_specs=[row_blk(D_MODEL), row_blk(D_MODEL),
                   pl.BlockSpec((tm // LOGIT_TILE, N_EXPERTS, LOGIT_TILE), lambda i: (i, 0, 0))],
        out_shape=[jax.ShapeDtypeStruct((stream.tokens, D_MODEL), F32),
                   jax.ShapeDtypeStruct((stream.tokens, D_MODEL), BF16),
                   jax.ShapeDtypeStruct((stream.tokens // LOGIT_TILE, N_EXPERTS, LOGIT_TILE), F32)],
        name=f"mix_out_l{layer}_s{stream.seq}",
    )(*mixed, w_out, x, mods, g2_row, mods, mods, wr_t)


def _sort_desc_lanes(x):
    rows, n = x.shape
    tiles = [x[:, c * LANES:(c + 1) * LANES] for c in range(n // LANES)]
    lane = lax.broadcasted_iota(I32, (rows, LANES), 1)
    k = 2
    while k <= n:
        j = k // 2
        while j >= 1:
            if j < LANES:
                lower = (lane & j) == 0
                for c in range(len(tiles)):
                    t = tiles[c]
                    partner = jnp.where(lower, pltpu.roll(t, LANES - j, axis=1), pltpu.roll(t, j, axis=1))
                    desc = ((lane & k) == 0) if k < LANES else (((c * LANES) & k) == 0)
                    take_max = (lower == desc) if k < LANES else (lower if desc else jnp.logical_not(lower))
                    tiles[c] = jnp.where(take_max, jnp.maximum(t, partner), jnp.minimum(t, partner))
            else:
                jc = j // LANES
                new = list(tiles)
                for c in range(len(tiles)):
                    take_max = ((c & jc) == 0) == (((c * LANES) & k) == 0)
                    new[c] = (jnp.maximum if take_max else jnp.minimum)(tiles[c], tiles[c ^ jc])
                tiles = new
            j //= 2
        k *= 2
    return tiles


def _router_kernel(cap, lg_ref, pos_ref, g_ref):
    x = lg_ref[...]
    n_b, n_e, n_tok = x.shape
    e = jnp.exp(x - jnp.max(x, axis=1, keepdims=True))
    aff = (e / jnp.sum(e, axis=1, keepdims=True)).reshape(n_b * n_e, n_tok)
    srt = _sort_desc_lanes(aff)
    lane_k = (cap - 1) % LANES
    thr = srt[(cap - 1) // LANES][:, lane_k:lane_k + 1]
    gt = aff > thr
    eq = aff == thr
    n_gt = jnp.sum(gt.astype(F32), axis=1, keepdims=True)
    before = (lax.broadcasted_iota(I32, (n_tok, n_tok), 0)
              < lax.broadcasted_iota(I32, (n_tok, n_tok), 1)).astype(BF16)
    eq_rank = _dot(eq.astype(BF16), before)
    sel = gt | (eq & (eq_rank < cap - n_gt))
    slot = _dot(sel.astype(BF16), before).astype(I32)
    pos_ref[...] = jnp.where(sel, slot, -1).reshape(n_b, n_e, n_tok)
    g_ref[...] = jnp.where(sel, aff, 0.0).reshape(n_b, n_e, n_tok)


def _router(logits, stream):
    shape = (stream.n_req, N_EXPERTS, stream.seq)
    spec = pl.BlockSpec(shape, lambda i: (0, 0, 0))
    return _deferred(
        functools.partial(_router_kernel, stream.cap), grid=(1,),
        in_specs=[spec], out_specs=[spec, spec],
        out_shape=[jax.ShapeDtypeStruct(shape, I32), jax.ShapeDtypeStruct(shape, F32)],
        name=f"router_s{stream.seq}",
    )(logits)


def _gather_kernel(cap, seq, h_ref, pos_ref, xs_ref):
    slot = lax.broadcasted_iota(I32, (cap, seq), 0)
    for r in range(pos_ref.shape[0]):
        onehot = jnp.concatenate([(slot == pos_ref[r, e:e + 1, :]).astype(BF16) for e in range(N_EXPERTS)], axis=0)
        xs = _dot(onehot, h_ref[r * seq:(r + 1) * seq, :]).astype(BF16)
        for e in range(N_EXPERTS):
            xs_ref[e, r * cap:(r + 1) * cap, :] = xs[e * cap:(e + 1) * cap]


def _gather(h2, pos, stream):
    cap, seq, per = stream.cap, stream.seq, stream.req_per_step
    return _deferred(
        functools.partial(_gather_kernel, cap, seq),
        grid=(stream.n_req // per,),
        in_specs=[pl.BlockSpec((per * seq, D_MODEL), lambda i: (i, 0)),
                  pl.BlockSpec((per, N_EXPERTS, seq), lambda i: (i, 0, 0))],
        out_specs=pl.BlockSpec((N_EXPERTS, per * cap, D_MODEL), lambda i: (0, i, 0)),
        out_shape=jax.ShapeDtypeStruct((N_EXPERTS, stream.n_req * cap, D_MODEL), BF16),
        name=f"gather_s{seq}",
    )(h2, pos)


def _ffn_kernel(xa_ref, xb_ref, wg_ref, wu_ref, wd_ref, ya_ref, yb_ref, acc):
    j = pl.program_id(1)
    rows_a = xa_ref.shape[1]

    @pl.when(j == 0)
    def _():
        acc[...] = jnp.zeros_like(acc)

    x = jnp.concatenate([xa_ref[0], xb_ref[0]], axis=0)
    a = _dot(x, wg_ref[0, 0].astype(BF16))
    u = _dot(x, wu_ref[0, 0].astype(BF16))
    acc[...] += _dot(((a * jax.nn.sigmoid(a)) * u).astype(BF16), wd_ref[0, 0].astype(BF16))

    @pl.when(j == pl.num_programs(1) - 1)
    def _():
        ya_ref[0] = acc[0:rows_a, :].astype(BF16)
        yb_ref[0] = acc[rows_a:, :].astype(BF16)


def _ffn(xs_a, xs_b, layer, w_gate, w_up, w_down):
    tf = FF_TILE
    rows_a, rows_b = xs_a.shape[1], xs_b.shape[1]
    x_spec = lambda rows: pl.BlockSpec((1, rows, D_MODEL), lambda e, j: (e, 0, 0))
    return pl.pallas_call(
        _ffn_kernel,
        grid=(N_EXPERTS, EXPERT_FF // tf),
        in_specs=[x_spec(rows_a), x_spec(rows_b),
                  pl.BlockSpec((1, 1, D_MODEL, tf), lambda e, j: (layer, e, 0, j)),
                  pl.BlockSpec((1, 1, D_MODEL, tf), lambda e, j: (layer, e, 0, j)),
                  pl.BlockSpec((1, 1, tf, D_MODEL), lambda e, j: (layer, e, j, 0))],
        out_specs=[x_spec(rows_a), x_spec(rows_b)],
        out_shape=[jax.ShapeDtypeStruct(xs_a.shape, BF16), jax.ShapeDtypeStruct(xs_b.shape, BF16)],
        scratch_shapes=[pltpu.VMEM((rows_a + rows_b, D_MODEL), F32)],
        compiler_params=_params(2, FFN_VMEM_LIMIT_BYTES),
        name=f"ffn_l{layer}",
    )(xs_a, xs_b, w_gate, w_up, w_down)


def _scatter_kernel(cap, seq, final, y_ref, pos_ref, g_ref, x_ref, gate_ref, fg_ref, o_ref):
    slot = lax.broadcasted_iota(I32, (cap, seq), 0)
    for r in range(pos_ref.shape[0]):
        onehots, gated = [], []
        for e in range(N_EXPERTS):
            hit = slot == pos_ref[r, e:e + 1, :]
            gate = jnp.sum(jnp.where(hit, g_ref[r, e:e + 1, :], 0.0), axis=1, keepdims=True)
            gated.append((y_ref[e, r * cap:(r + 1) * cap, :].astype(F32) * gate).astype(BF16))
            onehots.append(hit.astype(BF16))
        moe = lax.dot_general(jnp.concatenate(onehots, axis=0), jnp.concatenate(gated, axis=0),
                              (((0,), (0,)), ((), ())), preferred_element_type=F32)
        rows = slice(r * seq, (r + 1) * seq)
        xn = x_ref[rows, :] + gate_ref[0] * moe
        if final:
            xn = xn * lax.rsqrt(jnp.mean(xn * xn, axis=-1, keepdims=True) + EPS) * fg_ref[...]
        o_ref[rows, :] = xn


def _scatter(y, pos, g, xn, mods, layer, stream, final, final_g_row):
    cap, seq, per = stream.cap, stream.seq, stream.req_per_step
    tok_blk = pl.BlockSpec((per * seq, D_MODEL), lambda i: (i, 0))
    sel_blk = pl.BlockSpec((per, N_EXPERTS, seq), lambda i: (i, 0, 0))
    return _deferred(
        functools.partial(_scatter_kernel, cap, seq, final),
        grid=(stream.n_req // per,),
        in_specs=[pl.BlockSpec((N_EXPERTS, per * cap, D_MODEL), lambda i: (0, i, 0)),
                  sel_blk, sel_blk, tok_blk,
                  _mod_spec(layer, stream, 5, per * seq),
                  pl.BlockSpec((1, D_MODEL), lambda i: (0, 0))],
        out_specs=tok_blk,
        out_shape=jax.ShapeDtypeStruct((stream.tokens, D_MODEL), F32),
        name=f"scatter_l{layer}_s{seq}",
    )(y, pos, g, xn, mods, final_g_row)


def _axial_rope_tables(rows, head_dim):
    f32 = np.float32
    row = np.repeat(np.arange(rows, dtype=f32), GRID_W)
    col = np.tile(np.arange(GRID_W, dtype=f32), rows)
    nf = head_dim // 4
    inv = (f32(ROPE_BASE) ** (-np.arange(nf, dtype=f32) / f32(nf))).astype(f32)
    ar = row[:, None] * inv[None]
    ac = col[:, None] * inv[None]
    ang = np.concatenate([ar, ar, ac, ac], axis=-1).astype(f32)
    return np.cos(ang).astype(f32), np.sin(ang).astype(f32)


def _block_diag(w):
    eye = jnp.eye(LRU_BLOCKS, dtype=w.dtype)
    return jnp.einsum('dnkj,nm->dnkmj', w, eye).reshape(2, LRU_WIDTH, LRU_WIDTH)


def kernel(x_prompt, x_sample, cache_win_k, cache_win_v, state_lru, cache_diff_k, cache_diff_v, c, c_ctx, ada_w, ada_b, norm_g, final_g, even_w_in, even_w_out, conv_w, conv_b, lru_wa, lru_ba, lru_wx, lru_bx, lru_lambda, win_sink, odd_w_in, odd_w_out, diff_lq1, diff_lk1, diff_lq2, diff_lk2, diff_subln_g, moe_router, moe_w_gate, moe_w_up, moe_w_down):
    cv_t = jnp.concatenate([c_ctx[None], c, jnp.zeros((COND_ROWS - N_COND, D_MODEL), F32)], axis=0).T
    mods = _adaln(cv_t, ada_w, ada_b).reshape(DEPTH * COND_ROWS, 1, 6 * D_MODEL)

    cos, sin = _axial_rope_tables(DEC_SEQ // GRID_W, WIN_HEAD_DIM)
    rope_win = rope_diff = (jnp.asarray(np.tile(cos, (1, 2))), jnp.asarray(np.tile(sin, (1, 2))))

    xs = [x_prompt.reshape(CTX.tokens, D_MODEL), x_sample.reshape(LAT.tokens, D_MODEL)]
    final_g_row = final_g.reshape(1, D_MODEL)
    outs = {}

    def both(make):
        return _run_each(make(0, CTX), make(1, LAT))

    for layer in range(DEPTH):
        idx = layer // 2
        even = layer % 2 == 0
        w_in = (even_w_in if even else odd_w_in)[idx]
        w_out = (even_w_out if even else odd_w_out)[idx]
        wr_t = moe_router[layer].T
        g1_row = norm_g[layer, 0].reshape(1, D_MODEL)
        g2_row = norm_g[layer, 1].reshape(1, D_MODEL)
        if even:
            wa = _block_diag(lru_wa[idx]).astype(BF16)
            wx = _block_diag(lru_wx[idx]).astype(BF16)
            (proj_c, win_k, win_v), (proj_d,) = _run_each(_proj_win(xs[0], g1_row, mods, layer, CTX, w_in),
                                                         _proj(xs[1], g1_row, mods, layer, LAT, w_in))
            projs = (proj_c, proj_d)
            h0s = (jnp.zeros((CTX.n_req, 2, LRU_WIDTH), F32), state_lru[:, idx])
            (y_c, h_fin), (y_d, _) = both(lambda si, st: _lru(
                projs[si], st, conv_w[idx], conv_b[idx].reshape(1, LRU_WIDTH), wa, wx,
                lru_ba[idx], lru_bx[idx], lru_lambda[idx], h0s[si]))
            pack = lambda t: t[:, idx].transpose(0, 2, 1, 3).reshape(DEC_BATCH, PAST_LEN, WIN_KV)
            (o_c,), (o_d,) = _run_each(
                _win_attn(proj_c, CTX, win_sink[idx]),
                _win_attn(proj_d, LAT, win_sink[idx], cache=(pack(cache_win_k), pack(cache_win_v)), rope=rope_win))
            outs["win_k"] = win_k[:, None]
            outs["win_v"] = win_v[:, None]
            outs["lru"] = h_fin[:, None]
            mixed = ([y_c, o_c], [y_d, o_d])
        else:
            lambda_init = 0.8 - 0.6 * math.exp(-0.3 * layer)
            vec = lambda t: t[idx].reshape(1, DIFF_HEAD_DIM)
            args = (vec(diff_lq1), vec(diff_lk1), vec(diff_lq2), vec(diff_lk2), diff_subln_g[idx])
            (q, kh, vh), (proj_d,) = _run_each(_proj_heads(xs[0], g1_row, mods, layer, CTX, w_in),
                                              _proj(xs[1], g1_row, mods, layer, LAT, w_in))
            (o_c,), (o_d,) = _run_each(
                _diff_attn((q, kh, vh), CTX, lambda_init, *args),
                _diff_attn(proj_d, LAT, lambda_init, *args,
                           cache=(cache_diff_k[:, idx:idx + 1], cache_diff_v[:, idx:idx + 1]), rope=rope_diff))
            outs["diff_k"] = kh[:, None]
            outs["diff_v"] = vh[:, None]
            mixed = ([o_c], [o_d])

        mix = both(lambda si, st: _mix_out(mixed[si], w_out, xs[si], g2_row, mods, layer, st, wr_t))
        xns, h2s = (mix[0][0], mix[1][0]), (mix[0][1], mix[1][1])

        def request_major(lg, st):
            per_req = st.seq // LOGIT_TILE
            lg = lg.reshape(st.n_req, per_req, N_EXPERTS, LOGIT_TILE).transpose(0, 2, 1, 3)
            return lg.reshape(st.n_req, N_EXPERTS, st.seq)

        lgs = (request_major(mix[0][2], CTX), request_major(mix[1][2], LAT))
        routes = both(lambda si, st: _router(lgs[si], st))
        (rows_c,), (rows_d,) = both(lambda si, st: _gather(h2s[si], routes[si][0], st))
        ys = _ffn(rows_c, rows_d, layer, moe_w_gate, moe_w_up, moe_w_down)
        final = layer == DEPTH - 1
        (x_c,), (x_d,) = both(lambda si, st: _scatter(ys[si], routes[si][0], routes[si][1], xns[si], mods, layer, st,
                                                      final, final_g_row))
        xs = [x_c, x_d]


    y_prompt = xs[0].reshape(BATCH, SEQ, D_MODEL)
    y_sample = xs[1].reshape(DEC_BATCH, DEC_SEQ, D_MODEL)
    return (y_prompt, y_sample, outs["win_k"], outs["win_v"], outs["lru"], outs["diff_k"], outs["diff_v"])
```

```python
import functools
import math
from typing import Callable, NamedTuple

import jax
import jax.numpy as jnp
import numpy as np
from jax import lax
from jax.experimental import pallas as pl
from jax.experimental.pallas import tpu as pltpu

F32 = jnp.float32
BF16 = jnp.bfloat16
I32 = jnp.int32

D_MODEL = 1024
BATCH = 16
SEQ = 256
DEPTH = 2
DEC_BATCH = 2
DEC_SEQ = 1024
PAST_LEN = 512
GRID_W = 64
LRU_WIDTH = D_MODEL // 2
LRU_BLOCKS = 8
LRU_BLOCK = LRU_WIDTH // LRU_BLOCKS
CONV_W = 4
LRU_C = 8.0
WIN_HEADS = 8
WIN_KV_HEADS = 2
WIN_REP = WIN_HEADS // WIN_KV_HEADS
WIN_HEAD_DIM = 64
WINDOW = 128
WIN_Q = WIN_HEADS * WIN_HEAD_DIM
WIN_KV = WIN_KV_HEADS * WIN_HEAD_DIM
EVEN_IN = 2 * LRU_WIDTH + WIN_Q + 2 * WIN_KV
DIFF_HEADS = 8
DIFF_HEAD_DIM = 64
DIFF_QK = DIFF_HEADS * 2 * DIFF_HEAD_DIM
DIFF_V = DIFF_HEADS * 2 * DIFF_HEAD_DIM
ODD_IN = 2 * DIFF_QK + DIFF_V
N_EXPERTS = 16
EXPERT_FF = 2 * D_MODEL
CAPACITY_FACTOR = 2
ROPE_BASE = 10000.0
EPS = 1e-6
NEG_INF = -1e30

LANES = 128
SUBLANES = 8
VMEM_LIMIT_BYTES = 48 * 1024 * 1024

N_COND = 1 + DEC_BATCH
COND_ROWS = SUBLANES
ADALN_TILE = 2048
TOKEN_TILE = 512
EVEN_TOKEN_TILE = 1024
MIX_TILE = 1024
ROUTE_ROWS = 1024
LOGIT_TILE = 256
FF_TILE = 1024
FFN_VMEM_LIMIT_BYTES = 56 * 1024 * 1024
LRU_CHUNK = 256
WIN_BLOCKS_PER_ITER = 2
ATT_Q_TILE = 256
ATT_K_TILE = 256


class Stream:
    def __init__(self, n_req, seq, cond0, cond_step):
        self.n_req, self.seq, self.cond0, self.cond_step = n_req, seq, cond0, cond_step
        self.tokens = n_req * seq
        self.cap = CAPACITY_FACTOR * seq // N_EXPERTS
        self.req_per_step = max(1, ROUTE_ROWS // seq)

    def cond_of_row(self, row):
        return self.cond0 + self.cond_step * (row // self.seq)


CTX = Stream(BATCH, SEQ, 0, 0)
LAT = Stream(DEC_BATCH, DEC_SEQ, 1, 1)


def _params(n_axes, vmem_limit_bytes=VMEM_LIMIT_BYTES):
    return pltpu.CompilerParams(dimension_semantics=("arbitrary",) * n_axes,
                                vmem_limit_bytes=vmem_limit_bytes)


class Call(NamedTuple):
    kernel: Callable
    steps: int
    in_specs: tuple
    out_specs: tuple
    out_shapes: tuple
    scratch: tuple
    args: tuple
    name: str


def _deferred(kernel, *, grid, in_specs, out_specs, out_shape, scratch_shapes=(), name):
    (steps,) = grid
    as_tuple = lambda v: tuple(v) if isinstance(v, (list, tuple)) else (v,)
    return lambda *args: Call(kernel, steps, tuple(in_specs), as_tuple(out_specs), as_tuple(out_shape),
                              tuple(scratch_shapes), args, name)


def _run(call):
    return pl.pallas_call(
        call.kernel, grid=(call.steps,), in_specs=list(call.in_specs), out_specs=list(call.out_specs),
        out_shape=list(call.out_shapes), scratch_shapes=list(call.scratch),
        compiler_params=_params(1), name=call.name)(*call.args)


def _run_each(a, b):
    return _run(a), _run(b)


def _resident_spec(shape):
    return pl.BlockSpec(shape, lambda i: (0, 0), pipeline_mode=pl.Buffered(1))


def _mod_spec(layer, stream, k, rows_per_step):
    return pl.BlockSpec(
        (1, 1, D_MODEL),
        lambda i, *_: (layer * COND_ROWS + stream.cond_of_row(i * rows_per_step), 0, k))


def _dot(a, b):
    return jnp.dot(a, b, preferred_element_type=F32)


def _dot_nt(a, b):
    return lax.dot_general(a, b, (((1,), (1,)), ((), ())), preferred_element_type=F32)


def _modnorm(x, g, shift, scale):
    y = x * lax.rsqrt(jnp.mean(x * x, axis=-1, keepdims=True) + EPS)
    return (y * g) * (1.0 + scale) + shift


def _lane_half_masks(shape):
    lane = lax.broadcasted_iota(I32, shape, len(shape) - 1)
    left = (lane & (LANES - 1)) < LANES // 2
    return left, jnp.logical_not(left)


def _rope(x, cos, sin):
    parts = []
    for c in range(x.shape[1] // LANES):
        xs = x[:, c * LANES:(c + 1) * LANES]
        lane = lax.broadcasted_iota(I32, xs.shape, 1)
        first = (lane & 31) < 16
        rot = jnp.where(first, -pltpu.roll(xs, LANES - 16, axis=1), pltpu.roll(xs, 16, axis=1))
        parts.append(xs * cos + rot * sin)
    return parts[0] if len(parts) == 1 else jnp.concatenate(parts, axis=1)


def _adaln_kernel(cv_ref, w_ref, b_ref, o_ref):
    cv = cv_ref[...]
    s = cv * jax.nn.sigmoid(cv)
    w = w_ref[0]
    ridx = lax.broadcasted_iota(I32, (COND_ROWS, w.shape[1]), 0)
    out = jnp.zeros((COND_ROWS, w.shape[1]), F32)
    for r in range(N_COND):
        out = jnp.where(ridx == r, jnp.sum(w * s[:, r:r + 1], axis=0, keepdims=True), out)
    o_ref[0] = out + b_ref[0]


def _adaln(cv_t, ada_w, ada_b):
    tn = ADALN_TILE
    return pl.pallas_call(
        _adaln_kernel,
        grid=(DEPTH, 6 * D_MODEL // tn),
        in_specs=[pl.BlockSpec((D_MODEL, COND_ROWS), lambda l, j: (0, 0)),
                  pl.BlockSpec((1, D_MODEL, tn), lambda l, j: (l, 0, j)),
                  pl.BlockSpec((1, 1, tn), lambda l, j: (l, 0, j))],
        out_specs=pl.BlockSpec((1, COND_ROWS, tn), lambda l, j: (l, 0, j)),
        out_shape=jax.ShapeDtypeStruct((DEPTH, COND_ROWS, 6 * D_MODEL), F32),
        compiler_params=_params(2),
        name="adaln",
    )(cv_t, ada_w, ada_b.reshape(DEPTH, 1, 6 * D_MODEL))


def _proj_kernel(x_ref, g_ref, sh_ref, sc_ref, w_ref, o_ref):
    h = _modnorm(x_ref[...], g_ref[...], sh_ref[0], sc_ref[0])
    o_ref[...] = _dot(h.astype(BF16), w_ref[...].astype(BF16))


def _proj(x, g_row, mods, layer, stream, w, tm=TOKEN_TILE):
    n_out = w.shape[1]
    return _deferred(
        _proj_kernel,
        grid=(stream.tokens // tm,),
        in_specs=[pl.BlockSpec((tm, D_MODEL), lambda i: (i, 0)),
                  pl.BlockSpec((1, D_MODEL), lambda i: (0, 0)),
                  _mod_spec(layer, stream, 0, tm),
                  _mod_spec(layer, stream, 1, tm),
                  _resident_spec((D_MODEL, n_out))],
        out_specs=pl.BlockSpec((tm, n_out), lambda i: (i, 0)),
        out_shape=jax.ShapeDtypeStruct((stream.tokens, n_out), F32),
        name=f"proj_l{layer}_s{stream.seq}",
    )(x, g_row, mods, mods, w)


def _proj_heads_kernel(seq, x_ref, g_ref, sh_ref, sc_ref, w_ref, q_ref, k_ref, v_ref):
    h = _modnorm(x_ref[...], g_ref[...], sh_ref[0], sc_ref[0])
    res = _dot(h.astype(BF16), w_ref[...].astype(BF16))
    q_ref[...] = res[:, 0:DIFF_QK].astype(BF16)
    for r in range(x_ref.shape[0] // seq):
        rows = slice(r * seq, (r + 1) * seq)
        for hh in range(DIFF_HEADS):
            k_ref[r, 0, hh] = res[rows, DIFF_QK + hh * LANES:DIFF_QK + (hh + 1) * LANES]
            v_ref[r, 0, hh] = res[rows, 2 * DIFF_QK + hh * LANES:2 * DIFF_QK + (hh + 1) * LANES]


def _proj_heads(x, g_row, mods, layer, stream, w):
    tm, seq = TOKEN_TILE, stream.seq
    head_shape = (stream.n_req, 1, DIFF_HEADS, seq, 2 * DIFF_HEAD_DIM)
    head_spec = pl.BlockSpec((tm // seq, 1, DIFF_HEADS, seq, 2 * DIFF_HEAD_DIM), lambda i: (i, 0, 0, 0, 0))
    return _deferred(
        functools.partial(_proj_heads_kernel, seq),
        grid=(stream.tokens // tm,),
        in_specs=[pl.BlockSpec((tm, D_MODEL), lambda i: (i, 0)),
                  pl.BlockSpec((1, D_MODEL), lambda i: (0, 0)),
                  _mod_spec(layer, stream, 0, tm),
                  _mod_spec(layer, stream, 1, tm),
                  _resident_spec((D_MODEL, ODD_IN))],
        out_specs=[pl.BlockSpec((tm, DIFF_QK), lambda i: (i, 0)), head_spec, head_spec],
        out_shape=[jax.ShapeDtypeStruct((stream.tokens, DIFF_QK), BF16),
                   jax.ShapeDtypeStruct(head_shape, F32), jax.ShapeDtypeStruct(head_shape, F32)],
        name=f"proj_heads_l{layer}_s{seq}",
    )(x, g_row, mods, mods, w)


def _proj_win_kernel(seq, x_ref, g_ref, sh_ref, sc_ref, w_ref, o_ref, k_ref, v_ref):
    h = _modnorm(x_ref[...], g_ref[...], sh_ref[0], sc_ref[0])
    res = _dot(h.astype(BF16), w_ref[...].astype(BF16))
    o_ref[...] = res
    k0 = 2 * LRU_WIDTH + WIN_Q
    for r in range(x_ref.shape[0] // seq):
        rows = slice(r * seq, (r + 1) * seq)
        for g in range(WIN_KV_HEADS):
            k_ref[r, 0, g] = res[rows, k0 + g * WIN_HEAD_DIM:k0 + (g + 1) * WIN_HEAD_DIM]
            v_ref[r, 0, g] = res[rows, k0 + WIN_KV + g * WIN_HEAD_DIM:k0 + WIN_KV + (g + 1) * WIN_HEAD_DIM]


def _proj_win(x, g_row, mods, layer, stream, w):
    tm, seq = EVEN_TOKEN_TILE, stream.seq
    head_shape = (stream.n_req, 1, WIN_KV_HEADS, seq, WIN_HEAD_DIM)
    head_spec = pl.BlockSpec((tm // seq, 1, WIN_KV_HEADS, seq, WIN_HEAD_DIM), lambda i: (i, 0, 0, 0, 0))
    return _deferred(
        functools.partial(_proj_win_kernel, seq),
        grid=(stream.tokens // tm,),
        in_specs=[pl.BlockSpec((tm, D_MODEL), lambda i: (i, 0)),
                  pl.BlockSpec((1, D_MODEL), lambda i: (0, 0)),
                  _mod_spec(layer, stream, 0, tm),
                  _mod_spec(layer, stream, 1, tm),
                  _resident_spec((D_MODEL, EVEN_IN))],
        out_specs=[pl.BlockSpec((tm, EVEN_IN), lambda i: (i, 0)), head_spec, head_spec],
        out_shape=[jax.ShapeDtypeStruct((stream.tokens, EVEN_IN), F32),
                   jax.ShapeDtypeStruct(head_shape, F32), jax.ShapeDtypeStruct(head_shape, F32)],
        name=f"proj_win_l{layer}_s{seq}",
    )(x, g_row, mods, mods, w)


def _lru_kernel(seq, xl_ref, gl_ref, cw_ref, cb_ref, wa_ref, wx_ref, ba_ref, bx_ref, lam_ref, h0_ref,
                y_ref, hfin_ref, xe, af, bf, ab, bb):
    width = LRU_WIDTH
    ch = LRU_CHUNK
    seg = seq // SUBLANES
    lead = (CONV_W // 2) * SUBLANES

    def to_segment_major(x):
        return jnp.transpose(x.reshape(SUBLANES, seg, LANES), (1, 0, 2)).reshape(seq, LANES)

    def to_time_major(x):
        return jnp.transpose(x.reshape(seg, SUBLANES, LANES), (1, 0, 2)).reshape(seq, LANES)

    for c in range(width // LANES):
        cols = slice(c * LANES, (c + 1) * LANES)
        xe[lead:lead + seq, cols] = to_segment_major(xl_ref[:, cols])
    sub = lax.broadcasted_iota(I32, (SUBLANES, width), 0)
    for k in range(CONV_W // 2):
        prev = xe[lead + (seg - 2 + k) * SUBLANES:lead + (seg - 1 + k) * SUBLANES, :]
        xe[k * SUBLANES:(k + 1) * SUBLANES, :] = jnp.where(sub >= 1, pltpu.roll(prev, 1, axis=0), 0.0)
    first = xe[lead:lead + SUBLANES, :]
    xe[lead + seq:lead + seq + SUBLANES, :] = jnp.where(sub < SUBLANES - 1,
                                                        pltpu.roll(first, SUBLANES - 1, axis=0), 0.0)

    lam = lam_ref[...]
    z = -lam
    softplus = jnp.maximum(z, 0.0) + jnp.log1p(jnp.exp(-jnp.abs(z)))
    cw = cw_ref[...]
    cb = cb_ref[...]

    def gates_chunk(c, carry):
        r0 = pl.multiple_of(c * ch, ch)
        xc = xe[pl.ds(r0, ch), :] * cw[0:1]
        for j in range(1, CONV_W):
            xc = xc + xe[pl.ds(pl.multiple_of(r0 + j * SUBLANES, SUBLANES), ch), :] * cw[j:j + 1]
        xc = xc + cb
        xcb = xc.astype(BF16)
        for d, (a_s, b_s) in enumerate(((af, bf), (ab, bb))):
            r = jax.nn.sigmoid(_dot(xcb, wa_ref[d]) + ba_ref[d:d + 1])
            ig = jax.nn.sigmoid(_dot(xcb, wx_ref[d]) + bx_ref[d:d + 1])
            log_a = (-LRU_C * r) * softplus[d:d + 1]
            a = jnp.exp(log_a)
            v = 1.0 - a * a
            a_s[pl.ds(r0, ch), :] = a
            b_s[pl.ds(r0, ch), :] = jnp.where(v > 0.0, v * lax.rsqrt(v), 0.0) * (ig * xc)
        return carry

    lax.fori_loop(0, seq // ch, gates_chunk, 0)

    def scan_step(k, carry):
        hf, pf, hb, pb = carry
        rf = pl.multiple_of(k * SUBLANES, SUBLANES)
        rb = pl.multiple_of((seg - 1 - k) * SUBLANES, SUBLANES)
        a = af[pl.ds(rf, SUBLANES), :]
        hf = a * hf + bf[pl.ds(rf, SUBLANES), :]
        pf = a * pf
        bf[pl.ds(rf, SUBLANES), :] = hf
        af[pl.ds(rf, SUBLANES), :] = pf
        a = ab[pl.ds(rb, SUBLANES), :]
        hb = a * hb + bb[pl.ds(rb, SUBLANES), :]
        pb = a * pb
        bb[pl.ds(rb, SUBLANES), :] = hb
        ab[pl.ds(rb, SUBLANES), :] = pb
        return hf, pf, hb, pb

    zeros = jnp.zeros((SUBLANES, width), F32)
    ones = jnp.ones((SUBLANES, width), F32)
    hf, pf, hb, pb = lax.fori_loop(0, seg, scan_step, (zeros, ones, zeros, ones))

    h0 = h0_ref[0]
    state, carry_f = h0[0:1], zeros
    for g in range(SUBLANES):
        carry_f = jnp.where(sub == g, state, carry_f)
        state = pf[g:g + 1] * state + hf[g:g + 1]
    hfin_ref[0, 0:1, :] = state
    state, carry_b = h0[1:2], zeros
    for g in reversed(range(SUBLANES)):
        carry_b = jnp.where(sub == g, state, carry_b)
        state = pb[g:g + 1] * state + hb[g:g + 1]
    hfin_ref[0, 1:2, :] = state

    groups = (ch // SUBLANES, SUBLANES, width)

    def combine_chunk(c, carry):
        r0 = pl.multiple_of(c * ch, ch)
        rows = pl.ds(r0, ch)
        h_fwd = bf[rows, :].reshape(groups) + af[rows, :].reshape(groups) * carry_f
        h_bwd = bb[rows, :].reshape(groups) + ab[rows, :].reshape(groups) * carry_b
        bf[rows, :] = (h_fwd + h_bwd).reshape(ch, width)
        return carry

    lax.fori_loop(0, seq // ch, combine_chunk, 0)

    for c in range(width // LANES):
        cols = slice(c * LANES, (c + 1) * LANES)
        y_ref[:, cols] = (to_time_major(bf[:, cols]) * jax.nn.gelu(gl_ref[:, cols])).astype(BF16)


def _lru(proj, stream, conv_w, conv_b, wa, wx, ba, bx, lam, h0):
    seq, width = stream.seq, LRU_WIDTH
    full2 = lambda b: (0, 0)
    full3 = lambda b: (0, 0, 0)
    return _deferred(
        functools.partial(_lru_kernel, seq),
        grid=(stream.n_req,),
        in_specs=[pl.BlockSpec((seq, width), lambda b: (b, 0)),
                  pl.BlockSpec((seq, width), lambda b: (b, 1)),
                  pl.BlockSpec((CONV_W, width), full2),
                  pl.BlockSpec((1, width), full2),
                  pl.BlockSpec((2, width, width), full3),
                  pl.BlockSpec((2, width, width), full3),
                  pl.BlockSpec((2, width), full2),
                  pl.BlockSpec((2, width), full2),
                  pl.BlockSpec((2, width), full2),
                  pl.BlockSpec((1, 2, width), lambda b: (b, 0, 0))],
        out_specs=[pl.BlockSpec((seq, width), lambda b: (b, 0)),
                   pl.BlockSpec((1, 2, width), lambda b: (b, 0, 0))],
        out_shape=[jax.ShapeDtypeStruct((stream.tokens, width), BF16),
                   jax.ShapeDtypeStruct((stream.n_req, 2, width), F32)],
        scratch_shapes=[pltpu.VMEM((seq + (CONV_W - 1) * SUBLANES, width), F32)] + [pltpu.VMEM((seq, width), F32)] * 4,
        name=f"lru_s{seq}",
    )(proj, proj, conv_w, conv_b, wa, wx, ba, bx, lam, h0)


def _attend(q, chunks, s_ref):
    tile_max = None
    spans = []
    off = 0
    for keys, _ in chunks:
        s = _dot_nt(q, keys())
        n = s.shape[1]
        s_ref[:, off:off + n] = s
        for c in range(n // LANES):
            t = s[:, c * LANES:(c + 1) * LANES]
            tile_max = t if tile_max is None else jnp.maximum(tile_max, t)
        spans.append((off, n))
        off += n
    m = jnp.max(tile_max, axis=-1, keepdims=True)
    acc = None
    for (_, values), (o, n) in zip(chunks, spans):
        part = _dot(jnp.exp(s_ref[:, o:o + n] - m).astype(BF16), values())
        acc = part if acc is None else acc + part
    return acc


def _split_groups(kk):
    left, right = _lane_half_masks(kk.shape)
    g0_l = jnp.where(left, kk, 0.0)
    g1_r = jnp.where(right, kk, 0.0)
    return ((g0_l, pltpu.roll(g0_l, LANES // 2, axis=1)), (pltpu.roll(g1_r, LANES // 2, axis=1), g1_r))


def _win_ctx_kernel(sink_ref, q_ref, kv_ref, o_ref):
    scale = WIN_HEAD_DIM ** -0.5
    seq = q_ref.shape[0]
    ks = _split_groups(kv_ref[:, 0:LANES])
    vs = _split_groups(kv_ref[:, LANES:2 * LANES])
    top = lax.broadcasted_iota(I32, (2 * seq, 1), 0) < seq
    outs = [None] * (WIN_HEADS // 2)
    for g in range(WIN_KV_HEADS):
        pairs = (2 * g, 2 * g + 1)
        qs = jnp.concatenate([q_ref[:, p * LANES:(p + 1) * LANES] for p in pairs], axis=0)
        qs = (qs * scale).astype(BF16)
        for side in range(2):
            sk = jnp.where(top, sink_ref[2 * pairs[0] + side], sink_ref[2 * pairs[1] + side])
            s = _dot_nt(qs, ks[g][side].astype(BF16))
            m = jnp.maximum(jnp.max(s, axis=-1, keepdims=True), sk)
            e = jnp.exp(s - m)
            den = jnp.sum(e, axis=-1, keepdims=True) + jnp.exp(sk - m)
            o = _dot(e.astype(BF16), vs[g][side].astype(BF16)) * (1.0 / den)
            for k, p in enumerate(pairs):
                part = o[k * seq:(k + 1) * seq]
                outs[p] = part if outs[p] is None else outs[p] + part
    for p in range(WIN_HEADS // 2):
        o_ref[:, p * LANES:(p + 1) * LANES] = outs[p].astype(BF16)


def _win_lat_kernel(sink_ref, q_ref, kv_ref, ck_ref, cv_ref, cos_ref, sin_ref, o_ref,
                    kl_s, vl_s, kc_s, vc_s):
    seq, wn = DEC_SEQ, WINDOW
    scale = WIN_HEAD_DIM ** -0.5
    kr = _rope(kv_ref[:, 0:LANES], cos_ref[...], sin_ref[...])
    ck, cv = (jnp.concatenate([ref[0, 0, g] for g in range(WIN_KV_HEADS)], axis=1) for ref in (ck_ref, cv_ref))
    for src, dst in ((_split_groups(kr), kl_s), (_split_groups(kv_ref[:, LANES:2 * LANES]), vl_s),
                     (_split_groups(ck), kc_s), (_split_groups(cv), vc_s)):
        for g in range(WIN_KV_HEADS):
            for side in range(2):
                dst[2 * g + side] = src[g][side].astype(BF16)

    top = lax.broadcasted_iota(I32, (2 * wn, 1), 0) < wn

    def q_block(i, carry):
        for sub in range(WIN_BLOCKS_PER_ITER):
            blk = WIN_BLOCKS_PER_ITER * i + sub
            r0 = pl.multiple_of(blk * wn, wn)
            start = pl.multiple_of(jnp.clip((blk - 1) * wn, 0, seq - 3 * wn), wn)
            qr = _rope(q_ref[pl.ds(r0, wn), :], cos_ref[pl.ds(r0, wn), :], sin_ref[pl.ds(r0, wn), :]) * scale
            qpos = r0 + (lax.broadcasted_iota(I32, (2 * wn, 3 * wn), 0) & (wn - 1))
            kpos = start + lax.broadcasted_iota(I32, (2 * wn, 3 * wn), 1)
            valid = jnp.abs(qpos - kpos) <= wn
            outs = [None] * (WIN_HEADS // 2)
            for g in range(WIN_KV_HEADS):
                pairs = (2 * g, 2 * g + 1)
                qs = jnp.concatenate([qr[:, p * LANES:(p + 1) * LANES] for p in pairs], axis=0).astype(BF16)
                for side in range(2):
                    idx = 2 * g + side
                    sk = jnp.where(top, sink_ref[2 * pairs[0] + side], sink_ref[2 * pairs[1] + side])
                    sl = _dot_nt(qs, kl_s[idx, pl.ds(start, 3 * wn), :])
                    sl = jnp.where(valid, sl, NEG_INF)
                    sc = _dot_nt(qs, kc_s[idx])
                    m = jnp.maximum(jnp.maximum(jnp.max(sl, axis=-1, keepdims=True),
                                                jnp.max(sc, axis=-1, keepdims=True)), sk)
                    el = jnp.exp(sl - m)
                    ec = jnp.exp(sc - m)
                    den = (jnp.sum(el, axis=-1, keepdims=True) + jnp.sum(ec, axis=-1, keepdims=True)
                           + jnp.exp(sk - m))
                    o = (_dot(el.astype(BF16), vl_s[idx, pl.ds(start, 3 * wn), :])
                         + _dot(ec.astype(BF16), vc_s[idx])) * (1.0 / den)
                    for k, p in enumerate(pairs):
                        part = o[k * wn:(k + 1) * wn]
                        outs[p] = part if outs[p] is None else outs[p] + part
            for p in range(WIN_HEADS // 2):
                o_ref[pl.ds(r0, wn), p * LANES:(p + 1) * LANES] = outs[p].astype(BF16)
        return carry

    lax.fori_loop(0, seq // (WIN_BLOCKS_PER_ITER * wn), q_block, 0)


def _win_attn(proj, stream, sink, cache=None, rope=None):
    seq = stream.seq
    q_spec = pl.BlockSpec((seq, WIN_Q), lambda b: (b, 2 * LRU_WIDTH // WIN_Q))
    kv_spec = pl.BlockSpec((seq, 2 * WIN_KV), lambda b: (b, (2 * LRU_WIDTH + WIN_Q) // (2 * WIN_KV)))
    sink_spec = pl.BlockSpec(memory_space=pltpu.SMEM)
    out_spec = pl.BlockSpec((seq, WIN_Q), lambda b: (b, 0))
    out_shape = jax.ShapeDtypeStruct((stream.tokens, WIN_Q), BF16)
    if cache is None:
        return _deferred(
            _win_ctx_kernel, grid=(stream.n_req,),
            in_specs=[sink_spec, q_spec, kv_spec], out_specs=out_spec, out_shape=out_shape,
            name="win_attn_ctx",
        )(sink, proj, proj)
    ck, cv, cache_layer = cache
    cos, sin = rope
    cache_spec = pl.BlockSpec((1, 1, WIN_KV_HEADS, PAST_LEN, WIN_HEAD_DIM), lambda b: (b, cache_layer, 0, 0, 0))
    table_spec = pl.BlockSpec((seq, LANES), lambda b: (0, 0))
    return _deferred(
        _win_lat_kernel, grid=(stream.n_req,),
        in_specs=[sink_spec, q_spec, kv_spec, cache_spec, cache_spec, table_spec, table_spec],
        out_specs=out_spec, out_shape=out_shape,
        scratch_shapes=[pltpu.VMEM((4, seq, LANES), BF16), pltpu.VMEM((4, seq, LANES), BF16),
                        pltpu.VMEM((4, PAST_LEN, LANES), BF16), pltpu.VMEM((4, PAST_LEN, LANES), BF16)],
        name="win_attn_lat",
    )(sink, proj, proj, ck, cv, cos, sin)


def _diff_lambda(lq1_ref, lk1_ref, lq2_ref, lk2_ref, lambda_init):
    t1 = jnp.sum(lq1_ref[...] * lk1_ref[...], axis=-1, keepdims=True)
    t2 = jnp.sum(lq2_ref[...] * lk2_ref[...], axis=-1, keepdims=True)
    return jnp.exp(t1) - jnp.exp(t2) + lambda_init


def _subln(o, g_row, lambda_init):
    o = o * lax.rsqrt(jnp.mean(o * o, axis=-1, keepdims=True) + EPS) * g_row
    return o * (1.0 - lambda_init)


def _component_keys(k):
    left, right = _lane_half_masks(k.shape)
    return jnp.where(left, k, 0.0).astype(BF16), jnp.where(right, k, 0.0).astype(BF16)


def _values_with_ones(v):
    return jnp.concatenate([v.astype(BF16), jnp.ones(v.shape, BF16)], axis=1)


def _diff_combine(accs, lam):
    o1, o2 = accs[0][:, 0:LANES], accs[1][:, 0:LANES]
    return o1 * (1.0 / accs[0][:, LANES:]) - o2 * (lam * (1.0 / accs[1][:, LANES:]))


def _diff_ctx_kernel(lambda_init, lq1_ref, lk1_ref, lq2_ref, lk2_ref, sg_ref, q_ref, k_ref, v_ref, o_ref, s_ref):
    scale = DIFF_HEAD_DIM ** -0.5
    tq = ATT_Q_TILE
    lam = _diff_lambda(lq1_ref, lk1_ref, lq2_ref, lk2_ref, lambda_init)
    for h in range(DIFF_HEADS):
        cols = slice(h * LANES, (h + 1) * LANES)
        keys = _component_keys(k_ref[0, 0, h])
        vals = _values_with_ones(v_ref[0, 0, h])
        for qt in range(q_ref.shape[0] // tq):
            rows = slice(qt * tq, (qt + 1) * tq)
            q = (q_ref[rows, cols] * scale).astype(BF16)
            slots = [s_ref.at[(4 * h + 2 * qt + c) % s_ref.shape[0]] for c in range(2)]
            accs = [_attend(q, [(lambda: keys[c], lambda: vals)], slots[c]) for c in range(2)]
            o = _diff_combine(accs, lam)
            o_ref[rows, cols] = _subln(o, sg_ref[h:h + 1, :], lambda_init).astype(BF16)


def _diff_lat_kernel(lambda_init, lq1_ref, lk1_ref, lq2_ref, lk2_ref, sg_ref, q_ref, k_ref, v_ref,
                     ck_ref, cv_ref, cos_ref, sin_ref, o_ref, kl_s, kc_s, vl_s, vc_s, s_ref):
    seq, tq, tk = DEC_SEQ, ATT_Q_TILE, ATT_K_TILE
    scale = DIFF_HEAD_DIM ** -0.5
    lam = _diff_lambda(lq1_ref, lk1_ref, lq2_ref, lk2_ref, lambda_init)
    for src, dst in ((_rope(k_ref[...], cos_ref[...], sin_ref[...]), kl_s), (ck_ref[0, 0, 0], kc_s)):
        dst[0], dst[1] = _component_keys(src)
    vl_s[...] = _values_with_ones(v_ref[...])
    vc_s[...] = _values_with_ones(cv_ref[0, 0, 0])
    sg = sg_ref[0]

    for qt in range(seq // tq):
        rows = slice(qt * tq, (qt + 1) * tq)
        qr = (_rope(q_ref[rows, :], cos_ref[rows, :], sin_ref[rows, :]) * scale).astype(BF16)
        accs = []
        for comp in range(2):
            chunks = [(lambda j=j: kl_s[comp, j * tk:(j + 1) * tk, :],
                       lambda j=j: vl_s[j * tk:(j + 1) * tk, :]) for j in range(seq // tk)]
            chunks += [(lambda j=j: kc_s[comp, j * tk:(j + 1) * tk, :],
                        lambda j=j: vc_s[j * tk:(j + 1) * tk, :]) for j in range(PAST_LEN // tk)]
            accs.append(_attend(qr, chunks, s_ref.at[2 * qt + comp]))
        o_ref[rows, :] = _subln(_diff_combine(accs, lam), sg, lambda_init).astype(BF16)


def _diff_attn(proj, stream, lambda_init, lq1, lk1, lq2, lk2, subln_g, cache=None, rope=None):
    seq = stream.seq
    vec_spec = lambda nd: pl.BlockSpec((1, DIFF_HEAD_DIM), lambda *_: (0, 0))
    out_shape = jax.ShapeDtypeStruct((stream.tokens, DIFF_V), BF16)
    if cache is None:
        q, kh, vh = proj
        head_spec = pl.BlockSpec((1, 1, DIFF_HEADS, seq, LANES), lambda b: (b, 0, 0, 0, 0))
        return _deferred(
            functools.partial(_diff_ctx_kernel, lambda_init), grid=(stream.n_req,),
            in_specs=[vec_spec(1)] * 4 + [pl.BlockSpec((DIFF_HEADS, LANES), lambda b: (0, 0)),
                                          pl.BlockSpec((seq, DIFF_QK), lambda b: (b, 0)), head_spec, head_spec],
            out_specs=pl.BlockSpec((seq, DIFF_V), lambda b: (b, 0)), out_shape=out_shape,
            scratch_shapes=[pltpu.VMEM((8, ATT_Q_TILE, seq), F32)],
            name="diff_attn_ctx",
        )(lq1, lk1, lq2, lk2, subln_g, q, kh, vh)
    ck, cv = cache
    cos, sin = rope
    nh = DIFF_HEADS
    blk = lambda c: pl.BlockSpec((seq, LANES), lambda i: (i // nh, c * nh + i % nh))
    cache_spec = pl.BlockSpec((1, 1, 1, PAST_LEN, LANES), lambda i: (i // nh, 0, i % nh, 0, 0))
    table_spec = pl.BlockSpec((seq, LANES), lambda i: (0, 0))
    return _deferred(
        functools.partial(_diff_lat_kernel, lambda_init), grid=(stream.n_req * nh,),
        in_specs=[vec_spec(2)] * 4 + [pl.BlockSpec((1, 1, LANES), lambda i: (i % nh, 0, 0)),
                                      blk(0), blk(1), blk(2), cache_spec, cache_spec, table_spec, table_spec],
        out_specs=pl.BlockSpec((seq, LANES), lambda i: (i // nh, i % nh)), out_shape=out_shape,
        scratch_shapes=[pltpu.VMEM((2, seq, LANES), BF16), pltpu.VMEM((2, PAST_LEN, LANES), BF16),
                        pltpu.VMEM((seq, 2 * LANES), BF16), pltpu.VMEM((PAST_LEN, 2 * LANES), BF16),
                        pltpu.VMEM((2 * seq // ATT_Q_TILE, ATT_Q_TILE, seq + PAST_LEN), F32)],
        name="diff_attn_lat",
    )(lq1, lk1, lq2, lk2, subln_g.reshape(DIFF_HEADS, 1, LANES), proj, proj, proj, ck, cv, cos, sin)


def _mix_out_kernel(n_in, *refs):
    a_refs = refs[:n_in]
    w_ref, x_ref, gate_ref, g2_ref, sh2_ref, sc2_ref, wr_ref, xn_ref, h2_ref, lg_ref = refs[n_in:]
    kp = D_MODEL // n_in
    wr = wr_ref[...]
    w_hi = wr.astype(BF16)
    w_lo = (wr - w_hi.astype(F32)).astype(BF16)
    acc = None
    for k, a_ref in enumerate(a_refs):
        part = _dot(a_ref[...], w_ref[k * kp:(k + 1) * kp, :].astype(BF16))
        acc = part if acc is None else acc + part
    xn = x_ref[...] + gate_ref[0] * acc
    xn_ref[...] = xn
    h2 = _modnorm(xn, g2_ref[...], sh2_ref[0], sc2_ref[0])
    h_hi = h2.astype(BF16)
    h2_ref[...] = h_hi
    h_lo = (h2 - h_hi.astype(F32)).astype(BF16)
    by_hi = _dot_nt(jnp.concatenate([w_hi, w_lo], axis=0), h_hi)
    lg = by_hi[0:N_EXPERTS] + (by_hi[N_EXPERTS:] + _dot_nt(w_hi, h_lo))
    for c in range(lg.shape[1] // LOGIT_TILE):
        lg_ref[c] = lg[:, c * LOGIT_TILE:(c + 1) * LOGIT_TILE]


def _mix_out(mixed, w_out, x, g2_row, mods, layer, stream, wr_t):
    tm = MIX_TILE
    n_in = len(mixed)
    kp = D_MODEL // n_in
    row_blk = lambda width: pl.BlockSpec((tm, width), lambda i: (i, 0))
    return _deferred(
        functools.partial(_mix_out_kernel, n_in),
        grid=(stream.tokens // tm,),
        in_specs=[row_blk(kp)] * n_in + [
            _resident_spec((D_MODEL, D_MODEL)),
            row_blk(D_MODEL),
            _mod_spec(layer, stream, 2, tm),
            pl.BlockSpec((1, D_MODEL), lambda i: (0, 0)),
            _mod_spec(layer, stream, 3, tm),
            _mod_spec(layer, stream, 4, tm),
            pl.BlockSpec((N_EXPERTS, D_MODEL), lambda i: (0, 0))],
        out_specs=[row_blk(D_MODEL), row_blk(D_MODEL),
                   pl.BlockSpec((tm // LOGIT_TILE, N_EXPERTS, LOGIT_TILE), lambda i: (i, 0, 0))],
        out_shape=[jax.ShapeDtypeStruct((stream.tokens, D_MODEL), F32),
                   jax.ShapeDtypeStruct((stream.tokens, D_MODEL), BF16),
                   jax.ShapeDtypeStruct((stream.tokens // LOGIT_TILE, N_EXPERTS, LOGIT_TILE), F32)],
        name=f"mix_out_l{layer}_s{stream.seq}",
    )(*mixed, w_out, x, mods, g2_row, mods, mods, wr_t)


def _sort_desc_lanes(x):
    rows, n = x.shape
    tiles = [x[:, c * LANES:(c + 1) * LANES] for c in range(n // LANES)]
    lane = lax.broadcasted_iota(I32, (rows, LANES), 1)
    k = 2
    while k <= n:
        j = k // 2
        while j >= 1:
            if j < LANES:
                lower = (lane & j) == 0
                for c in range(len(tiles)):
                    t = tiles[c]
                    partner = jnp.where(lower, pltpu.roll(t, LANES - j, axis=1), pltpu.roll(t, j, axis=1))
                    desc = ((lane & k) == 0) if k < LANES else (((c * LANES) & k) == 0)
                    take_max = (lower == desc) if k < LANES else (lower if desc else jnp.logical_not(lower))
                    tiles[c] = jnp.where(take_max, jnp.maximum(t, partner), jnp.minimum(t, partner))
            else:
                jc = j // LANES
                new = list(tiles)
                for c in range(len(tiles)):
                    take_max = ((c & jc) == 0) == (((c * LANES) & k) == 0)
                    new[c] = (jnp.maximum if take_max else jnp.minimum)(tiles[c], tiles[c ^ jc])
                tiles = new
            j //= 2
        k *= 2
    return tiles


def _router_kernel(cap, lg_ref, pos_ref, g_ref):
    x = lg_ref[...]
    n_b, n_e, n_tok = x.shape
    e = jnp.exp(x - jnp.max(x, axis=1, keepdims=True))
    aff = (e / jnp.sum(e, axis=1, keepdims=True)).reshape(n_b * n_e, n_tok)
    srt = _sort_desc_lanes(aff)
    lane_k = (cap - 1) % LANES
    thr = srt[(cap - 1) // LANES][:, lane_k:lane_k + 1]
    gt = aff > thr
    eq = aff == thr
    n_gt = jnp.sum(gt.astype(F32), axis=1, keepdims=True)
    before = (lax.broadcasted_iota(I32, (n_tok, n_tok), 0)
              < lax.broadcasted_iota(I32, (n_tok, n_tok), 1)).astype(BF16)
    eq_rank = _dot(eq.astype(BF16), before)
    sel = gt | (eq & (eq_rank < cap - n_gt))
    slot = _dot(sel.astype(BF16), before).astype(I32)
    pos_ref[...] = jnp.where(sel, slot, -1).reshape(n_b, n_e, n_tok)
    g_ref[...] = jnp.where(sel, aff, 0.0).reshape(n_b, n_e, n_tok)


def _router(logits, stream):
    shape = (stream.n_req, N_EXPERTS, stream.seq)
    spec = pl.BlockSpec(shape, lambda i: (0, 0, 0))
    return _deferred(
        functools.partial(_router_kernel, stream.cap), grid=(1,),
        in_specs=[spec], out_specs=[spec, spec],
        out_shape=[jax.ShapeDtypeStruct(shape, I32), jax.ShapeDtypeStruct(shape, F32)],
        name=f"router_s{stream.seq}",
    )(logits)


def _gather_kernel(cap, seq, h_ref, pos_ref, xs_ref):
    slot = lax.broadcasted_iota(I32, (cap, seq), 0)
    for r in range(pos_ref.shape[0]):
        onehot = jnp.concatenate([(slot == pos_ref[r, e:e + 1, :]).astype(BF16) for e in range(N_EXPERTS)], axis=0)
        xs = _dot(onehot, h_ref[r * seq:(r + 1) * seq, :]).astype(BF16)
        for e in range(N_EXPERTS):
            xs_ref[e, r * cap:(r + 1) * cap, :] = xs[e * cap:(e + 1) * cap]


def _gather(h2, pos, stream):
    cap, seq, per = stream.cap, stream.seq, stream.req_per_step
    return _deferred(
        functools.partial(_gather_kernel, cap, seq),
        grid=(stream.n_req // per,),
        in_specs=[pl.BlockSpec((per * seq, D_MODEL), lambda i: (i, 0)),
                  pl.BlockSpec((per, N_EXPERTS, seq), lambda i: (i, 0, 0))],
        out_specs=pl.BlockSpec((N_EXPERTS, per * cap, D_MODEL), lambda i: (0, i, 0)),
        out_shape=jax.ShapeDtypeStruct((N_EXPERTS, stream.n_req * cap, D_MODEL), BF16),
        name=f"gather_s{seq}",
    )(h2, pos)


def _ffn_kernel(xa_ref, xb_ref, wg_ref, wu_ref, wd_ref, ya_ref, yb_ref, acc):
    j = pl.program_id(1)
    rows_a = xa_ref.shape[1]

    @pl.when(j == 0)
    def _():
        acc[...] = jnp.zeros_like(acc)

    x = jnp.concatenate([xa_ref[0], xb_ref[0]], axis=0)
    a = _dot(x, wg_ref[0, 0].astype(BF16))
    u = _dot(x, wu_ref[0, 0].astype(BF16))
    acc[...] += _dot(((a * jax.nn.sigmoid(a)) * u).astype(BF16), wd_ref[0, 0].astype(BF16))

    @pl.when(j == pl.num_programs(1) - 1)
    def _():
        ya_ref[0] = acc[0:rows_a, :].astype(BF16)
        yb_ref[0] = acc[rows_a:, :].astype(BF16)


def _ffn(xs_a, xs_b, layer, w_gate, w_up, w_down):
    tf = FF_TILE
    rows_a, rows_b = xs_a.shape[1], xs_b.shape[1]
    x_spec = lambda rows: pl.BlockSpec((1, rows, D_MODEL), lambda e, j: (e, 0, 0))
    return pl.pallas_call(
        _ffn_kernel,
        grid=(N_EXPERTS, EXPERT_FF // tf),
        in_specs=[x_spec(rows_a), x_spec(rows_b),
                  pl.BlockSpec((1, 1, D_MODEL, tf), lambda e, j: (layer, e, 0, j)),
                  pl.BlockSpec((1, 1, D_MODEL, tf), lambda e, j: (layer, e, 0, j)),
                  pl.BlockSpec((1, 1, tf, D_MODEL), lambda e, j: (layer, e, j, 0))],
        out_specs=[x_spec(rows_a), x_spec(rows_b)],
        out_shape=[jax.ShapeDtypeStruct(xs_a.shape, BF16), jax.ShapeDtypeStruct(xs_b.shape, BF16)],
        scratch_shapes=[pltpu.VMEM((rows_a + rows_b, D_MODEL), F32)],
        compiler_params=_params(2, FFN_VMEM_LIMIT_BYTES),
        name=f"ffn_l{layer}",
    )(xs_a, xs_b, w_gate, w_up, w_down)


def _scatter_kernel(cap, seq, final, y_ref, pos_ref, g_ref, x_ref, gate_ref, fg_ref, o_ref):
    slot = lax.broadcasted_iota(I32, (cap, seq), 0)
    for r in range(pos_ref.shape[0]):
        onehots, gated = [], []
        for e in range(N_EXPERTS):
            hit = slot == pos_ref[r, e:e + 1, :]
            gate = jnp.sum(jnp.where(hit, g_ref[r, e:e + 1, :], 0.0), axis=1, keepdims=True)
            gated.append((y_ref[e, r * cap:(r + 1) * cap, :].astype(F32) * gate).astype(BF16))
            onehots.append(hit.astype(BF16))
        moe = lax.dot_general(jnp.concatenate(onehots, axis=0), jnp.concatenate(gated, axis=0),
                              (((0,), (0,)), ((), ())), preferred_element_type=F32)
        rows = slice(r * seq, (r + 1) * seq)
        xn = x_ref[rows, :] + gate_ref[0] * moe
        if final:
            xn = xn * lax.rsqrt(jnp.mean(xn * xn, axis=-1, keepdims=True) + EPS) * fg_ref[...]
        o_ref[rows, :] = xn


def _scatter(y, pos, g, xn, mods, layer, stream, final, final_g_row):
    cap, seq, per = stream.cap, stream.seq, stream.req_per_step
    tok_blk = pl.BlockSpec((per * seq, D_MODEL), lambda i: (i, 0))
    sel_blk = pl.BlockSpec((per, N_EXPERTS, seq), lambda i: (i, 0, 0))
    return _deferred(
        functools.partial(_scatter_kernel, cap, seq, final),
        grid=(stream.n_req // per,),
        in_specs=[pl.BlockSpec((N_EXPERTS, per * cap, D_MODEL), lambda i: (0, i, 0)),
                  sel_blk, sel_blk, tok_blk,
                  _mod_spec(layer, stream, 5, per * seq),
                  pl.BlockSpec((1, D_MODEL), lambda i: (0, 0))],
        out_specs=tok_blk,
        out_shape=jax.ShapeDtypeStruct((stream.tokens, D_MODEL), F32),
        name=f"scatter_l{layer}_s{seq}",
    )(y, pos, g, xn, mods, final_g_row)


def _axial_rope_tables(rows, head_dim):
    f32 = np.float32
    row = np.repeat(np.arange(rows, dtype=f32), GRID_W)
    col = np.tile(np.arange(GRID_W, dtype=f32), rows)
    nf = head_dim // 4
    inv = (f32(ROPE_BASE) ** (-np.arange(nf, dtype=f32) / f32(nf))).astype(f32)
    ar = row[:, None] * inv[None]
    ac = col[:, None] * inv[None]
    ang = np.concatenate([ar, ar, ac, ac], axis=-1).astype(f32)
    return np.cos(ang).astype(f32), np.sin(ang).astype(f32)


def _block_diag(w):
    rows = [jnp.pad(w[:, n], ((0, 0), (0, 0), (n * LRU_BLOCK, (LRU_BLOCKS - 1 - n) * LRU_BLOCK)))
            for n in range(LRU_BLOCKS)]
    return jnp.concatenate(rows, axis=1)


def kernel(x_prompt, x_sample, cache_win_k, cache_win_v, state_lru, cache_diff_k, cache_diff_v, c, c_ctx, ada_w, ada_b, norm_g, final_g, even_w_in, even_w_out, conv_w, conv_b, lru_wa, lru_ba, lru_wx, lru_bx, lru_lambda, win_sink, odd_w_in, odd_w_out, diff_lq1, diff_lk1, diff_lq2, diff_lk2, diff_subln_g, moe_router, moe_w_gate, moe_w_up, moe_w_down):
    cv_t = jnp.concatenate([c_ctx[None], c, jnp.zeros((COND_ROWS - N_COND, D_MODEL), F32)], axis=0).T
    mods = _adaln(cv_t, ada_w, ada_b).reshape(DEPTH * COND_ROWS, 1, 6 * D_MODEL)

    cos, sin = _axial_rope_tables(DEC_SEQ // GRID_W, WIN_HEAD_DIM)
    rope_win = rope_diff = (jnp.asarray(np.tile(cos, (1, 2))), jnp.asarray(np.tile(sin, (1, 2))))

    xs = [x_prompt.reshape(CTX.tokens, D_MODEL), x_sample.reshape(LAT.tokens, D_MODEL)]
    final_g_row = final_g.reshape(1, D_MODEL)
    outs = {}

    def both(make):
        return _run_each(make(0, CTX), make(1, LAT))

    routers_t = jnp.swapaxes(moe_router, 1, 2)
    for layer in range(DEPTH):
        idx = layer // 2
        even = layer % 2 == 0
        w_in = (even_w_in if even else odd_w_in)[idx]
        w_out = (even_w_out if even else odd_w_out)[idx]
        wr_t = routers_t[layer]
        g1_row = norm_g[layer, 0].reshape(1, D_MODEL)
        g2_row = norm_g[layer, 1].reshape(1, D_MODEL)
        if even:
            wa = _block_diag(lru_wa[idx]).astype(BF16)
            wx = _block_diag(lru_wx[idx]).astype(BF16)
            (proj_c, win_k, win_v), (proj_d,) = _run_each(_proj_win(xs[0], g1_row, mods, layer, CTX, w_in),
                                                         _proj(xs[1], g1_row, mods, layer, LAT, w_in, EVEN_TOKEN_TILE))
            projs = (proj_c, proj_d)
            h0s = (jnp.zeros((CTX.n_req, 2, LRU_WIDTH), F32), state_lru[:, idx])
            (y_c, h_fin), (y_d, _) = both(lambda si, st: _lru(
                projs[si], st, conv_w[idx], conv_b[idx].reshape(1, LRU_WIDTH), wa, wx,
                lru_ba[idx], lru_bx[idx], lru_lambda[idx], h0s[si]))
            (o_c,), (o_d,) = _run_each(
                _win_attn(proj_c, CTX, win_sink[idx]),
                _win_attn(proj_d, LAT, win_sink[idx], cache=(cache_win_k, cache_win_v, idx), rope=rope_win))
            outs["win_k"] = win_k
            outs["win_v"] = win_v
            outs["lru"] = h_fin[:, None]
            mixed = ([y_c, o_c], [y_d, o_d])
        else:
            lambda_init = 0.8 - 0.6 * math.exp(-0.3 * layer)
            vec = lambda t: t[idx].reshape(1, DIFF_HEAD_DIM)
            args = (vec(diff_lq1), vec(diff_lk1), vec(diff_lq2), vec(diff_lk2), diff_subln_g[idx])
            (q, kh, vh), (proj_d,) = _run_each(_proj_heads(xs[0], g1_row, mods, layer, CTX, w_in),
                                              _proj(xs[1], g1_row, mods, layer, LAT, w_in))
            (o_c,), (o_d,) = _run_each(
                _diff_attn((q, kh, vh), CTX, lambda_init, *args),
                _diff_attn(proj_d, LAT, lambda_init, *args,
                           cache=(cache_diff_k[:, idx:idx + 1], cache_diff_v[:, idx:idx + 1]), rope=rope_diff))
            outs["diff_k"] = kh
            outs["diff_v"] = vh
            mixed = ([o_c], [o_d])

        mix = both(lambda si, st: _mix_out(mixed[si], w_out, xs[si], g2_row, mods, layer, st, wr_t))
        xns, h2s = (mix[0][0], mix[1][0]), (mix[0][1], mix[1][1])

        def request_major(lg, st):
            per_req = st.seq // LOGIT_TILE
            lg = lg.reshape(st.n_req, per_req, N_EXPERTS, LOGIT_TILE).transpose(0, 2, 1, 3)
            return lg.reshape(st.n_req, N_EXPERTS, st.seq)

        lgs = (request_major(mix[0][2], CTX), request_major(mix[1][2], LAT))
        routes = both(lambda si, st: _router(lgs[si], st))
        (rows_c,), (rows_d,) = both(lambda si, st: _gather(h2s[si], routes[si][0], st))
        ys = _ffn(rows_c, rows_d, layer, moe_w_gate, moe_w_up, moe_w_down)
        final = layer == DEPTH - 1
        (x_c,), (x_d,) = both(lambda si, st: _scatter(ys[si], routes[si][0], routes[si][1], xns[si], mods, layer, st,
                                                      final, final_g_row))
        xs = [x_c, x_d]


    y_prompt = xs[0].reshape(BATCH, SEQ, D_MODEL)
    y_sample = xs[1].reshape(DEC_BATCH, DEC_SEQ, D_MODEL)
    return (y_prompt, y_sample, outs["win_k"], outs["win_v"], outs["lru"], outs["diff_k"], outs["diff_v"])
```

```python
import functools
import math
from typing import Callable, NamedTuple

import jax
import jax.numpy as jnp
import numpy as np
from jax import lax
from jax.experimental import pallas as pl
from jax.experimental.pallas import tpu as pltpu

F32 = jnp.float32
BF16 = jnp.bfloat16
I32 = jnp.int32

D_MODEL = 1024
BATCH = 16
SEQ = 256
DEPTH = 2
DEC_BATCH = 2
DEC_SEQ = 1024
PAST_LEN = 512
GRID_W = 64
LRU_WIDTH = D_MODEL // 2
LRU_BLOCKS = 8
LRU_BLOCK = LRU_WIDTH // LRU_BLOCKS
CONV_W = 4
LRU_C = 8.0
WIN_HEADS = 8
WIN_KV_HEADS = 2
WIN_REP = WIN_HEADS // WIN_KV_HEADS
WIN_HEAD_DIM = 64
WINDOW = 128
WIN_Q = WIN_HEADS * WIN_HEAD_DIM
WIN_KV = WIN_KV_HEADS * WIN_HEAD_DIM
EVEN_IN = 2 * LRU_WIDTH + WIN_Q + 2 * WIN_KV
DIFF_HEADS = 8
DIFF_HEAD_DIM = 64
DIFF_QK = DIFF_HEADS * 2 * DIFF_HEAD_DIM
DIFF_V = DIFF_HEADS * 2 * DIFF_HEAD_DIM
ODD_IN = 2 * DIFF_QK + DIFF_V
N_EXPERTS = 16
EXPERT_FF = 2 * D_MODEL
CAPACITY_FACTOR = 2
ROPE_BASE = 10000.0
EPS = 1e-6
NEG_INF = -1e30

LANES = 128
SUBLANES = 8
VMEM_LIMIT_BYTES = 48 * 1024 * 1024

N_COND = 1 + DEC_BATCH
COND_ROWS = SUBLANES
ADALN_TILE = 2048
TOKEN_TILE = 512
EVEN_TOKEN_TILE = 1024
MIX_TILE = 1024
ROUTE_ROWS = 1024
LOGIT_TILE = 256
FF_TILE = 1024
LRU_CHUNK = 256
WIN_BLOCKS_PER_ITER = 2
ATT_Q_TILE = 256
ATT_K_TILE = 256


class Stream:
    def __init__(self, n_req, seq, cond0, cond_step):
        self.n_req, self.seq, self.cond0, self.cond_step = n_req, seq, cond0, cond_step
        self.tokens = n_req * seq
        self.cap = CAPACITY_FACTOR * seq // N_EXPERTS
        self.req_per_step = max(1, ROUTE_ROWS // seq)

    def cond_of_row(self, row):
        return self.cond0 + self.cond_step * (row // self.seq)


CTX = Stream(BATCH, SEQ, 0, 0)
LAT = Stream(DEC_BATCH, DEC_SEQ, 1, 1)


def _params(n_axes, vmem_limit_bytes=VMEM_LIMIT_BYTES):
    return pltpu.CompilerParams(dimension_semantics=("arbitrary",) * n_axes,
                                vmem_limit_bytes=vmem_limit_bytes)


class Call(NamedTuple):
    kernel: Callable
    steps: int
    in_specs: tuple
    out_specs: tuple
    out_shapes: tuple
    scratch: tuple
    args: tuple
    name: str


def _deferred(kernel, *, grid, in_specs, out_specs, out_shape, scratch_shapes=(), name):
    (steps,) = grid
    as_tuple = lambda v: tuple(v) if isinstance(v, (list, tuple)) else (v,)
    return lambda *args: Call(kernel, steps, tuple(in_specs), as_tuple(out_specs), as_tuple(out_shape),
                              tuple(scratch_shapes), args, name)


def _run(call):
    return pl.pallas_call(
        call.kernel, grid=(call.steps,), in_specs=list(call.in_specs), out_specs=list(call.out_specs),
        out_shape=list(call.out_shapes), scratch_shapes=list(call.scratch),
        compiler_params=_params(1), name=call.name)(*call.args)


def _run_each(a, b):
    return _run(a), _run(b)


def _resident_spec(shape):
    return pl.BlockSpec(shape, lambda i: (0, 0), pipeline_mode=pl.Buffered(1))


def _mod_spec(layer, stream, k, rows_per_step):
    return pl.BlockSpec(
        (1, 1, D_MODEL),
        lambda i, *_: (layer * COND_ROWS + stream.cond_of_row(i * rows_per_step), 0, k))


def _dot(a, b):
    return jnp.dot(a, b, preferred_element_type=F32)


def _dot_nt(a, b):
    return lax.dot_general(a, b, (((1,), (1,)), ((), ())), preferred_element_type=F32)


def _modnorm(x, g, shift, scale):
    y = x * lax.rsqrt(jnp.mean(x * x, axis=-1, keepdims=True) + EPS)
    return (y * g) * (1.0 + scale) + shift


def _lane_half_masks(shape):
    lane = lax.broadcasted_iota(I32, shape, len(shape) - 1)
    left = (lane & (LANES - 1)) < LANES // 2
    return left, jnp.logical_not(left)


def _rope(x, cos, sin):
    parts = []
    for c in range(x.shape[1] // LANES):
        xs = x[:, c * LANES:(c + 1) * LANES]
        lane = lax.broadcasted_iota(I32, xs.shape, 1)
        first = (lane & 31) < 16
        rot = jnp.where(first, -pltpu.roll(xs, LANES - 16, axis=1), pltpu.roll(xs, 16, axis=1))
        parts.append(xs * cos + rot * sin)
    return parts[0] if len(parts) == 1 else jnp.concatenate(parts, axis=1)


def _adaln_kernel(cv_ref, w_ref, b_ref, o_ref):
    cv = cv_ref[...]
    s = cv * jax.nn.sigmoid(cv)
    w = w_ref[0]
    ridx = lax.broadcasted_iota(I32, (COND_ROWS, w.shape[1]), 0)
    out = jnp.zeros((COND_ROWS, w.shape[1]), F32)
    for r in range(N_COND):
        out = jnp.where(ridx == r, jnp.sum(w * s[:, r:r + 1], axis=0, keepdims=True), out)
    o_ref[0] = out + b_ref[0]


def _adaln(cv_t, ada_w, ada_b):
    tn = ADALN_TILE
    return pl.pallas_call(
        _adaln_kernel,
        grid=(DEPTH, 6 * D_MODEL // tn),
        in_specs=[pl.BlockSpec((D_MODEL, COND_ROWS), lambda l, j: (0, 0)),
                  pl.BlockSpec((1, D_MODEL, tn), lambda l, j: (l, 0, j)),
                  pl.BlockSpec((1, 1, tn), lambda l, j: (l, 0, j))],
        out_specs=pl.BlockSpec((1, COND_ROWS, tn), lambda l, j: (l, 0, j)),
        out_shape=jax.ShapeDtypeStruct((DEPTH, COND_ROWS, 6 * D_MODEL), F32),
        compiler_params=_params(2),
        name="adaln",
    )(cv_t, ada_w, ada_b.reshape(DEPTH, 1, 6 * D_MODEL))


def _proj_kernel(x_ref, g_ref, sh_ref, sc_ref, w_ref, o_ref):
    h = _modnorm(x_ref[...], g_ref[...], sh_ref[0], sc_ref[0])
    o_ref[...] = _dot(h.astype(BF16), w_ref[...].astype(BF16))


def _proj(x, g_row, mods, layer, stream, w, tm=TOKEN_TILE):
    n_out = w.shape[1]
    return _deferred(
        _proj_kernel,
        grid=(stream.tokens // tm,),
        in_specs=[pl.BlockSpec((tm, D_MODEL), lambda i: (i, 0)),
                  pl.BlockSpec((1, D_MODEL), lambda i: (0, 0)),
                  _mod_spec(layer, stream, 0, tm),
                  _mod_spec(layer, stream, 1, tm),
                  _resident_spec((D_MODEL, n_out))],
        out_specs=pl.BlockSpec((tm, n_out), lambda i: (i, 0)),
        out_shape=jax.ShapeDtypeStruct((stream.tokens, n_out), F32),
        name=f"proj_l{layer}_s{stream.seq}",
    )(x, g_row, mods, mods, w)


def _proj_heads_kernel(seq, x_ref, g_ref, sh_ref, sc_ref, w_ref, q_ref, k_ref, v_ref):
    h = _modnorm(x_ref[...], g_ref[...], sh_ref[0], sc_ref[0])
    res = _dot(h.astype(BF16), w_ref[...].astype(BF16))
    q_ref[...] = res[:, 0:DIFF_QK].astype(BF16)
    for r in range(x_ref.shape[0] // seq):
        rows = slice(r * seq, (r + 1) * seq)
        for hh in range(DIFF_HEADS):
            k_ref[r, 0, hh] = res[rows, DIFF_QK + hh * LANES:DIFF_QK + (hh + 1) * LANES]
            v_ref[r, 0, hh] = res[rows, 2 * DIFF_QK + hh * LANES:2 * DIFF_QK + (hh + 1) * LANES]


def _proj_heads(x, g_row, mods, layer, stream, w):
    tm, seq = TOKEN_TILE, stream.seq
    head_shape = (stream.n_req, 1, DIFF_HEADS, seq, 2 * DIFF_HEAD_DIM)
    head_spec = pl.BlockSpec((tm // seq, 1, DIFF_HEADS, seq, 2 * DIFF_HEAD_DIM), lambda i: (i, 0, 0, 0, 0))
    return _deferred(
        functools.partial(_proj_heads_kernel, seq),
        grid=(stream.tokens // tm,),
        in_specs=[pl.BlockSpec((tm, D_MODEL), lambda i: (i, 0)),
                  pl.BlockSpec((1, D_MODEL), lambda i: (0, 0)),
                  _mod_spec(layer, stream, 0, tm),
                  _mod_spec(layer, stream, 1, tm),
                  _resident_spec((D_MODEL, ODD_IN))],
        out_specs=[pl.BlockSpec((tm, DIFF_QK), lambda i: (i, 0)), head_spec, head_spec],
        out_shape=[jax.ShapeDtypeStruct((stream.tokens, DIFF_QK), BF16),
                   jax.ShapeDtypeStruct(head_shape, F32), jax.ShapeDtypeStruct(head_shape, F32)],
        name=f"proj_heads_l{layer}_s{seq}",
    )(x, g_row, mods, mods, w)


def _proj_win_kernel(seq, x_ref, g_ref, sh_ref, sc_ref, w_ref, o_ref, k_ref, v_ref):
    h = _modnorm(x_ref[...], g_ref[...], sh_ref[0], sc_ref[0])
    res = _dot(h.astype(BF16), w_ref[...].astype(BF16))
    o_ref[...] = res
    k0 = 2 * LRU_WIDTH + WIN_Q
    for r in range(x_ref.shape[0] // seq):
        rows = slice(r * seq, (r + 1) * seq)
        for g in range(WIN_KV_HEADS):
            k_ref[r, 0, g] = res[rows, k0 + g * WIN_HEAD_DIM:k0 + (g + 1) * WIN_HEAD_DIM]
            v_ref[r, 0, g] = res[rows, k0 + WIN_KV + g * WIN_HEAD_DIM:k0 + WIN_KV + (g + 1) * WIN_HEAD_DIM]


def _proj_win(x, g_row, mods, layer, stream, w):
    tm, seq = EVEN_TOKEN_TILE, stream.seq
    head_shape = (stream.n_req, 1, WIN_KV_HEADS, seq, WIN_HEAD_DIM)
    head_spec = pl.BlockSpec((tm // seq, 1, WIN_KV_HEADS, seq, WIN_HEAD_DIM), lambda i: (i, 0, 0, 0, 0))
    return _deferred(
        functools.partial(_proj_win_kernel, seq),
        grid=(stream.tokens // tm,),
        in_specs=[pl.BlockSpec((tm, D_MODEL), lambda i: (i, 0)),
                  pl.BlockSpec((1, D_MODEL), lambda i: (0, 0)),
                  _mod_spec(layer, stream, 0, tm),
                  _mod_spec(layer, stream, 1, tm),
                  _resident_spec((D_MODEL, EVEN_IN))],
        out_specs=[pl.BlockSpec((tm, EVEN_IN), lambda i: (i, 0)), head_spec, head_spec],
        out_shape=[jax.ShapeDtypeStruct((stream.tokens, EVEN_IN), F32),
                   jax.ShapeDtypeStruct(head_shape, F32), jax.ShapeDtypeStruct(head_shape, F32)],
        name=f"proj_win_l{layer}_s{seq}",
    )(x, g_row, mods, mods, w)


def _lru_kernel(seq, xl_ref, gl_ref, cw_ref, cb_ref, wa_ref, wx_ref, ba_ref, bx_ref, lam_ref, h0_ref,
                y_ref, hfin_ref, xe, af, bf, ab, bb):
    width = LRU_WIDTH
    ch = LRU_CHUNK
    seg = seq // SUBLANES
    lead = (CONV_W // 2) * SUBLANES

    def to_segment_major(x):
        return jnp.transpose(x.reshape(SUBLANES, seg, LANES), (1, 0, 2)).reshape(seq, LANES)

    def to_time_major(x):
        return jnp.transpose(x.reshape(seg, SUBLANES, LANES), (1, 0, 2)).reshape(seq, LANES)

    for c in range(width // LANES):
        cols = slice(c * LANES, (c + 1) * LANES)
        xe[lead:lead + seq, cols] = to_segment_major(xl_ref[:, cols])
    sub = lax.broadcasted_iota(I32, (SUBLANES, width), 0)
    for k in range(CONV_W // 2):
        prev = xe[lead + (seg - 2 + k) * SUBLANES:lead + (seg - 1 + k) * SUBLANES, :]
        xe[k * SUBLANES:(k + 1) * SUBLANES, :] = jnp.where(sub >= 1, pltpu.roll(prev, 1, axis=0), 0.0)
    first = xe[lead:lead + SUBLANES, :]
    xe[lead + seq:lead + seq + SUBLANES, :] = jnp.where(sub < SUBLANES - 1,
                                                        pltpu.roll(first, SUBLANES - 1, axis=0), 0.0)

    lam = lam_ref[...]
    z = -lam
    softplus = jnp.maximum(z, 0.0) + jnp.log1p(jnp.exp(-jnp.abs(z)))
    cw = cw_ref[...]
    cb = cb_ref[...]

    def gates_chunk(c, carry):
        r0 = pl.multiple_of(c * ch, ch)
        xc = xe[pl.ds(r0, ch), :] * cw[0:1]
        for j in range(1, CONV_W):
            xc = xc + xe[pl.ds(pl.multiple_of(r0 + j * SUBLANES, SUBLANES), ch), :] * cw[j:j + 1]
        xc = xc + cb
        xcb = xc.astype(BF16)
        for d, (a_s, b_s) in enumerate(((af, bf), (ab, bb))):
            r = jax.nn.sigmoid(_dot(xcb, wa_ref[d]) + ba_ref[d:d + 1])
            ig = jax.nn.sigmoid(_dot(xcb, wx_ref[d]) + bx_ref[d:d + 1])
            log_a = (-LRU_C * r) * softplus[d:d + 1]
            a = jnp.exp(log_a)
            v = 1.0 - a * a
            a_s[pl.ds(r0, ch), :] = a
            b_s[pl.ds(r0, ch), :] = jnp.where(v > 0.0, v * lax.rsqrt(v), 0.0) * (ig * xc)
        return carry

    lax.fori_loop(0, seq // ch, gates_chunk, 0)

    def scan_step(k, carry):
        hf, pf, hb, pb = carry
        rf = pl.multiple_of(k * SUBLANES, SUBLANES)
        rb = pl.multiple_of((seg - 1 - k) * SUBLANES, SUBLANES)
        a = af[pl.ds(rf, SUBLANES), :]
        hf = a * hf + bf[pl.ds(rf, SUBLANES), :]
        pf = a * pf
        bf[pl.ds(rf, SUBLANES), :] = hf
        af[pl.ds(rf, SUBLANES), :] = pf
        a = ab[pl.ds(rb, SUBLANES), :]
        hb = a * hb + bb[pl.ds(rb, SUBLANES), :]
        pb = a * pb
        bb[pl.ds(rb, SUBLANES), :] = hb
        ab[pl.ds(rb, SUBLANES), :] = pb
        return hf, pf, hb, pb

    zeros = jnp.zeros((SUBLANES, width), F32)
    ones = jnp.ones((SUBLANES, width), F32)
    hf, pf, hb, pb = lax.fori_loop(0, seg, scan_step, (zeros, ones, zeros, ones))

    h0 = h0_ref[0]
    state, carry_f = h0[0:1], zeros
    for g in range(SUBLANES):
        carry_f = jnp.where(sub == g, state, carry_f)
        state = pf[g:g + 1] * state + hf[g:g + 1]
    hfin_ref[0, 0:1, :] = state
    state, carry_b = h0[1:2], zeros
    for g in reversed(range(SUBLANES)):
        carry_b = jnp.where(sub == g, state, carry_b)
        state = pb[g:g + 1] * state + hb[g:g + 1]
    hfin_ref[0, 1:2, :] = state

    groups = (ch // SUBLANES, SUBLANES, width)

    def combine_chunk(c, carry):
        r0 = pl.multiple_of(c * ch, ch)
        rows = pl.ds(r0, ch)
        h_fwd = bf[rows, :].reshape(groups) + af[rows, :].reshape(groups) * carry_f
        h_bwd = bb[rows, :].reshape(groups) + ab[rows, :].reshape(groups) * carry_b
        bf[rows, :] = (h_fwd + h_bwd).reshape(ch, width)
        return carry

    lax.fori_loop(0, seq // ch, combine_chunk, 0)

    for c in range(width // LANES):
        cols = slice(c * LANES, (c + 1) * LANES)
        y_ref[:, cols] = (to_time_major(bf[:, cols]) * jax.nn.gelu(gl_ref[:, cols])).astype(BF16)


def _lru(proj, stream, conv_w, conv_b, wa, wx, ba, bx, lam, h0):
    seq, width = stream.seq, LRU_WIDTH
    full2 = lambda b: (0, 0)
    full3 = lambda b: (0, 0, 0)
    return _deferred(
        functools.partial(_lru_kernel, seq),
        grid=(stream.n_req,),
        in_specs=[pl.BlockSpec((seq, width), lambda b: (b, 0)),
                  pl.BlockSpec((seq, width), lambda b: (b, 1)),
                  pl.BlockSpec((CONV_W, width), full2),
                  pl.BlockSpec((1, width), full2),
                  pl.BlockSpec((2, width, width), full3),
                  pl.BlockSpec((2, width, width), full3),
                  pl.BlockSpec((2, width), full2),
                  pl.BlockSpec((2, width), full2),
                  pl.BlockSpec((2, width), full2),
                  pl.BlockSpec((1, 2, width), lambda b: (b, 0, 0))],
        out_specs=[pl.BlockSpec((seq, width), lambda b: (b, 0)),
                   pl.BlockSpec((1, 2, width), lambda b: (b, 0, 0))],
        out_shape=[jax.ShapeDtypeStruct((stream.tokens, width), BF16),
                   jax.ShapeDtypeStruct((stream.n_req, 2, width), F32)],
        scratch_shapes=[pltpu.VMEM((seq + (CONV_W - 1) * SUBLANES, width), F32)] + [pltpu.VMEM((seq, width), F32)] * 4,
        name=f"lru_s{seq}",
    )(proj, proj, conv_w, conv_b, wa, wx, ba, bx, lam, h0)


def _attend(q, chunks, s_ref):
    tile_max = None
    spans = []
    off = 0
    for keys, _ in chunks:
        s = _dot_nt(q, keys())
        n = s.shape[1]
        s_ref[:, off:off + n] = s
        for c in range(n // LANES):
            t = s[:, c * LANES:(c + 1) * LANES]
            tile_max = t if tile_max is None else jnp.maximum(tile_max, t)
        spans.append((off, n))
        off += n
    m = jnp.max(tile_max, axis=-1, keepdims=True)
    acc = None
    for (_, values), (o, n) in zip(chunks, spans):
        part = _dot(jnp.exp(s_ref[:, o:o + n] - m).astype(BF16), values())
        acc = part if acc is None else acc + part
    return acc


def _split_groups(kk):
    left, right = _lane_half_masks(kk.shape)
    g0_l = jnp.where(left, kk, 0.0)
    g1_r = jnp.where(right, kk, 0.0)
    return ((g0_l, pltpu.roll(g0_l, LANES // 2, axis=1)), (pltpu.roll(g1_r, LANES // 2, axis=1), g1_r))


def _win_ctx_kernel(sink_ref, q_ref, kv_ref, o_ref):
    scale = WIN_HEAD_DIM ** -0.5
    seq = q_ref.shape[0]
    ks = _split_groups(kv_ref[:, 0:LANES])
    vs = _split_groups(kv_ref[:, LANES:2 * LANES])
    top = lax.broadcasted_iota(I32, (2 * seq, 1), 0) < seq
    outs = [None] * (WIN_HEADS // 2)
    for g in range(WIN_KV_HEADS):
        pairs = (2 * g, 2 * g + 1)
        qs = jnp.concatenate([q_ref[:, p * LANES:(p + 1) * LANES] for p in pairs], axis=0)
        qs = (qs * scale).astype(BF16)
        for side in range(2):
            sk = jnp.where(top, sink_ref[2 * pairs[0] + side], sink_ref[2 * pairs[1] + side])
            s = _dot_nt(qs, ks[g][side].astype(BF16))
            m = jnp.maximum(jnp.max(s, axis=-1, keepdims=True), sk)
            e = jnp.exp(s - m)
            den = jnp.sum(e, axis=-1, keepdims=True) + jnp.exp(sk - m)
            o = _dot(e.astype(BF16), vs[g][side].astype(BF16)) * (1.0 / den)
            for k, p in enumerate(pairs):
                part = o[k * seq:(k + 1) * seq]
                outs[p] = part if outs[p] is None else outs[p] + part
    for p in range(WIN_HEADS // 2):
        o_ref[:, p * LANES:(p + 1) * LANES] = outs[p].astype(BF16)


def _win_lat_kernel(sink_ref, q_ref, kv_ref, ck_ref, cv_ref, cos_ref, sin_ref, o_ref,
                    kl_s, vl_s, kc_s, vc_s):
    seq, wn = DEC_SEQ, WINDOW
    scale = WIN_HEAD_DIM ** -0.5
    kr = _rope(kv_ref[:, 0:LANES], cos_ref[...], sin_ref[...])
    for src, dst in ((_split_groups(kr), kl_s), (_split_groups(kv_ref[:, LANES:2 * LANES]), vl_s),
                     (_split_groups(ck_ref[0]), kc_s), (_split_groups(cv_ref[0]), vc_s)):
        for g in range(WIN_KV_HEADS):
            for side in range(2):
                dst[2 * g + side] = src[g][side].astype(BF16)

    top = lax.broadcasted_iota(I32, (2 * wn, 1), 0) < wn

    def q_block(i, carry):
        for sub in range(WIN_BLOCKS_PER_ITER):
            blk = WIN_BLOCKS_PER_ITER * i + sub
            r0 = pl.multiple_of(blk * wn, wn)
            start = pl.multiple_of(jnp.clip((blk - 1) * wn, 0, seq - 3 * wn), wn)
            qr = _rope(q_ref[pl.ds(r0, wn), :], cos_ref[pl.ds(r0, wn), :], sin_ref[pl.ds(r0, wn), :]) * scale
            qpos = r0 + (lax.broadcasted_iota(I32, (2 * wn, 3 * wn), 0) & (wn - 1))
            kpos = start + lax.broadcasted_iota(I32, (2 * wn, 3 * wn), 1)
            valid = jnp.abs(qpos - kpos) <= wn
            outs = [None] * (WIN_HEADS // 2)
            for g in range(WIN_KV_HEADS):
                pairs = (2 * g, 2 * g + 1)
                qs = jnp.concatenate([qr[:, p * LANES:(p + 1) * LANES] for p in pairs], axis=0).astype(BF16)
                for side in range(2):
                    idx = 2 * g + side
                    sk = jnp.where(top, sink_ref[2 * pairs[0] + side], sink_ref[2 * pairs[1] + side])
                    sl = _dot_nt(qs, kl_s[idx, pl.ds(start, 3 * wn), :])
                    sl = jnp.where(valid, sl, NEG_INF)
                    sc = _dot_nt(qs, kc_s[idx])
                    m = jnp.maximum(jnp.maximum(jnp.max(sl, axis=-1, keepdims=True),
                                                jnp.max(sc, axis=-1, keepdims=True)), sk)
                    el = jnp.exp(sl - m)
                    ec = jnp.exp(sc - m)
                    den = (jnp.sum(el, axis=-1, keepdims=True) + jnp.sum(ec, axis=-1, keepdims=True)
                           + jnp.exp(sk - m))
                    o = (_dot(el.astype(BF16), vl_s[idx, pl.ds(start, 3 * wn), :])
                         + _dot(ec.astype(BF16), vc_s[idx])) * (1.0 / den)
                    for k, p in enumerate(pairs):
                        part = o[k * wn:(k + 1) * wn]
                        outs[p] = part if outs[p] is None else outs[p] + part
            for p in range(WIN_HEADS // 2):
                o_ref[pl.ds(r0, wn), p * LANES:(p + 1) * LANES] = outs[p].astype(BF16)
        return carry

    lax.fori_loop(0, seq // (WIN_BLOCKS_PER_ITER * wn), q_block, 0)


def _win_attn(proj, stream, sink, cache=None, rope=None):
    seq = stream.seq
    q_spec = pl.BlockSpec((seq, WIN_Q), lambda b: (b, 2 * LRU_WIDTH // WIN_Q))
    kv_spec = pl.BlockSpec((seq, 2 * WIN_KV), lambda b: (b, (2 * LRU_WIDTH + WIN_Q) // (2 * WIN_KV)))
    sink_spec = pl.BlockSpec(memory_space=pltpu.SMEM)
    out_spec = pl.BlockSpec((seq, WIN_Q), lambda b: (b, 0))
    out_shape = jax.ShapeDtypeStruct((stream.tokens, WIN_Q), BF16)
    if cache is None:
        return _deferred(
            _win_ctx_kernel, grid=(stream.n_req,),
            in_specs=[sink_spec, q_spec, kv_spec], out_specs=out_spec, out_shape=out_shape,
            name="win_attn_ctx",
        )(sink, proj, proj)
    ck, cv = cache
    cos, sin = rope
    cache_spec = pl.BlockSpec((1, PAST_LEN, LANES), lambda b: (b, 0, 0))
    table_spec = pl.BlockSpec((seq, LANES), lambda b: (0, 0))
    return _deferred(
        _win_lat_kernel, grid=(stream.n_req,),
        in_specs=[sink_spec, q_spec, kv_spec, cache_spec, cache_spec, table_spec, table_spec],
        out_specs=out_spec, out_shape=out_shape,
        scratch_shapes=[pltpu.VMEM((4, seq, LANES), BF16), pltpu.VMEM((4, seq, LANES), BF16),
                        pltpu.VMEM((4, PAST_LEN, LANES), BF16), pltpu.VMEM((4, PAST_LEN, LANES), BF16)],
        name="win_attn_lat",
    )(sink, proj, proj, ck, cv, cos, sin)


def _diff_lambda(lq1_ref, lk1_ref, lq2_ref, lk2_ref, lambda_init):
    t1 = jnp.sum(lq1_ref[...] * lk1_ref[...], axis=-1, keepdims=True)
    t2 = jnp.sum(lq2_ref[...] * lk2_ref[...], axis=-1, keepdims=True)
    return jnp.exp(t1) - jnp.exp(t2) + lambda_init


def _subln(o, g_row, lambda_init):
    o = o * lax.rsqrt(jnp.mean(o * o, axis=-1, keepdims=True) + EPS) * g_row
    return o * (1.0 - lambda_init)


def _component_keys(k):
    left, right = _lane_half_masks(k.shape)
    return jnp.where(left, k, 0.0).astype(BF16), jnp.where(right, k, 0.0).astype(BF16)


def _values_with_ones(v):
    return jnp.concatenate([v.astype(BF16), jnp.ones(v.shape, BF16)], axis=1)


def _diff_combine(accs, lam):
    o1, o2 = accs[0][:, 0:LANES], accs[1][:, 0:LANES]
    return o1 * (1.0 / accs[0][:, LANES:]) - o2 * (lam * (1.0 / accs[1][:, LANES:]))


def _diff_ctx_kernel(lambda_init, lq1_ref, lk1_ref, lq2_ref, lk2_ref, sg_ref, q_ref, k_ref, v_ref, o_ref, s_ref):
    scale = DIFF_HEAD_DIM ** -0.5
    tq = ATT_Q_TILE
    lam = _diff_lambda(lq1_ref, lk1_ref, lq2_ref, lk2_ref, lambda_init)
    for h in range(DIFF_HEADS):
        cols = slice(h * LANES, (h + 1) * LANES)
        keys = _component_keys(k_ref[0, 0, h])
        vals = _values_with_ones(v_ref[0, 0, h])
        for qt in range(q_ref.shape[0] // tq):
            rows = slice(qt * tq, (qt + 1) * tq)
            q = (q_ref[rows, cols] * scale).astype(BF16)
            slots = [s_ref.at[(4 * h + 2 * qt + c) % s_ref.shape[0]] for c in range(2)]
            accs = [_attend(q, [(lambda: keys[c], lambda: vals)], slots[c]) for c in range(2)]
            o = _diff_combine(accs, lam)
            o_ref[rows, cols] = _subln(o, sg_ref[h:h + 1, :], lambda_init).astype(BF16)


def _diff_lat_kernel(lambda_init, lq1_ref, lk1_ref, lq2_ref, lk2_ref, sg_ref, q_ref, k_ref, v_ref,
                     ck_ref, cv_ref, cos_ref, sin_ref, o_ref, kl_s, kc_s, vl_s, vc_s, s_ref):
    seq, tq, tk = DEC_SEQ, ATT_Q_TILE, ATT_K_TILE
    scale = DIFF_HEAD_DIM ** -0.5
    lam = _diff_lambda(lq1_ref, lk1_ref, lq2_ref, lk2_ref, lambda_init)
    for src, dst in ((_rope(k_ref[...], cos_ref[...], sin_ref[...]), kl_s), (ck_ref[0, 0, 0], kc_s)):
        dst[0], dst[1] = _component_keys(src)
    vl_s[...] = _values_with_ones(v_ref[...])
    vc_s[...] = _values_with_ones(cv_ref[0, 0, 0])
    sg = sg_ref[0]

    for qt in range(seq // tq):
        rows = slice(qt * tq, (qt + 1) * tq)
        qr = (_rope(q_ref[rows, :], cos_ref[rows, :], sin_ref[rows, :]) * scale).astype(BF16)
        accs = []
        for comp in range(2):
            chunks = [(lambda j=j: kl_s[comp, j * tk:(j + 1) * tk, :],
                       lambda j=j: vl_s[j * tk:(j + 1) * tk, :]) for j in range(seq // tk)]
            chunks += [(lambda j=j: kc_s[comp, j * tk:(j + 1) * tk, :],
                        lambda j=j: vc_s[j * tk:(j + 1) * tk, :]) for j in range(PAST_LEN // tk)]
            accs.append(_attend(qr, chunks, s_ref.at[2 * qt + comp]))
        o_ref[rows, :] = _subln(_diff_combine(accs, lam), sg, lambda_init).astype(BF16)


def _diff_attn(proj, stream, lambda_init, lq1, lk1, lq2, lk2, subln_g, cache=None, rope=None):
    seq = stream.seq
    vec_spec = lambda nd: pl.BlockSpec((1, DIFF_HEAD_DIM), lambda *_: (0, 0))
    out_shape = jax.ShapeDtypeStruct((stream.tokens, DIFF_V), BF16)
    if cache is None:
        q, kh, vh = proj
        head_spec = pl.BlockSpec((1, 1, DIFF_HEADS, seq, LANES), lambda b: (b, 0, 0, 0, 0))
        return _deferred(
            functools.partial(_diff_ctx_kernel, lambda_init), grid=(stream.n_req,),
            in_specs=[vec_spec(1)] * 4 + [pl.BlockSpec((DIFF_HEADS, LANES), lambda b: (0, 0)),
                                          pl.BlockSpec((seq, DIFF_QK), lambda b: (b, 0)), head_spec, head_spec],
            out_specs=pl.BlockSpec((seq, DIFF_V), lambda b: (b, 0)), out_shape=out_shape,
            scratch_shapes=[pltpu.VMEM((8, ATT_Q_TILE, seq), F32)],
            name="diff_attn_ctx",
        )(lq1, lk1, lq2, lk2, subln_g, q, kh, vh)
    ck, cv = cache
    cos, sin = rope
    nh = DIFF_HEADS
    blk = lambda c: pl.BlockSpec((seq, LANES), lambda i: (i // nh, c * nh + i % nh))
    cache_spec = pl.BlockSpec((1, 1, 1, PAST_LEN, LANES), lambda i: (i // nh, 0, i % nh, 0, 0))
    table_spec = pl.BlockSpec((seq, LANES), lambda i: (0, 0))
    return _deferred(
        functools.partial(_diff_lat_kernel, lambda_init), grid=(stream.n_req * nh,),
        in_specs=[vec_spec(2)] * 4 + [pl.BlockSpec((1, 1, LANES), lambda i: (i % nh, 0, 0)),
                                      blk(0), blk(1), blk(2), cache_spec, cache_spec, table_spec, table_spec],
        out_specs=pl.BlockSpec((seq, LANES), lambda i: (i // nh, i % nh)), out_shape=out_shape,
        scratch_shapes=[pltpu.VMEM((2, seq, LANES), BF16), pltpu.VMEM((2, PAST_LEN, LANES), BF16),
                        pltpu.VMEM((seq, 2 * LANES), BF16), pltpu.VMEM((PAST_LEN, 2 * LANES), BF16),
                        pltpu.VMEM((2 * seq // ATT_Q_TILE, ATT_Q_TILE, seq + PAST_LEN), F32)],
        name="diff_attn_lat",
    )(lq1, lk1, lq2, lk2, subln_g.reshape(DIFF_HEADS, 1, LANES), proj, proj, proj, ck, cv, cos, sin)


def _mix_out_kernel(n_in, *refs):
    a_refs = refs[:n_in]
    w_ref, x_ref, gate_ref, g2_ref, sh2_ref, sc2_ref, wr_ref, xn_ref, h2_ref, lg_ref = refs[n_in:]
    kp = D_MODEL // n_in
    wr = wr_ref[...]
    w_hi = wr.astype(BF16)
    w_lo = (wr - w_hi.astype(F32)).astype(BF16)
    acc = None
    for k, a_ref in enumerate(a_refs):
        part = _dot(a_ref[...], w_ref[k * kp:(k + 1) * kp, :].astype(BF16))
        acc = part if acc is None else acc + part
    xn = x_ref[...] + gate_ref[0] * acc
    xn_ref[...] = xn
    h2 = _modnorm(xn, g2_ref[...], sh2_ref[0], sc2_ref[0])
    h_hi = h2.astype(BF16)
    h2_ref[...] = h_hi
    h_lo = (h2 - h_hi.astype(F32)).astype(BF16)
    by_hi = _dot_nt(jnp.concatenate([w_hi, w_lo], axis=0), h_hi)
    lg = by_hi[0:N_EXPERTS] + (by_hi[N_EXPERTS:] + _dot_nt(w_hi, h_lo))
    for c in range(lg.shape[1] // LOGIT_TILE):
        lg_ref[c] = lg[:, c * LOGIT_TILE:(c + 1) * LOGIT_TILE]


def _mix_out(mixed, w_out, x, g2_row, mods, layer, stream, wr_t):
    tm = MIX_TILE
    n_in = len(mixed)
    kp = D_MODEL // n_in
    row_blk = lambda width: pl.BlockSpec((tm, width), lambda i: (i, 0))
    return _deferred(
        functools.partial(_mix_out_kernel, n_in),
        grid=(stream.tokens // tm,),
        in_specs=[row_blk(kp)] * n_in + [
            _resident_spec((D_MODEL, D_MODEL)),
            row_blk(D_MODEL),
            _mod_spec(layer, stream, 2, tm),
            pl.BlockSpec((1, D_MODEL), lambda i: (0, 0)),
            _mod_spec(layer, stream, 3, tm),
            _mod_spec(layer, stream, 4, tm),
            pl.BlockSpec((N_EXPERTS, D_MODEL), lambda i: (0, 0))],
        out_specs=[row_blk(D_MODEL), row_blk(D_MODEL),
                   pl.BlockSpec((tm // LOGIT_TILE, N_EXPERTS, LOGIT_TILE), lambda i: (i, 0, 0))],
        out_shape=[jax.ShapeDtypeStruct((stream.tokens, D_MODEL), F32),
                   jax.ShapeDtypeStruct((stream.tokens, D_MODEL), BF16),
                   jax.ShapeDtypeStruct((stream.tokens // LOGIT_TILE, N_EXPERTS, LOGIT_TILE), F32)],
        name=f"mix_out_l{layer}_s{stream.seq}",
    )(*mixed, w_out, x, mods, g2_row, mods, mods, wr_t)


def _sort_desc_lanes(x):
    rows, n = x.shape
    tiles = [x[:, c * LANES:(c + 1) * LANES] for c in range(n // LANES)]
    lane = lax.broadcasted_iota(I32, (rows, LANES), 1)
    k = 2
    while k <= n:
        j = k // 2
        while j >= 1:
            if j < LANES:
                lower = (lane & j) == 0
                for c in range(len(tiles)):
                    t = tiles[c]
                    partner = jnp.where(lower, pltpu.roll(t, LANES - j, axis=1), pltpu.roll(t, j, axis=1))
                    desc = ((lane & k) == 0) if k < LANES else (((c * LANES) & k) == 0)
                    take_max = (lower == desc) if k < LANES else (lower if desc else jnp.logical_not(lower))
                    tiles[c] = jnp.where(take_max, jnp.maximum(t, partner), jnp.minimum(t, partner))
            else:
                jc = j // LANES
                new = list(tiles)
                for c in range(len(tiles)):
                    take_max = ((c & jc) == 0) == (((c * LANES) & k) == 0)
                    new[c] = (jnp.maximum if take_max else jnp.minimum)(tiles[c], tiles[c ^ jc])
                tiles = new
            j //= 2
        k *= 2
    return tiles


def _router_kernel(cap, lg_ref, pos_ref, g_ref):
    x = lg_ref[...]
    n_b, n_e, n_tok = x.shape
    e = jnp.exp(x - jnp.max(x, axis=1, keepdims=True))
    aff = (e / jnp.sum(e, axis=1, keepdims=True)).reshape(n_b * n_e, n_tok)
    srt = _sort_desc_lanes(aff)
    lane_k = (cap - 1) % LANES
    thr = srt[(cap - 1) // LANES][:, lane_k:lane_k + 1]
    gt = aff > thr
    eq = aff == thr
    n_gt = jnp.sum(gt.astype(F32), axis=1, keepdims=True)
    before = (lax.broadcasted_iota(I32, (n_tok, n_tok), 0)
              < lax.broadcasted_iota(I32, (n_tok, n_tok), 1)).astype(BF16)
    eq_rank = _dot(eq.astype(BF16), before)
    sel = gt | (eq & (eq_rank < cap - n_gt))
    slot = _dot(sel.astype(BF16), before).astype(I32)
    pos_ref[...] = jnp.where(sel, slot, -1).reshape(n_b, n_e, n_tok)
    g_ref[...] = jnp.where(sel, aff, 0.0).reshape(n_b, n_e, n_tok)


def _router(logits, stream):
    shape = (stream.n_req, N_EXPERTS, stream.seq)
    spec = pl.BlockSpec(shape, lambda i: (0, 0, 0))
    return _deferred(
        functools.partial(_router_kernel, stream.cap), grid=(1,),
        in_specs=[spec], out_specs=[spec, spec],
        out_shape=[jax.ShapeDtypeStruct(shape, I32), jax.ShapeDtypeStruct(shape, F32)],
        name=f"router_s{stream.seq}",
    )(logits)


def _gather_kernel(cap, seq, h_ref, pos_ref, xs_ref):
    slot = lax.broadcasted_iota(I32, (cap, seq), 0)
    for r in range(pos_ref.shape[0]):
        onehot = jnp.concatenate([(slot == pos_ref[r, e:e + 1, :]).astype(BF16) for e in range(N_EXPERTS)], axis=0)
        xs = _dot(onehot, h_ref[r * seq:(r + 1) * seq, :]).astype(BF16)
        for e in range(N_EXPERTS):
            xs_ref[e, r * cap:(r + 1) * cap, :] = xs[e * cap:(e + 1) * cap]


def _gather(h2, pos, stream):
    cap, seq, per = stream.cap, stream.seq, stream.req_per_step
    return _deferred(
        functools.partial(_gather_kernel, cap, seq),
        grid=(stream.n_req // per,),
        in_specs=[pl.BlockSpec((per * seq, D_MODEL), lambda i: (i, 0)),
                  pl.BlockSpec((per, N_EXPERTS, seq), lambda i: (i, 0, 0))],
        out_specs=pl.BlockSpec((N_EXPERTS, per * cap, D_MODEL), lambda i: (0, i, 0)),
        out_shape=jax.ShapeDtypeStruct((N_EXPERTS, stream.n_req * cap, D_MODEL), BF16),
        name=f"gather_s{seq}",
    )(h2, pos)


def _ffn_kernel(xa_ref, xb_ref, wg_ref, wu_ref, wd_ref, ya_ref, yb_ref, acc):
    j = pl.program_id(1)
    rows_a = xa_ref.shape[1]

    @pl.when(j == 0)
    def _():
        acc[...] = jnp.zeros_like(acc)

    x = jnp.concatenate([xa_ref[0], xb_ref[0]], axis=0)
    a = _dot(x, wg_ref[0, 0].astype(BF16))
    u = _dot(x, wu_ref[0, 0].astype(BF16))
    acc[...] += _dot(((a * jax.nn.sigmoid(a)) * u).astype(BF16), wd_ref[0, 0].astype(BF16))

    @pl.when(j == pl.num_programs(1) - 1)
    def _():
        ya_ref[0] = acc[0:rows_a, :].astype(BF16)
        yb_ref[0] = acc[rows_a:, :].astype(BF16)


def _ffn(xs_a, xs_b, layer, w_gate, w_up, w_down):
    tf = FF_TILE
    rows_a, rows_b = xs_a.shape[1], xs_b.shape[1]
    x_spec = lambda rows: pl.BlockSpec((1, rows, D_MODEL), lambda e, j: (e, 0, 0))
    rows = rows_a + rows_b
    tile = D_MODEL * tf
    vmem_bytes = 3 * 2 * tile * 4 + 3 * tile * 2 + 2 * rows * tf * 4 + rows * D_MODEL * 4 + 2 * 2 * rows * D_MODEL * 2
    return pl.pallas_call(
        _ffn_kernel,
        grid=(N_EXPERTS, EXPERT_FF // tf),
        in_specs=[x_spec(rows_a), x_spec(rows_b),
                  pl.BlockSpec((1, 1, D_MODEL, tf), lambda e, j: (layer, e, 0, j)),
                  pl.BlockSpec((1, 1, D_MODEL, tf), lambda e, j: (layer, e, 0, j)),
                  pl.BlockSpec((1, 1, tf, D_MODEL), lambda e, j: (layer, e, j, 0))],
        out_specs=[x_spec(rows_a), x_spec(rows_b)],
        out_shape=[jax.ShapeDtypeStruct(xs_a.shape, BF16), jax.ShapeDtypeStruct(xs_b.shape, BF16)],
        scratch_shapes=[pltpu.VMEM((rows_a + rows_b, D_MODEL), F32)],
        compiler_params=_params(2, vmem_bytes),
        name=f"ffn_l{layer}",
    )(xs_a, xs_b, w_gate, w_up, w_down)


def _scatter_kernel(cap, seq, final, y_ref, pos_ref, g_ref, x_ref, gate_ref, fg_ref, o_ref):
    slot = lax.broadcasted_iota(I32, (cap, seq), 0)
    for r in range(pos_ref.shape[0]):
        onehots, gated = [], []
        for e in range(N_EXPERTS):
            hit = slot == pos_ref[r, e:e + 1, :]
            gate = jnp.sum(jnp.where(hit, g_ref[r, e:e + 1, :], 0.0), axis=1, keepdims=True)
            gated.append((y_ref[e, r * cap:(r + 1) * cap, :].astype(F32) * gate).astype(BF16))
            onehots.append(hit.astype(BF16))
        moe = lax.dot_general(jnp.concatenate(onehots, axis=0), jnp.concatenate(gated, axis=0),
                              (((0,), (0,)), ((), ())), preferred_element_type=F32)
        rows = slice(r * seq, (r + 1) * seq)
        xn = x_ref[rows, :] + gate_ref[0] * moe
        if final:
            xn = xn * lax.rsqrt(jnp.mean(xn * xn, axis=-1, keepdims=True) + EPS) * fg_ref[...]
        o_ref[rows, :] = xn


def _scatter(y, pos, g, xn, mods, layer, stream, final, final_g_row):
    cap, seq, per = stream.cap, stream.seq, stream.req_per_step
    tok_blk = pl.BlockSpec((per * seq, D_MODEL), lambda i: (i, 0))
    sel_blk = pl.BlockSpec((per, N_EXPERTS, seq), lambda i: (i, 0, 0))
    return _deferred(
        functools.partial(_scatter_kernel, cap, seq, final),
        grid=(stream.n_req // per,),
        in_specs=[pl.BlockSpec((N_EXPERTS, per * cap, D_MODEL), lambda i: (0, i, 0)),
                  sel_blk, sel_blk, tok_blk,
                  _mod_spec(layer, stream, 5, per * seq),
                  pl.BlockSpec((1, D_MODEL), lambda i: (0, 0))],
        out_specs=tok_blk,
        out_shape=jax.ShapeDtypeStruct((stream.tokens, D_MODEL), F32),
        name=f"scatter_l{layer}_s{seq}",
    )(y, pos, g, xn, mods, final_g_row)


def _axial_rope_tables(rows, head_dim):
    f32 = np.float32
    row = np.repeat(np.arange(rows, dtype=f32), GRID_W)
    col = np.tile(np.arange(GRID_W, dtype=f32), rows)
    nf = head_dim // 4
    inv = (f32(ROPE_BASE) ** (-np.arange(nf, dtype=f32) / f32(nf))).astype(f32)
    ar = row[:, None] * inv[None]
    ac = col[:, None] * inv[None]
    ang = np.concatenate([ar, ar, ac, ac], axis=-1).astype(f32)
    return np.cos(ang).astype(f32), np.sin(ang).astype(f32)


def _block_diag(w):
    rows = [jnp.pad(w[:, n], ((0, 0), (0, 0), (n * LRU_BLOCK, (LRU_BLOCKS - 1 - n) * LRU_BLOCK)))
            for n in range(LRU_BLOCKS)]
    return jnp.concatenate(rows, axis=1)


def kernel(x_prompt, x_sample, cache_win_k, cache_win_v, state_lru, cache_diff_k, cache_diff_v, c, c_ctx, ada_w, ada_b, norm_g, final_g, even_w_in, even_w_out, conv_w, conv_b, lru_wa, lru_ba, lru_wx, lru_bx, lru_lambda, win_sink, odd_w_in, odd_w_out, diff_lq1, diff_lk1, diff_lq2, diff_lk2, diff_subln_g, moe_router, moe_w_gate, moe_w_up, moe_w_down):
    cv_t = jnp.concatenate([c_ctx[None], c, jnp.zeros((COND_ROWS - N_COND, D_MODEL), F32)], axis=0).T
    mods = _adaln(cv_t, ada_w, ada_b).reshape(DEPTH * COND_ROWS, 1, 6 * D_MODEL)

    cos, sin = _axial_rope_tables(DEC_SEQ // GRID_W, WIN_HEAD_DIM)
    rope_win = rope_diff = (jnp.asarray(np.tile(cos, (1, 2))), jnp.asarray(np.tile(sin, (1, 2))))

    xs = [x_prompt.reshape(CTX.tokens, D_MODEL), x_sample.reshape(LAT.tokens, D_MODEL)]
    final_g_row = final_g.reshape(1, D_MODEL)
    outs = {}

    def both(make):
        return _run_each(make(0, CTX), make(1, LAT))

    routers_t = jnp.swapaxes(moe_router, 1, 2)
    for layer in range(DEPTH):
        idx = layer // 2
        even = layer % 2 == 0
        w_in = (even_w_in if even else odd_w_in)[idx]
        w_out = (even_w_out if even else odd_w_out)[idx]
        wr_t = routers_t[layer]
        g1_row = norm_g[layer, 0].reshape(1, D_MODEL)
        g2_row = norm_g[layer, 1].reshape(1, D_MODEL)
        if even:
            wa = _block_diag(lru_wa[idx]).astype(BF16)
            wx = _block_diag(lru_wx[idx]).astype(BF16)
            (proj_c, win_k, win_v), (proj_d,) = _run_each(_proj_win(xs[0], g1_row, mods, layer, CTX, w_in),
                                                         _proj(xs[1], g1_row, mods, layer, LAT, w_in, EVEN_TOKEN_TILE))
            projs = (proj_c, proj_d)
            h0s = (jnp.zeros((CTX.n_req, 2, LRU_WIDTH), F32), state_lru[:, idx])
            (y_c, h_fin), (y_d, _) = both(lambda si, st: _lru(
                projs[si], st, conv_w[idx], conv_b[idx].reshape(1, LRU_WIDTH), wa, wx,
                lru_ba[idx], lru_bx[idx], lru_lambda[idx], h0s[si]))
            pack = lambda t: t[:, idx].transpose(0, 2, 1, 3).reshape(DEC_BATCH, PAST_LEN, WIN_KV)
            (o_c,), (o_d,) = _run_each(
                _win_attn(proj_c, CTX, win_sink[idx]),
                _win_attn(proj_d, LAT, win_sink[idx], cache=(pack(cache_win_k), pack(cache_win_v)), rope=rope_win))
            outs["win_k"] = win_k
            outs["win_v"] = win_v
            outs["lru"] = h_fin[:, None]
            mixed = ([y_c, o_c], [y_d, o_d])
        else:
            lambda_init = 0.8 - 0.6 * math.exp(-0.3 * layer)
            vec = lambda t: t[idx].reshape(1, DIFF_HEAD_DIM)
            args = (vec(diff_lq1), vec(diff_lk1), vec(diff_lq2), vec(diff_lk2), diff_subln_g[idx])
            (q, kh, vh), (proj_d,) = _run_each(_proj_heads(xs[0], g1_row, mods, layer, CTX, w_in),
                                              _proj(xs[1], g1_row, mods, layer, LAT, w_in))
            (o_c,), (o_d,) = _run_each(
                _diff_attn((q, kh, vh), CTX, lambda_init, *args),
                _diff_attn(proj_d, LAT, lambda_init, *args,
                           cache=(cache_diff_k[:, idx:idx + 1], cache_diff_v[:, idx:idx + 1]), rope=rope_diff))
            outs["diff_k"] = kh
            outs["diff_v"] = vh
            mixed = ([o_c], [o_d])

        mix = both(lambda si, st: _mix_out(mixed[si], w_out, xs[si], g2_row, mods, layer, st, wr_t))
        xns, h2s = (mix[0][0], mix[1][0]), (mix[0][1], mix[1][1])

        def request_major(lg, st):
            per_req = st.seq // LOGIT_TILE
            lg = lg.reshape(st.n_req, per_req, N_EXPERTS, LOGIT_TILE).transpose(0, 2, 1, 3)
            return lg.reshape(st.n_req, N_EXPERTS, st.seq)

        lgs = (request_major(mix[0][2], CTX), request_major(mix[1][2], LAT))
        routes = both(lambda si, st: _router(lgs[si], st))
        (rows_c,), (rows_d,) = both(lambda si, st: _gather(h2s[si], routes[si][0], st))
        ys = _ffn(rows_c, rows_d, layer, moe_w_gate, moe_w_up, moe_w_down)
        final = layer == DEPTH - 1
        (x_c,), (x_d,) = both(lambda si, st: _scatter(ys[si], routes[si][0], routes[si][1], xns[si], mods, layer, st,
                                                      final, final_g_row))
        xs = [x_c, x_d]


    y_prompt = xs[0].reshape(BATCH, SEQ, D_MODEL)
    y_sample = xs[1].reshape(DEC_BATCH, DEC_SEQ, D_MODEL)
    return (y_prompt, y_sample, outs["win_k"], outs["win_v"], outs["lru"], outs["diff_k"], outs["diff_v"])
```

```python
import functools
import math
from typing import Callable, NamedTuple

import jax
import jax.numpy as jnp
import numpy as np
from jax import lax
from jax.experimental import pallas as pl
from jax.experimental.pallas import tpu as pltpu

F32 = jnp.float32
BF16 = jnp.bfloat16
I32 = jnp.int32

D_MODEL = 1024
BATCH = 16
SEQ = 256
DEPTH = 2
DEC_BATCH = 2
DEC_SEQ = 1024
PAST_LEN = 512
GRID_W = 64
LRU_WIDTH = D_MODEL // 2
LRU_BLOCKS = 8
LRU_BLOCK = LRU_WIDTH // LRU_BLOCKS
CONV_W = 4
LRU_C = 8.0
WIN_HEADS = 8
WIN_KV_HEADS = 2
WIN_REP = WIN_HEADS // WIN_KV_HEADS
WIN_HEAD_DIM = 64
WINDOW = 128
WIN_Q = WIN_HEADS * WIN_HEAD_DIM
WIN_KV = WIN_KV_HEADS * WIN_HEAD_DIM
EVEN_IN = 2 * LRU_WIDTH + WIN_Q + 2 * WIN_KV
DIFF_HEADS = 8
DIFF_HEAD_DIM = 64
DIFF_QK = DIFF_HEADS * 2 * DIFF_HEAD_DIM
DIFF_V = DIFF_HEADS * 2 * DIFF_HEAD_DIM
ODD_IN = 2 * DIFF_QK + DIFF_V
N_EXPERTS = 16
EXPERT_FF = 2 * D_MODEL
CAPACITY_FACTOR = 2
ROPE_BASE = 10000.0
EPS = 1e-6
NEG_INF = -1e30

LANES = 128
SUBLANES = 8
VMEM_LIMIT_BYTES = 48 * 1024 * 1024

N_COND = 1 + DEC_BATCH
COND_ROWS = SUBLANES
ADALN_TILE = 2048
TOKEN_TILE = 512
EVEN_TOKEN_TILE = 1024
MIX_TILE = 1024
ROUTE_ROWS = 1024
LOGIT_TILE = 256
FF_TILE = 1024
FFN_VMEM_LIMIT_BYTES = 56 * 1024 * 1024
LRU_CHUNK = 256
WIN_BLOCKS_PER_ITER = 2
ATT_Q_TILE = 256
ATT_K_TILE = 256


class Stream:
    def __init__(self, n_req, seq, cond0, cond_step):
        self.n_req, self.seq, self.cond0, self.cond_step = n_req, seq, cond0, cond_step
        self.tokens = n_req * seq
        self.cap = CAPACITY_FACTOR * seq // N_EXPERTS
        self.req_per_step = max(1, ROUTE_ROWS // seq)

    def cond_of_row(self, row):
        return self.cond0 + self.cond_step * (row // self.seq)


CTX = Stream(BATCH, SEQ, 0, 0)
LAT = Stream(DEC_BATCH, DEC_SEQ, 1, 1)


def _params(n_axes, vmem_limit_bytes=VMEM_LIMIT_BYTES):
    return pltpu.CompilerParams(dimension_semantics=("arbitrary",) * n_axes,
                                vmem_limit_bytes=vmem_limit_bytes)


class Call(NamedTuple):
    kernel: Callable
    steps: int
    in_specs: tuple
    out_specs: tuple
    out_shapes: tuple
    scratch: tuple
    args: tuple
    name: str


def _deferred(kernel, *, grid, in_specs, out_specs, out_shape, scratch_shapes=(), name):
    (steps,) = grid
    as_tuple = lambda v: tuple(v) if isinstance(v, (list, tuple)) else (v,)
    return lambda *args: Call(kernel, steps, tuple(in_specs), as_tuple(out_specs), as_tuple(out_shape),
                              tuple(scratch_shapes), args, name)


def _run(call):
    return pl.pallas_call(
        call.kernel, grid=(call.steps,), in_specs=list(call.in_specs), out_specs=list(call.out_specs),
        out_shape=list(call.out_shapes), scratch_shapes=list(call.scratch),
        compiler_params=_params(1), name=call.name)(*call.args)


def _run_each(a, b):
    return _run(a), _run(b)


def _resident_spec(shape):
    return pl.BlockSpec(shape, lambda i: (0, 0), pipeline_mode=pl.Buffered(1))


def _mod_spec(layer, stream, k, rows_per_step):
    return pl.BlockSpec(
        (1, 1, D_MODEL),
        lambda i, *_: (layer * COND_ROWS + stream.cond_of_row(i * rows_per_step), 0, k))


def _dot(a, b):
    return jnp.dot(a, b, preferred_element_type=F32)


def _dot_nt(a, b):
    return lax.dot_general(a, b, (((1,), (1,)), ((), ())), preferred_element_type=F32)


def _modnorm(x, g, shift, scale):
    y = x * lax.rsqrt(jnp.mean(x * x, axis=-1, keepdims=True) + EPS)
    return (y * g) * (1.0 + scale) + shift


def _lane_half_masks(shape):
    lane = lax.broadcasted_iota(I32, shape, len(shape) - 1)
    left = (lane & (LANES - 1)) < LANES // 2
    return left, jnp.logical_not(left)


def _rope(x, cos, sin):
    parts = []
    for c in range(x.shape[1] // LANES):
        xs = x[:, c * LANES:(c + 1) * LANES]
        lane = lax.broadcasted_iota(I32, xs.shape, 1)
        first = (lane & 31) < 16
        rot = jnp.where(first, -pltpu.roll(xs, LANES - 16, axis=1), pltpu.roll(xs, 16, axis=1))
        parts.append(xs * cos + rot * sin)
    return parts[0] if len(parts) == 1 else jnp.concatenate(parts, axis=1)


def _adaln_kernel(cv_ref, w_ref, b_ref, o_ref):
    cv = cv_ref[...]
    s = cv * jax.nn.sigmoid(cv)
    w = w_ref[0]
    ridx = lax.broadcasted_iota(I32, (COND_ROWS, w.shape[1]), 0)
    out = jnp.zeros((COND_ROWS, w.shape[1]), F32)
    for r in range(N_COND):
        out = jnp.where(ridx == r, jnp.sum(w * s[:, r:r + 1], axis=0, keepdims=True), out)
    o_ref[0] = out + b_ref[0]


def _adaln(cv_t, ada_w, ada_b):
    tn = ADALN_TILE
    return pl.pallas_call(
        _adaln_kernel,
        grid=(DEPTH, 6 * D_MODEL // tn),
        in_specs=[pl.BlockSpec((D_MODEL, COND_ROWS), lambda l, j: (0, 0)),
                  pl.BlockSpec((1, D_MODEL, tn), lambda l, j: (l, 0, j)),
                  pl.BlockSpec((1, 1, tn), lambda l, j: (l, 0, j))],
        out_specs=pl.BlockSpec((1, COND_ROWS, tn), lambda l, j: (l, 0, j)),
        out_shape=jax.ShapeDtypeStruct((DEPTH, COND_ROWS, 6 * D_MODEL), F32),
        compiler_params=_params(2),
        name="adaln",
    )(cv_t, ada_w, ada_b.reshape(DEPTH, 1, 6 * D_MODEL))


def _proj_kernel(x_ref, g_ref, sh_ref, sc_ref, w_hbm, o_ref, w_s, sem):
    n_chunks = sem.shape[0]
    cols = w_s.shape[1] // n_chunks
    copies = [pltpu.make_async_copy(w_hbm.at[:, c * cols:(c + 1) * cols], w_s.at[:, c * cols:(c + 1) * cols],
                                    sem.at[c]) for c in range(n_chunks)]
    make_h = lambda: _modnorm(x_ref[...], g_ref[...], sh_ref[0], sc_ref[0]).astype(BF16)
    first = pl.program_id(0) == 0

    @pl.when(first)
    def _first_step():
        for cp in copies:
            cp.start()
        h = make_h()
        for c, cp in enumerate(copies):
            cp.wait()
            o_ref[:, c * cols:(c + 1) * cols] = _dot(h, w_s[:, c * cols:(c + 1) * cols].astype(BF16))

    @pl.when(jnp.logical_not(first))
    def _later_steps():
        o_ref[...] = _dot(make_h(), w_s[...].astype(BF16))


def _proj(x, g_row, mods, layer, stream, w, tm=TOKEN_TILE):
    n_out = w.shape[1]
    n_chunks = 3 if n_out % (3 * LANES) == 0 else 2
    return _deferred(
        _proj_kernel,
        grid=(stream.tokens // tm,),
        in_specs=[pl.BlockSpec((tm, D_MODEL), lambda i: (i, 0)),
                  pl.BlockSpec((1, D_MODEL), lambda i: (0, 0)),
                  _mod_spec(layer, stream, 0, tm),
                  _mod_spec(layer, stream, 1, tm),
                  pl.BlockSpec(memory_space=pl.ANY)],
        out_specs=pl.BlockSpec((tm, n_out), lambda i: (i, 0)),
        out_shape=jax.ShapeDtypeStruct((stream.tokens, n_out), F32),
        scratch_shapes=(pltpu.VMEM((D_MODEL, n_out), F32), pltpu.SemaphoreType.DMA((n_chunks,))),
        name=f"proj_l{layer}_s{stream.seq}",
    )(x, g_row, mods, mods, w)


def _proj_heads_kernel(seq, x_ref, g_ref, sh_ref, sc_ref, w_ref, q_ref, k_ref, v_ref):
    h = _modnorm(x_ref[...], g_ref[...], sh_ref[0], sc_ref[0])
    res = _dot(h.astype(BF16), w_ref[...].astype(BF16))
    q_ref[...] = res[:, 0:DIFF_QK].astype(BF16)
    for r in range(x_ref.shape[0] // seq):
        rows = slice(r * seq, (r + 1) * seq)
        for hh in range(DIFF_HEADS):
            k_ref[r, 0, hh] = res[rows, DIFF_QK + hh * LANES:DIFF_QK + (hh + 1) * LANES]
            v_ref[r, 0, hh] = res[rows, 2 * DIFF_QK + hh * LANES:2 * DIFF_QK + (hh + 1) * LANES]


def _proj_heads(x, g_row, mods, layer, stream, w):
    tm, seq = TOKEN_TILE, stream.seq
    head_shape = (stream.n_req, 1, DIFF_HEADS, seq, 2 * DIFF_HEAD_DIM)
    head_spec = pl.BlockSpec((tm // seq, 1, DIFF_HEADS, seq, 2 * DIFF_HEAD_DIM), lambda i: (i, 0, 0, 0, 0))
    return _deferred(
        functools.partial(_proj_heads_kernel, seq),
        grid=(stream.tokens // tm,),
        in_specs=[pl.BlockSpec((tm, D_MODEL), lambda i: (i, 0)),
                  pl.BlockSpec((1, D_MODEL), lambda i: (0, 0)),
                  _mod_spec(layer, stream, 0, tm),
                  _mod_spec(layer, stream, 1, tm),
                  _resident_spec((D_MODEL, ODD_IN))],
        out_specs=[pl.BlockSpec((tm, DIFF_QK), lambda i: (i, 0)), head_spec, head_spec],
        out_shape=[jax.ShapeDtypeStruct((stream.tokens, DIFF_QK), BF16),
                   jax.ShapeDtypeStruct(head_shape, F32), jax.ShapeDtypeStruct(head_shape, F32)],
        name=f"proj_heads_l{layer}_s{seq}",
    )(x, g_row, mods, mods, w)


def _proj_win_kernel(seq, x_ref, g_ref, sh_ref, sc_ref, w_ref, o_ref, k_ref, v_ref):
    h = _modnorm(x_ref[...], g_ref[...], sh_ref[0], sc_ref[0])
    res = _dot(h.astype(BF16), w_ref[...].astype(BF16))
    o_ref[...] = res
    k0 = 2 * LRU_WIDTH + WIN_Q
    for r in range(x_ref.shape[0] // seq):
        rows = slice(r * seq, (r + 1) * seq)
        for g in range(WIN_KV_HEADS):
            k_ref[r, 0, g] = res[rows, k0 + g * WIN_HEAD_DIM:k0 + (g + 1) * WIN_HEAD_DIM]
            v_ref[r, 0, g] = res[rows, k0 + WIN_KV + g * WIN_HEAD_DIM:k0 + WIN_KV + (g + 1) * WIN_HEAD_DIM]


def _proj_win(x, g_row, mods, layer, stream, w):
    tm, seq = EVEN_TOKEN_TILE, stream.seq
    head_shape = (stream.n_req, 1, WIN_KV_HEADS, seq, WIN_HEAD_DIM)
    head_spec = pl.BlockSpec((tm // seq, 1, WIN_KV_HEADS, seq, WIN_HEAD_DIM), lambda i: (i, 0, 0, 0, 0))
    return _deferred(
        functools.partial(_proj_win_kernel, seq),
        grid=(stream.tokens // tm,),
        in_specs=[pl.BlockSpec((tm, D_MODEL), lambda i: (i, 0)),
                  pl.BlockSpec((1, D_MODEL), lambda i: (0, 0)),
                  _mod_spec(layer, stream, 0, tm),
                  _mod_spec(layer, stream, 1, tm),
                  _resident_spec((D_MODEL, EVEN_IN))],
        out_specs=[pl.BlockSpec((tm, EVEN_IN), lambda i: (i, 0)), head_spec, head_spec],
        out_shape=[jax.ShapeDtypeStruct((stream.tokens, EVEN_IN), F32),
                   jax.ShapeDtypeStruct(head_shape, F32), jax.ShapeDtypeStruct(head_shape, F32)],
        name=f"proj_win_l{layer}_s{seq}",
    )(x, g_row, mods, mods, w)


def _lru_kernel(seq, xl_ref, gl_ref, cw_ref, cb_ref, wa_ref, wx_ref, ba_ref, bx_ref, lam_ref, h0_ref,
                y_ref, hfin_ref, xe, af, bf, ab, bb):
    width = LRU_WIDTH
    ch = LRU_CHUNK
    seg = seq // SUBLANES
    lead = (CONV_W // 2) * SUBLANES

    def to_segment_major(x):
        return jnp.transpose(x.reshape(SUBLANES, seg, LANES), (1, 0, 2)).reshape(seq, LANES)

    def to_time_major(x):
        return jnp.transpose(x.reshape(seg, SUBLANES, LANES), (1, 0, 2)).reshape(seq, LANES)

    for c in range(width // LANES):
        cols = slice(c * LANES, (c + 1) * LANES)
        xe[lead:lead + seq, cols] = to_segment_major(xl_ref[:, cols])
    sub = lax.broadcasted_iota(I32, (SUBLANES, width), 0)
    for k in range(CONV_W // 2):
        prev = xe[lead + (seg - 2 + k) * SUBLANES:lead + (seg - 1 + k) * SUBLANES, :]
        xe[k * SUBLANES:(k + 1) * SUBLANES, :] = jnp.where(sub >= 1, pltpu.roll(prev, 1, axis=0), 0.0)
    first = xe[lead:lead + SUBLANES, :]
    xe[lead + seq:lead + seq + SUBLANES, :] = jnp.where(sub < SUBLANES - 1,
                                                        pltpu.roll(first, SUBLANES - 1, axis=0), 0.0)

    lam = lam_ref[...]
    z = -lam
    softplus = jnp.maximum(z, 0.0) + jnp.log1p(jnp.exp(-jnp.abs(z)))
    cw = cw_ref[...]
    cb = cb_ref[...]

    def gates_chunk(c, carry):
        r0 = pl.multiple_of(c * ch, ch)
        xc = xe[pl.ds(r0, ch), :] * cw[0:1]
        for j in range(1, CONV_W):
            xc = xc + xe[pl.ds(pl.multiple_of(r0 + j * SUBLANES, SUBLANES), ch), :] * cw[j:j + 1]
        xc = xc + cb
        xcb = xc.astype(BF16)
        for d, (a_s, b_s) in enumerate(((af, bf), (ab, bb))):
            r = jax.nn.sigmoid(_dot(xcb, wa_ref[d]) + ba_ref[d:d + 1])
            ig = jax.nn.sigmoid(_dot(xcb, wx_ref[d]) + bx_ref[d:d + 1])
            log_a = (-LRU_C * r) * softplus[d:d + 1]
            a = jnp.exp(log_a)
            v = 1.0 - a * a
            a_s[pl.ds(r0, ch), :] = a
            b_s[pl.ds(r0, ch), :] = jnp.where(v > 0.0, v * lax.rsqrt(v), 0.0) * (ig * xc)
        return carry

    lax.fori_loop(0, seq // ch, gates_chunk, 0)

    def scan_step(k, carry):
        hf, pf, hb, pb = carry
        rf = pl.multiple_of(k * SUBLANES, SUBLANES)
        rb = pl.multiple_of((seg - 1 - k) * SUBLANES, SUBLANES)
        a = af[pl.ds(rf, SUBLANES), :]
        hf = a * hf + bf[pl.ds(rf, SUBLANES), :]
        pf = a * pf
        bf[pl.ds(rf, SUBLANES), :] = hf
        af[pl.ds(rf, SUBLANES), :] = pf
        a = ab[pl.ds(rb, SUBLANES), :]
        hb = a * hb + bb[pl.ds(rb, SUBLANES), :]
        pb = a * pb
        bb[pl.ds(rb, SUBLANES), :] = hb
        ab[pl.ds(rb, SUBLANES), :] = pb
        return hf, pf, hb, pb

    zeros = jnp.zeros((SUBLANES, width), F32)
    ones = jnp.ones((SUBLANES, width), F32)
    hf, pf, hb, pb = lax.fori_loop(0, seg, scan_step, (zeros, ones, zeros, ones))

    h0 = h0_ref[0]
    state, carry_f = h0[0:1], zeros
    for g in range(SUBLANES):
        carry_f = jnp.where(sub == g, state, carry_f)
        state = pf[g:g + 1] * state + hf[g:g + 1]
    hfin_ref[0, 0:1, :] = state
    state, carry_b = h0[1:2], zeros
    for g in reversed(range(SUBLANES)):
        carry_b = jnp.where(sub == g, state, carry_b)
        state = pb[g:g + 1] * state + hb[g:g + 1]
    hfin_ref[0, 1:2, :] = state

    groups = (ch // SUBLANES, SUBLANES, width)

    def combine_chunk(c, carry):
        r0 = pl.multiple_of(c * ch, ch)
        rows = pl.ds(r0, ch)
        h_fwd = bf[rows, :].reshape(groups) + af[rows, :].reshape(groups) * carry_f
        h_bwd = bb[rows, :].reshape(groups) + ab[rows, :].reshape(groups) * carry_b
        bf[rows, :] = (h_fwd + h_bwd).reshape(ch, width)
        return carry

    lax.fori_loop(0, seq // ch, combine_chunk, 0)

    for c in range(width // LANES):
        cols = slice(c * LANES, (c + 1) * LANES)
        y_ref[:, cols] = (to_time_major(bf[:, cols]) * jax.nn.gelu(gl_ref[:, cols])).astype(BF16)


def _lru(proj, stream, conv_w, conv_b, wa, wx, ba, bx, lam, h0):
    seq, width = stream.seq, LRU_WIDTH
    full2 = lambda b: (0, 0)
    full3 = lambda b: (0, 0, 0)
    return _deferred(
        functools.partial(_lru_kernel, seq),
        grid=(stream.n_req,),
        in_specs=[pl.BlockSpec((seq, width), lambda b: (b, 0)),
                  pl.BlockSpec((seq, width), lambda b: (b, 1)),
                  pl.BlockSpec((CONV_W, width), full2),
                  pl.BlockSpec((1, width), full2),
                  pl.BlockSpec((2, width, width), full3),
                  pl.BlockSpec((2, width, width), full3),
                  pl.BlockSpec((2, width), full2),
                  pl.BlockSpec((2, width), full2),
                  pl.BlockSpec((2, width), full2),
                  pl.BlockSpec((1, 2, width), lambda b: (b, 0, 0))],
        out_specs=[pl.BlockSpec((seq, width), lambda b: (b, 0)),
                   pl.BlockSpec((1, 2, width), lambda b: (b, 0, 0))],
        out_shape=[jax.ShapeDtypeStruct((stream.tokens, width), BF16),
                   jax.ShapeDtypeStruct((stream.n_req, 2, width), F32)],
        scratch_shapes=[pltpu.VMEM((seq + (CONV_W - 1) * SUBLANES, width), F32)] + [pltpu.VMEM((seq, width), F32)] * 4,
        name=f"lru_s{seq}",
    )(proj, proj, conv_w, conv_b, wa, wx, ba, bx, lam, h0)


def _attend(q, chunks, s_ref):
    tile_max = None
    spans = []
    off = 0
    for keys, _ in chunks:
        s = _dot_nt(q, keys())
        n = s.shape[1]
        s_ref[:, off:off + n] = s
        for c in range(n // LANES):
            t = s[:, c * LANES:(c + 1) * LANES]
            tile_max = t if tile_max is None else jnp.maximum(tile_max, t)
        spans.append((off, n))
        off += n
    m = jnp.max(tile_max, axis=-1, keepdims=True)
    acc = None
    for (_, values), (o, n) in zip(chunks, spans):
        part = _dot(jnp.exp(s_ref[:, o:o + n] - m).astype(BF16), values())
        acc = part if acc is None else acc + part
    return acc


def _split_groups(kk):
    left, right = _lane_half_masks(kk.shape)
    g0_l = jnp.where(left, kk, 0.0)
    g1_r = jnp.where(right, kk, 0.0)
    return ((g0_l, pltpu.roll(g0_l, LANES // 2, axis=1)), (pltpu.roll(g1_r, LANES // 2, axis=1), g1_r))


def _win_ctx_kernel(sink_ref, q_ref, kv_ref, o_ref):
    scale = WIN_HEAD_DIM ** -0.5
    seq = q_ref.shape[0]
    ks = _split_groups(kv_ref[:, 0:LANES])
    vs = _split_groups(kv_ref[:, LANES:2 * LANES])
    top = lax.broadcasted_iota(I32, (2 * seq, 1), 0) < seq
    outs = [None] * (WIN_HEADS // 2)
    for g in range(WIN_KV_HEADS):
        pairs = (2 * g, 2 * g + 1)
        qs = jnp.concatenate([q_ref[:, p * LANES:(p + 1) * LANES] for p in pairs], axis=0)
        qs = (qs * scale).astype(BF16)
        for side in range(2):
            sk = jnp.where(top, sink_ref[2 * pairs[0] + side], sink_ref[2 * pairs[1] + side])
            s = _dot_nt(qs, ks[g][side].astype(BF16))
            m = jnp.maximum(jnp.max(s, axis=-1, keepdims=True), sk)
            e = jnp.exp(s - m)
            den = jnp.sum(e, axis=-1, keepdims=True) + jnp.exp(sk - m)
            o = _dot(e.astype(BF16), vs[g][side].astype(BF16)) * (1.0 / den)
            for k, p in enumerate(pairs):
                part = o[k * seq:(k + 1) * seq]
                outs[p] = part if outs[p] is None else outs[p] + part
    for p in range(WIN_HEADS // 2):
        o_ref[:, p * LANES:(p + 1) * LANES] = outs[p].astype(BF16)


def _win_lat_kernel(sink_ref, q_ref, kv_ref, ck_ref, cv_ref, cos_ref, sin_ref, o_ref,
                    kl_s, vl_s, kc_s, vc_s):
    seq, wn = DEC_SEQ, WINDOW
    scale = WIN_HEAD_DIM ** -0.5
    kr = _rope(kv_ref[:, 0:LANES], cos_ref[...], sin_ref[...])
    for src, dst in ((_split_groups(kr), kl_s), (_split_groups(kv_ref[:, LANES:2 * LANES]), vl_s),
                     (_split_groups(ck_ref[0]), kc_s), (_split_groups(cv_ref[0]), vc_s)):
        for g in range(WIN_KV_HEADS):
            for side in range(2):
                dst[2 * g + side] = src[g][side].astype(BF16)

    top = lax.broadcasted_iota(I32, (2 * wn, 1), 0) < wn

    def q_block(i, carry):
        for sub in range(WIN_BLOCKS_PER_ITER):
            blk = WIN_BLOCKS_PER_ITER * i + sub
            r0 = pl.multiple_of(blk * wn, wn)
            start = pl.multiple_of(jnp.clip((blk - 1) * wn, 0, seq - 3 * wn), wn)
            qr = _rope(q_ref[pl.ds(r0, wn), :], cos_ref[pl.ds(r0, wn), :], sin_ref[pl.ds(r0, wn), :]) * scale
            qpos = r0 + (lax.broadcasted_iota(I32, (2 * wn, 3 * wn), 0) & (wn - 1))
            kpos = start + lax.broadcasted_iota(I32, (2 * wn, 3 * wn), 1)
            valid = jnp.abs(qpos - kpos) <= wn
            outs = [None] * (WIN_HEADS // 2)
            for g in range(WIN_KV_HEADS):
                pairs = (2 * g, 2 * g + 1)
                qs = jnp.concatenate([qr[:, p * LANES:(p + 1) * LANES] for p in pairs], axis=0).astype(BF16)
                for side in range(2):
                    idx = 2 * g + side
                    sk = jnp.where(top, sink_ref[2 * pairs[0] + side], sink_ref[2 * pairs[1] + side])
                    sl = _dot_nt(qs, kl_s[idx, pl.ds(start, 3 * wn), :])
                    sl = jnp.where(valid, sl, NEG_INF)
                    sc = _dot_nt(qs, kc_s[idx])
                    m = jnp.maximum(jnp.maximum(jnp.max(sl, axis=-1, keepdims=True),
                                                jnp.max(sc, axis=-1, keepdims=True)), sk)
                    el = jnp.exp(sl - m)
                    ec = jnp.exp(sc - m)
                    den = (jnp.sum(el, axis=-1, keepdims=True) + jnp.sum(ec, axis=-1, keepdims=True)
                           + jnp.exp(sk - m))
                    o = (_dot(el.astype(BF16), vl_s[idx, pl.ds(start, 3 * wn), :])
                         + _dot(ec.astype(BF16), vc_s[idx])) * (1.0 / den)
                    for k, p in enumerate(pairs):
                        part = o[k * wn:(k + 1) * wn]
                        outs[p] = part if outs[p] is None else outs[p] + part
            for p in range(WIN_HEADS // 2):
                o_ref[pl.ds(r0, wn), p * LANES:(p + 1) * LANES] = outs[p].astype(BF16)
        return carry

    lax.fori_loop(0, seq // (WIN_BLOCKS_PER_ITER * wn), q_block, 0)


def _win_attn(proj, stream, sink, cache=None, rope=None):
    seq = stream.seq
    q_spec = pl.BlockSpec((seq, WIN_Q), lambda b: (b, 2 * LRU_WIDTH // WIN_Q))
    kv_spec = pl.BlockSpec((seq, 2 * WIN_KV), lambda b: (b, (2 * LRU_WIDTH + WIN_Q) // (2 * WIN_KV)))
    sink_spec = pl.BlockSpec(memory_space=pltpu.SMEM)
    out_spec = pl.BlockSpec((seq, WIN_Q), lambda b: (b, 0))
    out_shape = jax.ShapeDtypeStruct((stream.tokens, WIN_Q), BF16)
    if cache is None:
        return _deferred(
            _win_ctx_kernel, grid=(stream.n_req,),
            in_specs=[sink_spec, q_spec, kv_spec], out_specs=out_spec, out_shape=out_shape,
            name="win_attn_ctx",
        )(sink, proj, proj)
    ck, cv = cache
    cos, sin = rope
    cache_spec = pl.BlockSpec((1, PAST_LEN, LANES), lambda b: (b, 0, 0))
    table_spec = pl.BlockSpec((seq, LANES), lambda b: (0, 0))
    return _deferred(
        _win_lat_kernel, grid=(stream.n_req,),
        in_specs=[sink_spec, q_spec, kv_spec, cache_spec, cache_spec, table_spec, table_spec],
        out_specs=out_spec, out_shape=out_shape,
        scratch_shapes=[pltpu.VMEM((4, seq, LANES), BF16), pltpu.VMEM((4, seq, LANES), BF16),
                        pltpu.VMEM((4, PAST_LEN, LANES), BF16), pltpu.VMEM((4, PAST_LEN, LANES), BF16)],
        name="win_attn_lat",
    )(sink, proj, proj, ck, cv, cos, sin)


def _diff_lambda(lq1_ref, lk1_ref, lq2_ref, lk2_ref, lambda_init):
    t1 = jnp.sum(lq1_ref[...] * lk1_ref[...], axis=-1, keepdims=True)
    t2 = jnp.sum(lq2_ref[...] * lk2_ref[...], axis=-1, keepdims=True)
    return jnp.exp(t1) - jnp.exp(t2) + lambda_init


def _subln(o, g_row, lambda_init):
    o = o * lax.rsqrt(jnp.mean(o * o, axis=-1, keepdims=True) + EPS) * g_row
    return o * (1.0 - lambda_init)


def _component_keys(k):
    left, right = _lane_half_masks(k.shape)
    return jnp.where(left, k, 0.0).astype(BF16), jnp.where(right, k, 0.0).astype(BF16)


def _values_with_ones(v):
    return jnp.concatenate([v.astype(BF16), jnp.ones(v.shape, BF16)], axis=1)


def _diff_combine(accs, lam):
    o1, o2 = accs[0][:, 0:LANES], accs[1][:, 0:LANES]
    return o1 * (1.0 / accs[0][:, LANES:]) - o2 * (lam * (1.0 / accs[1][:, LANES:]))


def _diff_ctx_kernel(lambda_init, lq1_ref, lk1_ref, lq2_ref, lk2_ref, sg_ref, q_ref, k_ref, v_ref, o_ref, s_ref):
    scale = DIFF_HEAD_DIM ** -0.5
    tq = ATT_Q_TILE
    lam = _diff_lambda(lq1_ref, lk1_ref, lq2_ref, lk2_ref, lambda_init)
    for h in range(DIFF_HEADS):
        cols = slice(h * LANES, (h + 1) * LANES)
        keys = _component_keys(k_ref[0, 0, h])
        vals = _values_with_ones(v_ref[0, 0, h])
        for qt in range(q_ref.shape[0] // tq):
            rows = slice(qt * tq, (qt + 1) * tq)
            q = (q_ref[rows, cols] * scale).astype(BF16)
            slots = [s_ref.at[(4 * h + 2 * qt + c) % s_ref.shape[0]] for c in range(2)]
            accs = [_attend(q, [(lambda: keys[c], lambda: vals)], slots[c]) for c in range(2)]
            o = _diff_combine(accs, lam)
            o_ref[rows, cols] = _subln(o, sg_ref[h:h + 1, :], lambda_init).astype(BF16)


def _diff_lat_kernel(lambda_init, lq1_ref, lk1_ref, lq2_ref, lk2_ref, sg_ref, q_ref, k_ref, v_ref,
                     ck_ref, cv_ref, cos_ref, sin_ref, o_ref, kl_s, kc_s, vl_s, vc_s, s_ref):
    seq, tq, tk = DEC_SEQ, ATT_Q_TILE, ATT_K_TILE
    scale = DIFF_HEAD_DIM ** -0.5
    lam = _diff_lambda(lq1_ref, lk1_ref, lq2_ref, lk2_ref, lambda_init)
    for src, dst in ((_rope(k_ref[...], cos_ref[...], sin_ref[...]), kl_s), (ck_ref[0, 0, 0], kc_s)):
        dst[0], dst[1] = _component_keys(src)
    vl_s[...] = _values_with_ones(v_ref[...])
    vc_s[...] = _values_with_ones(cv_ref[0, 0, 0])
    sg = sg_ref[0]

    for qt in range(seq // tq):
        rows = slice(qt * tq, (qt + 1) * tq)
        qr = (_rope(q_ref[rows, :], cos_ref[rows, :], sin_ref[rows, :]) * scale).astype(BF16)
        accs = []
        for comp in range(2):
            chunks = [(lambda j=j: kl_s[comp, j * tk:(j + 1) * tk, :],
                       lambda j=j: vl_s[j * tk:(j + 1) * tk, :]) for j in range(seq // tk)]
            chunks += [(lambda j=j: kc_s[comp, j * tk:(j + 1) * tk, :],
                        lambda j=j: vc_s[j * tk:(j + 1) * tk, :]) for j in range(PAST_LEN // tk)]
            accs.append(_attend(qr, chunks, s_ref.at[2 * qt + comp]))
        o_ref[rows, :] = _subln(_diff_combine(accs, lam), sg, lambda_init).astype(BF16)


def _diff_attn(proj, stream, lambda_init, lq1, lk1, lq2, lk2, subln_g, cache=None, rope=None):
    seq = stream.seq
    vec_spec = lambda nd: pl.BlockSpec((1, DIFF_HEAD_DIM), lambda *_: (0, 0))
    out_shape = jax.ShapeDtypeStruct((stream.tokens, DIFF_V), BF16)
    if cache is None:
        q, kh, vh = proj
        head_spec = pl.BlockSpec((1, 1, DIFF_HEADS, seq, LANES), lambda b: (b, 0, 0, 0, 0))
        return _deferred(
            functools.partial(_diff_ctx_kernel, lambda_init), grid=(stream.n_req,),
            in_specs=[vec_spec(1)] * 4 + [pl.BlockSpec((DIFF_HEADS, LANES), lambda b: (0, 0)),
                                          pl.BlockSpec((seq, DIFF_QK), lambda b: (b, 0)), head_spec, head_spec],
            out_specs=pl.BlockSpec((seq, DIFF_V), lambda b: (b, 0)), out_shape=out_shape,
            scratch_shapes=[pltpu.VMEM((8, ATT_Q_TILE, seq), F32)],
            name="diff_attn_ctx",
        )(lq1, lk1, lq2, lk2, subln_g, q, kh, vh)
    ck, cv = cache
    cos, sin = rope
    nh = DIFF_HEADS
    blk = lambda c: pl.BlockSpec((seq, LANES), lambda i: (i // nh, c * nh + i % nh))
    cache_spec = pl.BlockSpec((1, 1, 1, PAST_LEN, LANES), lambda i: (i // nh, 0, i % nh, 0, 0))
    table_spec = pl.BlockSpec((seq, LANES), lambda i: (0, 0))
    return _deferred(
        functools.partial(_diff_lat_kernel, lambda_init), grid=(stream.n_req * nh,),
        in_specs=[vec_spec(2)] * 4 + [pl.BlockSpec((1, 1, LANES), lambda i: (i % nh, 0, 0)),
                                      blk(0), blk(1), blk(2), cache_spec, cache_spec, table_spec, table_spec],
        out_specs=pl.BlockSpec((seq, LANES), lambda i: (i // nh, i % nh)), out_shape=out_shape,
        scratch_shapes=[pltpu.VMEM((2, seq, LANES), BF16), pltpu.VMEM((2, PAST_LEN, LANES), BF16),
                        pltpu.VMEM((seq, 2 * LANES), BF16), pltpu.VMEM((PAST_LEN, 2 * LANES), BF16),
                        pltpu.VMEM((2 * seq // ATT_Q_TILE, ATT_Q_TILE, seq + PAST_LEN), F32)],
        name="diff_attn_lat",
    )(lq1, lk1, lq2, lk2, subln_g.reshape(DIFF_HEADS, 1, LANES), proj, proj, proj, ck, cv, cos, sin)


def _mix_out_kernel(n_in, *refs):
    a_refs = refs[:n_in]
    w_ref, x_ref, gate_ref, g2_ref, sh2_ref, sc2_ref, wr_ref, xn_ref, h2_ref, lg_ref = refs[n_in:]
    kp = D_MODEL // n_in
    wr = wr_ref[...]
    w_hi = wr.astype(BF16)
    w_lo = (wr - w_hi.astype(F32)).astype(BF16)
    acc = None
    for k, a_ref in enumerate(a_refs):
        part = _dot(a_ref[...], w_ref[k * kp:(k + 1) * kp, :].astype(BF16))
        acc = part if acc is None else acc + part
    xn = x_ref[...] + gate_ref[0] * acc
    xn_ref[...] = xn
    h2 = _modnorm(xn, g2_ref[...], sh2_ref[0], sc2_ref[0])
    h_hi = h2.astype(BF16)
    h2_ref[...] = h_hi
    h_lo = (h2 - h_hi.astype(F32)).astype(BF16)
    by_hi = _dot_nt(jnp.concatenate([w_hi, w_lo], axis=0), h_hi)
    lg = by_hi[0:N_EXPERTS] + (by_hi[N_EXPERTS:] + _dot_nt(w_hi, h_lo))
    for c in range(lg.shape[1] // LOGIT_TILE):
        lg_ref[c] = lg[:, c * LOGIT_TILE:(c + 1) * LOGIT_TILE]


def _mix_out(mixed, w_out, x, g2_row, mods, layer, stream, wr_t):
    tm = MIX_TILE
    n_in = len(mixed)
    kp = D_MODEL // n_in
    row_blk = lambda width: pl.BlockSpec((tm, width), lambda i: (i, 0))
    return _deferred(
        functools.partial(_mix_out_kernel, n_in),
        grid=(stream.tokens // tm,),
        in_specs=[row_blk(kp)] * n_in + [
            _resident_spec((D_MODEL, D_MODEL)),
            row_blk(D_MODEL),
            _mod_spec(layer, stream, 2, tm),
            pl.BlockSpec((1, D_MODEL), lambda i: (0, 0)),
            _mod_spec(layer, stream, 3, tm),
            _mod_spec(layer, stream, 4, tm),
            pl.BlockSpec((N_EXPERTS, D_MODEL), lambda i: (0, 0))],
        out_specs=[row_blk(D_MODEL), row_blk(D_MODEL),
                   pl.BlockSpec((tm // LOGIT_TILE, N_EXPERTS, LOGIT_TILE), lambda i: (i, 0, 0))],
        out_shape=[jax.ShapeDtypeStruct((stream.tokens, D_MODEL), F32),
                   jax.ShapeDtypeStruct((stream.tokens, D_MODEL), BF16),
                   jax.ShapeDtypeStruct((stream.tokens // LOGIT_TILE, N_EXPERTS, LOGIT_TILE), F32)],
        name=f"mix_out_l{layer}_s{stream.seq}",
    )(*mixed, w_out, x, mods, g2_row, mods, mods, wr_t)


def _sort_desc_lanes(x):
    rows, n = x.shape
    tiles = [x[:, c * LANES:(c + 1) * LANES] for c in range(n // LANES)]
    lane = lax.broadcasted_iota(I32, (rows, LANES), 1)
    k = 2
    while k <= n:
        j = k // 2
        while j >= 1:
            if j < LANES:
                lower = (lane & j) == 0
                for c in range(len(tiles)):
                    t = tiles[c]
                    partner = jnp.where(lower, pltpu.roll(t, LANES - j, axis=1), pltpu.roll(t, j, axis=1))
                    desc = ((lane & k) == 0) if k < LANES else (((c * LANES) & k) == 0)
                    take_max = (lower == desc) if k < LANES else (lower if desc else jnp.logical_not(lower))
                    tiles[c] = jnp.where(take_max, jnp.maximum(t, partner), jnp.minimum(t, partner))
            else:
                jc = j // LANES
                new = list(tiles)
                for c in range(len(tiles)):
                    take_max = ((c & jc) == 0) == (((c * LANES) & k) == 0)
                    new[c] = (jnp.maximum if take_max else jnp.minimum)(tiles[c], tiles[c ^ jc])
                tiles = new
            j //= 2
        k *= 2
    return tiles


def _router_kernel(cap, lg_ref, pos_ref, g_ref):
    x = lg_ref[...]
    n_b, n_e, n_tok = x.shape
    e = jnp.exp(x - jnp.max(x, axis=1, keepdims=True))
    aff = (e / jnp.sum(e, axis=1, keepdims=True)).reshape(n_b * n_e, n_tok)
    srt = _sort_desc_lanes(aff)
    lane_k = (cap - 1) % LANES
    thr = srt[(cap - 1) // LANES][:, lane_k:lane_k + 1]
    gt = aff > thr
    eq = aff == thr
    n_gt = jnp.sum(gt.astype(F32), axis=1, keepdims=True)
    before = (lax.broadcasted_iota(I32, (n_tok, n_tok), 0)
              < lax.broadcasted_iota(I32, (n_tok, n_tok), 1)).astype(BF16)
    eq_rank = _dot(eq.astype(BF16), before)
    sel = gt | (eq & (eq_rank < cap - n_gt))
    slot = _dot(sel.astype(BF16), before).astype(I32)
    pos_ref[...] = jnp.where(sel, slot, -1).reshape(n_b, n_e, n_tok)
    g_ref[...] = jnp.where(sel, aff, 0.0).reshape(n_b, n_e, n_tok)


def _router(logits, stream):
    shape = (stream.n_req, N_EXPERTS, stream.seq)
    spec = pl.BlockSpec(shape, lambda i: (0, 0, 0))
    return _deferred(
        functools.partial(_router_kernel, stream.cap), grid=(1,),
        in_specs=[spec], out_specs=[spec, spec],
        out_shape=[jax.ShapeDtypeStruct(shape, I32), jax.ShapeDtypeStruct(shape, F32)],
        name=f"router_s{stream.seq}",
    )(logits)


def _gather_kernel(cap, seq, h_ref, pos_ref, xs_ref):
    slot = lax.broadcasted_iota(I32, (cap, seq), 0)
    for r in range(pos_ref.shape[0]):
        onehot = jnp.concatenate([(slot == pos_ref[r, e:e + 1, :]).astype(BF16) for e in range(N_EXPERTS)], axis=0)
        xs = _dot(onehot, h_ref[r * seq:(r + 1) * seq, :]).astype(BF16)
        for e in range(N_EXPERTS):
            xs_ref[e, r * cap:(r + 1) * cap, :] = xs[e * cap:(e + 1) * cap]


def _gather(h2, pos, stream):
    cap, seq, per = stream.cap, stream.seq, stream.req_per_step
    return _deferred(
        functools.partial(_gather_kernel, cap, seq),
        grid=(stream.n_req // per,),
        in_specs=[pl.BlockSpec((per * seq, D_MODEL), lambda i: (i, 0)),
                  pl.BlockSpec((per, N_EXPERTS, seq), lambda i: (i, 0, 0))],
        out_specs=pl.BlockSpec((N_EXPERTS, per * cap, D_MODEL), lambda i: (0, i, 0)),
        out_shape=jax.ShapeDtypeStruct((N_EXPERTS, stream.n_req * cap, D_MODEL), BF16),
        name=f"gather_s{seq}",
    )(h2, pos)


def _ffn_kernel(xa_ref, xb_ref, wg_ref, wu_ref, wd_ref, ya_ref, yb_ref, acc):
    j = pl.program_id(1)
    rows_a = xa_ref.shape[1]

    @pl.when(j == 0)
    def _():
        acc[...] = jnp.zeros_like(acc)

    x = jnp.concatenate([xa_ref[0], xb_ref[0]], axis=0)
    a = _dot(x, wg_ref[0, 0].astype(BF16))
    u = _dot(x, wu_ref[0, 0].astype(BF16))
    acc[...] += _dot(((a * jax.nn.sigmoid(a)) * u).astype(BF16), wd_ref[0, 0].astype(BF16))

    @pl.when(j == pl.num_programs(1) - 1)
    def _():
        ya_ref[0] = acc[0:rows_a, :].astype(BF16)
        yb_ref[0] = acc[rows_a:, :].astype(BF16)


def _ffn(xs_a, xs_b, layer, w_gate, w_up, w_down):
    tf = FF_TILE
    rows_a, rows_b = xs_a.shape[1], xs_b.shape[1]
    x_spec = lambda rows: pl.BlockSpec((1, rows, D_MODEL), lambda e, j: (e, 0, 0))
    return pl.pallas_call(
        _ffn_kernel,
        grid=(N_EXPERTS, EXPERT_FF // tf),
        in_specs=[x_spec(rows_a), x_spec(rows_b),
                  pl.BlockSpec((1, 1, D_MODEL, tf), lambda e, j: (layer, e, 0, j)),
                  pl.BlockSpec((1, 1, D_MODEL, tf), lambda e, j: (layer, e, 0, j)),
                  pl.BlockSpec((1, 1, tf, D_MODEL), lambda e, j: (layer, e, j, 0))],
        out_specs=[x_spec(rows_a), x_spec(rows_b)],
        out_shape=[jax.ShapeDtypeStruct(xs_a.shape, BF16), jax.ShapeDtypeStruct(xs_b.shape, BF16)],
        scratch_shapes=[pltpu.VMEM((rows_a + rows_b, D_MODEL), F32)],
        compiler_params=_params(2, FFN_VMEM_LIMIT_BYTES),
        name=f"ffn_l{layer}",
    )(xs_a, xs_b, w_gate, w_up, w_down)


def _scatter_kernel(cap, seq, final, y_ref, pos_ref, g_ref, x_ref, gate_ref, fg_ref, o_ref):
    slot = lax.broadcasted_iota(I32, (cap, seq), 0)
    for r in range(pos_ref.shape[0]):
        onehots, gated = [], []
        for e in range(N_EXPERTS):
            hit = slot == pos_ref[r, e:e + 1, :]
            gate = jnp.sum(jnp.where(hit, g_ref[r, e:e + 1, :], 0.0), axis=1, keepdims=True)
            gated.append((y_ref[e, r * cap:(r + 1) * cap, :].astype(F32) * gate).astype(BF16))
            onehots.append(hit.astype(BF16))
        moe = lax.dot_general(jnp.concatenate(onehots, axis=0), jnp.concatenate(gated, axis=0),
                              (((0,), (0,)), ((), ())), preferred_element_type=F32)
        rows = slice(r * seq, (r + 1) * seq)
        xn = x_ref[rows, :] + gate_ref[0] * moe
        if final:
            xn = xn * lax.rsqrt(jnp.mean(xn * xn, axis=-1, keepdims=True) + EPS) * fg_ref[...]
        o_ref[rows, :] = xn


def _scatter(y, pos, g, xn, mods, layer, stream, final, final_g_row):
    cap, seq, per = stream.cap, stream.seq, stream.req_per_step
    tok_blk = pl.BlockSpec((per * seq, D_MODEL), lambda i: (i, 0))
    sel_blk = pl.BlockSpec((per, N_EXPERTS, seq), lambda i: (i, 0, 0))
    return _deferred(
        functools.partial(_scatter_kernel, cap, seq, final),
        grid=(stream.n_req // per,),
        in_specs=[pl.BlockSpec((N_EXPERTS, per * cap, D_MODEL), lambda i: (0, i, 0)),
                  sel_blk, sel_blk, tok_blk,
                  _mod_spec(layer, stream, 5, per * seq),
                  pl.BlockSpec((1, D_MODEL), lambda i: (0, 0))],
        out_specs=tok_blk,
        out_shape=jax.ShapeDtypeStruct((stream.tokens, D_MODEL), F32),
        name=f"scatter_l{layer}_s{seq}",
    )(y, pos, g, xn, mods, final_g_row)


def _axial_rope_tables(rows, head_dim):
    f32 = np.float32
    row = np.repeat(np.arange(rows, dtype=f32), GRID_W)
    col = np.tile(np.arange(GRID_W, dtype=f32), rows)
    nf = head_dim // 4
    inv = (f32(ROPE_BASE) ** (-np.arange(nf, dtype=f32) / f32(nf))).astype(f32)
    ar = row[:, None] * inv[None]
    ac = col[:, None] * inv[None]
    ang = np.concatenate([ar, ar, ac, ac], axis=-1).astype(f32)
    return np.cos(ang).astype(f32), np.sin(ang).astype(f32)


def _block_diag(w):
    rows = [jnp.pad(w[:, n], ((0, 0), (0, 0), (n * LRU_BLOCK, (LRU_BLOCKS - 1 - n) * LRU_BLOCK)))
            for n in range(LRU_BLOCKS)]
    return jnp.concatenate(rows, axis=1)


def kernel(x_prompt, x_sample, cache_win_k, cache_win_v, state_lru, cache_diff_k, cache_diff_v, c, c_ctx, ada_w, ada_b, norm_g, final_g, even_w_in, even_w_out, conv_w, conv_b, lru_wa, lru_ba, lru_wx, lru_bx, lru_lambda, win_sink, odd_w_in, odd_w_out, diff_lq1, diff_lk1, diff_lq2, diff_lk2, diff_subln_g, moe_router, moe_w_gate, moe_w_up, moe_w_down):
    cv_t = jnp.concatenate([c_ctx[None], c, jnp.zeros((COND_ROWS - N_COND, D_MODEL), F32)], axis=0).T
    mods = _adaln(cv_t, ada_w, ada_b).reshape(DEPTH * COND_ROWS, 1, 6 * D_MODEL)

    cos, sin = _axial_rope_tables(DEC_SEQ // GRID_W, WIN_HEAD_DIM)
    rope_win = rope_diff = (jnp.asarray(np.tile(cos, (1, 2))), jnp.asarray(np.tile(sin, (1, 2))))

    xs = [x_prompt.reshape(CTX.tokens, D_MODEL), x_sample.reshape(LAT.tokens, D_MODEL)]
    final_g_row = final_g.reshape(1, D_MODEL)
    outs = {}

    def both(make):
        return _run_each(make(0, CTX), make(1, LAT))

    routers_t = jnp.swapaxes(moe_router, 1, 2)
    for layer in range(DEPTH):
        idx = layer // 2
        even = layer % 2 == 0
        w_in = (even_w_in if even else odd_w_in)[idx]
        w_out = (even_w_out if even else odd_w_out)[idx]
        wr_t = routers_t[layer]
        g1_row = norm_g[layer, 0].reshape(1, D_MODEL)
        g2_row = norm_g[layer, 1].reshape(1, D_MODEL)
        if even:
            wa = _block_diag(lru_wa[idx]).astype(BF16)
            wx = _block_diag(lru_wx[idx]).astype(BF16)
            (proj_c, win_k, win_v), (proj_d,) = _run_each(_proj_win(xs[0], g1_row, mods, layer, CTX, w_in),
                                                         _proj(xs[1], g1_row, mods, layer, LAT, w_in, EVEN_TOKEN_TILE))
            projs = (proj_c, proj_d)
            h0s = (jnp.zeros((CTX.n_req, 2, LRU_WIDTH), F32), state_lru[:, idx])
            (y_c, h_fin), (y_d, _) = both(lambda si, st: _lru(
                projs[si], st, conv_w[idx], conv_b[idx].reshape(1, LRU_WIDTH), wa, wx,
                lru_ba[idx], lru_bx[idx], lru_lambda[idx], h0s[si]))
            pack = lambda t: t[:, idx].transpose(0, 2, 1, 3).reshape(DEC_BATCH, PAST_LEN, WIN_KV)
            (o_c,), (o_d,) = _run_each(
                _win_attn(proj_c, CTX, win_sink[idx]),
                _win_attn(proj_d, LAT, win_sink[idx], cache=(pack(cache_win_k), pack(cache_win_v)), rope=rope_win))
            outs["win_k"] = win_k
            outs["win_v"] = win_v
            outs["lru"] = h_fin[:, None]
            mixed = ([y_c, o_c], [y_d, o_d])
        else:
            lambda_init = 0.8 - 0.6 * math.exp(-0.3 * layer)
            vec = lambda t: t[idx].reshape(1, DIFF_HEAD_DIM)
            args = (vec(diff_lq1), vec(diff_lk1), vec(diff_lq2), vec(diff_lk2), diff_subln_g[idx])
            (q, kh, vh), (proj_d,) = _run_each(_proj_heads(xs[0], g1_row, mods, layer, CTX, w_in),
                                              _proj(xs[1], g1_row, mods, layer, LAT, w_in))
            (o_c,), (o_d,) = _run_each(
                _diff_attn((q, kh, vh), CTX, lambda_init, *args),
                _diff_attn(proj_d, LAT, lambda_init, *args,
                           cache=(cache_diff_k[:, idx:idx + 1], cache_diff_v[:, idx:idx + 1]), rope=rope_diff))
            outs["diff_k"] = kh
            outs["diff_v"] = vh
            mixed = ([o_c], [o_d])

        mix = both(lambda si, st: _mix_out(mixed[si], w_out, xs[si], g2_row, mods, layer, st, wr_t))
        xns, h2s = (mix[0][0], mix[1][0]), (mix[0][1], mix[1][1])

        def request_major(lg, st):
            per_req = st.seq // LOGIT_TILE
            lg = lg.reshape(st.n_req, per_req, N_EXPERTS, LOGIT_TILE).transpose(0, 2, 1, 3)
            return lg.reshape(st.n_req, N_EXPERTS, st.seq)

        lgs = (request_major(mix[0][2], CTX), request_major(mix[1][2], LAT))
        routes = both(lambda si, st: _router(lgs[si], st))
        (rows_c,), (rows_d,) = both(lambda si, st: _gather(h2s[si], routes[si][0], st))
        ys = _ffn(rows_c, rows_d, layer, moe_w_gate, moe_w_up, moe_w_down)
        final = layer == DEPTH - 1
        (x_c,), (x_d,) = both(lambda si, st: _scatter(ys[si], routes[si][0], routes[si][1], xns[si], mods, layer, st,
                                                      final, final_g_row))
        xs = [x_c, x_d]


    y_prompt = xs[0].reshape(BATCH, SEQ, D_MODEL)
    y_sample = xs[1].reshape(DEC_BATCH, DEC_SEQ, D_MODEL)
    return (y_prompt, y_sample, outs["win_k"], outs["win_v"], outs["lru"], outs["diff_k"], outs["diff_v"])
```

```python
import functools
import math
from typing import Callable, NamedTuple

import jax
import jax.numpy as jnp
import numpy as np
from jax import lax
from jax.experimental import pallas as pl
from jax.experimental.pallas import tpu as pltpu

F32 = jnp.float32
BF16 = jnp.bfloat16
I32 = jnp.int32

D_MODEL = 1024
BATCH = 16
SEQ = 256
DEPTH = 2
DEC_BATCH = 2
DEC_SEQ = 1024
PAST_LEN = 512
GRID_W = 64
LRU_WIDTH = D_MODEL // 2
LRU_BLOCKS = 8
LRU_BLOCK = LRU_WIDTH // LRU_BLOCKS
CONV_W = 4
LRU_C = 8.0
WIN_HEADS = 8
WIN_KV_HEADS = 2
WIN_REP = WIN_HEADS // WIN_KV_HEADS
WIN_HEAD_DIM = 64
WINDOW = 128
WIN_Q = WIN_HEADS * WIN_HEAD_DIM
WIN_KV = WIN_KV_HEADS * WIN_HEAD_DIM
EVEN_IN = 2 * LRU_WIDTH + WIN_Q + 2 * WIN_KV
DIFF_HEADS = 8
DIFF_HEAD_DIM = 64
DIFF_QK = DIFF_HEADS * 2 * DIFF_HEAD_DIM
DIFF_V = DIFF_HEADS * 2 * DIFF_HEAD_DIM
ODD_IN = 2 * DIFF_QK + DIFF_V
N_EXPERTS = 16
EXPERT_FF = 2 * D_MODEL
CAPACITY_FACTOR = 2
ROPE_BASE = 10000.0
EPS = 1e-6
NEG_INF = -1e30

LANES = 128
SUBLANES = 8
VMEM_LIMIT_BYTES = 48 * 1024 * 1024

N_COND = 1 + DEC_BATCH
COND_ROWS = SUBLANES
ADALN_TILE = 3072
TOKEN_TILE = 512
EVEN_TOKEN_TILE = 1024
MIX_TILE = 1024
ROUTE_ROWS = 1024
LOGIT_TILE = 256
FF_TILE = 1024
FFN_VMEM_LIMIT_BYTES = 56 * 1024 * 1024
LRU_CHUNK = 256
WIN_BLOCKS_PER_ITER = 2
ATT_Q_TILE = 256
ATT_K_TILE = 256


class Stream:
    def __init__(self, n_req, seq, cond0, cond_step):
        self.n_req, self.seq, self.cond0, self.cond_step = n_req, seq, cond0, cond_step
        self.tokens = n_req * seq
        self.cap = CAPACITY_FACTOR * seq // N_EXPERTS
        self.req_per_step = max(1, ROUTE_ROWS // seq)

    def cond_of_row(self, row):
        return self.cond0 + self.cond_step * (row // self.seq)


CTX = Stream(BATCH, SEQ, 0, 0)
LAT = Stream(DEC_BATCH, DEC_SEQ, 1, 1)


def _params(n_axes, vmem_limit_bytes=VMEM_LIMIT_BYTES):
    return pltpu.CompilerParams(dimension_semantics=("arbitrary",) * n_axes,
                                vmem_limit_bytes=vmem_limit_bytes)


class Call(NamedTuple):
    kernel: Callable
    steps: int
    in_specs: tuple
    out_specs: tuple
    out_shapes: tuple
    scratch: tuple
    args: tuple
    name: str


def _deferred(kernel, *, grid, in_specs, out_specs, out_shape, scratch_shapes=(), name):
    (steps,) = grid
    as_tuple = lambda v: tuple(v) if isinstance(v, (list, tuple)) else (v,)
    return lambda *args: Call(kernel, steps, tuple(in_specs), as_tuple(out_specs), as_tuple(out_shape),
                              tuple(scratch_shapes), args, name)


def _run(call):
    return pl.pallas_call(
        call.kernel, grid=(call.steps,), in_specs=list(call.in_specs), out_specs=list(call.out_specs),
        out_shape=list(call.out_shapes), scratch_shapes=list(call.scratch),
        compiler_params=_params(1), name=call.name)(*call.args)


def _run_each(a, b):
    return _run(a), _run(b)


def _resident_spec(shape):
    return pl.BlockSpec(shape, lambda i: (0, 0), pipeline_mode=pl.Buffered(1))


def _mod_spec(layer, stream, k, rows_per_step):
    return pl.BlockSpec(
        (1, 1, D_MODEL),
        lambda i, *_: (layer * COND_ROWS + stream.cond_of_row(i * rows_per_step), 0, k))


def _dot(a, b):
    return jnp.dot(a, b, preferred_element_type=F32)


def _dot_nt(a, b):
    return lax.dot_general(a, b, (((1,), (1,)), ((), ())), preferred_element_type=F32)


def _modnorm(x, g, shift, scale):
    y = x * lax.rsqrt(jnp.mean(x * x, axis=-1, keepdims=True) + EPS)
    return (y * g) * (1.0 + scale) + shift


def _lane_half_masks(shape):
    lane = lax.broadcasted_iota(I32, shape, len(shape) - 1)
    left = (lane & (LANES - 1)) < LANES // 2
    return left, jnp.logical_not(left)


def _rope(x, cos, sin):
    parts = []
    for c in range(x.shape[1] // LANES):
        xs = x[:, c * LANES:(c + 1) * LANES]
        lane = lax.broadcasted_iota(I32, xs.shape, 1)
        first = (lane & 31) < 16
        rot = jnp.where(first, -pltpu.roll(xs, LANES - 16, axis=1), pltpu.roll(xs, 16, axis=1))
        parts.append(xs * cos + rot * sin)
    return parts[0] if len(parts) == 1 else jnp.concatenate(parts, axis=1)


def _adaln_kernel(cv_ref, w_ref, b_ref, o_ref):
    cv = cv_ref[...]
    s = cv * jax.nn.sigmoid(cv)
    w = w_ref[0]
    ridx = lax.broadcasted_iota(I32, (COND_ROWS, w.shape[1]), 0)
    out = jnp.zeros((COND_ROWS, w.shape[1]), F32)
    for r in range(N_COND):
        out = jnp.where(ridx == r, jnp.sum(w * s[:, r:r + 1], axis=0, keepdims=True), out)
    o_ref[0] = out + b_ref[0]


def _adaln(cv_t, ada_w, ada_b):
    tn = ADALN_TILE
    return pl.pallas_call(
        _adaln_kernel,
        grid=(DEPTH, 6 * D_MODEL // tn),
        in_specs=[pl.BlockSpec((D_MODEL, COND_ROWS), lambda l, j: (0, 0)),
                  pl.BlockSpec((1, D_MODEL, tn), lambda l, j: (l, 0, j)),
                  pl.BlockSpec((1, 1, tn), lambda l, j: (l, 0, j))],
        out_specs=pl.BlockSpec((1, COND_ROWS, tn), lambda l, j: (l, 0, j)),
        out_shape=jax.ShapeDtypeStruct((DEPTH, COND_ROWS, 6 * D_MODEL), F32),
        compiler_params=_params(2),
        name="adaln",
    )(cv_t, ada_w, ada_b.reshape(DEPTH, 1, 6 * D_MODEL))


def _proj_kernel(x_ref, g_ref, sh_ref, sc_ref, w_ref, o_ref):
    h = _modnorm(x_ref[...], g_ref[...], sh_ref[0], sc_ref[0])
    o_ref[...] = _dot(h.astype(BF16), w_ref[...].astype(BF16))


def _proj(x, g_row, mods, layer, stream, w, tm=TOKEN_TILE):
    n_out = w.shape[1]
    return _deferred(
        _proj_kernel,
        grid=(stream.tokens // tm,),
        in_specs=[pl.BlockSpec((tm, D_MODEL), lambda i: (i, 0)),
                  pl.BlockSpec((1, D_MODEL), lambda i: (0, 0)),
                  _mod_spec(layer, stream, 0, tm),
                  _mod_spec(layer, stream, 1, tm),
                  _resident_spec((D_MODEL, n_out))],
        out_specs=pl.BlockSpec((tm, n_out), lambda i: (i, 0)),
        out_shape=jax.ShapeDtypeStruct((stream.tokens, n_out), F32),
        name=f"proj_l{layer}_s{stream.seq}",
    )(x, g_row, mods, mods, w)


def _proj_heads_kernel(seq, x_ref, g_ref, sh_ref, sc_ref, w_ref, q_ref, k_ref, v_ref):
    h = _modnorm(x_ref[...], g_ref[...], sh_ref[0], sc_ref[0])
    res = _dot(h.astype(BF16), w_ref[...].astype(BF16))
    q_ref[...] = res[:, 0:DIFF_QK].astype(BF16)
    for r in range(x_ref.shape[0] // seq):
        rows = slice(r * seq, (r + 1) * seq)
        for hh in range(DIFF_HEADS):
            k_ref[r, 0, hh] = res[rows, DIFF_QK + hh * LANES:DIFF_QK + (hh + 1) * LANES]
            v_ref[r, 0, hh] = res[rows, 2 * DIFF_QK + hh * LANES:2 * DIFF_QK + (hh + 1) * LANES]


def _proj_heads(x, g_row, mods, layer, stream, w):
    tm, seq = TOKEN_TILE, stream.seq
    head_shape = (stream.n_req, 1, DIFF_HEADS, seq, 2 * DIFF_HEAD_DIM)
    head_spec = pl.BlockSpec((tm // seq, 1, DIFF_HEADS, seq, 2 * DIFF_HEAD_DIM), lambda i: (i, 0, 0, 0, 0))
    return _deferred(
        functools.partial(_proj_heads_kernel, seq),
        grid=(stream.tokens // tm,),
        in_specs=[pl.BlockSpec((tm, D_MODEL), lambda i: (i, 0)),
                  pl.BlockSpec((1, D_MODEL), lambda i: (0, 0)),
                  _mod_spec(layer, stream, 0, tm),
                  _mod_spec(layer, stream, 1, tm),
                  _resident_spec((D_MODEL, ODD_IN))],
        out_specs=[pl.BlockSpec((tm, DIFF_QK), lambda i: (i, 0)), head_spec, head_spec],
        out_shape=[jax.ShapeDtypeStruct((stream.tokens, DIFF_QK), BF16),
                   jax.ShapeDtypeStruct(head_shape, F32), jax.ShapeDtypeStruct(head_shape, F32)],
        name=f"proj_heads_l{layer}_s{seq}",
    )(x, g_row, mods, mods, w)


def _proj_win_kernel(seq, x_ref, g_ref, sh_ref, sc_ref, w_ref, o_ref, k_ref, v_ref):
    h = _modnorm(x_ref[...], g_ref[...], sh_ref[0], sc_ref[0])
    res = _dot(h.astype(BF16), w_ref[...].astype(BF16))
    o_ref[...] = res
    k0 = 2 * LRU_WIDTH + WIN_Q
    for r in range(x_ref.shape[0] // seq):
        rows = slice(r * seq, (r + 1) * seq)
        for g in range(WIN_KV_HEADS):
            k_ref[r, 0, g] = res[rows, k0 + g * WIN_HEAD_DIM:k0 + (g + 1) * WIN_HEAD_DIM]
            v_ref[r, 0, g] = res[rows, k0 + WIN_KV + g * WIN_HEAD_DIM:k0 + WIN_KV + (g + 1) * WIN_HEAD_DIM]


def _proj_win(x, g_row, mods, layer, stream, w):
    tm, seq = EVEN_TOKEN_TILE, stream.seq
    head_shape = (stream.n_req, 1, WIN_KV_HEADS, seq, WIN_HEAD_DIM)
    head_spec = pl.BlockSpec((tm // seq, 1, WIN_KV_HEADS, seq, WIN_HEAD_DIM), lambda i: (i, 0, 0, 0, 0))
    return _deferred(
        functools.partial(_proj_win_kernel, seq),
        grid=(stream.tokens // tm,),
        in_specs=[pl.BlockSpec((tm, D_MODEL), lambda i: (i, 0)),
                  pl.BlockSpec((1, D_MODEL), lambda i: (0, 0)),
                  _mod_spec(layer, stream, 0, tm),
                  _mod_spec(layer, stream, 1, tm),
                  _resident_spec((D_MODEL, EVEN_IN))],
        out_specs=[pl.BlockSpec((tm, EVEN_IN), lambda i: (i, 0)), head_spec, head_spec],
        out_shape=[jax.ShapeDtypeStruct((stream.tokens, EVEN_IN), F32),
                   jax.ShapeDtypeStruct(head_shape, F32), jax.ShapeDtypeStruct(head_shape, F32)],
        name=f"proj_win_l{layer}_s{seq}",
    )(x, g_row, mods, mods, w)


def _lru_kernel(seq, xl_ref, gl_ref, cw_ref, cb_ref, wa_ref, wx_ref, ba_ref, bx_ref, lam_ref, h0_ref,
                y_ref, hfin_ref, xe, af, bf, ab, bb):
    width = LRU_WIDTH
    ch = LRU_CHUNK
    seg = seq // SUBLANES
    lead = (CONV_W // 2) * SUBLANES

    def to_segment_major(x):
        return jnp.transpose(x.reshape(SUBLANES, seg, LANES), (1, 0, 2)).reshape(seq, LANES)

    def to_time_major(x):
        return jnp.transpose(x.reshape(seg, SUBLANES, LANES), (1, 0, 2)).reshape(seq, LANES)

    for c in range(width // LANES):
        cols = slice(c * LANES, (c + 1) * LANES)
        xe[lead:lead + seq, cols] = to_segment_major(xl_ref[:, cols])
    sub = lax.broadcasted_iota(I32, (SUBLANES, width), 0)
    for k in range(CONV_W // 2):
        prev = xe[lead + (seg - 2 + k) * SUBLANES:lead + (seg - 1 + k) * SUBLANES, :]
        xe[k * SUBLANES:(k + 1) * SUBLANES, :] = jnp.where(sub >= 1, pltpu.roll(prev, 1, axis=0), 0.0)
    first = xe[lead:lead + SUBLANES, :]
    xe[lead + seq:lead + seq + SUBLANES, :] = jnp.where(sub < SUBLANES - 1,
                                                        pltpu.roll(first, SUBLANES - 1, axis=0), 0.0)

    lam = lam_ref[...]
    z = -lam
    softplus = jnp.maximum(z, 0.0) + jnp.log1p(jnp.exp(-jnp.abs(z)))
    cw = cw_ref[...]
    cb = cb_ref[...]

    def gates_chunk(c, carry):
        r0 = pl.multiple_of(c * ch, ch)
        xc = xe[pl.ds(r0, ch), :] * cw[0:1]
        for j in range(1, CONV_W):
            xc = xc + xe[pl.ds(pl.multiple_of(r0 + j * SUBLANES, SUBLANES), ch), :] * cw[j:j + 1]
        xc = xc + cb
        xcb = xc.astype(BF16)
        for d, (a_s, b_s) in enumerate(((af, bf), (ab, bb))):
            r = jax.nn.sigmoid(_dot(xcb, wa_ref[d]) + ba_ref[d:d + 1])
            ig = jax.nn.sigmoid(_dot(xcb, wx_ref[d]) + bx_ref[d:d + 1])
            log_a = (-LRU_C * r) * softplus[d:d + 1]
            a = jnp.exp(log_a)
            v = 1.0 - a * a
            a_s[pl.ds(r0, ch), :] = a
            b_s[pl.ds(r0, ch), :] = jnp.where(v > 0.0, v * lax.rsqrt(v), 0.0) * (ig * xc)
        return carry

    lax.fori_loop(0, seq // ch, gates_chunk, 0)

    def scan_step(k, carry):
        hf, pf, hb, pb = carry
        rf = pl.multiple_of(k * SUBLANES, SUBLANES)
        rb = pl.multiple_of((seg - 1 - k) * SUBLANES, SUBLANES)
        a = af[pl.ds(rf, SUBLANES), :]
        hf = a * hf + bf[pl.ds(rf, SUBLANES), :]
        pf = a * pf
        bf[pl.ds(rf, SUBLANES), :] = hf
        af[pl.ds(rf, SUBLANES), :] = pf
        a = ab[pl.ds(rb, SUBLANES), :]
        hb = a * hb + bb[pl.ds(rb, SUBLANES), :]
        pb = a * pb
        bb[pl.ds(rb, SUBLANES), :] = hb
        ab[pl.ds(rb, SUBLANES), :] = pb
        return hf, pf, hb, pb

    zeros = jnp.zeros((SUBLANES, width), F32)
    ones = jnp.ones((SUBLANES, width), F32)
    hf, pf, hb, pb = lax.fori_loop(0, seg, scan_step, (zeros, ones, zeros, ones))

    h0 = h0_ref[0]
    state, carry_f = h0[0:1], zeros
    for g in range(SUBLANES):
        carry_f = jnp.where(sub == g, state, carry_f)
        state = pf[g:g + 1] * state + hf[g:g + 1]
    hfin_ref[0, 0:1, :] = state
    state, carry_b = h0[1:2], zeros
    for g in reversed(range(SUBLANES)):
        carry_b = jnp.where(sub == g, state, carry_b)
        state = pb[g:g + 1] * state + hb[g:g + 1]
    hfin_ref[0, 1:2, :] = state

    groups = (ch // SUBLANES, SUBLANES, width)

    def combine_chunk(c, carry):
        r0 = pl.multiple_of(c * ch, ch)
        rows = pl.ds(r0, ch)
        h_fwd = bf[rows, :].reshape(groups) + af[rows, :].reshape(groups) * carry_f
        h_bwd = bb[rows, :].reshape(groups) + ab[rows, :].reshape(groups) * carry_b
        bf[rows, :] = (h_fwd + h_bwd).reshape(ch, width)
        return carry

    lax.fori_loop(0, seq // ch, combine_chunk, 0)

    for c in range(width // LANES):
        cols = slice(c * LANES, (c + 1) * LANES)
        y_ref[:, cols] = (to_time_major(bf[:, cols]) * jax.nn.gelu(gl_ref[:, cols])).astype(BF16)


def _lru(proj, stream, conv_w, conv_b, wa, wx, ba, bx, lam, h0):
    seq, width = stream.seq, LRU_WIDTH
    full2 = lambda b: (0, 0)
    full3 = lambda b: (0, 0, 0)
    return _deferred(
        functools.partial(_lru_kernel, seq),
        grid=(stream.n_req,),
        in_specs=[pl.BlockSpec((seq, width), lambda b: (b, 0)),
                  pl.BlockSpec((seq, width), lambda b: (b, 1)),
                  pl.BlockSpec((CONV_W, width), full2),
                  pl.BlockSpec((1, width), full2),
                  pl.BlockSpec((2, width, width), full3),
                  pl.BlockSpec((2, width, width), full3),
                  pl.BlockSpec((2, width), full2),
                  pl.BlockSpec((2, width), full2),
                  pl.BlockSpec((2, width), full2),
                  pl.BlockSpec((1, 2, width), lambda b: (b, 0, 0))],
        out_specs=[pl.BlockSpec((seq, width), lambda b: (b, 0)),
                   pl.BlockSpec((1, 2, width), lambda b: (b, 0, 0))],
        out_shape=[jax.ShapeDtypeStruct((stream.tokens, width), BF16),
                   jax.ShapeDtypeStruct((stream.n_req, 2, width), F32)],
        scratch_shapes=[pltpu.VMEM((seq + (CONV_W - 1) * SUBLANES, width), F32)] + [pltpu.VMEM((seq, width), F32)] * 4,
        name=f"lru_s{seq}",
    )(proj, proj, conv_w, conv_b, wa, wx, ba, bx, lam, h0)


def _attend(q, chunks, s_ref):
    tile_max = None
    spans = []
    off = 0
    for keys, _ in chunks:
        s = _dot_nt(q, keys())
        n = s.shape[1]
        s_ref[:, off:off + n] = s
        for c in range(n // LANES):
            t = s[:, c * LANES:(c + 1) * LANES]
            tile_max = t if tile_max is None else jnp.maximum(tile_max, t)
        spans.append((off, n))
        off += n
    m = jnp.max(tile_max, axis=-1, keepdims=True)
    acc = None
    for (_, values), (o, n) in zip(chunks, spans):
        part = _dot(jnp.exp(s_ref[:, o:o + n] - m).astype(BF16), values())
        acc = part if acc is None else acc + part
    return acc


def _split_groups(kk):
    left, right = _lane_half_masks(kk.shape)
    g0_l = jnp.where(left, kk, 0.0)
    g1_r = jnp.where(right, kk, 0.0)
    return ((g0_l, pltpu.roll(g0_l, LANES // 2, axis=1)), (pltpu.roll(g1_r, LANES // 2, axis=1), g1_r))


def _win_ctx_kernel(sink_ref, q_ref, kv_ref, o_ref):
    scale = WIN_HEAD_DIM ** -0.5
    seq = q_ref.shape[0]
    ks = _split_groups(kv_ref[:, 0:LANES])
    vs = _split_groups(kv_ref[:, LANES:2 * LANES])
    top = lax.broadcasted_iota(I32, (2 * seq, 1), 0) < seq
    outs = [None] * (WIN_HEADS // 2)
    for g in range(WIN_KV_HEADS):
        pairs = (2 * g, 2 * g + 1)
        qs = jnp.concatenate([q_ref[:, p * LANES:(p + 1) * LANES] for p in pairs], axis=0)
        qs = (qs * scale).astype(BF16)
        for side in range(2):
            sk = jnp.where(top, sink_ref[2 * pairs[0] + side], sink_ref[2 * pairs[1] + side])
            s = _dot_nt(qs, ks[g][side].astype(BF16))
            m = jnp.maximum(jnp.max(s, axis=-1, keepdims=True), sk)
            e = jnp.exp(s - m)
            den = jnp.sum(e, axis=-1, keepdims=True) + jnp.exp(sk - m)
            o = _dot(e.astype(BF16), vs[g][side].astype(BF16)) * (1.0 / den)
            for k, p in enumerate(pairs):
                part = o[k * seq:(k + 1) * seq]
                outs[p] = part if outs[p] is None else outs[p] + part
    for p in range(WIN_HEADS // 2):
        o_ref[:, p * LANES:(p + 1) * LANES] = outs[p].astype(BF16)


def _win_lat_kernel(sink_ref, q_ref, kv_ref, ck_ref, cv_ref, cos_ref, sin_ref, o_ref,
                    kl_s, vl_s, kc_s, vc_s):
    seq, wn = DEC_SEQ, WINDOW
    scale = WIN_HEAD_DIM ** -0.5
    kr = _rope(kv_ref[:, 0:LANES], cos_ref[...], sin_ref[...])
    for src, dst in ((_split_groups(kr), kl_s), (_split_groups(kv_ref[:, LANES:2 * LANES]), vl_s),
                     (_split_groups(ck_ref[0]), kc_s), (_split_groups(cv_ref[0]), vc_s)):
        for g in range(WIN_KV_HEADS):
            for side in range(2):
                dst[2 * g + side] = src[g][side].astype(BF16)

    top = lax.broadcasted_iota(I32, (2 * wn, 1), 0) < wn

    def q_block(i, carry):
        for sub in range(WIN_BLOCKS_PER_ITER):
            blk = WIN_BLOCKS_PER_ITER * i + sub
            r0 = pl.multiple_of(blk * wn, wn)
            start = pl.multiple_of(jnp.clip((blk - 1) * wn, 0, seq - 3 * wn), wn)
            qr = _rope(q_ref[pl.ds(r0, wn), :], cos_ref[pl.ds(r0, wn), :], sin_ref[pl.ds(r0, wn), :]) * scale
            qpos = r0 + (lax.broadcasted_iota(I32, (2 * wn, 3 * wn), 0) & (wn - 1))
            kpos = start + lax.broadcasted_iota(I32, (2 * wn, 3 * wn), 1)
            valid = jnp.abs(qpos - kpos) <= wn
            outs = [None] * (WIN_HEADS // 2)
            for g in range(WIN_KV_HEADS):
                pairs = (2 * g, 2 * g + 1)
                qs = jnp.concatenate([qr[:, p * LANES:(p + 1) * LANES] for p in pairs], axis=0).astype(BF16)
                for side in range(2):
                    idx = 2 * g + side
                    sk = jnp.where(top, sink_ref[2 * pairs[0] + side], sink_ref[2 * pairs[1] + side])
                    sl = _dot_nt(qs, kl_s[idx, pl.ds(start, 3 * wn), :])
                    sl = jnp.where(valid, sl, NEG_INF)
                    sc = _dot_nt(qs, kc_s[idx])
                    m = jnp.maximum(jnp.maximum(jnp.max(sl, axis=-1, keepdims=True),
                                                jnp.max(sc, axis=-1, keepdims=True)), sk)
                    el = jnp.exp(sl - m)
                    ec = jnp.exp(sc - m)
                    den = (jnp.sum(el, axis=-1, keepdims=True) + jnp.sum(ec, axis=-1, keepdims=True)
                           + jnp.exp(sk - m))
                    o = (_dot(el.astype(BF16), vl_s[idx, pl.ds(start, 3 * wn), :])
                         + _dot(ec.astype(BF16), vc_s[idx])) * (1.0 / den)
                    for k, p in enumerate(pairs):
                        part = o[k * wn:(k + 1) * wn]
                        outs[p] = part if outs[p] is None else outs[p] + part
            for p in range(WIN_HEADS // 2):
                o_ref[pl.ds(r0, wn), p * LANES:(p + 1) * LANES] = outs[p].astype(BF16)
        return carry

    lax.fori_loop(0, seq // (WIN_BLOCKS_PER_ITER * wn), q_block, 0)


def _win_attn(proj, stream, sink, cache=None, rope=None):
    seq = stream.seq
    q_spec = pl.BlockSpec((seq, WIN_Q), lambda b: (b, 2 * LRU_WIDTH // WIN_Q))
    kv_spec = pl.BlockSpec((seq, 2 * WIN_KV), lambda b: (b, (2 * LRU_WIDTH + WIN_Q) // (2 * WIN_KV)))
    sink_spec = pl.BlockSpec(memory_space=pltpu.SMEM)
    out_spec = pl.BlockSpec((seq, WIN_Q), lambda b: (b, 0))
    out_shape = jax.ShapeDtypeStruct((stream.tokens, WIN_Q), BF16)
    if cache is None:
        return _deferred(
            _win_ctx_kernel, grid=(stream.n_req,),
            in_specs=[sink_spec, q_spec, kv_spec], out_specs=out_spec, out_shape=out_shape,
            name="win_attn_ctx",
        )(sink, proj, proj)
    ck, cv = cache
    cos, sin = rope
    cache_spec = pl.BlockSpec((1, PAST_LEN, LANES), lambda b: (b, 0, 0))
    table_spec = pl.BlockSpec((seq, LANES), lambda b: (0, 0))
    return _deferred(
        _win_lat_kernel, grid=(stream.n_req,),
        in_specs=[sink_spec, q_spec, kv_spec, cache_spec, cache_spec, table_spec, table_spec],
        out_specs=out_spec, out_shape=out_shape,
        scratch_shapes=[pltpu.VMEM((4, seq, LANES), BF16), pltpu.VMEM((4, seq, LANES), BF16),
                        pltpu.VMEM((4, PAST_LEN, LANES), BF16), pltpu.VMEM((4, PAST_LEN, LANES), BF16)],
        name="win_attn_lat",
    )(sink, proj, proj, ck, cv, cos, sin)


def _diff_lambda(lq1_ref, lk1_ref, lq2_ref, lk2_ref, lambda_init):
    t1 = jnp.sum(lq1_ref[...] * lk1_ref[...], axis=-1, keepdims=True)
    t2 = jnp.sum(lq2_ref[...] * lk2_ref[...], axis=-1, keepdims=True)
    return jnp.exp(t1) - jnp.exp(t2) + lambda_init


def _subln(o, g_row, lambda_init):
    o = o * lax.rsqrt(jnp.mean(o * o, axis=-1, keepdims=True) + EPS) * g_row
    return o * (1.0 - lambda_init)


def _component_keys(k):
    left, right = _lane_half_masks(k.shape)
    return jnp.where(left, k, 0.0).astype(BF16), jnp.where(right, k, 0.0).astype(BF16)


def _values_with_ones(v):
    return jnp.concatenate([v.astype(BF16), jnp.ones(v.shape, BF16)], axis=1)


def _diff_combine(accs, lam):
    o1, o2 = accs[0][:, 0:LANES], accs[1][:, 0:LANES]
    return o1 * (1.0 / accs[0][:, LANES:]) - o2 * (lam * (1.0 / accs[1][:, LANES:]))


def _diff_ctx_kernel(lambda_init, lq1_ref, lk1_ref, lq2_ref, lk2_ref, sg_ref, q_ref, k_ref, v_ref, o_ref, s_ref):
    scale = DIFF_HEAD_DIM ** -0.5
    tq = ATT_Q_TILE
    lam = _diff_lambda(lq1_ref, lk1_ref, lq2_ref, lk2_ref, lambda_init)
    for h in range(DIFF_HEADS):
        cols = slice(h * LANES, (h + 1) * LANES)
        keys = _component_keys(k_ref[0, 0, h])
        vals = _values_with_ones(v_ref[0, 0, h])
        for qt in range(q_ref.shape[0] // tq):
            rows = slice(qt * tq, (qt + 1) * tq)
            q = (q_ref[rows, cols] * scale).astype(BF16)
            slots = [s_ref.at[(4 * h + 2 * qt + c) % s_ref.shape[0]] for c in range(2)]
            accs = [_attend(q, [(lambda: keys[c], lambda: vals)], slots[c]) for c in range(2)]
            o = _diff_combine(accs, lam)
            o_ref[rows, cols] = _subln(o, sg_ref[h:h + 1, :], lambda_init).astype(BF16)


def _diff_lat_kernel(lambda_init, lq1_ref, lk1_ref, lq2_ref, lk2_ref, sg_ref, q_ref, k_ref, v_ref,
                     ck_ref, cv_ref, cos_ref, sin_ref, o_ref, kl_s, kc_s, vl_s, vc_s, s_ref):
    seq, tq, tk = DEC_SEQ, ATT_Q_TILE, ATT_K_TILE
    scale = DIFF_HEAD_DIM ** -0.5
    lam = _diff_lambda(lq1_ref, lk1_ref, lq2_ref, lk2_ref, lambda_init)
    for src, dst in ((_rope(k_ref[...], cos_ref[...], sin_ref[...]), kl_s), (ck_ref[0, 0, 0], kc_s)):
        dst[0], dst[1] = _component_keys(src)
    vl_s[...] = _values_with_ones(v_ref[...])
    vc_s[...] = _values_with_ones(cv_ref[0, 0, 0])
    sg = sg_ref[0]

    for qt in range(seq // tq):
        rows = slice(qt * tq, (qt + 1) * tq)
        qr = (_rope(q_ref[rows, :], cos_ref[rows, :], sin_ref[rows, :]) * scale).astype(BF16)
        accs = []
        for comp in range(2):
            chunks = [(lambda j=j: kl_s[comp, j * tk:(j + 1) * tk, :],
                       lambda j=j: vl_s[j * tk:(j + 1) * tk, :]) for j in range(seq // tk)]
            chunks += [(lambda j=j: kc_s[comp, j * tk:(j + 1) * tk, :],
                        lambda j=j: vc_s[j * tk:(j + 1) * tk, :]) for j in range(PAST_LEN // tk)]
            accs.append(_attend(qr, chunks, s_ref.at[2 * qt + comp]))
        o_ref[rows, :] = _subln(_diff_combine(accs, lam), sg, lambda_init).astype(BF16)


def _diff_attn(proj, stream, lambda_init, lq1, lk1, lq2, lk2, subln_g, cache=None, rope=None):
    seq = stream.seq
    vec_spec = lambda nd: pl.BlockSpec((1, DIFF_HEAD_DIM), lambda *_: (0, 0))
    out_shape = jax.ShapeDtypeStruct((stream.tokens, DIFF_V), BF16)
    if cache is None:
        q, kh, vh = proj
        head_spec = pl.BlockSpec((1, 1, DIFF_HEADS, seq, LANES), lambda b: (b, 0, 0, 0, 0))
        return _deferred(
            functools.partial(_diff_ctx_kernel, lambda_init), grid=(stream.n_req,),
            in_specs=[vec_spec(1)] * 4 + [pl.BlockSpec((DIFF_HEADS, LANES), lambda b: (0, 0)),
                                          pl.BlockSpec((seq, DIFF_QK), lambda b: (b, 0)), head_spec, head_spec],
            out_specs=pl.BlockSpec((seq, DIFF_V), lambda b: (b, 0)), out_shape=out_shape,
            scratch_shapes=[pltpu.VMEM((8, ATT_Q_TILE, seq), F32)],
            name="diff_attn_ctx",
        )(lq1, lk1, lq2, lk2, subln_g, q, kh, vh)
    ck, cv = cache
    cos, sin = rope
    nh = DIFF_HEADS
    blk = lambda c: pl.BlockSpec((seq, LANES), lambda i: (i // nh, c * nh + i % nh))
    cache_spec = pl.BlockSpec((1, 1, 1, PAST_LEN, LANES), lambda i: (i // nh, 0, i % nh, 0, 0))
    table_spec = pl.BlockSpec((seq, LANES), lambda i: (0, 0))
    return _deferred(
        functools.partial(_diff_lat_kernel, lambda_init), grid=(stream.n_req * nh,),
        in_specs=[vec_spec(2)] * 4 + [pl.BlockSpec((1, 1, LANES), lambda i: (i % nh, 0, 0)),
                                      blk(0), blk(1), blk(2), cache_spec, cache_spec, table_spec, table_spec],
        out_specs=pl.BlockSpec((seq, LANES), lambda i: (i // nh, i % nh)), out_shape=out_shape,
        scratch_shapes=[pltpu.VMEM((2, seq, LANES), BF16), pltpu.VMEM((2, PAST_LEN, LANES), BF16),
                        pltpu.VMEM((seq, 2 * LANES), BF16), pltpu.VMEM((PAST_LEN, 2 * LANES), BF16),
                        pltpu.VMEM((2 * seq // ATT_Q_TILE, ATT_Q_TILE, seq + PAST_LEN), F32)],
        name="diff_attn_lat",
    )(lq1, lk1, lq2, lk2, subln_g.reshape(DIFF_HEADS, 1, LANES), proj, proj, proj, ck, cv, cos, sin)


def _mix_out_kernel(n_in, *refs):
    a_refs = refs[:n_in]
    w_ref, x_ref, gate_ref, g2_ref, sh2_ref, sc2_ref, wr_ref, xn_ref, h2_ref, lg_ref = refs[n_in:]
    kp = D_MODEL // n_in
    wr = wr_ref[...]
    w_hi = wr.astype(BF16)
    w_lo = (wr - w_hi.astype(F32)).astype(BF16)
    acc = None
    for k, a_ref in enumerate(a_refs):
        part = _dot(a_ref[...], w_ref[k * kp:(k + 1) * kp, :].astype(BF16))
        acc = part if acc is None else acc + part
    xn = x_ref[...] + gate_ref[0] * acc
    xn_ref[...] = xn
    h2 = _modnorm(xn, g2_ref[...], sh2_ref[0], sc2_ref[0])
    h_hi = h2.astype(BF16)
    h2_ref[...] = h_hi
    h_lo = (h2 - h_hi.astype(F32)).astype(BF16)
    by_hi = _dot_nt(jnp.concatenate([w_hi, w_lo], axis=0), h_hi)
    lg = by_hi[0:N_EXPERTS] + (by_hi[N_EXPERTS:] + _dot_nt(w_hi, h_lo))
    for c in range(lg.shape[1] // LOGIT_TILE):
        lg_ref[c] = lg[:, c * LOGIT_TILE:(c + 1) * LOGIT_TILE]


def _mix_out(mixed, w_out, x, g2_row, mods, layer, stream, wr_t):
    tm = MIX_TILE
    n_in = len(mixed)
    kp = D_MODEL // n_in
    row_blk = lambda width: pl.BlockSpec((tm, width), lambda i: (i, 0))
    return _deferred(
        functools.partial(_mix_out_kernel, n_in),
        grid=(stream.tokens // tm,),
        in_specs=[row_blk(kp)] * n_in + [
            _resident_spec((D_MODEL, D_MODEL)),
            row_blk(D_MODEL),
            _mod_spec(layer, stream, 2, tm),
            pl.BlockSpec((1, D_MODEL), lambda i: (0, 0)),
            _mod_spec(layer, stream, 3, tm),
            _mod_spec(layer, stream, 4, tm),
            pl.BlockSpec((N_EXPERTS, D_MODEL), lambda i: (0, 0))],
        out_specs=[row_blk(D_MODEL), row_blk(D_MODEL),
                   pl.BlockSpec((tm // LOGIT_TILE, N_EXPERTS, LOGIT_TILE), lambda i: (i, 0, 0))],
        out_shape=[jax.ShapeDtypeStruct((stream.tokens, D_MODEL), F32),
                   jax.ShapeDtypeStruct((stream.tokens, D_MODEL), BF16),
                   jax.ShapeDtypeStruct((stream.tokens // LOGIT_TILE, N_EXPERTS, LOGIT_TILE), F32)],
        name=f"mix_out_l{layer}_s{stream.seq}",
    )(*mixed, w_out, x, mods, g2_row, mods, mods, wr_t)


def _sort_desc_lanes(x):
    rows, n = x.shape
    tiles = [x[:, c * LANES:(c + 1) * LANES] for c in range(n // LANES)]
    lane = lax.broadcasted_iota(I32, (rows, LANES), 1)
    k = 2
    while k <= n:
        j = k // 2
        while j >= 1:
            if j < LANES:
                lower = (lane & j) == 0
                for c in range(len(tiles)):
                    t = tiles[c]
                    partner = jnp.where(lower, pltpu.roll(t, LANES - j, axis=1), pltpu.roll(t, j, axis=1))
                    desc = ((lane & k) == 0) if k < LANES else (((c * LANES) & k) == 0)
                    take_max = (lower == desc) if k < LANES else (lower if desc else jnp.logical_not(lower))
                    tiles[c] = jnp.where(take_max, jnp.maximum(t, partner), jnp.minimum(t, partner))
            else:
                jc = j // LANES
                new = list(tiles)
                for c in range(len(tiles)):
                    take_max = ((c & jc) == 0) == (((c * LANES) & k) == 0)
                    new[c] = (jnp.maximum if take_max else jnp.minimum)(tiles[c], tiles[c ^ jc])
                tiles = new
            j //= 2
        k *= 2
    return tiles


def _router_kernel(cap, lg_ref, pos_ref, g_ref):
    x = lg_ref[...]
    n_b, n_e, n_tok = x.shape
    e = jnp.exp(x - jnp.max(x, axis=1, keepdims=True))
    aff = (e / jnp.sum(e, axis=1, keepdims=True)).reshape(n_b * n_e, n_tok)
    srt = _sort_desc_lanes(aff)
    lane_k = (cap - 1) % LANES
    thr = srt[(cap - 1) // LANES][:, lane_k:lane_k + 1]
    gt = aff > thr
    eq = aff == thr
    n_gt = jnp.sum(gt.astype(F32), axis=1, keepdims=True)
    before = (lax.broadcasted_iota(I32, (n_tok, n_tok), 0)
              < lax.broadcasted_iota(I32, (n_tok, n_tok), 1)).astype(BF16)
    eq_rank = _dot(eq.astype(BF16), before)
    sel = gt | (eq & (eq_rank < cap - n_gt))
    slot = _dot(sel.astype(BF16), before).astype(I32)
    pos_ref[...] = jnp.where(sel, slot, -1).reshape(n_b, n_e, n_tok)
    g_ref[...] = jnp.where(sel, aff, 0.0).reshape(n_b, n_e, n_tok)


def _router(logits, stream):
    shape = (stream.n_req, N_EXPERTS, stream.seq)
    spec = pl.BlockSpec(shape, lambda i: (0, 0, 0))
    return _deferred(
        functools.partial(_router_kernel, stream.cap), grid=(1,),
        in_specs=[spec], out_specs=[spec, spec],
        out_shape=[jax.ShapeDtypeStruct(shape, I32), jax.ShapeDtypeStruct(shape, F32)],
        name=f"router_s{stream.seq}",
    )(logits)


def _gather_kernel(cap, seq, h_ref, pos_ref, xs_ref):
    slot = lax.broadcasted_iota(I32, (cap, seq), 0)
    for r in range(pos_ref.shape[0]):
        onehot = jnp.concatenate([(slot == pos_ref[r, e:e + 1, :]).astype(BF16) for e in range(N_EXPERTS)], axis=0)
        xs = _dot(onehot, h_ref[r * seq:(r + 1) * seq, :]).astype(BF16)
        for e in range(N_EXPERTS):
            xs_ref[e, r * cap:(r + 1) * cap, :] = xs[e * cap:(e + 1) * cap]


def _gather(h2, pos, stream):
    cap, seq, per = stream.cap, stream.seq, stream.req_per_step
    return _deferred(
        functools.partial(_gather_kernel, cap, seq),
        grid=(stream.n_req // per,),
        in_specs=[pl.BlockSpec((per * seq, D_MODEL), lambda i: (i, 0)),
                  pl.BlockSpec((per, N_EXPERTS, seq), lambda i: (i, 0, 0))],
        out_specs=pl.BlockSpec((N_EXPERTS, per * cap, D_MODEL), lambda i: (0, i, 0)),
        out_shape=jax.ShapeDtypeStruct((N_EXPERTS, stream.n_req * cap, D_MODEL), BF16),
        name=f"gather_s{seq}",
    )(h2, pos)


def _ffn_kernel(xa_ref, xb_ref, wg_ref, wu_ref, wd_ref, ya_ref, yb_ref, acc):
    j = pl.program_id(1)
    rows_a = xa_ref.shape[1]

    @pl.when(j == 0)
    def _():
        acc[...] = jnp.zeros_like(acc)

    x = jnp.concatenate([xa_ref[0], xb_ref[0]], axis=0)
    a = _dot(x, wg_ref[0, 0].astype(BF16))
    u = _dot(x, wu_ref[0, 0].astype(BF16))
    acc[...] += _dot(((a * jax.nn.sigmoid(a)) * u).astype(BF16), wd_ref[0, 0].astype(BF16))

    @pl.when(j == pl.num_programs(1) - 1)
    def _():
        ya_ref[0] = acc[0:rows_a, :].astype(BF16)
        yb_ref[0] = acc[rows_a:, :].astype(BF16)


def _ffn(xs_a, xs_b, layer, w_gate, w_up, w_down):
    tf = FF_TILE
    rows_a, rows_b = xs_a.shape[1], xs_b.shape[1]
    x_spec = lambda rows: pl.BlockSpec((1, rows, D_MODEL), lambda e, j: (e, 0, 0))
    return pl.pallas_call(
        _ffn_kernel,
        grid=(N_EXPERTS, EXPERT_FF // tf),
        in_specs=[x_spec(rows_a), x_spec(rows_b),
                  pl.BlockSpec((1, 1, D_MODEL, tf), lambda e, j: (layer, e, 0, j)),
                  pl.BlockSpec((1, 1, D_MODEL, tf), lambda e, j: (layer, e, 0, j)),
                  pl.BlockSpec((1, 1, tf, D_MODEL), lambda e, j: (layer, e, j, 0))],
        out_specs=[x_spec(rows_a), x_spec(rows_b)],
        out_shape=[jax.ShapeDtypeStruct(xs_a.shape, BF16), jax.ShapeDtypeStruct(xs_b.shape, BF16)],
        scratch_shapes=[pltpu.VMEM((rows_a + rows_b, D_MODEL), F32)],
        compiler_params=_params(2, FFN_VMEM_LIMIT_BYTES),
        name=f"ffn_l{layer}",
    )(xs_a, xs_b, w_gate, w_up, w_down)


def _scatter_kernel(cap, seq, final, y_ref, pos_ref, g_ref, x_ref, gate_ref, fg_ref, o_ref):
    slot = lax.broadcasted_iota(I32, (cap, seq), 0)
    for r in range(pos_ref.shape[0]):
        onehots, gated = [], []
        for e in range(N_EXPERTS):
            hit = slot == pos_ref[r, e:e + 1, :]
            gate = jnp.sum(jnp.where(hit, g_ref[r, e:e + 1, :], 0.0), axis=1, keepdims=True)
            gated.append((y_ref[e, r * cap:(r + 1) * cap, :].astype(F32) * gate).astype(BF16))
            onehots.append(hit.astype(BF16))
        moe = lax.dot_general(jnp.concatenate(onehots, axis=0), jnp.concatenate(gated, axis=0),
                              (((0,), (0,)), ((), ())), preferred_element_type=F32)
        rows = slice(r * seq, (r + 1) * seq)
        xn = x_ref[rows, :] + gate_ref[0] * moe
        if final:
            xn = xn * lax.rsqrt(jnp.mean(xn * xn, axis=-1, keepdims=True) + EPS) * fg_ref[...]
        o_ref[rows, :] = xn


def _scatter(y, pos, g, xn, mods, layer, stream, final, final_g_row):
    cap, seq, per = stream.cap, stream.seq, stream.req_per_step
    tok_blk = pl.BlockSpec((per * seq, D_MODEL), lambda i: (i, 0))
    sel_blk = pl.BlockSpec((per, N_EXPERTS, seq), lambda i: (i, 0, 0))
    return _deferred(
        functools.partial(_scatter_kernel, cap, seq, final),
        grid=(stream.n_req // per,),
        in_specs=[pl.BlockSpec((N_EXPERTS, per * cap, D_MODEL), lambda i: (0, i, 0)),
                  sel_blk, sel_blk, tok_blk,
                  _mod_spec(layer, stream, 5, per * seq),
                  pl.BlockSpec((1, D_MODEL), lambda i: (0, 0))],
        out_specs=tok_blk,
        out_shape=jax.ShapeDtypeStruct((stream.tokens, D_MODEL), F32),
        name=f"scatter_l{layer}_s{seq}",
    )(y, pos, g, xn, mods, final_g_row)


def _axial_rope_tables(rows, head_dim):
    f32 = np.float32
    row = np.repeat(np.arange(rows, dtype=f32), GRID_W)
    col = np.tile(np.arange(GRID_W, dtype=f32), rows)
    nf = head_dim // 4
    inv = (f32(ROPE_BASE) ** (-np.arange(nf, dtype=f32) / f32(nf))).astype(f32)
    ar = row[:, None] * inv[None]
    ac = col[:, None] * inv[None]
    ang = np.concatenate([ar, ar, ac, ac], axis=-1).astype(f32)
    return np.cos(ang).astype(f32), np.sin(ang).astype(f32)


def _block_diag(w):
    rows = [jnp.pad(w[:, n], ((0, 0), (0, 0), (n * LRU_BLOCK, (LRU_BLOCKS - 1 - n) * LRU_BLOCK)))
            for n in range(LRU_BLOCKS)]
    return jnp.concatenate(rows, axis=1)


def kernel(x_prompt, x_sample, cache_win_k, cache_win_v, state_lru, cache_diff_k, cache_diff_v, c, c_ctx, ada_w, ada_b, norm_g, final_g, even_w_in, even_w_out, conv_w, conv_b, lru_wa, lru_ba, lru_wx, lru_bx, lru_lambda, win_sink, odd_w_in, odd_w_out, diff_lq1, diff_lk1, diff_lq2, diff_lk2, diff_subln_g, moe_router, moe_w_gate, moe_w_up, moe_w_down):
    cv_t = jnp.concatenate([c_ctx[None], c, jnp.zeros((COND_ROWS - N_COND, D_MODEL), F32)], axis=0).T
    mods = _adaln(cv_t, ada_w, ada_b).reshape(DEPTH * COND_ROWS, 1, 6 * D_MODEL)

    cos, sin = _axial_rope_tables(DEC_SEQ // GRID_W, WIN_HEAD_DIM)
    rope_win = rope_diff = (jnp.asarray(np.tile(cos, (1, 2))), jnp.asarray(np.tile(sin, (1, 2))))

    xs = [x_prompt.reshape(CTX.tokens, D_MODEL), x_sample.reshape(LAT.tokens, D_MODEL)]
    final_g_row = final_g.reshape(1, D_MODEL)
    outs = {}

    def both(make):
        return _run_each(make(0, CTX), make(1, LAT))

    routers_t = jnp.swapaxes(moe_router, 1, 2)
    for layer in range(DEPTH):
        idx = layer // 2
        even = layer % 2 == 0
        w_in = (even_w_in if even else odd_w_in)[idx]
        w_out = (even_w_out if even else odd_w_out)[idx]
        wr_t = routers_t[layer]
        g1_row = norm_g[layer, 0].reshape(1, D_MODEL)
        g2_row = norm_g[layer, 1].reshape(1, D_MODEL)
        if even:
            wa = _block_diag(lru_wa[idx]).astype(BF16)
            wx = _block_diag(lru_wx[idx]).astype(BF16)
            (proj_c, win_k, win_v), (proj_d,) = _run_each(_proj_win(xs[0], g1_row, mods, layer, CTX, w_in),
                                                         _proj(xs[1], g1_row, mods, layer, LAT, w_in, EVEN_TOKEN_TILE))
            projs = (proj_c, proj_d)
            h0s = (jnp.zeros((CTX.n_req, 2, LRU_WIDTH), F32), state_lru[:, idx])
            (y_c, h_fin), (y_d, _) = both(lambda si, st: _lru(
                projs[si], st, conv_w[idx], conv_b[idx].reshape(1, LRU_WIDTH), wa, wx,
                lru_ba[idx], lru_bx[idx], lru_lambda[idx], h0s[si]))
            pack = lambda t: t[:, idx].transpose(0, 2, 1, 3).reshape(DEC_BATCH, PAST_LEN, WIN_KV)
            (o_c,), (o_d,) = _run_each(
                _win_attn(proj_c, CTX, win_sink[idx]),
                _win_attn(proj_d, LAT, win_sink[idx], cache=(pack(cache_win_k), pack(cache_win_v)), rope=rope_win))
            outs["win_k"] = win_k
            outs["win_v"] = win_v
            outs["lru"] = h_fin[:, None]
            mixed = ([y_c, o_c], [y_d, o_d])
        else:
            lambda_init = 0.8 - 0.6 * math.exp(-0.3 * layer)
            vec = lambda t: t[idx].reshape(1, DIFF_HEAD_DIM)
            args = (vec(diff_lq1), vec(diff_lk1), vec(diff_lq2), vec(diff_lk2), diff_subln_g[idx])
            (q, kh, vh), (proj_d,) = _run_each(_proj_heads(xs[0], g1_row, mods, layer, CTX, w_in),
                                              _proj(xs[1], g1_row, mods, layer, LAT, w_in))
            (o_c,), (o_d,) = _run_each(
                _diff_attn((q, kh, vh), CTX, lambda_init, *args),
                _diff_attn(proj_d, LAT, lambda_init, *args,
                           cache=(cache_diff_k[:, idx:idx + 1], cache_diff_v[:, idx:idx + 1]), rope=rope_diff))
            outs["diff_k"] = kh
            outs["diff_v"] = vh
            mixed = ([o_c], [o_d])

        mix = both(lambda si, st: _mix_out(mixed[si], w_out, xs[si], g2_row, mods, layer, st, wr_t))
        xns, h2s = (mix[0][0], mix[1][0]), (mix[0][1], mix[1][1])

        def request_major(lg, st):
            per_req = st.seq // LOGIT_TILE
            lg = lg.reshape(st.n_req, per_req, N_EXPERTS, LOGIT_TILE).transpose(0, 2, 1, 3)
            return lg.reshape(st.n_req, N_EXPERTS, st.seq)

        lgs = (request_major(mix[0][2], CTX), request_major(mix[1][2], LAT))
        routes = both(lambda si, st: _router(lgs[si], st))
        (rows_c,), (rows_d,) = both(lambda si, st: _gather(h2s[si], routes[si][0], st))
        ys = _ffn(rows_c, rows_d, layer, moe_w_gate, moe_w_up, moe_w_down)
        final = layer == DEPTH - 1
        (x_c,), (x_d,) = both(lambda si, st: _scatter(ys[si], routes[si][0], routes[si][1], xns[si], mods, layer, st,
                                                      final, final_g_row))
        xs = [x_c, x_d]


    y_prompt = xs[0].reshape(BATCH, SEQ, D_MODEL)
    y_sample = xs[1].reshape(DEC_BATCH, DEC_SEQ, D_MODEL)
    return (y_prompt, y_sample, outs["win_k"], outs["win_v"], outs["lru"], outs["diff_k"], outs["diff_v"])
```
